```python
import jax, jax.numpy as jnp
from jax import lax
import numpy as np

D_MODEL = 2048
BATCH = 8
SEQ = 4096
DEPTH = 1

D_MIX = D_MODEL
HEAD_DIM = 64
RWKV_WIDTH = D_MIX // 2
FOX_WIDTH = D_MIX - RWKV_WIDTH
RWKV_HEADS = RWKV_WIDTH // HEAD_DIM
FOX_HEADS = FOX_WIDTH // HEAD_DIM
DECAY_LORA = max(32, int(round(1.8 * D_MODEL ** 0.5 / 32)) * 32)
ICLR_LORA = max(32, int(round(1.8 * D_MODEL ** 0.5 / 32)) * 32)
GATE_LORA = max(32, int(round(0.6 * D_MODEL ** 0.8 / 32)) * 32)
RWKV_COLS = 3 * RWKV_WIDTH + DECAY_LORA + ICLR_LORA + GATE_LORA
FOX_COLS = 3 * FOX_WIDTH + FOX_HEADS
N_IN = RWKV_COLS + FOX_COLS
D_FF = -(-(8 * D_MODEL) // (3 * 256)) * 256
PLE_DIM = 256
BLOCK_Q = 128
RMS_EPS = 1e-6
GN_EPS = 64e-5

kernel_name = "rwkv7_fox_hybrid_layer"


def rmsnorm(x, g, eps=RMS_EPS):
    xf = x.astype(jnp.float32)
    y = xf * lax.rsqrt(jnp.mean(xf * xf, axis=-1, keepdims=True) + eps)
    return (y * g.astype(jnp.float32)).astype(x.dtype)


def token_shift(u):
    return jnp.pad(u, ((0, 0), (1, 0), (0, 0)))[:, :-1]


def rwkv7_scan(r, w, k, v, kk, a):
    b, s, h, n = r.shape
    seq_first = lambda t: jnp.swapaxes(t, 0, 1)

    def step(state, inp):
        r_t, w_t, k_t, v_t, kk_t, a_t = inp
        sa = jnp.einsum('bhvk,bhk->bhv', state, -kk_t)
        state = (state * w_t[:, :, None, :]
                 + sa[..., None] * (kk_t * a_t)[:, :, None, :]
                 + v_t[..., None] * k_t[:, :, None, :])
        y_t = jnp.einsum('bhvk,bhk->bhv', state, r_t)
        return state, y_t

    state0 = jnp.zeros((b, h, n, n), jnp.float32)
    _, y = lax.scan(step, state0, tuple(seq_first(t) for t in (r, w, k, v, kk, a)))
    return seq_first(y)


def rwkv7_mix(u, mu, w0, w2, a0, a2, g2, k_k, k_a, r_k, lnx_g, lnx_b):
    b, s, _ = u.shape
    u = (u + (token_shift(u) - u) * mu).astype(jnp.float32)
    r, k, v, xw, xa, xg = jnp.split(
        u, np.cumsum([RWKV_WIDTH, RWKV_WIDTH, RWKV_WIDTH, DECAY_LORA, ICLR_LORA]).tolist(), axis=-1)
    w_log = -jax.nn.softplus(-(w0 + jnp.tanh(xw) @ w2)) - 0.5
    decay = jnp.exp(-jnp.exp(w_log))
    a = jax.nn.sigmoid(a0 + xa @ a2)
    g = jax.nn.sigmoid(xg) @ g2
    heads = lambda t: t.reshape(b, s, RWKV_HEADS, HEAD_DIM)
    kk = heads(k * k_k)
    kk = kk / jnp.maximum(jnp.linalg.norm(kk, axis=-1, keepdims=True), 1e-12)
    k = k * (1.0 + (a - 1.0) * k_a)
    rh, kh, vh = heads(r), heads(k), heads(v)
    y = rwkv7_scan(rh, heads(decay), kh, vh, kk, heads(a))
    mean = jnp.mean(y, axis=-1, keepdims=True)
    var = jnp.mean(jnp.square(y - mean), axis=-1, keepdims=True)
    y = ((y - mean) * lax.rsqrt(var + GN_EPS)).reshape(b, s, RWKV_WIDTH) * lnx_g + lnx_b
    bonus = jnp.sum(rh * kh * r_k, axis=-1, keepdims=True) * vh
    y = (y + bonus.reshape(b, s, RWKV_WIDTH)) * g
    return y


def fox_mix(u, q_norm_g, k_norm_g, fgate_b):
    b, s, _ = u.shape
    q, k, v, f_raw = jnp.split(u, [FOX_WIDTH, 2 * FOX_WIDTH, 3 * FOX_WIDTH], axis=-1)
    heads = lambda t: t.reshape(b, s, FOX_HEADS, HEAD_DIM).astype(jnp.float32)
    q = rmsnorm(heads(q), q_norm_g)
    k = rmsnorm(heads(k), k_norm_g)
    v = heads(v)
    log_f = jax.nn.log_sigmoid(f_raw.astype(jnp.float32) + fgate_b)
    c = jnp.cumsum(log_f, axis=1).transpose(0, 2, 1)
    scale = HEAD_DIM ** -0.5
    outs = []
    for blk in range(s // BLOCK_Q):
        s0, s1 = blk * BLOCK_Q, (blk + 1) * BLOCK_Q
        scores = jnp.einsum('bqhd,bkhd->bhqk', q[:, s0:s1], k[:, :s1]) * scale
        bias = c[:, :, s0:s1, None] - c[:, :, None, :s1]
        causal = jnp.arange(s1)[None, :] <= (s0 + jnp.arange(BLOCK_Q))[:, None]
        scores = jnp.where(causal, scores + bias, -jnp.inf)
        probs = jax.nn.softmax(scores, axis=-1)
        outs.append(jnp.einsum('bhqk,bkhd->bqhd', probs, v[:, :s1]))
    return jnp.concatenate(outs, axis=1).reshape(b, s, FOX_WIDTH)


def _fwd_setup_inputs(seed: int = 0) -> dict:
    key = jax.random.key(seed)
    ks = jax.random.split(key, 32)
    nrm = lambda k, shape, sc: jax.random.normal(k, shape, jnp.float32) * sc
    uni = lambda k, shape, lo, hi: jax.random.uniform(k, shape, jnp.float32, lo, hi)
    L = DEPTH
    return {
        "x": nrm(ks[0], (BATCH, SEQ, D_MODEL), 1.0),
        "p": nrm(ks[1], (DEPTH, BATCH, SEQ, PLE_DIM), 1.0),
        "attn_norm_g": 1.0 + nrm(ks[2], (L, D_MODEL), 0.02),
        "w_in": nrm(ks[3], (L, D_MODEL, N_IN), D_MODEL ** -0.5),
        "shift_mu": uni(ks[4], (L, RWKV_COLS), 0.0, 1.0),
        "w0": uni(ks[5], (L, RWKV_WIDTH), -6.0, 1.0),
        "w2": nrm(ks[6], (L, DECAY_LORA, RWKV_WIDTH), 0.5 * DECAY_LORA ** -0.5),
        "a0": nrm(ks[7], (L, RWKV_WIDTH), 0.1),
        "a2": nrm(ks[8], (L, ICLR_LORA, RWKV_WIDTH), 0.5 * ICLR_LORA ** -0.5),
        "g2": nrm(ks[9], (L, GATE_LORA, RWKV_WIDTH), GATE_LORA ** -0.5),
        "k_k": 0.85 + nrm(ks[10], (L, RWKV_WIDTH), 0.05),
        "k_a": 1.0 + nrm(ks[11], (L, RWKV_WIDTH), 0.05),
        "r_k": nrm(ks[12], (L, RWKV_HEADS, HEAD_DIM), 0.1),
        "lnx_g": 1.0 + nrm(ks[13], (L, RWKV_WIDTH), 0.02),
        "lnx_b": nrm(ks[14], (L, RWKV_WIDTH), 0.02),
        "q_norm_g": 1.0 + nrm(ks[15], (L, HEAD_DIM), 0.02),
        "k_norm_g": 1.0 + nrm(ks[16], (L, HEAD_DIM), 0.02),
        "fgate_b": uni(ks[17], (L, FOX_HEADS), 1.0, 6.0),
        "w_out": nrm(ks[18], (L, D_MIX, D_MODEL), D_MIX ** -0.5),
        "ffn_norm_g": 1.0 + nrm(ks[19], (L, D_MODEL), 0.02),
        "w_gate": nrm(ks[20], (L, D_MODEL, D_FF), D_MODEL ** -0.5),
        "w_up": nrm(ks[21], (L, D_MODEL, D_FF), D_MODEL ** -0.5),
        "w_down": nrm(ks[22], (L, D_FF, D_MODEL), D_FF ** -0.5),
        "ple_proj": nrm(ks[23], (L, PLE_DIM, D_MODEL), PLE_DIM ** -0.5),
        "ple_norm_g": 1.0 + nrm(ks[24], (L, D_MODEL), 0.02),
        "ple_gate_norm_g": 1.0 + nrm(ks[25], (L, D_MODEL), 0.02),
        "ple_gate_w": nrm(ks[26], (L, D_MODEL, D_MODEL), D_MODEL ** -0.5),
        "ple_gate_b": nrm(ks[27], (L, D_MODEL), 0.02),
    }


def _fwd_reference(x, p, attn_norm_g, w_in, shift_mu, w0, w2, a0, a2, g2, k_k, k_a, r_k,
              lnx_g, lnx_b, q_norm_g, k_norm_g, fgate_b, w_out, ffn_norm_g, w_gate,
              w_up, w_down, ple_proj, ple_norm_g, ple_gate_norm_g, ple_gate_w, ple_gate_b):
    h = x
    for i in range(DEPTH):
        u = rmsnorm(h, attn_norm_g[i]) @ w_in[i]
        u_rwkv, u_fox = u[..., :RWKV_COLS], u[..., RWKV_COLS:]
        y_rwkv = rwkv7_mix(u_rwkv, shift_mu[i], w0[i], w2[i], a0[i], a2[i], g2[i],
                           k_k[i], k_a[i], r_k[i], lnx_g[i], lnx_b[i])
        y_fox = fox_mix(u_fox, q_norm_g[i], k_norm_g[i], fgate_b[i])
        y = jnp.concatenate([y_rwkv, y_fox], axis=-1).astype(h.dtype)
        h = h + y @ w_out[i]
        hn = rmsnorm(h, ffn_norm_g[i])
        h = h + (jax.nn.silu(hn @ w_gate[i]) * (hn @ w_up[i])) @ w_down[i]
        e = rmsnorm(p[i] @ ple_proj[i], ple_norm_g[i])
        gate = jax.nn.sigmoid(rmsnorm(h, ple_gate_norm_g[i]) @ ple_gate_w[i] + ple_gate_b[i])
        h = h + gate * e
    return h


import jax as _jax
import jax.numpy as _jnp

TWIN_FORMAT = 'train_step'
FWD_PARAMS = ['x', 'p', 'attn_norm_g', 'w_in', 'shift_mu', 'w0', 'w2', 'a0', 'a2', 'g2', 'k_k', 'k_a', 'r_k', 'lnx_g', 'lnx_b', 'q_norm_g', 'k_norm_g', 'fgate_b', 'w_out', 'ffn_norm_g', 'w_gate', 'w_up', 'w_down', 'ple_proj', 'ple_norm_g', 'ple_gate_norm_g', 'ple_gate_w', 'ple_gate_b']
TWIN_WEIGHTS = ['attn_norm_g', 'w_in', 'shift_mu', 'w0', 'w2', 'a0', 'a2', 'g2', 'k_k', 'k_a', 'r_k', 'lnx_g', 'lnx_b', 'q_norm_g', 'k_norm_g', 'fgate_b', 'w_out', 'ffn_norm_g', 'w_gate', 'w_up', 'w_down', 'ple_proj', 'ple_norm_g', 'ple_gate_norm_g', 'ple_gate_w', 'ple_gate_b']
TWIN_DIFF_INPUT = 'x'
TWIN_INPUTS = ['x', 'p', 'attn_norm_g', 'w_in', 'shift_mu', 'w0', 'w2', 'a0', 'a2', 'g2', 'k_k', 'k_a', 'r_k', 'lnx_g', 'lnx_b', 'q_norm_g', 'k_norm_g', 'fgate_b', 'w_out', 'ffn_norm_g', 'w_gate', 'w_up', 'w_down', 'ple_proj', 'ple_norm_g', 'ple_gate_norm_g', 'ple_gate_w', 'ple_gate_b', 'loss_target', 'm_attn_norm_g', 'm_w_in', 'm_shift_mu', 'm_w0', 'm_w2', 'm_a0', 'm_a2', 'm_g2', 'm_k_k', 'm_k_a', 'm_r_k', 'm_lnx_g', 'm_lnx_b', 'm_q_norm_g', 'm_k_norm_g', 'm_fgate_b', 'm_w_out', 'm_ffn_norm_g', 'm_w_gate', 'm_w_up', 'm_w_down', 'm_ple_proj', 'm_ple_norm_g', 'm_ple_gate_norm_g', 'm_ple_gate_w', 'm_ple_gate_b', 'v_attn_norm_g', 'v_w_in', 'v_shift_mu', 'v_w0', 'v_w2', 'v_a0', 'v_a2', 'v_g2', 'v_k_k', 'v_k_a', 'v_r_k', 'v_lnx_g', 'v_lnx_b', 'v_q_norm_g', 'v_k_norm_g', 'v_fgate_b', 'v_w_out', 'v_ffn_norm_g', 'v_w_gate', 'v_w_up', 'v_w_down', 'v_ple_proj', 'v_ple_norm_g', 'v_ple_gate_norm_g', 'v_ple_gate_w', 'v_ple_gate_b']
TWIN_OUTPUTS = ['loss', 'grad_x', 'grad_attn_norm_g', 'grad_w_in', 'grad_shift_mu', 'grad_w0', 'grad_w2', 'grad_a0', 'grad_a2', 'grad_g2', 'grad_k_k', 'grad_k_a', 'grad_r_k', 'grad_lnx_g', 'grad_lnx_b', 'grad_q_norm_g', 'grad_k_norm_g', 'grad_fgate_b', 'grad_w_out', 'grad_ffn_norm_g', 'grad_w_gate', 'grad_w_up', 'grad_w_down', 'grad_ple_proj', 'grad_ple_norm_g', 'grad_ple_gate_norm_g', 'grad_ple_gate_w', 'grad_ple_gate_b', 'delta_attn_norm_g', 'delta_w_in', 'delta_shift_mu', 'delta_w0', 'delta_w2', 'delta_a0', 'delta_a2', 'delta_g2', 'delta_k_k', 'delta_k_a', 'delta_r_k', 'delta_lnx_g', 'delta_lnx_b', 'delta_q_norm_g', 'delta_k_norm_g', 'delta_fgate_b', 'delta_w_out', 'delta_ffn_norm_g', 'delta_w_gate', 'delta_w_up', 'delta_w_down', 'delta_ple_proj', 'delta_ple_norm_g', 'delta_ple_gate_norm_g', 'delta_ple_gate_w', 'delta_ple_gate_b', 'new_m_attn_norm_g', 'new_m_w_in', 'new_m_shift_mu', 'new_m_w0', 'new_m_w2', 'new_m_a0', 'new_m_a2', 'new_m_g2', 'new_m_k_k', 'new_m_k_a', 'new_m_r_k', 'new_m_lnx_g', 'new_m_lnx_b', 'new_m_q_norm_g', 'new_m_k_norm_g', 'new_m_fgate_b', 'new_m_w_out', 'new_m_ffn_norm_g', 'new_m_w_gate', 'new_m_w_up', 'new_m_w_down', 'new_m_ple_proj', 'new_m_ple_norm_g', 'new_m_ple_gate_norm_g', 'new_m_ple_gate_w', 'new_m_ple_gate_b', 'new_v_attn_norm_g', 'new_v_w_in', 'new_v_shift_mu', 'new_v_w0', 'new_v_w2', 'new_v_a0', 'new_v_a2', 'new_v_g2', 'new_v_k_k', 'new_v_k_a', 'new_v_r_k', 'new_v_lnx_g', 'new_v_lnx_b', 'new_v_q_norm_g', 'new_v_k_norm_g', 'new_v_fgate_b', 'new_v_w_out', 'new_v_ffn_norm_g', 'new_v_w_gate', 'new_v_w_up', 'new_v_w_down', 'new_v_ple_proj', 'new_v_ple_norm_g', 'new_v_ple_gate_norm_g', 'new_v_ple_gate_w', 'new_v_ple_gate_b']
TWIN_LEAF_KINDS = {'loss': 'loss', 'grad_x': 'grad_x', 'grad_attn_norm_g': 'grad_w', 'grad_w_in': 'grad_w', 'grad_shift_mu': 'grad_w', 'grad_w0': 'grad_w', 'grad_w2': 'grad_w', 'grad_a0': 'grad_w', 'grad_a2': 'grad_w', 'grad_g2': 'grad_w', 'grad_k_k': 'grad_w', 'grad_k_a': 'grad_w', 'grad_r_k': 'grad_w', 'grad_lnx_g': 'grad_w', 'grad_lnx_b': 'grad_w', 'grad_q_norm_g': 'grad_w', 'grad_k_norm_g': 'grad_w', 'grad_fgate_b': 'grad_w', 'grad_w_out': 'grad_w', 'grad_ffn_norm_g': 'grad_w', 'grad_w_gate': 'grad_w', 'grad_w_up': 'grad_w', 'grad_w_down': 'grad_w', 'grad_ple_proj': 'grad_w', 'grad_ple_norm_g': 'grad_w', 'grad_ple_gate_norm_g': 'grad_w', 'grad_ple_gate_w': 'grad_w', 'grad_ple_gate_b': 'grad_w', 'delta_attn_norm_g': 'delta_w', 'delta_w_in': 'delta_w', 'delta_shift_mu': 'delta_w', 'delta_w0': 'delta_w', 'delta_w2': 'delta_w', 'delta_a0': 'delta_w', 'delta_a2': 'delta_w', 'delta_g2': 'delta_w', 'delta_k_k': 'delta_w', 'delta_k_a': 'delta_w', 'delta_r_k': 'delta_w', 'delta_lnx_g': 'delta_w', 'delta_lnx_b': 'delta_w', 'delta_q_norm_g': 'delta_w', 'delta_k_norm_g': 'delta_w', 'delta_fgate_b': 'delta_w', 'delta_w_out': 'delta_w', 'delta_ffn_norm_g': 'delta_w', 'delta_w_gate': 'delta_w', 'delta_w_up': 'delta_w', 'delta_w_down': 'delta_w', 'delta_ple_proj': 'delta_w', 'delta_ple_norm_g': 'delta_w', 'delta_ple_gate_norm_g': 'delta_w', 'delta_ple_gate_w': 'delta_w', 'delta_ple_gate_b': 'delta_w', 'new_m_attn_norm_g': 'new_m', 'new_m_w_in': 'new_m', 'new_m_shift_mu': 'new_m', 'new_m_w0': 'new_m', 'new_m_w2': 'new_m', 'new_m_a0': 'new_m', 'new_m_a2': 'new_m', 'new_m_g2': 'new_m', 'new_m_k_k': 'new_m', 'new_m_k_a': 'new_m', 'new_m_r_k': 'new_m', 'new_m_lnx_g': 'new_m', 'new_m_lnx_b': 'new_m', 'new_m_q_norm_g': 'new_m', 'new_m_k_norm_g': 'new_m', 'new_m_fgate_b': 'new_m', 'new_m_w_out': 'new_m', 'new_m_ffn_norm_g': 'new_m', 'new_m_w_gate': 'new_m', 'new_m_w_up': 'new_m', 'new_m_w_down': 'new_m', 'new_m_ple_proj': 'new_m', 'new_m_ple_norm_g': 'new_m', 'new_m_ple_gate_norm_g': 'new_m', 'new_m_ple_gate_w': 'new_m', 'new_m_ple_gate_b': 'new_m', 'new_v_attn_norm_g': 'new_v', 'new_v_w_in': 'new_v', 'new_v_shift_mu': 'new_v', 'new_v_w0': 'new_v', 'new_v_w2': 'new_v', 'new_v_a0': 'new_v', 'new_v_a2': 'new_v', 'new_v_g2': 'new_v', 'new_v_k_k': 'new_v', 'new_v_k_a': 'new_v', 'new_v_r_k': 'new_v', 'new_v_lnx_g': 'new_v', 'new_v_lnx_b': 'new_v', 'new_v_q_norm_g': 'new_v', 'new_v_k_norm_g': 'new_v', 'new_v_fgate_b': 'new_v', 'new_v_w_out': 'new_v', 'new_v_ffn_norm_g': 'new_v', 'new_v_w_gate': 'new_v', 'new_v_w_up': 'new_v', 'new_v_w_down': 'new_v', 'new_v_ple_proj': 'new_v', 'new_v_ple_norm_g': 'new_v', 'new_v_ple_gate_norm_g': 'new_v', 'new_v_ple_gate_w': 'new_v', 'new_v_ple_gate_b': 'new_v'}


def _forward(args):
    return _fwd_reference(*[args[k] for k in FWD_PARAMS])


def _output_shape():
    def fwd():
        inp = _fwd_setup_inputs(0)
        return _fwd_reference(*[inp[k] for k in FWD_PARAMS])
    out = _jax.eval_shape(fwd)
    return out.shape, out.dtype

N_MICROBATCH = 1
ADAM_LR = 0.001
ADAM_B1 = 0.9
ADAM_B2 = 0.999
ADAM_EPS = 1e-08
ADAM_WD = 0.01
ADAM_STEP = 10
PER_EXAMPLE_BATCH_AXIS = {'x': 0, 'p': 1, 'loss_target': 0}
SHARED_INPUTS = []
_WEIGHT_DTYPES = {'attn_norm_g': _jnp.float32, 'w_in': _jnp.float32, 'shift_mu': _jnp.float32, 'w0': _jnp.float32, 'w2': _jnp.float32, 'a0': _jnp.float32, 'a2': _jnp.float32, 'g2': _jnp.float32, 'k_k': _jnp.float32, 'k_a': _jnp.float32, 'r_k': _jnp.float32, 'lnx_g': _jnp.float32, 'lnx_b': _jnp.float32, 'q_norm_g': _jnp.float32, 'k_norm_g': _jnp.float32, 'fgate_b': _jnp.float32, 'w_out': _jnp.float32, 'ffn_norm_g': _jnp.float32, 'w_gate': _jnp.float32, 'w_up': _jnp.float32, 'w_down': _jnp.float32, 'ple_proj': _jnp.float32, 'ple_norm_g': _jnp.float32, 'ple_gate_norm_g': _jnp.float32, 'ple_gate_w': _jnp.float32, 'ple_gate_b': _jnp.float32}
MOMENT_SCALE = {'attn_norm_g': 1.243959e+00, 'w_in': 1.105798e-01, 'shift_mu': 2.061780e+00, 'w0': 5.588043e-02, 'w2': 5.991671e-03, 'a0': 2.625993e-01, 'a2': 5.113505e-02, 'g2': 4.213150e+00, 'k_k': 2.124572e-01, 'k_a': 5.300389e-01, 'r_k': 3.146799e+00, 'lnx_g': 7.112018e+00, 'lnx_b': 1.410080e+00, 'q_norm_g': 8.223908e+00, 'k_norm_g': 8.210659e+00, 'fgate_b': 3.140580e+01, 'w_out': 1.296750e-01, 'ffn_norm_g': 1.239778e+01, 'w_gate': 7.675887e-02, 'w_up': 7.965044e-02, 'w_down': 1.211877e-01, 'ple_proj': 7.904973e-02, 'ple_norm_g': 4.721112e+00, 'ple_gate_norm_g': 4.956398e-01, 'ple_gate_w': 3.768599e-02, 'ple_gate_b': 1.666578e+00}


def _to_microbatches(a, axis):
    t = _jnp.moveaxis(a, axis, 0)
    t = t.reshape((N_MICROBATCH, t.shape[0] // N_MICROBATCH) + t.shape[1:])
    return _jnp.moveaxis(t, 1, axis + 1)


def setup_inputs(seed: int = 0) -> dict:
    inp = _fwd_setup_inputs(seed)
    key = _jax.random.fold_in(_jax.random.key(seed), 7919)
    shape, _ = _output_shape()
    out = dict(inp)
    out["loss_target"] = _jax.random.normal(_jax.random.fold_in(key, 0), shape, _jnp.float32)
    for i, name in enumerate(TWIN_WEIGHTS):
        w = inp[name].astype(_jnp.float32)
        if MOMENT_SCALE is None:
            s = _jnp.sqrt(_jnp.mean(_jnp.square(w)) + 1e-30)
        else:
            s = MOMENT_SCALE[name]
        km, kv = _jax.random.split(_jax.random.fold_in(key, i + 1))
        out[name] = w
        out["m_" + name] = s * _jax.random.normal(km, w.shape, _jnp.float32)
        out["v_" + name] = (s * s) * _jax.random.uniform(kv, w.shape, _jnp.float32, 0.5, 1.5)
    if N_MICROBATCH > 1:
        for name, axis in PER_EXAMPLE_BATCH_AXIS.items():
            out[name] = _to_microbatches(out[name], axis)
    return {'x': out['x'], 'p': out['p'], 'attn_norm_g': out['attn_norm_g'], 'w_in': out['w_in'], 'shift_mu': out['shift_mu'], 'w0': out['w0'], 'w2': out['w2'], 'a0': out['a0'], 'a2': out['a2'], 'g2': out['g2'], 'k_k': out['k_k'], 'k_a': out['k_a'], 'r_k': out['r_k'], 'lnx_g': out['lnx_g'], 'lnx_b': out['lnx_b'], 'q_norm_g': out['q_norm_g'], 'k_norm_g': out['k_norm_g'], 'fgate_b': out['fgate_b'], 'w_out': out['w_out'], 'ffn_norm_g': out['ffn_norm_g'], 'w_gate': out['w_gate'], 'w_up': out['w_up'], 'w_down': out['w_down'], 'ple_proj': out['ple_proj'], 'ple_norm_g': out['ple_norm_g'], 'ple_gate_norm_g': out['ple_gate_norm_g'], 'ple_gate_w': out['ple_gate_w'], 'ple_gate_b': out['ple_gate_b'], 'loss_target': out['loss_target'], 'm_attn_norm_g': out['m_attn_norm_g'], 'm_w_in': out['m_w_in'], 'm_shift_mu': out['m_shift_mu'], 'm_w0': out['m_w0'], 'm_w2': out['m_w2'], 'm_a0': out['m_a0'], 'm_a2': out['m_a2'], 'm_g2': out['m_g2'], 'm_k_k': out['m_k_k'], 'm_k_a': out['m_k_a'], 'm_r_k': out['m_r_k'], 'm_lnx_g': out['m_lnx_g'], 'm_lnx_b': out['m_lnx_b'], 'm_q_norm_g': out['m_q_norm_g'], 'm_k_norm_g': out['m_k_norm_g'], 'm_fgate_b': out['m_fgate_b'], 'm_w_out': out['m_w_out'], 'm_ffn_norm_g': out['m_ffn_norm_g'], 'm_w_gate': out['m_w_gate'], 'm_w_up': out['m_w_up'], 'm_w_down': out['m_w_down'], 'm_ple_proj': out['m_ple_proj'], 'm_ple_norm_g': out['m_ple_norm_g'], 'm_ple_gate_norm_g': out['m_ple_gate_norm_g'], 'm_ple_gate_w': out['m_ple_gate_w'], 'm_ple_gate_b': out['m_ple_gate_b'], 'v_attn_norm_g': out['v_attn_norm_g'], 'v_w_in': out['v_w_in'], 'v_shift_mu': out['v_shift_mu'], 'v_w0': out['v_w0'], 'v_w2': out['v_w2'], 'v_a0': out['v_a0'], 'v_a2': out['v_a2'], 'v_g2': out['v_g2'], 'v_k_k': out['v_k_k'], 'v_k_a': out['v_k_a'], 'v_r_k': out['v_r_k'], 'v_lnx_g': out['v_lnx_g'], 'v_lnx_b': out['v_lnx_b'], 'v_q_norm_g': out['v_q_norm_g'], 'v_k_norm_g': out['v_k_norm_g'], 'v_fgate_b': out['v_fgate_b'], 'v_w_out': out['v_w_out'], 'v_ffn_norm_g': out['v_ffn_norm_g'], 'v_w_gate': out['v_w_gate'], 'v_w_up': out['v_w_up'], 'v_w_down': out['v_w_down'], 'v_ple_proj': out['v_ple_proj'], 'v_ple_norm_g': out['v_ple_norm_g'], 'v_ple_gate_norm_g': out['v_ple_gate_norm_g'], 'v_ple_gate_w': out['v_ple_gate_w'], 'v_ple_gate_b': out['v_ple_gate_b']}


def _loss(weights, diff, rest, loss_target):
    with _jax.named_scope("forward"):
        args = {**rest, TWIN_DIFF_INPUT: diff, **{k: w.astype(_WEIGHT_DTYPES[k]) for k, w in weights.items()}}
        y = _forward(args)
    with _jax.named_scope("loss_head"):
        err = _jnp.square(y.astype(_jnp.float32) - loss_target)
        return 0.5 * _jnp.sum(_jnp.mean(err, axis=-1)) if err.ndim else 0.5 * err


def _adamw(w, g, m, v):
    m = ADAM_B1 * m + (1.0 - ADAM_B1) * g
    v = ADAM_B2 * v + (1.0 - ADAM_B2) * _jnp.square(g)
    m_hat = m / (1.0 - ADAM_B1 ** ADAM_STEP)
    v_hat = v / (1.0 - ADAM_B2 ** ADAM_STEP)
    delta = -ADAM_LR * (m_hat / (_jnp.sqrt(v_hat) + ADAM_EPS) + ADAM_WD * w)
    return delta, m, v


def reference(x, p, attn_norm_g, w_in, shift_mu, w0, w2, a0, a2, g2, k_k, k_a, r_k, lnx_g, lnx_b, q_norm_g, k_norm_g, fgate_b, w_out, ffn_norm_g, w_gate, w_up, w_down, ple_proj, ple_norm_g, ple_gate_norm_g, ple_gate_w, ple_gate_b, loss_target, m_attn_norm_g, m_w_in, m_shift_mu, m_w0, m_w2, m_a0, m_a2, m_g2, m_k_k, m_k_a, m_r_k, m_lnx_g, m_lnx_b, m_q_norm_g, m_k_norm_g, m_fgate_b, m_w_out, m_ffn_norm_g, m_w_gate, m_w_up, m_w_down, m_ple_proj, m_ple_norm_g, m_ple_gate_norm_g, m_ple_gate_w, m_ple_gate_b, v_attn_norm_g, v_w_in, v_shift_mu, v_w0, v_w2, v_a0, v_a2, v_g2, v_k_k, v_k_a, v_r_k, v_lnx_g, v_lnx_b, v_q_norm_g, v_k_norm_g, v_fgate_b, v_w_out, v_ffn_norm_g, v_w_gate, v_w_up, v_w_down, v_ple_proj, v_ple_norm_g, v_ple_gate_norm_g, v_ple_gate_w, v_ple_gate_b):
    given = dict(x=x, p=p, attn_norm_g=attn_norm_g, w_in=w_in, shift_mu=shift_mu, w0=w0, w2=w2, a0=a0, a2=a2, g2=g2, k_k=k_k, k_a=k_a, r_k=r_k, lnx_g=lnx_g, lnx_b=lnx_b, q_norm_g=q_norm_g, k_norm_g=k_norm_g, fgate_b=fgate_b, w_out=w_out, ffn_norm_g=ffn_norm_g, w_gate=w_gate, w_up=w_up, w_down=w_down, ple_proj=ple_proj, ple_norm_g=ple_norm_g, ple_gate_norm_g=ple_gate_norm_g, ple_gate_w=ple_gate_w, ple_gate_b=ple_gate_b, loss_target=loss_target, m_attn_norm_g=m_attn_norm_g, m_w_in=m_w_in, m_shift_mu=m_shift_mu, m_w0=m_w0, m_w2=m_w2, m_a0=m_a0, m_a2=m_a2, m_g2=m_g2, m_k_k=m_k_k, m_k_a=m_k_a, m_r_k=m_r_k, m_lnx_g=m_lnx_g, m_lnx_b=m_lnx_b, m_q_norm_g=m_q_norm_g, m_k_norm_g=m_k_norm_g, m_fgate_b=m_fgate_b, m_w_out=m_w_out, m_ffn_norm_g=m_ffn_norm_g, m_w_gate=m_w_gate, m_w_up=m_w_up, m_w_down=m_w_down, m_ple_proj=m_ple_proj, m_ple_norm_g=m_ple_norm_g, m_ple_gate_norm_g=m_ple_gate_norm_g, m_ple_gate_w=m_ple_gate_w, m_ple_gate_b=m_ple_gate_b, v_attn_norm_g=v_attn_norm_g, v_w_in=v_w_in, v_shift_mu=v_shift_mu, v_w0=v_w0, v_w2=v_w2, v_a0=v_a0, v_a2=v_a2, v_g2=v_g2, v_k_k=v_k_k, v_k_a=v_k_a, v_r_k=v_r_k, v_lnx_g=v_lnx_g, v_lnx_b=v_lnx_b, v_q_norm_g=v_q_norm_g, v_k_norm_g=v_k_norm_g, v_fgate_b=v_fgate_b, v_w_out=v_w_out, v_ffn_norm_g=v_ffn_norm_g, v_w_gate=v_w_gate, v_w_up=v_w_up, v_w_down=v_w_down, v_ple_proj=v_ple_proj, v_ple_norm_g=v_ple_norm_g, v_ple_gate_norm_g=v_ple_gate_norm_g, v_ple_gate_w=v_ple_gate_w, v_ple_gate_b=v_ple_gate_b)
    weights = {n: given[n] for n in TWIN_WEIGHTS}
    shared = {n: given[n] for n in SHARED_INPUTS}
    per_example = {n: given[n] for n in ['x', 'p']}
    grad_fn = _jax.value_and_grad(_loss, argnums=(0, 1))

    def one_microbatch(ex, loss_target):
        ex = dict(ex)
        diff = ex.pop(TWIN_DIFF_INPUT)
        return grad_fn(weights, diff, {**shared, **ex}, loss_target)

    if N_MICROBATCH == 1:
        loss, (grad_w, grad_x) = one_microbatch(per_example, given["loss_target"])
    else:
        def body(carry, xs):
            loss_sum, grad_sum = carry
            l_k, (gw_k, gx_k) = one_microbatch(xs[0], xs[1])
            with _jax.named_scope("update"):
                return (loss_sum + l_k, _jax.tree.map(_jnp.add, grad_sum, gw_k)), gx_k

        init = (_jnp.zeros((), _jnp.float32), _jax.tree.map(_jnp.zeros_like, weights))
        (loss, grad_w), grad_x = _jax.lax.scan(body, init, (per_example, given["loss_target"]))
    with _jax.named_scope("update"):
        delta_w, new_m, new_v = {}, {}, {}
        for n in TWIN_WEIGHTS:
            delta_w[n], new_m[n], new_v[n] = _adamw(weights[n], grad_w[n], given["m_" + n], given["v_" + n])
    return (loss, grad_x, *[grad_w[n] for n in TWIN_WEIGHTS], *[delta_w[n] for n in TWIN_WEIGHTS],
            *[new_m[n] for n in TWIN_WEIGHTS], *[new_v[n] for n in TWIN_WEIGHTS])
```

```python
import functools

import jax
import jax.numpy as jnp
from jax import lax
from jax.experimental import pallas as pl
from jax.experimental.pallas import tpu as pltpu

F32 = jnp.float32
BF16 = jnp.bfloat16
LANE = 128
HEAD = 64
RMS_EPS = 1e-6
GN_EPS = 64e-5
ADAM_LR, ADAM_B1, ADAM_B2, ADAM_EPS, ADAM_WD, ADAM_STEP = 0.001, 0.9, 0.999, 1e-08, 0.01, 10
VMEM_LIMIT = 56 * 1024 * 1024
NEG = -1e30
MESH = pl.DeviceIdType.MESH
N_CHIPS = 4
N_DEV = 8


def _rup(n, m):
    return -(-n // m) * m


def _pick(n, cands):
    for c in cands:
        if n % c == 0:
            return c
    return n


def _cparams(*sem):
    return pltpu.CompilerParams(dimension_semantics=sem, vmem_limit_bytes=VMEM_LIMIT)


def _mm(a, b, *, ta=False, tb=False, add=None, out_dtype=F32, name):
    M, K = (a.shape[1], a.shape[0]) if ta else a.shape
    N = b.shape[0] if tb else b.shape[1]
    tm = _pick(M, (512, 256, 128))
    tn = _pick(N, (512, 640, 256, 128))
    tk = _pick(K, (512, 640, 256, 128))
    nk = K // tk
    dn = (((0 if ta else 1,), (1 if tb else 0,)), ((), ()))

    def body(*refs):
        if add is None:
            a_ref, b_ref, o_ref, acc = refs
        else:
            a_ref, b_ref, add_ref, o_ref, acc = refs
        ks = pl.program_id(2)

        @pl.when(ks == 0)
        def _():
            acc[...] = jnp.zeros_like(acc)

        acc[...] += lax.dot_general(a_ref[...].astype(BF16), b_ref[...].astype(BF16), dn,
                                    preferred_element_type=F32)

        @pl.when(ks == nk - 1)
        def _():
            res = acc[...]
            if add is not None:
                res = res + add_ref[...].astype(F32)
            o_ref[...] = res.astype(out_dtype)

    a_spec = pl.BlockSpec((tk, tm), lambda i, j, k: (k, i)) if ta else pl.BlockSpec((tm, tk), lambda i, j, k: (i, k))
    b_spec = pl.BlockSpec((tn, tk), lambda i, j, k: (j, k)) if tb else pl.BlockSpec((tk, tn), lambda i, j, k: (k, j))
    o_spec = pl.BlockSpec((tm, tn), lambda i, j, k: (i, j))
    ins, specs = [a, b], [a_spec, b_spec]
    if add is not None:
        ins.append(add)
        specs.append(o_spec)
    return pl.pallas_call(
        body, name=name, grid=(M // tm, N // tn, nk), in_specs=specs, out_specs=o_spec,
        out_shape=jax.ShapeDtypeStruct((M, N), out_dtype),
        scratch_shapes=[pltpu.VMEM((tm, tn), F32)],
        compiler_params=_cparams("parallel", "parallel", "arbitrary"),
    )(*ins)


def _rowwise(fn, rows, consts, outs, accs=(), *, tile, name, prev=(), nxt=()):
    rows = [r if isinstance(r, tuple) else (r, r.shape[1], 0) for r in rows]
    T = rows[0][0].shape[0]
    tile = min(tile, T)
    n = T // tile
    sub = 8
    nr, npv, nnx, ncst, no, na = len(rows), len(prev), len(nxt), len(consts), len(outs), len(accs)

    def body(*refs):
        i = pl.program_id(0)
        it = iter(refs)
        rv = [next(it)[...] for _ in range(nr)]
        pv = [jnp.where(i > 0, next(it)[sub - 1:sub, :], 0.0) for _ in range(npv)]
        nv = [jnp.where(i < n - 1, next(it)[0:1, :], 0.0) for _ in range(nnx)]
        cv = [next(it)[...] for _ in range(ncst)]
        o_refs = [next(it) for _ in range(no)]
        a_refs = [next(it) for _ in range(na)]
        res = fn(*rv, *pv, *nv, *cv)
        if not isinstance(res, (tuple, list)):
            res = (res,)
        for r, o in zip(o_refs, res[:no]):
            r[...] = o.astype(r.dtype)
        if na:
            @pl.when(i == 0)
            def _():
                for r in a_refs:
                    r[...] = jnp.zeros_like(r)
            for r, o in zip(a_refs, res[no:]):
                r[...] += o.astype(F32)

    in_specs = [pl.BlockSpec((tile, w), functools.partial(lambda cb, i: (i, cb), cb)) for _, w, cb in rows]
    in_specs += [pl.BlockSpec((sub, a.shape[1]), lambda i: (jnp.maximum(i * (tile // sub) - 1, 0), 0)) for a in prev]
    in_specs += [pl.BlockSpec((sub, a.shape[1]), lambda i: (jnp.minimum((i + 1) * (tile // sub), T // sub - 1), 0)) for a in nxt]
    in_specs += [pl.BlockSpec(c.shape, lambda i: (0, 0)) for c in consts]
    out_specs = [pl.BlockSpec((tile, c), lambda i: (i, 0)) for c, _ in outs]
    out_specs += [pl.BlockSpec(s, lambda i: (0, 0)) for s in accs]
    out_shape = [jax.ShapeDtypeStruct((T, c), d) for c, d in outs] + [jax.ShapeDtypeStruct(s, F32) for s in accs]
    res = pl.pallas_call(
        body, name=name, grid=(n,), in_specs=in_specs, out_specs=out_specs, out_shape=out_shape,
        compiler_params=_cparams("arbitrary"),
    )(*[r[0] for r in rows], *prev, *nxt, *consts)
    return res


@jax.custom_vjp
def _bdot(a, b):
    return jnp.dot(a.astype(BF16), b.astype(BF16), preferred_element_type=F32)


def _bdot_fwd(a, b):
    return _bdot(a, b), (a.astype(BF16), b.astype(BF16))


def _bdot_bwd(res, ct):
    a, b = res
    c = ct.astype(BF16)
    return (lax.dot_general(c, b, (((1,), (1,)), ((), ())), preferred_element_type=F32),
            lax.dot_general(a, c, (((0,), (0,)), ((), ())), preferred_element_type=F32))


_bdot.defvjp(_bdot_fwd, _bdot_bwd)


def _xdot(a, b):
    return jnp.dot(a, b, precision=lax.Precision.HIGHEST, preferred_element_type=F32)


def _rms(x, g, eps=RMS_EPS):
    return x * lax.rsqrt(jnp.mean(x * x, axis=-1, keepdims=True) + eps) * g


def _softplus(x):
    return jnp.maximum(x, 0.0) + jnp.log(1.0 + jnp.exp(-jnp.abs(x)))


def _sigmoid(x):
    return 1.0 / (1.0 + jnp.exp(-x))


def _seg_mats(width):
    h = lax.broadcasted_iota(jnp.int32, (width, LANE), 0) // HEAD
    j = lax.broadcasted_iota(jnp.int32, (width, LANE), 1)
    seg = (h == j).astype(F32)
    return seg, seg.T


def _prep_rwkv(um, w0, w2, a0, a2, g2, k_k, k_a, seg, segt, *, dims):
    W, DLp, ALp, GLp = dims
    r, k, v = um[:, :W], um[:, W:2 * W], um[:, 2 * W:3 * W]
    o = 3 * W
    xw, xa, xg = um[:, o:o + DLp], um[:, o + DLp:o + DLp + ALp], um[:, o + DLp + ALp:o + DLp + ALp + GLp]
    w_log = -_softplus(-(w0 + _bdot(jnp.tanh(xw), w2))) - 0.5
    decay = jnp.exp(-jnp.exp(w_log))
    a = _sigmoid(a0 + _bdot(xa, a2))
    g = _bdot(_sigmoid(xg), g2)
    kk = k * k_k
    nrm = jnp.sqrt(_xdot(_xdot(kk * kk, seg), segt))
    kk = kk / jnp.maximum(nrm, 1e-12)
    k2 = k * (1.0 + (a - 1.0) * k_a)
    return r, decay, k2, v, kk, kk * a, g


def _shift_mix(u, uprev, mu):
    first = lax.broadcasted_iota(jnp.int32, u.shape, 0) == 0
    sh = jnp.where(first, uprev, pltpu.roll(u, 1, 0))
    return u + (sh - u) * mu, sh


def _post_rwkv(y, r, k2, v, g, lnx_g, lnx_b, r_k, seg, segt):
    inv = 1.0 / HEAD
    mean = _xdot(_xdot(y, seg), segt) * inv
    yc = y - mean
    var = _xdot(_xdot(yc * yc, seg), segt) * inv
    yn = yc * lax.rsqrt(var + GN_EPS) * lnx_g + lnx_b
    bonus = _xdot(_xdot(r * k2 * r_k, seg), segt) * v
    return (yn + bonus) * g


def _prep_fox(uf, qg, kg, fb, seg, segt, *, dims):
    FW, FHp = dims
    q, k, v, f = uf[:, :FW], uf[:, FW:2 * FW], uf[:, 2 * FW:3 * FW], uf[:, 3 * FW:3 * FW + FHp]
    inv = 1.0 / HEAD
    qn = q * lax.rsqrt(_xdot(_xdot(q * q, seg), segt) * inv + RMS_EPS) * qg * (HEAD ** -0.5)
    kn = k * lax.rsqrt(_xdot(_xdot(k * k, seg), segt) * inv + RMS_EPS) * kg
    return qn, kn, v, -_softplus(-(f + fb))


def _tail(h2, pe, z, png, pgb):
    return h2 + _sigmoid(z + pgb) * _rms(pe, png)


def _swiglu(gt, up):
    return gt * _sigmoid(gt) * up


def _cumsum(x, *, reverse, name):
    T, C = x.shape
    tc = _pick(T, (256, 128))
    n = T // tc
    i0 = lax.broadcasted_iota(jnp.int32, (tc, tc), 0)
    i1 = lax.broadcasted_iota(jnp.int32, (tc, tc), 1)
    tri = ((i0 <= i1) if reverse else (i0 >= i1)).astype(BF16)

    def body(x_ref, tri_ref, o_ref, carry):
        i = pl.program_id(0)

        @pl.when(i == 0)
        def _():
            carry[...] = jnp.zeros_like(carry)

        v = x_ref[...]
        hi = v.astype(BF16)
        r1 = v - hi.astype(F32)
        mid = r1.astype(BF16)
        lo = (r1 - mid.astype(F32)).astype(BF16)
        t = tri_ref[...]
        d = lambda p: jnp.dot(t, p, preferred_element_type=F32)
        c = d(hi) + d(mid) + d(lo) + carry[0:1, :]
        o_ref[...] = c
        edge = c[0:1, :] if reverse else c[tc - 1:tc, :]
        carry[...] = jnp.broadcast_to(edge, carry.shape)

    blk = pl.BlockSpec((tc, C), (lambda i: (n - 1 - i, 0)) if reverse else (lambda i: (i, 0)))
    return pl.pallas_call(
        body, name=name, grid=(n,), in_specs=[blk, pl.BlockSpec((tc, tc), lambda i: (0, 0))], out_specs=blk,
        out_shape=jax.ShapeDtypeStruct((T, C), F32), scratch_shapes=[pltpu.VMEM((8, C), F32)],
        compiler_params=_cparams("arbitrary"),
    )(x, tri)


def _col(tile, lane, t):
    return jnp.sum(jnp.where(lane == t, tile, 0.0), axis=1, keepdims=True)


def _scan_fwd(r, w, k, kk, kka, vT, *, hg, tc):
    H, T, N = r.shape
    nc = T // tc

    def body(r_ref, w_ref, k_ref, kk_ref, kka_ref, vT_ref, yT_ref, chk_ref, s_ref):
        @pl.when(pl.program_id(1) == 0)
        def _():
            s_ref[...] = jnp.zeros_like(s_ref)

        chk_ref[:, 0] = s_ref[...]
        yT_ref[...] = jnp.zeros_like(yT_ref)
        lane = lax.broadcasted_iota(jnp.int32, (N, tc), 1)

        def step(t, S):
            out = []
            for h in range(hg):
                row = lambda ref: ref[h, pl.ds(t, 1), :]
                vcol = _col(vT_ref[h], lane, t)
                sa = -jnp.sum(S[h] * row(kk_ref), axis=1, keepdims=True)
                Sn = S[h] * row(w_ref) + sa * row(kka_ref) + vcol * row(k_ref)
                y = jnp.sum(Sn * row(r_ref), axis=1, keepdims=True)
                yT_ref[h] = jnp.where(lane == t, y, yT_ref[h])
                out.append(Sn)
            return tuple(out)

        S = lax.fori_loop(0, tc, step, tuple(s_ref[h] for h in range(hg)))
        for h in range(hg):
            s_ref[h] = S[h]

    rows = pl.BlockSpec((hg, tc, N), lambda g, c: (g, c, 0))
    cols = pl.BlockSpec((hg, N, tc), lambda g, c: (g, 0, c))
    return pl.pallas_call(
        body, name="scan_fwd", grid=(H // hg, nc), in_specs=[rows] * 5 + [cols],
        out_specs=[cols, pl.BlockSpec((hg, 1, N, N), lambda g, c: (g, c, 0, 0))],
        out_shape=[jax.ShapeDtypeStruct((H, N, T), F32), jax.ShapeDtypeStruct((H, nc, N, N), F32)],
        scratch_shapes=[pltpu.VMEM((hg, N, N), F32)],
        compiler_params=_cparams("arbitrary", "arbitrary"),
    )(r, w, k, kk, kka, vT)


def _scan_bwd(r, w, k, kk, kka, vT, dyT, chk, *, hg, tc):
    H, T, N = r.shape
    nc = T // tc

    def body(r_ref, w_ref, k_ref, kk_ref, kka_ref, vT_ref, dyT_ref, chk_ref,
             dr_ref, dw_ref, dk_ref, dkk_ref, dkka_ref, dvT_ref, sp_ref, ds_ref):
        @pl.when(pl.program_id(1) == 0)
        def _():
            ds_ref[...] = jnp.zeros_like(ds_ref)

        dvT_ref[...] = jnp.zeros_like(dvT_ref)
        lane = lax.broadcasted_iota(jnp.int32, (N, tc), 1)

        def fstep(t, S):
            out = []
            for h in range(hg):
                row = lambda ref: ref[h, pl.ds(t, 1), :]
                sp_ref[h, t] = S[h]
                vcol = _col(vT_ref[h], lane, t)
                sa = -jnp.sum(S[h] * row(kk_ref), axis=1, keepdims=True)
                out.append(S[h] * row(w_ref) + sa * row(kka_ref) + vcol * row(k_ref))
            return tuple(out)

        S = lax.fori_loop(0, tc, fstep, tuple(chk_ref[h, 0] for h in range(hg)))
        for h in range(hg):
            sp_ref[h, tc] = S[h]

        def bstep(i, carry):
            t = tc - 1 - i
            for h in range(hg):
                row = lambda ref: ref[h, pl.ds(t, 1), :]
                rr, wr, kr, kkr, kkar = row(r_ref), row(w_ref), row(k_ref), row(kk_ref), row(kka_ref)
                Sp = sp_ref[h, t]
                Sn = sp_ref[h, t + 1]
                dycol = _col(dyT_ref[h], lane, t)
                vcol = _col(vT_ref[h], lane, t)
                dSn = ds_ref[h] + dycol * rr
                dr_ref[h, pl.ds(t, 1), :] = jnp.sum(Sn * dycol, axis=0, keepdims=True)
                sa = -jnp.sum(Sp * kkr, axis=1, keepdims=True)
                dw_ref[h, pl.ds(t, 1), :] = jnp.sum(dSn * Sp, axis=0, keepdims=True)
                dsa = jnp.sum(dSn * kkar, axis=1, keepdims=True)
                dkka_ref[h, pl.ds(t, 1), :] = jnp.sum(dSn * sa, axis=0, keepdims=True)
                dvcol = jnp.sum(dSn * kr, axis=1, keepdims=True)
                dk_ref[h, pl.ds(t, 1), :] = jnp.sum(dSn * vcol, axis=0, keepdims=True)
                dkk_ref[h, pl.ds(t, 1), :] = -jnp.sum(Sp * dsa, axis=0, keepdims=True)
                ds_ref[h] = dSn * wr - dsa * kkr
                dvT_ref[h] = jnp.where(lane == t, dvcol, dvT_ref[h])
            return carry

        lax.fori_loop(0, tc, bstep, 0)

    rows = pl.BlockSpec((hg, tc, N), lambda g, c: (g, nc - 1 - c, 0))
    cols = pl.BlockSpec((hg, N, tc), lambda g, c: (g, 0, nc - 1 - c))
    return pl.pallas_call(
        body, name="scan_bwd", grid=(H // hg, nc),
        in_specs=[rows] * 5 + [cols, cols, pl.BlockSpec((hg, 1, N, N), lambda g, c: (g, nc - 1 - c, 0, 0))],
        out_specs=[rows] * 5 + [cols],
        out_shape=[jax.ShapeDtypeStruct((H, T, N), F32)] * 5 + [jax.ShapeDtypeStruct((H, N, T), F32)],
        scratch_shapes=[pltpu.VMEM((hg, tc + 1, N, N), F32), pltpu.VMEM((hg, N, N), F32)],
        compiler_params=_cparams("arbitrary", "arbitrary"),
    )(r, w, k, kk, kka, vT, dyT, chk)


_NT = (((1,), (1,)), ((), ()))
_TN = (((0,), (0,)), ((), ()))


def _scores(q, k, cc, cr, qi, ki, tb):
    s = lax.dot_general(q, k, _NT, preferred_element_type=F32) + cc - cr
    row = lax.broadcasted_iota(jnp.int32, (tb, tb), 0) + qi * tb
    col = lax.broadcasted_iota(jnp.int32, (tb, tb), 1) + ki * tb
    return jnp.where(row >= col, s, NEG)


def _attn_fwd(q, k, v, ccol, crow, *, tb):
    H, T, N = q.shape
    nb = T // tb

    def body(q_ref, k_ref, v_ref, cc_ref, cr_ref, o_ref, lse_ref, m_s, l_s, acc_s):
        qi, ki = pl.program_id(1), pl.program_id(2)

        @pl.when(ki == 0)
        def _():
            m_s[...] = jnp.full_like(m_s, NEG)
            l_s[...] = jnp.zeros_like(l_s)
            acc_s[...] = jnp.zeros_like(acc_s)

        @pl.when(ki <= qi)
        def _():
            s = _scores(q_ref[0], k_ref[0], cc_ref[0], cr_ref[0], qi, ki, tb)
            m_new = jnp.maximum(m_s[...], jnp.max(s, axis=1, keepdims=True))
            p = jnp.exp(s - m_new)
            alpha = jnp.exp(m_s[...] - m_new)
            l_s[...] = alpha * l_s[...] + jnp.sum(p, axis=1, keepdims=True)
            acc_s[...] = alpha * acc_s[...] + jnp.dot(p.astype(BF16), v_ref[0], preferred_element_type=F32)
            m_s[...] = m_new

        @pl.when(ki == qi)
        def _():
            o_ref[0] = acc_s[...] / l_s[...]
            lse_ref[0] = m_s[...] + jnp.log(l_s[...])

    qs = pl.BlockSpec((1, tb, N), lambda h, i, j: (h, i, 0))
    ks = pl.BlockSpec((1, tb, N), lambda h, i, j: (h, jnp.minimum(i, j), 0))
    col = pl.BlockSpec((1, tb, 1), lambda h, i, j: (h, i, 0))
    rowk = pl.BlockSpec((1, 1, tb), lambda h, i, j: (h, 0, jnp.minimum(i, j)))
    return pl.pallas_call(
        body, name="fox_fwd", grid=(H, nb, nb), in_specs=[qs, ks, ks, col, rowk], out_specs=[qs, col],
        out_shape=[jax.ShapeDtypeStruct((H, T, N), F32), jax.ShapeDtypeStruct((H, T, 1), F32)],
        scratch_shapes=[pltpu.VMEM((tb, 1), F32), pltpu.VMEM((tb, 1), F32), pltpu.VMEM((tb, N), F32)],
        compiler_params=_cparams("parallel", "parallel", "arbitrary"),
    )(q, k, v, ccol, crow)


def _attn_bwd_q(q, k, v, ccol, crow, o, lse, do, *, tb):
    H, T, N = q.shape
    nb = T // tb

    def body(q_ref, k_ref, v_ref, cc_ref, cr_ref, o_ref, lse_ref, do_ref, dq_ref, dc_ref, dq_s, dc_s):
        qi, ki = pl.program_id(1), pl.program_id(2)

        @pl.when(ki == 0)
        def _():
            dq_s[...] = jnp.zeros_like(dq_s)
            dc_s[...] = jnp.zeros_like(dc_s)

        @pl.when(ki <= qi)
        def _():
            s = _scores(q_ref[0], k_ref[0], cc_ref[0], cr_ref[0], qi, ki, tb)
            p = jnp.exp(s - lse_ref[0])
            do_ = do_ref[0]
            dp = lax.dot_general(do_.astype(BF16), v_ref[0], _NT, preferred_element_type=F32)
            delta = jnp.sum(do_ * o_ref[0], axis=1, keepdims=True)
            ds = p * (dp - delta)
            dq_s[...] += jnp.dot(ds.astype(BF16), k_ref[0], preferred_element_type=F32)
            dc_s[...] += jnp.sum(ds, axis=1, keepdims=True)

        @pl.when(ki == qi)
        def _():
            dq_ref[0] = dq_s[...]
            dc_ref[0] = dc_s[...]

    qs = pl.BlockSpec((1, tb, N), lambda h, i, j: (h, i, 0))
    ks = pl.BlockSpec((1, tb, N), lambda h, i, j: (h, jnp.minimum(i, j), 0))
    col = pl.BlockSpec((1, tb, 1), lambda h, i, j: (h, i, 0))
    rowk = pl.BlockSpec((1, 1, tb), lambda h, i, j: (h, 0, jnp.minimum(i, j)))
    return pl.pallas_call(
        body, name="fox_bwd_q", grid=(H, nb, nb), in_specs=[qs, ks, ks, col, rowk, qs, col, qs], out_specs=[qs, col],
        out_shape=[jax.ShapeDtypeStruct((H, T, N), F32), jax.ShapeDtypeStruct((H, T, 1), F32)],
        scratch_shapes=[pltpu.VMEM((tb, N), F32), pltpu.VMEM((tb, 1), F32)],
        compiler_params=_cparams("parallel", "parallel", "arbitrary"),
    )(q, k, v, ccol, crow, o, lse, do)


def _attn_bwd_kv(q, k, v, ccol, crow, o, lse, do, *, tb):
    H, T, N = q.shape
    nb = T // tb

    def body(q_ref, k_ref, v_ref, cc_ref, cr_ref, o_ref, lse_ref, do_ref, dk_ref, dv_ref, dc_ref, dk_s, dv_s, dc_s):
        ki, qi = pl.program_id(1), pl.program_id(2)

        @pl.when(qi == 0)
        def _():
            dk_s[...] = jnp.zeros_like(dk_s)
            dv_s[...] = jnp.zeros_like(dv_s)
            dc_s[...] = jnp.zeros_like(dc_s)

        @pl.when(qi >= ki)
        def _():
            s = _scores(q_ref[0], k_ref[0], cc_ref[0], cr_ref[0], qi, ki, tb)
            p = jnp.exp(s - lse_ref[0])
            do_ = do_ref[0]
            dp = lax.dot_general(do_.astype(BF16), v_ref[0], _NT, preferred_element_type=F32)
            delta = jnp.sum(do_ * o_ref[0], axis=1, keepdims=True)
            ds = p * (dp - delta)
            dv_s[...] += lax.dot_general(p.astype(BF16), do_.astype(BF16), _TN, preferred_element_type=F32)
            dk_s[...] += lax.dot_general(ds.astype(BF16), q_ref[0], _TN, preferred_element_type=F32)
            dc_s[...] += jnp.sum(ds, axis=0, keepdims=True)

        @pl.when(qi == nb - 1)
        def _():
            dk_ref[0] = dk_s[...]
            dv_ref[0] = dv_s[...]
            dc_ref[0] = dc_s[...]

    qs = pl.BlockSpec((1, tb, N), lambda h, j, i: (h, jnp.maximum(i, j), 0))
    ks = pl.BlockSpec((1, tb, N), lambda h, j, i: (h, j, 0))
    col = pl.BlockSpec((1, tb, 1), lambda h, j, i: (h, jnp.maximum(i, j), 0))
    rowk = pl.BlockSpec((1, 1, tb), lambda h, j, i: (h, 0, j))
    return pl.pallas_call(
        body, name="fox_bwd_kv", grid=(H, nb, nb), in_specs=[qs, ks, ks, col, rowk, qs, col, qs],
        out_specs=[ks, ks, rowk],
        out_shape=[jax.ShapeDtypeStruct((H, T, N), F32)] * 2 + [jax.ShapeDtypeStruct((H, 1, T), F32)],
        scratch_shapes=[pltpu.VMEM((tb, N), F32), pltpu.VMEM((tb, N), F32), pltpu.VMEM((1, tb), F32)],
        compiler_params=_cparams("parallel", "parallel", "arbitrary"),
    )(q, k, v, ccol, crow, o, lse, do)


def _heads(x):
    T = x.shape[0]
    return x.reshape(T, -1, HEAD).transpose(1, 0, 2)


def _headsT(x):
    T = x.shape[0]
    return x.reshape(T, -1, HEAD).transpose(1, 2, 0)


def _unheads(x):
    return x.transpose(1, 0, 2).reshape(x.shape[1], -1)


def _unheadsT(x):
    return x.transpose(2, 0, 1).reshape(x.shape[2], -1)


def _padc(x, n):
    return jnp.pad(x, ((0, 0), (0, n - x.shape[1])))


def _padr(x, n):
    return jnp.pad(x, ((0, n - x.shape[0]), (0, 0)))


class _Dims:
    def __init__(self, W, DL, AL, GL, FH):
        self.W, self.DL, self.AL, self.GL, self.FH = W, DL, AL, GL, FH
        self.DLp, self.ALp, self.GLp, self.FHp = _rup(DL, LANE), _rup(AL, LANE), _rup(GL, LANE), _rup(FH, LANE)
        self.FW = FH * HEAD
        self.RC = 3 * W + DL + AL + GL
        self.RP = 3 * W + self.DLp + self.ALp + self.GLp
        self.FC = 3 * self.FW + FH
        self.FP = 3 * self.FW + self.FHp

    def pad_r(self, a):
        W, o = self.W, 3 * self.W
        return jnp.concatenate([a[:, :o], _padc(a[:, o:o + self.DL], self.DLp),
                                _padc(a[:, o + self.DL:o + self.DL + self.AL], self.ALp),
                                _padc(a[:, o + self.DL + self.AL:self.RC], self.GLp)], axis=1)

    def unpad_r(self, a):
        o = 3 * self.W
        return jnp.concatenate([a[:, :o], a[:, o:o + self.DL], a[:, o + self.DLp:o + self.DLp + self.AL],
                                a[:, o + self.DLp + self.ALp:o + self.DLp + self.ALp + self.GL]], axis=1)

    def pad_f(self, a):
        return _padc(a, self.FP)

    def unpad_f(self, a):
        return a[:, :self.FC]


def _local_step(x, p, tgt, Wt, vec, d):
    T, D = x.shape
    W, FW = d.W, d.FW
    H = W // HEAD
    seg, segt = _seg_mats(W)
    segf, segft = _seg_mats(FW)
    T1 = 256
    rk_flat = vec["r_k"].reshape(1, W)
    mu = d.pad_r(vec["shift_mu"])
    qg = jnp.tile(vec["q_norm_g"], (1, d.FH))
    kg = jnp.tile(vec["k_norm_g"], (1, d.FH))
    fb = _padc(vec["fgate_b"], d.FHp)
    pdims = (W, d.DLp, d.ALp, d.GLp)
    fdims = (FW, d.FHp)

    (xn,) = _rowwise(lambda x_, g_: _rms(x_, g_), [x], [vec["attn_norm_g"]], [(D, BF16)], tile=T1, name="norm_attn")
    u_r = _mm(xn, Wt["w_in_r"], name="mm_in_r")
    u_f = _mm(xn, Wt["w_in_f"], name="mm_in_f")

    prep_consts = [mu, vec["w0"], Wt["w2"].astype(F32), vec["a0"], Wt["a2"].astype(F32), Wt["g2"].astype(F32), vec["k_k"], vec["k_a"], seg, segt]

    def prep_fwd(u_, up_, mu_, *cs):
        um, _ = _shift_mix(u_, up_, mu_)
        return _prep_rwkv(um, *cs, dims=pdims)

    r, dec, k2, v, kk, kka, g = _rowwise(prep_fwd, [u_r], prep_consts, [(W, F32)] * 7, tile=128, name="rwkv_prep", prev=[u_r])
    hg, tc = min(4, H), 128
    yT, chk = _scan_fwd(_heads(r), _heads(dec), _heads(k2), _heads(kk), _heads(kka), _headsT(v), hg=hg, tc=tc)
    y = _unheadsT(yT)
    post_consts = [vec["lnx_g"], vec["lnx_b"], rk_flat, seg, segt]
    (y_r,) = _rowwise(_post_rwkv, [y, r, k2, v, g], post_consts, [(W, BF16)], tile=T1, name="rwkv_post")

    fox_consts = [qg, kg, fb, segf, segft]
    qn, kn, vf, logf = _rowwise(functools.partial(_prep_fox, dims=fdims), [u_f], fox_consts,
                                [(FW, BF16), (FW, BF16), (FW, BF16), (d.FHp, F32)], tile=T1, name="fox_prep")
    c = _cumsum(logf, reverse=False, name="fox_cumsum")
    cT = c[:, :d.FH].T
    ccol, crow = cT[:, :, None], cT[:, None, :]
    tb = _pick(T, (256, 128))
    qh, kh, vh = _heads(qn), _heads(kn), _heads(vf)
    o, lse = _attn_fwd(qh, kh, vh, ccol, crow, tb=tb)
    y_f = _unheads(o)

    ycat = jnp.concatenate([y_r, y_f.astype(BF16)], axis=1)
    h1 = _mm(ycat, Wt["w_out"], add=x, name="mm_out")
    (hn,) = _rowwise(lambda h_, g_: _rms(h_, g_), [h1], [vec["ffn_norm_g"]], [(D, BF16)], tile=T1, name="norm_ffn")
    gt = _mm(hn, Wt["w_gate"], name="mm_gate")
    up = _mm(hn, Wt["w_up"], name="mm_up")
    (act,) = _rowwise(_swiglu, [gt, up], [], [(gt.shape[1], BF16)], tile=T1, name="swiglu")
    h2 = _mm(act, Wt["w_down"], add=h1, name="mm_down")
    (hg_,) = _rowwise(lambda h_, g_: _rms(h_, g_), [h2], [vec["ple_gate_norm_g"]], [(D, BF16)], tile=T1, name="norm_gate")
    pe = _mm(p, Wt["ple_proj"], name="mm_ple")
    z = _mm(hg_, Wt["ple_gate_w"], name="mm_pgate")

    def tail(h2_, pe_, z_, tg_, png_, pgb_):
        h3, f = jax.vjp(_tail, h2_, pe_, z_, png_, pgb_)
        err = h3 - tg_
        dh3 = err * (1.0 / D)
        lt = 0.5 * jnp.sum(jnp.sum(err * err, axis=1, keepdims=True) * (1.0 / D), axis=0, keepdims=True)
        dh2_, dpe_, dz_, dpng_, dpgb_ = f(dh3)
        return dh2_, dpe_, dz_, jnp.broadcast_to(lt, (1, LANE)), dpng_, dpgb_

    dh3, dpe, dz, loss, g_png, g_pgb = _rowwise(
        tail, [h2, pe, z, tgt], [vec["ple_norm_g"], vec["ple_gate_b"]], [(D, F32), (D, BF16), (D, BF16)],
        [(1, LANE), (1, D), (1, D)], tile=T1, name="tail")
    G = {}
    gv = {"ple_norm_g": g_png, "ple_gate_b": g_pgb}
    G["ple_gate_w"] = _mm(hg_, dz, ta=True, out_dtype=BF16, name="gw_pgate")
    G["ple_proj"] = _mm(p, dpe, ta=True, out_dtype=BF16, name="gw_ple")
    d_hg = _mm(dz, Wt["ple_gate_w"], tb=True, name="mmb_pgate")

    def norm_bwd(h_, dres_, dn_, g_):
        _, f = jax.vjp(_rms, h_, g_)
        dh_, dg_ = f(dn_)
        return dres_ + dh_, dg_

    dh2, gv["ple_gate_norm_g"] = _rowwise(norm_bwd, [h2, dh3, d_hg], [vec["ple_gate_norm_g"]], [(D, F32)], [(1, D)],
                                          tile=T1, name="norm_gate_bwd")
    G["w_down"] = _mm(act, dh2, ta=True, out_dtype=BF16, name="gw_down")
    d_act = _mm(dh2, Wt["w_down"], tb=True, name="mmb_down")

    def swiglu_bwd(gt_, up_, da_):
        _, f = jax.vjp(_swiglu, gt_, up_)
        return f(da_)

    d_gt, d_up = _rowwise(swiglu_bwd, [gt, up, d_act], [], [(gt.shape[1], BF16)] * 2, tile=T1, name="swiglu_bwd")
    G["w_gate"] = _mm(hn, d_gt, ta=True, out_dtype=BF16, name="gw_gate")
    G["w_up"] = _mm(hn, d_up, ta=True, out_dtype=BF16, name="gw_up")
    d_hn = _mm(d_gt, Wt["w_gate"], tb=True, name="mmb_gate")
    d_hn = _mm(d_up, Wt["w_up"], tb=True, add=d_hn, name="mmb_up")
    dh1, gv["ffn_norm_g"] = _rowwise(norm_bwd, [h1, dh2, d_hn], [vec["ffn_norm_g"]], [(D, F32)], [(1, D)],
                                     tile=T1, name="norm_ffn_bwd")
    G["w_out"] = _mm(ycat, dh1, ta=True, out_dtype=BF16, name="gw_out")
    d_ycat = _mm(dh1, Wt["w_out"], tb=True, name="mmb_out")

    def post_bwd(y_, r_, k2_, v_, g_, dy_, *cs):
        lg, lb, rk, sg, sgt = cs
        _, f = jax.vjp(lambda *a: _post_rwkv(*a, sg, sgt), y_, r_, k2_, v_, g_, lg, lb, rk)
        return f(dy_)

    dy, dr1, dk1, dv1, dg, gv["lnx_g"], gv["lnx_b"], g_rk = _rowwise(
        post_bwd, [y, r, k2, v, g, (d_ycat, W, 0)], post_consts, [(W, F32)] * 5, [(1, W)] * 3, tile=128, name="rwkv_post_bwd")
    gv["r_k"] = g_rk.reshape(H, HEAD)
    dr, ddec, dk2, dkk, dkka, dvT = _scan_bwd(_heads(r), _heads(dec), _heads(k2), _heads(kk), _heads(kka), _headsT(v),
                                              _headsT(dy), chk, hg=hg, tc=tc)
    dr, ddec, dk2, dkk, dkka, dv = (_unheads(dr), _unheads(ddec), _unheads(dk2), _unheads(dkk), _unheads(dkka), _unheadsT(dvT))

    def prep_bwd(u_, dr_, dr1_, ddec_, dk2_, dk1_, dv_, dv1_, dkk_, dkka_, dg_, up_, mu_, *cs):
        um, sh = _shift_mix(u_, up_, mu_)
        cs_d, sg, sgt = cs[:7], cs[7], cs[8]
        _, f = jax.vjp(lambda um_, *c_: _prep_rwkv(um_, *c_, sg, sgt, dims=pdims), um, *cs_d)
        res = f((dr_ + dr1_, ddec_, dk2_ + dk1_, dv_ + dv1_, dkk_, dkka_, dg_))
        dum = res[0]
        dmu = jnp.sum(dum * (sh - u_), axis=0, keepdims=True)
        return (dum, dmu) + tuple(res[1:])

    LP = [Wt["w2"].shape, Wt["a2"].shape, Wt["g2"].shape]
    dum, g_mu, gv["w0"], g_w2, gv["a0"], g_a2, g_g2, gv["k_k"], gv["k_a"] = _rowwise(
        prep_bwd, [u_r, dr, dr1, ddec, dk2, dk1, dv, dv1, dkk, dkka, dg], prep_consts, [(d.RP, F32)],
        [(1, d.RP), (1, W), LP[0], (1, W), LP[1], LP[2], (1, W), (1, W)], tile=128, name="rwkv_prep_bwd", prev=[u_r])
    gv["shift_mu"] = d.unpad_r(g_mu)
    G["w2"], G["a2"], G["g2"] = g_w2, g_a2, g_g2
    (du_r,) = _rowwise(lambda a_, an_, mu_: a_ * (1.0 - mu_) + jnp.where(
        lax.broadcasted_iota(jnp.int32, a_.shape, 0) == a_.shape[0] - 1, an_, pltpu.roll(a_, a_.shape[0] - 1, 0)) * mu_,
        [dum], [mu], [(d.RP, BF16)], tile=T1, name="shift_bwd", nxt=[dum])

    do = _heads(d_ycat[:, W:])
    dq, dcq = _attn_bwd_q(qh, kh, vh, ccol, crow, o, lse, do, tb=tb)
    dk_, dv_, dck = _attn_bwd_kv(qh, kh, vh, ccol, crow, o, lse, do, tb=tb)
    dc = _padc((dcq[:, :, 0] - dck[:, 0, :]).T, d.FHp)
    dlogf = _cumsum(dc, reverse=True, name="fox_cumsum_bwd")

    def fox_bwd(uf_, dq_, dk__, dv__, dlf_, *cs):
        qg_, kg_, fb_, sg, sgt = cs
        _, f = jax.vjp(lambda uf__, a, b, c_: _prep_fox(uf__, a, b, c_, sg, sgt, dims=fdims), uf_, qg_, kg_, fb_)
        return f((dq_, dk__, dv__, dlf_))

    du_f, g_qg, g_kg, g_fb = _rowwise(fox_bwd, [u_f, _unheads(dq), _unheads(dk_), _unheads(dv_), dlogf], fox_consts,
                                      [(d.FP, BF16)], [(1, FW), (1, FW), (1, d.FHp)], tile=T1, name="fox_prep_bwd")
    gv["q_norm_g"] = g_qg.reshape(d.FH, HEAD).sum(0, keepdims=True)
    gv["k_norm_g"] = g_kg.reshape(d.FH, HEAD).sum(0, keepdims=True)
    gv["fgate_b"] = g_fb[:, :d.FH]

    G["w_in_r"] = _mm(xn, du_r, ta=True, out_dtype=BF16, name="gw_in_r")
    G["w_in_f"] = _mm(xn, du_f, ta=True, out_dtype=BF16, name="gw_in_f")
    d_xn = _mm(du_r, Wt["w_in_r"], tb=True, name="mmb_in_r")
    d_xn = _mm(du_f, Wt["w_in_f"], tb=True, add=d_xn, name="mmb_in_f")
    dx, gv["attn_norm_g"] = _rowwise(norm_bwd, [x, dh1, d_xn], [vec["attn_norm_g"]], [(D, F32)], [(1, D)],
                                     tile=T1, name="norm_attn_bwd")
    return loss, dx, G, gv


_HBM = pl.BlockSpec(memory_space=pl.ANY)
_OTHER_CHIPS = ((0, 1), (1, 0), (1, 1))


def _flip(v, bit):
    return 1 - v if bit else v


def _exchange(arrs, *, gather, name):
    n = len(arrs)

    def body(*refs):
        ins, outs = refs[:n], refs[n:2 * n]
        send_sems, recv_sems, own_sems = refs[2 * n:]
        x, y, c = lax.axis_index("x"), lax.axis_index("y"), lax.axis_index("c")
        me = 2 * x + y
        own, remote = [], []
        for a in range(n):
            cp = pltpu.make_async_copy(ins[a] if gather else ins[a].at[me], outs[a].at[me], own_sems.at[a])
            cp.start()
            own.append(cp)
        for a in range(n):
            for k, (dx, dy) in enumerate(_OTHER_CHIPS):
                px, py = _flip(x, dx), _flip(y, dy)
                cp = pltpu.make_async_remote_copy(
                    src_ref=ins[a] if gather else ins[a].at[2 * px + py], dst_ref=outs[a].at[me],
                    send_sem=send_sems.at[3 * a + k], recv_sem=recv_sems.at[3 * a + k],
                    device_id=(px, py, c), device_id_type=MESH)
                cp.start()
                remote.append(cp)
        for cp in remote:
            cp.wait()
        for cp in own:
            cp.wait()

    out_shape = [jax.ShapeDtypeStruct(((N_CHIPS,) + a.shape) if gather else a.shape, a.dtype) for a in arrs]
    return pl.pallas_call(
        body, name=name, in_specs=[_HBM] * n, out_specs=[_HBM] * n, out_shape=out_shape,
        scratch_shapes=[pltpu.SemaphoreType.DMA((3 * n,)), pltpu.SemaphoreType.DMA((3 * n,)), pltpu.SemaphoreType.DMA((n,))],
    )(*arrs)


def _swap_cores(arrs, *, name):
    n = len(arrs)

    def body(*refs):
        ins, outs = refs[:n], refs[n:2 * n]
        send_sems, recv_sems = refs[2 * n:]
        peer = (lax.axis_index("x"), lax.axis_index("y"), 1 - lax.axis_index("c"))
        cps = [pltpu.make_async_remote_copy(src_ref=ins[a], dst_ref=outs[a], send_sem=send_sems.at[a],
                                            recv_sem=recv_sems.at[a], device_id=peer, device_id_type=MESH) for a in range(n)]
        for cp in cps:
            cp.start()
        for cp in cps:
            cp.wait()

    return pl.pallas_call(
        body, name=name, in_specs=[_HBM] * n, out_specs=[_HBM] * n,
        out_shape=[jax.ShapeDtypeStruct(a.shape, a.dtype) for a in arrs],
        scratch_shapes=[pltpu.SemaphoreType.DMA((n,)), pltpu.SemaphoreType.DMA((n,))],
    )(*arrs)


def _allreduce_small(pack, *, name):
    R, C = pack.shape

    def body(p_ref, o_ref, recv, send_sems, recv_sems):
        x, y, c = lax.axis_index("x"), lax.axis_index("y"), lax.axis_index("c")
        me = 4 * x + 2 * y + c
        recv[me] = p_ref[...]
        cps = []
        for k in range(1, N_DEV):
            peer = (_flip(x, k & 4), _flip(y, k & 2), _flip(c, k & 1))
            cp = pltpu.make_async_remote_copy(src_ref=p_ref, dst_ref=recv.at[me], send_sem=send_sems.at[k - 1],
                                              recv_sem=recv_sems.at[k - 1], device_id=peer, device_id_type=MESH)
            cp.start()
            cps.append(cp)
        for cp in cps:
            cp.wait()
        acc = recv[0]
        for s in range(1, N_DEV):
            acc = acc + recv[s]
        o_ref[...] = acc

    vm = pl.BlockSpec(memory_space=pltpu.VMEM)
    return pl.pallas_call(
        body, name=name, in_specs=[vm], out_specs=vm, out_shape=jax.ShapeDtypeStruct((R, C), F32),
        scratch_shapes=[pltpu.VMEM((N_DEV, R, C), F32), pltpu.SemaphoreType.DMA((N_DEV - 1,)), pltpu.SemaphoreType.DMA((N_DEV - 1,))],
    )(pack)


def _sum_slots(a, *, name):
    S, R, C = a.shape
    tr = _pick(R, (256, 128, 64, 32, 16, 8))

    def body(a_ref, o_ref):
        acc = a_ref[0].astype(F32)
        for s in range(1, S):
            acc = acc + a_ref[s].astype(F32)
        o_ref[...] = acc

    return pl.pallas_call(
        body, name=name, grid=(R // tr,), in_specs=[pl.BlockSpec((S, tr, C), lambda i: (0, i, 0))],
        out_specs=pl.BlockSpec((tr, C), lambda i: (i, 0)), out_shape=jax.ShapeDtypeStruct((R, C), F32),
        compiler_params=_cparams("parallel"),
    )(a)


def _adamw(w, m, v, gs, *, name):
    R, C = w.shape
    tile = _pick(R, (128, 96, 64, 32, 16, 8))

    def fn(w_, m_, v_, *g_):
        g = g_[0]
        for e in g_[1:]:
            g = g + e
        m2 = ADAM_B1 * m_ + (1.0 - ADAM_B1) * g
        v2 = ADAM_B2 * v_ + (1.0 - ADAM_B2) * jnp.square(g)
        m_hat = m2 / (1.0 - ADAM_B1 ** ADAM_STEP)
        v_hat = v2 / (1.0 - ADAM_B2 ** ADAM_STEP)
        delta = -ADAM_LR * (m_hat / (jnp.sqrt(v_hat) + ADAM_EPS) + ADAM_WD * w_)
        return g, delta, m2, v2

    return _rowwise(fn, [w, m, v, *gs], [], [(C, F32)] * 4, tile=tile, name=name)


_ARGS = "x, p, attn_norm_g, w_in, shift_mu, w0, w2, a0, a2, g2, k_k, k_a, r_k, lnx_g, lnx_b, q_norm_g, k_norm_g, fgate_b, w_out, ffn_norm_g, w_gate, w_up, w_down, ple_proj, ple_norm_g, ple_gate_norm_g, ple_gate_w, ple_gate_b, loss_target, m_attn_norm_g, m_w_in, m_shift_mu, m_w0, m_w2, m_a0, m_a2, m_g2, m_k_k, m_k_a, m_r_k, m_lnx_g, m_lnx_b, m_q_norm_g, m_k_norm_g, m_fgate_b, m_w_out, m_ffn_norm_g, m_w_gate, m_w_up, m_w_down, m_ple_proj, m_ple_norm_g, m_ple_gate_norm_g, m_ple_gate_w, m_ple_gate_b, v_attn_norm_g, v_w_in, v_shift_mu, v_w0, v_w2, v_a0, v_a2, v_g2, v_k_k, v_k_a, v_r_k, v_lnx_g, v_lnx_b, v_q_norm_g, v_k_norm_g, v_fgate_b, v_w_out, v_ffn_norm_g, v_w_gate, v_w_up, v_w_down, v_ple_proj, v_ple_norm_g, v_ple_gate_norm_g, v_ple_gate_w, v_ple_gate_b".split(", ")
_WEIGHTS = _ARGS[2:28]
_COL_SHARDED = ("w_in", "w2", "a2", "g2", "w_gate", "w_up", "ple_proj")
_ROW_SHARDED = ("w_out", "w_down", "ple_gate_w")
_MATRICES = _COL_SHARDED + _ROW_SHARDED
_VECTORS = tuple(n for n in _WEIGHTS if n not in _MATRICES)


def _whole(name, g):
    if name in _COL_SHARDED:
        return g.transpose(1, 0, 2).reshape(g.shape[1], -1)
    return g.reshape(-1, g.shape[2])


def _pieces(name, a):
    if name in _COL_SHARDED:
        return a.reshape(a.shape[0], N_CHIPS, -1).transpose(1, 0, 2)
    return a.reshape(N_CHIPS, -1, a.shape[1])


def kernel(x, p, attn_norm_g, w_in, shift_mu, w0, w2, a0, a2, g2, k_k, k_a, r_k, lnx_g, lnx_b, q_norm_g, k_norm_g, fgate_b, w_out, ffn_norm_g, w_gate, w_up, w_down, ple_proj, ple_norm_g, ple_gate_norm_g, ple_gate_w, ple_gate_b, loss_target, m_attn_norm_g, m_w_in, m_shift_mu, m_w0, m_w2, m_a0, m_a2, m_g2, m_k_k, m_k_a, m_r_k, m_lnx_g, m_lnx_b, m_q_norm_g, m_k_norm_g, m_fgate_b, m_w_out, m_ffn_norm_g, m_w_gate, m_w_up, m_w_down, m_ple_proj, m_ple_norm_g, m_ple_gate_norm_g, m_ple_gate_w, m_ple_gate_b, v_attn_norm_g, v_w_in, v_shift_mu, v_w0, v_w2, v_a0, v_a2, v_g2, v_k_k, v_k_a, v_r_k, v_lnx_g, v_lnx_b, v_q_norm_g, v_k_norm_g, v_fgate_b, v_w_out, v_ffn_norm_g, v_w_gate, v_w_up, v_w_down, v_ple_proj, v_ple_norm_g, v_ple_gate_norm_g, v_ple_gate_w, v_ple_gate_b):
    A = dict(zip(_ARGS, (x, p, attn_norm_g, w_in, shift_mu, w0, w2, a0, a2, g2, k_k, k_a, r_k, lnx_g, lnx_b, q_norm_g, k_norm_g, fgate_b, w_out, ffn_norm_g, w_gate, w_up, w_down, ple_proj, ple_norm_g, ple_gate_norm_g, ple_gate_w, ple_gate_b, loss_target, m_attn_norm_g, m_w_in, m_shift_mu, m_w0, m_w2, m_a0, m_a2, m_g2, m_k_k, m_k_a, m_r_k, m_lnx_g, m_lnx_b, m_q_norm_g, m_k_norm_g, m_fgate_b, m_w_out, m_ffn_norm_g, m_w_gate, m_w_up, m_w_down, m_ple_proj, m_ple_norm_g, m_ple_gate_norm_g, m_ple_gate_w, m_ple_gate_b, v_attn_norm_g, v_w_in, v_shift_mu, v_w0, v_w2, v_a0, v_a2, v_g2, v_k_k, v_k_a, v_r_k, v_lnx_g, v_lnx_b, v_q_norm_g, v_k_norm_g, v_fgate_b, v_w_out, v_ffn_norm_g, v_w_gate, v_w_up, v_w_down, v_ple_proj, v_ple_norm_g, v_ple_gate_norm_g, v_ple_gate_w, v_ple_gate_b)))
    x, p, tgt = A["x"][0], A["p"][0, 0], A["loss_target"][0]
    d = _Dims(W=A["w0"].shape[-1], DL=A["w2"].shape[1], AL=A["a2"].shape[1], GL=A["g2"].shape[1], FH=A["fgate_b"].shape[-1])

    gathered = _exchange([A[n][0].astype(BF16) for n in _MATRICES], gather=True, name="gather_weights")
    full = {n: _whole(n, g) for n, g in zip(_MATRICES, gathered)}
    Wt = {n: full[n] for n in ("w_out", "w_gate", "w_up", "w_down", "ple_proj", "ple_gate_w")}
    Wt["w_in_r"] = d.pad_r(full["w_in"][:, :d.RC])
    Wt["w_in_f"] = d.pad_f(full["w_in"][:, d.RC:])
    Wt["w2"], Wt["a2"], Wt["g2"] = _padr(full["w2"], d.DLp), _padr(full["a2"], d.ALp), _padr(full["g2"], d.GLp)
    vec = {n: A[n].reshape(-1, A[n].shape[-1]) for n in _VECTORS}

    loss, dx, G, gv = _local_step(x, p, tgt, Wt, vec, d)

    gw = dict(G)
    gw["w_in"] = jnp.concatenate([d.unpad_r(G["w_in_r"]), d.unpad_f(G["w_in_f"])], axis=1)
    gw["w2"], gw["a2"], gw["g2"] = G["w2"][:d.DL], G["a2"][:d.AL], G["g2"][:d.GL]
    recv = _exchange([_pieces(n, gw[n]).astype(BF16) for n in _MATRICES], gather=False, name="scatter_grads")
    part = [_sum_slots(r, name="sum_" + n) for n, r in zip(_MATRICES, recv)]
    sib = _swap_cores(part, name="swap_cores")

    sizes = [1] + [A[n].size for n in _VECTORS]
    rows = _rup(_rup(sum(sizes), LANE) // LANE, 8)

    def pack(items):
        flat = jnp.concatenate([i.reshape(-1) for i in items])
        return jnp.pad(flat, (0, rows * LANE - flat.shape[0])).reshape(rows, LANE)

    red = _allreduce_small(pack([loss[0, :1]] + [gv[n] for n in _VECTORS]), name="allreduce_vectors")
    zero = jnp.zeros((1,), F32)
    upd = _adamw(pack([zero] + [A[n] for n in _VECTORS]), pack([zero] + [A["m_" + n] for n in _VECTORS]),
                 pack([zero + 1.0] + [A["v_" + n] for n in _VECTORS]), [red], name="adamw_vectors")
    offs = [0]
    for s in sizes:
        offs.append(offs[-1] + s)
    unpack = lambda a, i, n: a.reshape(-1)[offs[i + 1]:offs[i + 2]].reshape(A[n].shape)

    out = {"grad": {}, "delta": {}, "new_m": {}, "new_v": {}}
    for i, n in enumerate(_VECTORS):
        for kind, a in zip(out, upd):
            out[kind][n] = unpack(a, i, n)
    for n, mine, other in zip(_MATRICES, part, sib):
        res = _adamw(A[n][0], A["m_" + n][0], A["v_" + n][0], [mine, other], name="adamw_" + n)
        for kind, a in zip(out, res):
            out[kind][n] = a[None]
    return (red[0, 0], dx[None], *[out[k][n] for k in out for n in _WEIGHTS])
```

```python
import functools

import jax
import jax.numpy as jnp
from jax import lax
from jax.experimental import pallas as pl
from jax.experimental.pallas import tpu as pltpu

F32 = jnp.float32
BF16 = jnp.bfloat16
LANE = 128
HEAD = 64
RMS_EPS = 1e-6
GN_EPS = 64e-5
ADAM_LR, ADAM_B1, ADAM_B2, ADAM_EPS, ADAM_WD, ADAM_STEP = 0.001, 0.9, 0.999, 1e-08, 0.01, 10
VMEM_LIMIT = 56 * 1024 * 1024
NEG = -1e30
MESH = pl.DeviceIdType.MESH
N_CHIPS = 4
N_DEV = 8


def _rup(n, m):
    return -(-n // m) * m


def _pick(n, cands):
    for c in cands:
        if n % c == 0:
            return c
    return n


def _cparams(*sem):
    return pltpu.CompilerParams(dimension_semantics=sem, vmem_limit_bytes=VMEM_LIMIT)


def _mm(a, b, *, ta=False, tb=False, add=None, out_dtype=F32, name):
    M, K = (a.shape[1], a.shape[0]) if ta else a.shape
    N = b.shape[0] if tb else b.shape[1]
    tm = _pick(M, (1024, 512, 256, 128))
    tn = _pick(N, (1024, 512, 640, 256, 128))
    tk = _pick(K, (1024, 512, 640, 256, 128))
    nk = K // tk
    dn = (((0 if ta else 1,), (1 if tb else 0,)), ((), ()))

    def body(*refs):
        if add is None:
            a_ref, b_ref, o_ref, acc = refs
        else:
            a_ref, b_ref, add_ref, o_ref, acc = refs
        ks = pl.program_id(2)

        @pl.when(ks == 0)
        def _():
            acc[...] = jnp.zeros_like(acc)

        acc[...] += lax.dot_general(a_ref[...].astype(BF16), b_ref[...].astype(BF16), dn,
                                    preferred_element_type=F32)

        @pl.when(ks == nk - 1)
        def _():
            res = acc[...]
            if add is not None:
                res = res + add_ref[...].astype(F32)
            o_ref[...] = res.astype(out_dtype)

    a_spec = pl.BlockSpec((tk, tm), lambda i, j, k: (k, i)) if ta else pl.BlockSpec((tm, tk), lambda i, j, k: (i, k))
    b_spec = pl.BlockSpec((tn, tk), lambda i, j, k: (j, k)) if tb else pl.BlockSpec((tk, tn), lambda i, j, k: (k, j))
    o_spec = pl.BlockSpec((tm, tn), lambda i, j, k: (i, j))
    ins, specs = [a, b], [a_spec, b_spec]
    if add is not None:
        ins.append(add)
        specs.append(o_spec)
    return pl.pallas_call(
        body, name=name, grid=(M // tm, N // tn, nk), in_specs=specs, out_specs=o_spec,
        out_shape=jax.ShapeDtypeStruct((M, N), out_dtype),
        scratch_shapes=[pltpu.VMEM((tm, tn), F32)],
        compiler_params=_cparams("parallel", "parallel", "arbitrary"),
    )(*ins)


def _rowwise(fn, rows, consts, outs, accs=(), *, tile, name, prev=(), nxt=()):
    rows = [r if isinstance(r, tuple) else (r, r.shape[1], 0) for r in rows]
    T = rows[0][0].shape[0]
    tile = min(tile, T)
    n = T // tile
    sub = 8
    nr, npv, nnx, ncst, no, na = len(rows), len(prev), len(nxt), len(consts), len(outs), len(accs)

    def body(*refs):
        i = pl.program_id(0)
        it = iter(refs)
        rv = [next(it)[...] for _ in range(nr)]
        pv = [jnp.where(i > 0, next(it)[sub - 1:sub, :], 0.0) for _ in range(npv)]
        nv = [jnp.where(i < n - 1, next(it)[0:1, :], 0.0) for _ in range(nnx)]
        cv = [next(it)[...] for _ in range(ncst)]
        o_refs = [next(it) for _ in range(no)]
        a_refs = [next(it) for _ in range(na)]
        res = fn(*rv, *pv, *nv, *cv)
        if not isinstance(res, (tuple, list)):
            res = (res,)
        for r, o in zip(o_refs, res[:no]):
            r[...] = o.astype(r.dtype)
        if na:
            @pl.when(i == 0)
            def _():
                for r in a_refs:
                    r[...] = jnp.zeros_like(r)
            for r, o in zip(a_refs, res[no:]):
                r[...] += o.astype(F32)

    in_specs = [pl.BlockSpec((tile, w), functools.partial(lambda cb, i: (i, cb), cb)) for _, w, cb in rows]
    in_specs += [pl.BlockSpec((sub, a.shape[1]), lambda i: (jnp.maximum(i * (tile // sub) - 1, 0), 0)) for a in prev]
    in_specs += [pl.BlockSpec((sub, a.shape[1]), lambda i: (jnp.minimum((i + 1) * (tile // sub), T // sub - 1), 0)) for a in nxt]
    in_specs += [pl.BlockSpec(c.shape, lambda i: (0, 0)) for c in consts]
    out_specs = [pl.BlockSpec((tile, c), lambda i: (i, 0)) for c, _ in outs]
    out_specs += [pl.BlockSpec(s, lambda i: (0, 0)) for s in accs]
    out_shape = [jax.ShapeDtypeStruct((T, c), d) for c, d in outs] + [jax.ShapeDtypeStruct(s, F32) for s in accs]
    res = pl.pallas_call(
        body, name=name, grid=(n,), in_specs=in_specs, out_specs=out_specs, out_shape=out_shape,
        compiler_params=_cparams("arbitrary"),
    )(*[r[0] for r in rows], *prev, *nxt, *consts)
    return res


@jax.custom_vjp
def _bdot(a, b):
    return jnp.dot(a.astype(BF16), b.astype(BF16), preferred_element_type=F32)


def _bdot_fwd(a, b):
    return _bdot(a, b), (a.astype(BF16), b.astype(BF16))


def _bdot_bwd(res, ct):
    a, b = res
    c = ct.astype(BF16)
    return (lax.dot_general(c, b, (((1,), (1,)), ((), ())), preferred_element_type=F32),
            lax.dot_general(a, c, (((0,), (0,)), ((), ())), preferred_element_type=F32))


_bdot.defvjp(_bdot_fwd, _bdot_bwd)


def _xdot(a, b):
    return jnp.dot(a, b, precision=lax.Precision.HIGHEST, preferred_element_type=F32)


def _rms(x, g, eps=RMS_EPS):
    return x * lax.rsqrt(jnp.mean(x * x, axis=-1, keepdims=True) + eps) * g


def _softplus(x):
    return jnp.maximum(x, 0.0) + jnp.log(1.0 + jnp.exp(-jnp.abs(x)))


def _sigmoid(x):
    return 1.0 / (1.0 + jnp.exp(-x))


def _seg_mats(width):
    h = lax.broadcasted_iota(jnp.int32, (width, LANE), 0) // HEAD
    j = lax.broadcasted_iota(jnp.int32, (width, LANE), 1)
    seg = (h == j).astype(F32)
    return seg, seg.T


def _prep_rwkv(um, w0, w2, a0, a2, g2, k_k, k_a, seg, segt, *, dims):
    W, DLp, ALp, GLp = dims
    r, k, v = um[:, :W], um[:, W:2 * W], um[:, 2 * W:3 * W]
    o = 3 * W
    xw, xa, xg = um[:, o:o + DLp], um[:, o + DLp:o + DLp + ALp], um[:, o + DLp + ALp:o + DLp + ALp + GLp]
    w_log = -_softplus(-(w0 + _bdot(jnp.tanh(xw), w2))) - 0.5
    decay = jnp.exp(-jnp.exp(w_log))
    a = _sigmoid(a0 + _bdot(xa, a2))
    g = _bdot(_sigmoid(xg), g2)
    kk = k * k_k
    nrm = jnp.sqrt(_xdot(_xdot(kk * kk, seg), segt))
    kk = kk / jnp.maximum(nrm, 1e-12)
    k2 = k * (1.0 + (a - 1.0) * k_a)
    return r, decay, k2, v, kk, kk * a, g


def _shift_mix(u, uprev, mu):
    first = lax.broadcasted_iota(jnp.int32, u.shape, 0) == 0
    sh = jnp.where(first, uprev, pltpu.roll(u, 1, 0))
    return u + (sh - u) * mu, sh


def _post_rwkv(y, r, k2, v, g, lnx_g, lnx_b, r_k, seg, segt):
    inv = 1.0 / HEAD
    mean = _xdot(_xdot(y, seg), segt) * inv
    yc = y - mean
    var = _xdot(_xdot(yc * yc, seg), segt) * inv
    yn = yc * lax.rsqrt(var + GN_EPS) * lnx_g + lnx_b
    bonus = _xdot(_xdot(r * k2 * r_k, seg), segt) * v
    return (yn + bonus) * g


def _prep_fox(uf, qg, kg, fb, seg, segt, *, dims):
    FW, FHp = dims
    q, k, v, f = uf[:, :FW], uf[:, FW:2 * FW], uf[:, 2 * FW:3 * FW], uf[:, 3 * FW:3 * FW + FHp]
    inv = 1.0 / HEAD
    qn = q * lax.rsqrt(_xdot(_xdot(q * q, seg), segt) * inv + RMS_EPS) * qg * (HEAD ** -0.5)
    kn = k * lax.rsqrt(_xdot(_xdot(k * k, seg), segt) * inv + RMS_EPS) * kg
    return qn, kn, v, -_softplus(-(f + fb))


def _tail(h2, pe, z, png, pgb):
    return h2 + _sigmoid(z + pgb) * _rms(pe, png)


def _swiglu(gt, up):
    return gt * _sigmoid(gt) * up


def _cumsum(x, *, reverse, name):
    T, C = x.shape
    tc = _pick(T, (256, 128))
    n = T // tc
    i0 = lax.broadcasted_iota(jnp.int32, (tc, tc), 0)
    i1 = lax.broadcasted_iota(jnp.int32, (tc, tc), 1)
    tri = ((i0 <= i1) if reverse else (i0 >= i1)).astype(BF16)

    def body(x_ref, tri_ref, o_ref, carry):
        i = pl.program_id(0)

        @pl.when(i == 0)
        def _():
            carry[...] = jnp.zeros_like(carry)

        v = x_ref[...]
        hi = v.astype(BF16)
        r1 = v - hi.astype(F32)
        mid = r1.astype(BF16)
        lo = (r1 - mid.astype(F32)).astype(BF16)
        t = tri_ref[...]
        d = lambda p: jnp.dot(t, p, preferred_element_type=F32)
        c = d(hi) + d(mid) + d(lo) + carry[0:1, :]
        o_ref[...] = c
        edge = c[0:1, :] if reverse else c[tc - 1:tc, :]
        carry[...] = jnp.broadcast_to(edge, carry.shape)

    blk = pl.BlockSpec((tc, C), (lambda i: (n - 1 - i, 0)) if reverse else (lambda i: (i, 0)))
    return pl.pallas_call(
        body, name=name, grid=(n,), in_specs=[blk, pl.BlockSpec((tc, tc), lambda i: (0, 0))], out_specs=blk,
        out_shape=jax.ShapeDtypeStruct((T, C), F32), scratch_shapes=[pltpu.VMEM((8, C), F32)],
        compiler_params=_cparams("arbitrary"),
    )(x, tri)


BWD_HEADS_PER_TRIP = 4


def _col(tile, lane, t):
    return jnp.sum(jnp.where(lane == t, tile, 0.0), axis=1, keepdims=True)


def _scan_fwd(r, w, k, kk, kka, vT, *, hg, tc):
    H, T, N = r.shape
    nc = T // tc

    def body(r_ref, w_ref, k_ref, kk_ref, kka_ref, vT_ref, yT_ref, chk_ref, s_ref):
        @pl.when(pl.program_id(1) == 0)
        def _():
            s_ref[...] = jnp.zeros_like(s_ref)

        chk_ref[:, 0] = s_ref[...]
        yT_ref[...] = jnp.zeros_like(yT_ref)
        lane = lax.broadcasted_iota(jnp.int32, (N, tc), 1)

        def emit_y(S, h, t):
            y = jnp.sum(S * r_ref[h, pl.ds(jnp.maximum(t, 0), 1), :], axis=1, keepdims=True)
            yT_ref[h] = jnp.where(lane == t, y, yT_ref[h])

        def step(t, carry):
            for h in range(hg):
                row = lambda ref: ref[h, pl.ds(t, 1), :]
                S = s_ref[h]
                emit_y(S, h, t - 1)
                vcol = _col(vT_ref[h], lane, t)
                sa = -jnp.sum(S * row(kk_ref), axis=1, keepdims=True)
                s_ref[h] = S * row(w_ref) + sa * row(kka_ref) + vcol * row(k_ref)
            return carry

        lax.fori_loop(0, tc, step, 0)
        for h in range(hg):
            emit_y(s_ref[h], h, tc - 1)

    rows = pl.BlockSpec((hg, tc, N), lambda g, c: (g, c, 0))
    cols = pl.BlockSpec((hg, N, tc), lambda g, c: (g, 0, c))
    return pl.pallas_call(
        body, name="scan_fwd", grid=(H // hg, nc), in_specs=[rows] * 5 + [cols],
        out_specs=[cols, pl.BlockSpec((hg, 1, N, N), lambda g, c: (g, c, 0, 0))],
        out_shape=[jax.ShapeDtypeStruct((H, N, T), F32), jax.ShapeDtypeStruct((H, nc, N, N), F32)],
        scratch_shapes=[pltpu.VMEM((hg, N, N), F32)],
        compiler_params=_cparams("arbitrary", "arbitrary"),
    )(r, w, k, kk, kka, vT)


def _scan_bwd(r, w, k, kk, kka, vT, dyT, chk, *, hg, tc):
    H, T, N = r.shape
    nc = T // tc

    def body(r_ref, w_ref, k_ref, kk_ref, kka_ref, vT_ref, dyT_ref, chk_ref,
             dr_ref, dw_ref, dk_ref, dkk_ref, dkka_ref, dvT_ref, sp_ref, ds_ref):
        @pl.when(pl.program_id(1) == 0)
        def _():
            ds_ref[...] = jnp.zeros_like(ds_ref)

        dvT_ref[...] = jnp.zeros_like(dvT_ref)
        lane = lax.broadcasted_iota(jnp.int32, (N, tc), 1)

        for h in range(hg):
            sp_ref[h, 0] = chk_ref[h, 0]

        def fstep(t, carry):
            for h in range(hg):
                row = lambda ref: ref[h, pl.ds(t, 1), :]
                S = sp_ref[h, t]
                vcol = _col(vT_ref[h], lane, t)
                sa = -jnp.sum(S * row(kk_ref), axis=1, keepdims=True)
                sp_ref[h, t + 1] = S * row(w_ref) + sa * row(kka_ref) + vcol * row(k_ref)
            return carry

        lax.fori_loop(0, tc, fstep, 0)

        def bstep(h0, i, carry):
            t = tc - 1 - i
            for h in range(h0, min(h0 + BWD_HEADS_PER_TRIP, hg)):
                row = lambda ref: ref[h, pl.ds(t, 1), :]
                rr, wr, kr, kkr, kkar = row(r_ref), row(w_ref), row(k_ref), row(kk_ref), row(kka_ref)
                Sp = sp_ref[h, t]
                Sn = sp_ref[h, t + 1]
                dycol = _col(dyT_ref[h], lane, t)
                vcol = _col(vT_ref[h], lane, t)
                dS = ds_ref[h]
                dSn = dS + dycol * rr
                dsa = jnp.sum(dS * kkar, axis=1, keepdims=True) + dycol * jnp.sum(rr * kkar, axis=1, keepdims=True)
                dr_ref[h, pl.ds(t, 1), :] = jnp.sum(Sn * dycol, axis=0, keepdims=True)
                sa = -jnp.sum(Sp * kkr, axis=1, keepdims=True)
                dw_ref[h, pl.ds(t, 1), :] = jnp.sum(dSn * Sp, axis=0, keepdims=True)
                dkka_ref[h, pl.ds(t, 1), :] = jnp.sum(dSn * sa, axis=0, keepdims=True)
                dvcol = jnp.sum(dSn * kr, axis=1, keepdims=True)
                dk_ref[h, pl.ds(t, 1), :] = jnp.sum(dSn * vcol, axis=0, keepdims=True)
                dkk_ref[h, pl.ds(t, 1), :] = -jnp.sum(Sp * dsa, axis=0, keepdims=True)
                ds_ref[h] = dSn * wr - dsa * kkr
                dvT_ref[h] = jnp.where(lane == t, dvcol, dvT_ref[h])
            return carry

        for h0 in range(0, hg, BWD_HEADS_PER_TRIP):
            lax.fori_loop(0, tc, functools.partial(bstep, h0), 0)

    rows = pl.BlockSpec((hg, tc, N), lambda g, c: (g, nc - 1 - c, 0))
    cols = pl.BlockSpec((hg, N, tc), lambda g, c: (g, 0, nc - 1 - c))
    return pl.pallas_call(
        body, name="scan_bwd", grid=(H // hg, nc),
        in_specs=[rows] * 5 + [cols, cols, pl.BlockSpec((hg, 1, N, N), lambda g, c: (g, nc - 1 - c, 0, 0))],
        out_specs=[rows] * 5 + [cols],
        out_shape=[jax.ShapeDtypeStruct((H, T, N), F32)] * 5 + [jax.ShapeDtypeStruct((H, N, T), F32)],
        scratch_shapes=[pltpu.VMEM((hg, tc + 1, N, N), F32), pltpu.VMEM((hg, N, N), F32)],
        compiler_params=_cparams("arbitrary", "arbitrary"),
    )(r, w, k, kk, kka, vT, dyT, chk)


_NT = (((1,), (1,)), ((), ()))
_TN = (((0,), (0,)), ((), ()))


def _scores(q, k, cc, cr, masked):
    s = lax.dot_general(q, k, _NT, preferred_element_type=F32) + cc - cr
    if masked:
        tb = s.shape[0]
        keep = lax.broadcasted_iota(jnp.int32, (tb, tb), 0) >= lax.broadcasted_iota(jnp.int32, (tb, tb), 1)
        s = jnp.where(keep, s, NEG)
    return s


def _attn_specs(T, N, tb):
    blk = pl.BlockSpec((1, tb, N), lambda h, i: (h, i, 0))
    whole = pl.BlockSpec((1, T, N), lambda h, i: (h, 0, 0))
    col = pl.BlockSpec((1, tb, 1), lambda h, i: (h, i, 0))
    wcol = pl.BlockSpec((1, T, 1), lambda h, i: (h, 0, 0))
    row = pl.BlockSpec((1, 1, tb), lambda h, i: (h, 0, i))
    wrow = pl.BlockSpec((1, 1, T), lambda h, i: (h, 0, 0))
    return blk, whole, col, wcol, row, wrow


def _attn_fwd(q, k, v, ccol, crow, *, tb):
    H, T, N = q.shape

    def body(q_ref, k_ref, v_ref, cc_ref, cr_ref, o_ref, lse_ref, m_s, l_s, acc_s):
        qi = pl.program_id(1)
        m_s[...] = jnp.full_like(m_s, NEG)
        l_s[...] = jnp.zeros_like(l_s)
        acc_s[...] = jnp.zeros_like(acc_s)
        q_, cc = q_ref[0], cc_ref[0]

        def block(j, masked):
            at = pl.ds(pl.multiple_of(j * tb, tb), tb)
            s = _scores(q_, k_ref[0, at, :], cc, cr_ref[0, :, at], masked)
            m_new = jnp.maximum(m_s[...], jnp.max(s, axis=1, keepdims=True))
            p = jnp.exp(s - m_new)
            alpha = jnp.exp(m_s[...] - m_new)
            l_s[...] = alpha * l_s[...] + jnp.sum(p, axis=1, keepdims=True)
            acc_s[...] = alpha * acc_s[...] + jnp.dot(p.astype(BF16), v_ref[0, at, :], preferred_element_type=F32)
            m_s[...] = m_new

        def below(j, carry):
            block(j, False)
            return carry

        lax.fori_loop(0, qi, below, 0)
        block(qi, True)
        o_ref[0] = acc_s[...] / l_s[...]
        lse_ref[0] = m_s[...] + jnp.log(l_s[...])

    blk, whole, col, wcol, row, wrow = _attn_specs(T, N, tb)
    return pl.pallas_call(
        body, name="fox_fwd", grid=(H, T // tb), in_specs=[blk, whole, whole, col, wrow], out_specs=[blk, col],
        out_shape=[jax.ShapeDtypeStruct((H, T, N), F32), jax.ShapeDtypeStruct((H, T, 1), F32)],
        scratch_shapes=[pltpu.VMEM((tb, 1), F32), pltpu.VMEM((tb, 1), F32), pltpu.VMEM((tb, N), F32)],
        compiler_params=_cparams("parallel", "arbitrary"),
    )(q, k, v, ccol, crow)


def _attn_bwd_q(q, k, v, ccol, crow, o, lse, do, *, tb):
    H, T, N = q.shape

    def body(q_ref, k_ref, v_ref, cc_ref, cr_ref, o_ref, lse_ref, do_ref, dq_ref, dc_ref, dl_ref, dq_s, dc_s):
        qi = pl.program_id(1)
        dq_s[...] = jnp.zeros_like(dq_s)
        dc_s[...] = jnp.zeros_like(dc_s)
        q_, cc, lse_, do_ = q_ref[0], cc_ref[0], lse_ref[0], do_ref[0]
        delta = jnp.sum(do_ * o_ref[0], axis=1, keepdims=True)
        dl_ref[0] = delta
        dob = do_.astype(BF16)

        def block(j, masked):
            at = pl.ds(pl.multiple_of(j * tb, tb), tb)
            kb = k_ref[0, at, :]
            p = jnp.exp(_scores(q_, kb, cc, cr_ref[0, :, at], masked) - lse_)
            dp = lax.dot_general(dob, v_ref[0, at, :], _NT, preferred_element_type=F32)
            ds = p * (dp - delta)
            dq_s[...] += jnp.dot(ds.astype(BF16), kb, preferred_element_type=F32)
            dc_s[...] += jnp.sum(ds, axis=1, keepdims=True)

        def below(j, carry):
            block(j, False)
            return carry

        lax.fori_loop(0, qi, below, 0)
        block(qi, True)
        dq_ref[0] = dq_s[...]
        dc_ref[0] = dc_s[...]

    blk, whole, col, wcol, row, wrow = _attn_specs(T, N, tb)
    return pl.pallas_call(
        body, name="fox_bwd_q", grid=(H, T // tb), in_specs=[blk, whole, whole, col, wrow, blk, col, blk],
        out_specs=[blk, col, col],
        out_shape=[jax.ShapeDtypeStruct((H, T, N), F32)] + [jax.ShapeDtypeStruct((H, T, 1), F32)] * 2,
        scratch_shapes=[pltpu.VMEM((tb, N), F32), pltpu.VMEM((tb, 1), F32)],
        compiler_params=_cparams("parallel", "arbitrary"),
    )(q, k, v, ccol, crow, o, lse, do)


def _attn_bwd_kv(q, k, v, ccol, crow, lse, delta, do, *, tb):
    H, T, N = q.shape
    nb = T // tb

    def body(q_ref, k_ref, v_ref, cc_ref, cr_ref, lse_ref, dl_ref, do_ref, dk_ref, dv_ref, dc_ref, dk_s, dv_s, dc_s):
        ki = pl.program_id(1)
        dk_s[...] = jnp.zeros_like(dk_s)
        dv_s[...] = jnp.zeros_like(dv_s)
        dc_s[...] = jnp.zeros_like(dc_s)
        kb, vb, cr = k_ref[0], v_ref[0], cr_ref[0]

        def block(j, masked):
            at = pl.ds(pl.multiple_of(j * tb, tb), tb)
            qb, dob = q_ref[0, at, :], do_ref[0, at, :]
            p = jnp.exp(_scores(qb, kb, cc_ref[0, at, :], cr, masked) - lse_ref[0, at, :])
            dp = lax.dot_general(dob, vb, _NT, preferred_element_type=F32)
            ds = p * (dp - dl_ref[0, at, :])
            dv_s[...] += lax.dot_general(p.astype(BF16), dob, _TN, preferred_element_type=F32)
            dk_s[...] += lax.dot_general(ds.astype(BF16), qb, _TN, preferred_element_type=F32)
            dc_s[...] += jnp.sum(ds, axis=0, keepdims=True)

        def above(j, carry):
            block(j, False)
            return carry

        block(ki, True)
        lax.fori_loop(ki + 1, nb, above, 0)
        dk_ref[0] = dk_s[...]
        dv_ref[0] = dv_s[...]
        dc_ref[0] = dc_s[...]

    blk, whole, col, wcol, row, wrow = _attn_specs(T, N, tb)
    return pl.pallas_call(
        body, name="fox_bwd_kv", grid=(H, nb), in_specs=[whole, blk, blk, wcol, row, wcol, wcol, whole],
        out_specs=[blk, blk, row],
        out_shape=[jax.ShapeDtypeStruct((H, T, N), F32)] * 2 + [jax.ShapeDtypeStruct((H, 1, T), F32)],
        scratch_shapes=[pltpu.VMEM((tb, N), F32), pltpu.VMEM((tb, N), F32), pltpu.VMEM((1, tb), F32)],
        compiler_params=_cparams("parallel", "arbitrary"),
    )(q, k, v, ccol, crow, lse, delta, do)


def _heads(x):
    T = x.shape[0]
    return x.reshape(T, -1, HEAD).transpose(1, 0, 2)


def _headsT(x):
    T = x.shape[0]
    return x.reshape(T, -1, HEAD).transpose(1, 2, 0)


def _unheads(x):
    return x.transpose(1, 0, 2).reshape(x.shape[1], -1)


def _unheadsT(x):
    return x.transpose(2, 0, 1).reshape(x.shape[2], -1)


def _padc(x, n):
    return jnp.pad(x, ((0, 0), (0, n - x.shape[1])))


def _padr(x, n):
    return jnp.pad(x, ((0, n - x.shape[0]), (0, 0)))


class _Dims:
    def __init__(self, W, DL, AL, GL, FH):
        self.W, self.DL, self.AL, self.GL, self.FH = W, DL, AL, GL, FH
        self.DLp, self.ALp, self.GLp, self.FHp = _rup(DL, LANE), _rup(AL, LANE), _rup(GL, LANE), _rup(FH, LANE)
        self.FW = FH * HEAD
        self.RC = 3 * W + DL + AL + GL
        self.RP = 3 * W + self.DLp + self.ALp + self.GLp
        self.FC = 3 * self.FW + FH
        self.FP = 3 * self.FW + self.FHp

    def pad_r(self, a):
        W, o = self.W, 3 * self.W
        return jnp.concatenate([a[:, :o], _padc(a[:, o:o + self.DL], self.DLp),
                                _padc(a[:, o + self.DL:o + self.DL + self.AL], self.ALp),
                                _padc(a[:, o + self.DL + self.AL:self.RC], self.GLp)], axis=1)

    def unpad_r(self, a):
        o = 3 * self.W
        return jnp.concatenate([a[:, :o], a[:, o:o + self.DL], a[:, o + self.DLp:o + self.DLp + self.AL],
                                a[:, o + self.DLp + self.ALp:o + self.DLp + self.ALp + self.GL]], axis=1)

    def pad_f(self, a):
        return _padc(a, self.FP)

    def unpad_f(self, a):
        return a[:, :self.FC]


def _local_step(x, p, tgt, Wt, vec, d):
    T, D = x.shape
    W, FW = d.W, d.FW
    H = W // HEAD
    seg, segt = _seg_mats(W)
    segf, segft = _seg_mats(FW)
    T1 = 256
    rk_flat = vec["r_k"].reshape(1, W)
    mu = d.pad_r(vec["shift_mu"])
    qg = jnp.tile(vec["q_norm_g"], (1, d.FH))
    kg = jnp.tile(vec["k_norm_g"], (1, d.FH))
    fb = _padc(vec["fgate_b"], d.FHp)
    pdims = (W, d.DLp, d.ALp, d.GLp)
    fdims = (FW, d.FHp)

    (xn,) = _rowwise(lambda x_, g_: _rms(x_, g_), [x], [vec["attn_norm_g"]], [(D, BF16)], tile=T1, name="norm_attn")
    u_r = _mm(xn, Wt["w_in_r"], name="mm_in_r")
    u_f = _mm(xn, Wt["w_in_f"], name="mm_in_f")

    prep_consts = [mu, vec["w0"], Wt["w2"].astype(F32), vec["a0"], Wt["a2"].astype(F32), Wt["g2"].astype(F32), vec["k_k"], vec["k_a"], seg, segt]

    def prep_fwd(u_, up_, mu_, *cs):
        um, _ = _shift_mix(u_, up_, mu_)
        return _prep_rwkv(um, *cs, dims=pdims)

    r, dec, k2, v, kk, kka, g = _rowwise(prep_fwd, [u_r], prep_consts, [(W, F32)] * 7, tile=128, name="rwkv_prep", prev=[u_r])
    hg, tc = min(16, H), 128
    yT, chk = _scan_fwd(_heads(r), _heads(dec), _heads(k2), _heads(kk), _heads(kka), _headsT(v), hg=hg, tc=tc)
    y = _unheadsT(yT)
    post_consts = [vec["lnx_g"], vec["lnx_b"], rk_flat, seg, segt]
    (y_r,) = _rowwise(_post_rwkv, [y, r, k2, v, g], post_consts, [(W, BF16)], tile=T1, name="rwkv_post")

    fox_consts = [qg, kg, fb, segf, segft]
    qn, kn, vf, logf = _rowwise(functools.partial(_prep_fox, dims=fdims), [u_f], fox_consts,
                                [(FW, BF16), (FW, BF16), (FW, BF16), (d.FHp, F32)], tile=T1, name="fox_prep")
    c = _cumsum(logf, reverse=False, name="fox_cumsum")
    cT = c[:, :d.FH].T
    ccol, crow = cT[:, :, None], cT[:, None, :]
    tb = _pick(T, (512, 256, 128))
    qh, kh, vh = _heads(qn), _heads(kn), _heads(vf)
    o, lse = _attn_fwd(qh, kh, vh, ccol, crow, tb=tb)
    y_f = _unheads(o)

    ycat = jnp.concatenate([y_r, y_f.astype(BF16)], axis=1)
    h1 = _mm(ycat, Wt["w_out"], add=x, name="mm_out")
    (hn,) = _rowwise(lambda h_, g_: _rms(h_, g_), [h1], [vec["ffn_norm_g"]], [(D, BF16)], tile=T1, name="norm_ffn")
    gt = _mm(hn, Wt["w_gate"], name="mm_gate")
    up = _mm(hn, Wt["w_up"], name="mm_up")
    (act,) = _rowwise(_swiglu, [gt, up], [], [(gt.shape[1], BF16)], tile=T1, name="swiglu")
    h2 = _mm(act, Wt["w_down"], add=h1, name="mm_down")
    (hg_,) = _rowwise(lambda h_, g_: _rms(h_, g_), [h2], [vec["ple_gate_norm_g"]], [(D, BF16)], tile=T1, name="norm_gate")
    pe = _mm(p, Wt["ple_proj"], name="mm_ple")
    z = _mm(hg_, Wt["ple_gate_w"], name="mm_pgate")

    def tail(h2_, pe_, z_, tg_, png_, pgb_):
        h3, f = jax.vjp(_tail, h2_, pe_, z_, png_, pgb_)
        err = h3 - tg_
        dh3 = err * (1.0 / D)
        lt = 0.5 * jnp.sum(jnp.sum(err * err, axis=1, keepdims=True) * (1.0 / D), axis=0, keepdims=True)
        dh2_, dpe_, dz_, dpng_, dpgb_ = f(dh3)
        return dh2_, dpe_, dz_, jnp.broadcast_to(lt, (1, LANE)), dpng_, dpgb_

    dh3, dpe, dz, loss, g_png, g_pgb = _rowwise(
        tail, [h2, pe, z, tgt], [vec["ple_norm_g"], vec["ple_gate_b"]], [(D, F32), (D, BF16), (D, BF16)],
        [(1, LANE), (1, D), (1, D)], tile=T1, name="tail")
    G = {}
    gv = {"ple_norm_g": g_png, "ple_gate_b": g_pgb}
    G["ple_gate_w"] = _mm(hg_, dz, ta=True, out_dtype=BF16, name="gw_pgate")
    G["ple_proj"] = _mm(p, dpe, ta=True, out_dtype=BF16, name="gw_ple")
    d_hg = _mm(dz, Wt["ple_gate_w"], tb=True, name="mmb_pgate")

    def norm_bwd(h_, dres_, dn_, g_):
        _, f = jax.vjp(_rms, h_, g_)
        dh_, dg_ = f(dn_)
        return dres_ + dh_, dg_

    dh2, gv["ple_gate_norm_g"] = _rowwise(norm_bwd, [h2, dh3, d_hg], [vec["ple_gate_norm_g"]], [(D, F32)], [(1, D)],
                                          tile=T1, name="norm_gate_bwd")
    G["w_down"] = _mm(act, dh2, ta=True, out_dtype=BF16, name="gw_down")
    d_act = _mm(dh2, Wt["w_down"], tb=True, name="mmb_down")

    def swiglu_bwd(gt_, up_, da_):
        _, f = jax.vjp(_swiglu, gt_, up_)
        return f(da_)

    d_gt, d_up = _rowwise(swiglu_bwd, [gt, up, d_act], [], [(gt.shape[1], BF16)] * 2, tile=T1, name="swiglu_bwd")
    G["w_gate"] = _mm(hn, d_gt, ta=True, out_dtype=BF16, name="gw_gate")
    G["w_up"] = _mm(hn, d_up, ta=True, out_dtype=BF16, name="gw_up")
    d_hn = _mm(d_gt, Wt["w_gate"], tb=True, name="mmb_gate")
    d_hn = _mm(d_up, Wt["w_up"], tb=True, add=d_hn, name="mmb_up")
    dh1, gv["ffn_norm_g"] = _rowwise(norm_bwd, [h1, dh2, d_hn], [vec["ffn_norm_g"]], [(D, F32)], [(1, D)],
                                     tile=T1, name="norm_ffn_bwd")
    G["w_out"] = _mm(ycat, dh1, ta=True, out_dtype=BF16, name="gw_out")
    d_ycat = _mm(dh1, Wt["w_out"], tb=True, name="mmb_out")

    def post_bwd(y_, r_, k2_, v_, g_, dy_, *cs):
        lg, lb, rk, sg, sgt = cs
        _, f = jax.vjp(lambda *a: _post_rwkv(*a, sg, sgt), y_, r_, k2_, v_, g_, lg, lb, rk)
        return f(dy_)

    dy, dr1, dk1, dv1, dg, gv["lnx_g"], gv["lnx_b"], g_rk = _rowwise(
        post_bwd, [y, r, k2, v, g, (d_ycat, W, 0)], post_consts, [(W, F32)] * 5, [(1, W)] * 3, tile=128, name="rwkv_post_bwd")
    gv["r_k"] = g_rk.reshape(H, HEAD)
    dr, ddec, dk2, dkk, dkka, dvT = _scan_bwd(_heads(r), _heads(dec), _heads(k2), _heads(kk), _heads(kka), _headsT(v),
                                              _headsT(dy), chk, hg=min(8, H), tc=tc)
    dr, ddec, dk2, dkk, dkka, dv = (_unheads(dr), _unheads(ddec), _unheads(dk2), _unheads(dkk), _unheads(dkka), _unheadsT(dvT))

    def prep_bwd(u_, dr_, dr1_, ddec_, dk2_, dk1_, dv_, dv1_, dkk_, dkka_, dg_, up_, mu_, *cs):
        um, sh = _shift_mix(u_, up_, mu_)
        cs_d, sg, sgt = cs[:7], cs[7], cs[8]
        _, f = jax.vjp(lambda um_, *c_: _prep_rwkv(um_, *c_, sg, sgt, dims=pdims), um, *cs_d)
        res = f((dr_ + dr1_, ddec_, dk2_ + dk1_, dv_ + dv1_, dkk_, dkka_, dg_))
        dum = res[0]
        dmu = jnp.sum(dum * (sh - u_), axis=0, keepdims=True)
        return (dum, dmu) + tuple(res[1:])

    LP = [Wt["w2"].shape, Wt["a2"].shape, Wt["g2"].shape]
    dum, g_mu, gv["w0"], g_w2, gv["a0"], g_a2, g_g2, gv["k_k"], gv["k_a"] = _rowwise(
        prep_bwd, [u_r, dr, dr1, ddec, dk2, dk1, dv, dv1, dkk, dkka, dg], prep_consts, [(d.RP, F32)],
        [(1, d.RP), (1, W), LP[0], (1, W), LP[1], LP[2], (1, W), (1, W)], tile=128, name="rwkv_prep_bwd", prev=[u_r])
    gv["shift_mu"] = d.unpad_r(g_mu)
    G["w2"], G["a2"], G["g2"] = g_w2, g_a2, g_g2
    (du_r,) = _rowwise(lambda a_, an_, mu_: a_ * (1.0 - mu_) + jnp.where(
        lax.broadcasted_iota(jnp.int32, a_.shape, 0) == a_.shape[0] - 1, an_, pltpu.roll(a_, a_.shape[0] - 1, 0)) * mu_,
        [dum], [mu], [(d.RP, BF16)], tile=T1, name="shift_bwd", nxt=[dum])

    do = _heads(d_ycat[:, W:])
    dq, dcq, delta = _attn_bwd_q(qh, kh, vh, ccol, crow, o, lse, do, tb=tb)
    dk_, dv_, dck = _attn_bwd_kv(qh, kh, vh, ccol, crow, lse, delta, do.astype(BF16), tb=tb)
    dc = _padc((dcq[:, :, 0] - dck[:, 0, :]).T, d.FHp)
    dlogf = _cumsum(dc, reverse=True, name="fox_cumsum_bwd")

    def fox_bwd(uf_, dq_, dk__, dv__, dlf_, *cs):
        qg_, kg_, fb_, sg, sgt = cs
        _, f = jax.vjp(lambda uf__, a, b, c_: _prep_fox(uf__, a, b, c_, sg, sgt, dims=fdims), uf_, qg_, kg_, fb_)
        return f((dq_, dk__, dv__, dlf_))

    du_f, g_qg, g_kg, g_fb = _rowwise(fox_bwd, [u_f, _unheads(dq), _unheads(dk_), _unheads(dv_), dlogf], fox_consts,
                                      [(d.FP, BF16)], [(1, FW), (1, FW), (1, d.FHp)], tile=T1, name="fox_prep_bwd")
    gv["q_norm_g"] = g_qg.reshape(d.FH, HEAD).sum(0, keepdims=True)
    gv["k_norm_g"] = g_kg.reshape(d.FH, HEAD).sum(0, keepdims=True)
    gv["fgate_b"] = g_fb[:, :d.FH]

    G["w_in_r"] = _mm(xn, du_r, ta=True, out_dtype=BF16, name="gw_in_r")
    G["w_in_f"] = _mm(xn, du_f, ta=True, out_dtype=BF16, name="gw_in_f")
    d_xn = _mm(du_r, Wt["w_in_r"], tb=True, name="mmb_in_r")
    d_xn = _mm(du_f, Wt["w_in_f"], tb=True, add=d_xn, name="mmb_in_f")
    dx, gv["attn_norm_g"] = _rowwise(norm_bwd, [x, dh1, d_xn], [vec["attn_norm_g"]], [(D, F32)], [(1, D)],
                                     tile=T1, name="norm_attn_bwd")
    return loss, dx, G, gv


_HBM = pl.BlockSpec(memory_space=pl.ANY)
_OTHER_CHIPS = ((0, 1), (1, 0), (1, 1))


def _flip(v, bit):
    return 1 - v if bit else v


def _exchange(arrs, *, gather, name):
    n = len(arrs)

    def body(*refs):
        ins, outs = refs[:n], refs[n:2 * n]
        send_sems, recv_sems, own_sems = refs[2 * n:]
        x, y, c = lax.axis_index("x"), lax.axis_index("y"), lax.axis_index("c")
        me = 2 * x + y
        own, remote = [], []
        for a in range(n):
            cp = pltpu.make_async_copy(ins[a] if gather else ins[a].at[me], outs[a].at[me], own_sems.at[a])
            cp.start()
            own.append(cp)
        for a in range(n):
            for k, (dx, dy) in enumerate(_OTHER_CHIPS):
                px, py = _flip(x, dx), _flip(y, dy)
                cp = pltpu.make_async_remote_copy(
                    src_ref=ins[a] if gather else ins[a].at[2 * px + py], dst_ref=outs[a].at[me],
                    send_sem=send_sems.at[3 * a + k], recv_sem=recv_sems.at[3 * a + k],
                    device_id=(px, py, c), device_id_type=MESH)
                cp.start()
                remote.append(cp)
        for cp in remote:
            cp.wait()
        for cp in own:
            cp.wait()

    out_shape = [jax.ShapeDtypeStruct(((N_CHIPS,) + a.shape) if gather else a.shape, a.dtype) for a in arrs]
    return pl.pallas_call(
        body, name=name, in_specs=[_HBM] * n, out_specs=[_HBM] * n, out_shape=out_shape,
        scratch_shapes=[pltpu.SemaphoreType.DMA((3 * n,)), pltpu.SemaphoreType.DMA((3 * n,)), pltpu.SemaphoreType.DMA((n,))],
    )(*arrs)


def _swap_cores(arrs, *, name):
    n = len(arrs)

    def body(*refs):
        ins, outs = refs[:n], refs[n:2 * n]
        send_sems, recv_sems = refs[2 * n:]
        peer = (lax.axis_index("x"), lax.axis_index("y"), 1 - lax.axis_index("c"))
        cps = [pltpu.make_async_remote_copy(src_ref=ins[a], dst_ref=outs[a], send_sem=send_sems.at[a],
                                            recv_sem=recv_sems.at[a], device_id=peer, device_id_type=MESH) for a in range(n)]
        for cp in cps:
            cp.start()
        for cp in cps:
            cp.wait()

    return pl.pallas_call(
        body, name=name, in_specs=[_HBM] * n, out_specs=[_HBM] * n,
        out_shape=[jax.ShapeDtypeStruct(a.shape, a.dtype) for a in arrs],
        scratch_shapes=[pltpu.SemaphoreType.DMA((n,)), pltpu.SemaphoreType.DMA((n,))],
    )(*arrs)


def _allreduce_small(pack, *, name):
    R, C = pack.shape

    def body(p_ref, o_ref, recv, send_sems, recv_sems):
        x, y, c = lax.axis_index("x"), lax.axis_index("y"), lax.axis_index("c")
        me = 4 * x + 2 * y + c
        recv[me] = p_ref[...]
        cps = []
        for k in range(1, N_DEV):
            peer = (_flip(x, k & 4), _flip(y, k & 2), _flip(c, k & 1))
            cp = pltpu.make_async_remote_copy(src_ref=p_ref, dst_ref=recv.at[me], send_sem=send_sems.at[k - 1],
                                              recv_sem=recv_sems.at[k - 1], device_id=peer, device_id_type=MESH)
            cp.start()
            cps.append(cp)
        for cp in cps:
            cp.wait()
        acc = recv[0]
        for s in range(1, N_DEV):
            acc = acc + recv[s]
        o_ref[...] = acc

    vm = pl.BlockSpec(memory_space=pltpu.VMEM)
    return pl.pallas_call(
        body, name=name, in_specs=[vm], out_specs=vm, out_shape=jax.ShapeDtypeStruct((R, C), F32),
        scratch_shapes=[pltpu.VMEM((N_DEV, R, C), F32), pltpu.SemaphoreType.DMA((N_DEV - 1,)), pltpu.SemaphoreType.DMA((N_DEV - 1,))],
    )(pack)


def _sum_slots(a, *, name):
    S, R, C = a.shape
    tr = _pick(R, (256, 128, 64, 32, 16, 8))

    def body(a_ref, o_ref):
        acc = a_ref[0].astype(F32)
        for s in range(1, S):
            acc = acc + a_ref[s].astype(F32)
        o_ref[...] = acc

    return pl.pallas_call(
        body, name=name, grid=(R // tr,), in_specs=[pl.BlockSpec((S, tr, C), lambda i: (0, i, 0))],
        out_specs=pl.BlockSpec((tr, C), lambda i: (i, 0)), out_shape=jax.ShapeDtypeStruct((R, C), F32),
        compiler_params=_cparams("parallel"),
    )(a)


def _adamw(w, m, v, gs, *, name):
    R, C = w.shape
    tile = _pick(R, (128, 96, 64, 32, 16, 8))

    def fn(w_, m_, v_, *g_):
        g = g_[0]
        for e in g_[1:]:
            g = g + e
        m2 = ADAM_B1 * m_ + (1.0 - ADAM_B1) * g
        v2 = ADAM_B2 * v_ + (1.0 - ADAM_B2) * jnp.square(g)
        m_hat = m2 / (1.0 - ADAM_B1 ** ADAM_STEP)
        v_hat = v2 / (1.0 - ADAM_B2 ** ADAM_STEP)
        delta = -ADAM_LR * (m_hat / (jnp.sqrt(v_hat) + ADAM_EPS) + ADAM_WD * w_)
        return g, delta, m2, v2

    return _rowwise(fn, [w, m, v, *gs], [], [(C, F32)] * 4, tile=tile, name=name)


_ARGS = "x, p, attn_norm_g, w_in, shift_mu, w0, w2, a0, a2, g2, k_k, k_a, r_k, lnx_g, lnx_b, q_norm_g, k_norm_g, fgate_b, w_out, ffn_norm_g, w_gate, w_up, w_down, ple_proj, ple_norm_g, ple_gate_norm_g, ple_gate_w, ple_gate_b, loss_target, m_attn_norm_g, m_w_in, m_shift_mu, m_w0, m_w2, m_a0, m_a2, m_g2, m_k_k, m_k_a, m_r_k, m_lnx_g, m_lnx_b, m_q_norm_g, m_k_norm_g, m_fgate_b, m_w_out, m_ffn_norm_g, m_w_gate, m_w_up, m_w_down, m_ple_proj, m_ple_norm_g, m_ple_gate_norm_g, m_ple_gate_w, m_ple_gate_b, v_attn_norm_g, v_w_in, v_shift_mu, v_w0, v_w2, v_a0, v_a2, v_g2, v_k_k, v_k_a, v_r_k, v_lnx_g, v_lnx_b, v_q_norm_g, v_k_norm_g, v_fgate_b, v_w_out, v_ffn_norm_g, v_w_gate, v_w_up, v_w_down, v_ple_proj, v_ple_norm_g, v_ple_gate_norm_g, v_ple_gate_w, v_ple_gate_b".split(", ")
_WEIGHTS = _ARGS[2:28]
_COL_SHARDED = ("w_in", "w2", "a2", "g2", "w_gate", "w_up", "ple_proj")
_ROW_SHARDED = ("w_out", "w_down", "ple_gate_w")
_MATRICES = _COL_SHARDED + _ROW_SHARDED
_VECTORS = tuple(n for n in _WEIGHTS if n not in _MATRICES)


def _whole(name, g):
    if name in _COL_SHARDED:
        return g.transpose(1, 0, 2).reshape(g.shape[1], -1)
    return g.reshape(-1, g.shape[2])


def _pieces(name, a):
    if name in _COL_SHARDED:
        return a.reshape(a.shape[0], N_CHIPS, -1).transpose(1, 0, 2)
    return a.reshape(N_CHIPS, -1, a.shape[1])


def kernel(x, p, attn_norm_g, w_in, shift_mu, w0, w2, a0, a2, g2, k_k, k_a, r_k, lnx_g, lnx_b, q_norm_g, k_norm_g, fgate_b, w_out, ffn_norm_g, w_gate, w_up, w_down, ple_proj, ple_norm_g, ple_gate_norm_g, ple_gate_w, ple_gate_b, loss_target, m_attn_norm_g, m_w_in, m_shift_mu, m_w0, m_w2, m_a0, m_a2, m_g2, m_k_k, m_k_a, m_r_k, m_lnx_g, m_lnx_b, m_q_norm_g, m_k_norm_g, m_fgate_b, m_w_out, m_ffn_norm_g, m_w_gate, m_w_up, m_w_down, m_ple_proj, m_ple_norm_g, m_ple_gate_norm_g, m_ple_gate_w, m_ple_gate_b, v_attn_norm_g, v_w_in, v_shift_mu, v_w0, v_w2, v_a0, v_a2, v_g2, v_k_k, v_k_a, v_r_k, v_lnx_g, v_lnx_b, v_q_norm_g, v_k_norm_g, v_fgate_b, v_w_out, v_ffn_norm_g, v_w_gate, v_w_up, v_w_down, v_ple_proj, v_ple_norm_g, v_ple_gate_norm_g, v_ple_gate_w, v_ple_gate_b):
    A = dict(zip(_ARGS, (x, p, attn_norm_g, w_in, shift_mu, w0, w2, a0, a2, g2, k_k, k_a, r_k, lnx_g, lnx_b, q_norm_g, k_norm_g, fgate_b, w_out, ffn_norm_g, w_gate, w_up, w_down, ple_proj, ple_norm_g, ple_gate_norm_g, ple_gate_w, ple_gate_b, loss_target, m_attn_norm_g, m_w_in, m_shift_mu, m_w0, m_w2, m_a0, m_a2, m_g2, m_k_k, m_k_a, m_r_k, m_lnx_g, m_lnx_b, m_q_norm_g, m_k_norm_g, m_fgate_b, m_w_out, m_ffn_norm_g, m_w_gate, m_w_up, m_w_down, m_ple_proj, m_ple_norm_g, m_ple_gate_norm_g, m_ple_gate_w, m_ple_gate_b, v_attn_norm_g, v_w_in, v_shift_mu, v_w0, v_w2, v_a0, v_a2, v_g2, v_k_k, v_k_a, v_r_k, v_lnx_g, v_lnx_b, v_q_norm_g, v_k_norm_g, v_fgate_b, v_w_out, v_ffn_norm_g, v_w_gate, v_w_up, v_w_down, v_ple_proj, v_ple_norm_g, v_ple_gate_norm_g, v_ple_gate_w, v_ple_gate_b)))
    x, p, tgt = A["x"][0], A["p"][0, 0], A["loss_target"][0]
    d = _Dims(W=A["w0"].shape[-1], DL=A["w2"].shape[1], AL=A["a2"].shape[1], GL=A["g2"].shape[1], FH=A["fgate_b"].shape[-1])

    gathered = _exchange([A[n][0].astype(BF16) for n in _MATRICES], gather=True, name="gather_weights")
    full = {n: _whole(n, g) for n, g in zip(_MATRICES, gathered)}
    Wt = {n: full[n] for n in ("w_out", "w_gate", "w_up", "w_down", "ple_proj", "ple_gate_w")}
    Wt["w_in_r"] = d.pad_r(full["w_in"][:, :d.RC])
    Wt["w_in_f"] = d.pad_f(full["w_in"][:, d.RC:])
    Wt["w2"], Wt["a2"], Wt["g2"] = _padr(full["w2"], d.DLp), _padr(full["a2"], d.ALp), _padr(full["g2"], d.GLp)
    vec = {n: A[n].reshape(-1, A[n].shape[-1]) for n in _VECTORS}

    loss, dx, G, gv = _local_step(x, p, tgt, Wt, vec, d)

    gw = dict(G)
    gw["w_in"] = jnp.concatenate([d.unpad_r(G["w_in_r"]), d.unpad_f(G["w_in_f"])], axis=1)
    gw["w2"], gw["a2"], gw["g2"] = G["w2"][:d.DL], G["a2"][:d.AL], G["g2"][:d.GL]
    recv = _exchange([_pieces(n, gw[n]).astype(BF16) for n in _MATRICES], gather=False, name="scatter_grads")
    part = [_sum_slots(r, name="sum_" + n) for n, r in zip(_MATRICES, recv)]
    sib = _swap_cores(part, name="swap_cores")

    sizes = [1] + [A[n].size for n in _VECTORS]
    rows = _rup(_rup(sum(sizes), LANE) // LANE, 8)

    def pack(items):
        flat = jnp.concatenate([i.reshape(-1) for i in items])
        return jnp.pad(flat, (0, rows * LANE - flat.shape[0])).reshape(rows, LANE)

    red = _allreduce_small(pack([loss[0, :1]] + [gv[n] for n in _VECTORS]), name="allreduce_vectors")
    zero = jnp.zeros((1,), F32)
    upd = _adamw(pack([zero] + [A[n] for n in _VECTORS]), pack([zero] + [A["m_" + n] for n in _VECTORS]),
                 pack([zero + 1.0] + [A["v_" + n] for n in _VECTORS]), [red], name="adamw_vectors")
    offs = [0]
    for s in sizes:
        offs.append(offs[-1] + s)
    unpack = lambda a, i, n: a.reshape(-1)[offs[i + 1]:offs[i + 2]].reshape(A[n].shape)

    out = {"grad": {}, "delta": {}, "new_m": {}, "new_v": {}}
    for i, n in enumerate(_VECTORS):
        for kind, a in zip(out, upd):
            out[kind][n] = unpack(a, i, n)
    for n, mine, other in zip(_MATRICES, part, sib):
        res = _adamw(A[n][0], A["m_" + n][0], A["v_" + n][0], [mine, other], name="adamw_" + n)
        for kind, a in zip(out, res):
            out[kind][n] = a[None]
    return (red[0, 0], dx[None], *[out[k][n] for k in out for n in _WEIGHTS])
```

```python
import functools

import jax
import jax.numpy as jnp
from jax import lax
from jax.experimental import pallas as pl
from jax.experimental.pallas import tpu as pltpu

F32 = jnp.float32
BF16 = jnp.bfloat16
LANE = 128
HEAD = 64
RMS_EPS = 1e-6
GN_EPS = 64e-5
ADAM_LR, ADAM_B1, ADAM_B2, ADAM_EPS, ADAM_WD, ADAM_STEP = 0.001, 0.9, 0.999, 1e-08, 0.01, 10
VMEM_LIMIT = 56 * 1024 * 1024
NEG = -1e30
MESH = pl.DeviceIdType.MESH
N_CHIPS = 4
N_DEV = 8


def _rup(n, m):
    return -(-n // m) * m


def _pick(n, cands):
    for c in cands:
        if n % c == 0:
            return c
    return n


def _cparams(*sem):
    return pltpu.CompilerParams(dimension_semantics=sem, vmem_limit_bytes=VMEM_LIMIT)


def _mm(a, b, *, ta=False, tb=False, add=None, out_dtype=F32, name):
    M, K = (a.shape[1], a.shape[0]) if ta else a.shape
    N = b.shape[0] if tb else b.shape[1]
    tm = _pick(M, (1024, 512, 256, 128))
    tn = _pick(N, (1024, 512, 640, 256, 128))
    tk = _pick(K, (1024, 512, 640, 256, 128))
    nk = K // tk
    dn = (((0 if ta else 1,), (1 if tb else 0,)), ((), ()))

    def body(*refs):
        if add is None:
            a_ref, b_ref, o_ref, acc = refs
        else:
            a_ref, b_ref, add_ref, o_ref, acc = refs
        ks = pl.program_id(2)

        @pl.when(ks == 0)
        def _():
            acc[...] = jnp.zeros_like(acc)

        acc[...] += lax.dot_general(a_ref[...].astype(BF16), b_ref[...].astype(BF16), dn,
                                    preferred_element_type=F32)

        @pl.when(ks == nk - 1)
        def _():
            res = acc[...]
            if add is not None:
                res = res + add_ref[...].astype(F32)
            o_ref[...] = res.astype(out_dtype)

    a_spec = pl.BlockSpec((tk, tm), lambda i, j, k: (k, i)) if ta else pl.BlockSpec((tm, tk), lambda i, j, k: (i, k))
    b_spec = pl.BlockSpec((tn, tk), lambda i, j, k: (j, k)) if tb else pl.BlockSpec((tk, tn), lambda i, j, k: (k, j))
    o_spec = pl.BlockSpec((tm, tn), lambda i, j, k: (i, j))
    ins, specs = [a, b], [a_spec, b_spec]
    if add is not None:
        ins.append(add)
        specs.append(o_spec)
    return pl.pallas_call(
        body, name=name, grid=(M // tm, N // tn, nk), in_specs=specs, out_specs=o_spec,
        out_shape=jax.ShapeDtypeStruct((M, N), out_dtype),
        scratch_shapes=[pltpu.VMEM((tm, tn), F32)],
        compiler_params=_cparams("parallel", "parallel", "arbitrary"),
    )(*ins)


def _rowwise(fn, rows, consts, outs, accs=(), *, tile, name, prev=(), nxt=()):
    rows = [r if isinstance(r, tuple) else (r, r.shape[1], 0) for r in rows]
    T = rows[0][0].shape[0]
    tile = min(tile, T)
    n = T // tile
    sub = 8
    nr, npv, nnx, ncst, no, na = len(rows), len(prev), len(nxt), len(consts), len(outs), len(accs)

    def body(*refs):
        i = pl.program_id(0)
        it = iter(refs)
        rv = [next(it)[...] for _ in range(nr)]
        pv = [jnp.where(i > 0, next(it)[sub - 1:sub, :], 0.0) for _ in range(npv)]
        nv = [jnp.where(i < n - 1, next(it)[0:1, :], 0.0) for _ in range(nnx)]
        cv = [next(it)[...] for _ in range(ncst)]
        o_refs = [next(it) for _ in range(no)]
        a_refs = [next(it) for _ in range(na)]
        res = fn(*rv, *pv, *nv, *cv)
        if not isinstance(res, (tuple, list)):
            res = (res,)
        for r, o in zip(o_refs, res[:no]):
            r[...] = o.astype(r.dtype)
        if na:
            @pl.when(i == 0)
            def _():
                for r in a_refs:
                    r[...] = jnp.zeros_like(r)
            for r, o in zip(a_refs, res[no:]):
                r[...] += o.astype(F32)

    in_specs = [pl.BlockSpec((tile, w), functools.partial(lambda cb, i: (i, cb), cb)) for _, w, cb in rows]
    in_specs += [pl.BlockSpec((sub, a.shape[1]), lambda i: (jnp.maximum(i * (tile // sub) - 1, 0), 0)) for a in prev]
    in_specs += [pl.BlockSpec((sub, a.shape[1]), lambda i: (jnp.minimum((i + 1) * (tile // sub), T // sub - 1), 0)) for a in nxt]
    in_specs += [pl.BlockSpec(c.shape, lambda i: (0, 0)) for c in consts]
    out_specs = [pl.BlockSpec((tile, c), lambda i: (i, 0)) for c, _ in outs]
    out_specs += [pl.BlockSpec(s, lambda i: (0, 0)) for s in accs]
    out_shape = [jax.ShapeDtypeStruct((T, c), d) for c, d in outs] + [jax.ShapeDtypeStruct(s, F32) for s in accs]
    res = pl.pallas_call(
        body, name=name, grid=(n,), in_specs=in_specs, out_specs=out_specs, out_shape=out_shape,
        compiler_params=_cparams("arbitrary"),
    )(*[r[0] for r in rows], *prev, *nxt, *consts)
    return res


@jax.custom_vjp
def _bdot(a, b):
    return jnp.dot(a.astype(BF16), b.astype(BF16), preferred_element_type=F32)


def _bdot_fwd(a, b):
    return _bdot(a, b), (a.astype(BF16), b.astype(BF16))


def _bdot_bwd(res, ct):
    a, b = res
    c = ct.astype(BF16)
    return (lax.dot_general(c, b, (((1,), (1,)), ((), ())), preferred_element_type=F32),
            lax.dot_general(a, c, (((0,), (0,)), ((), ())), preferred_element_type=F32))


_bdot.defvjp(_bdot_fwd, _bdot_bwd)


def _xdot(a, b):
    return jnp.dot(a, b, precision=lax.Precision.HIGHEST, preferred_element_type=F32)


def _rms(x, g, eps=RMS_EPS):
    return x * lax.rsqrt(jnp.mean(x * x, axis=-1, keepdims=True) + eps) * g


def _softplus(x):
    return jnp.maximum(x, 0.0) + jnp.log(1.0 + jnp.exp(-jnp.abs(x)))


def _sigmoid(x):
    return 1.0 / (1.0 + jnp.exp(-x))


def _seg_mats(width):
    h = lax.broadcasted_iota(jnp.int32, (width, LANE), 0) // HEAD
    j = lax.broadcasted_iota(jnp.int32, (width, LANE), 1)
    seg = (h == j).astype(F32)
    return seg, seg.T


def _prep_rwkv(um, w0, w2, a0, a2, g2, k_k, k_a, seg, segt, *, dims):
    W, DLp, ALp, GLp = dims
    r, k, v = um[:, :W], um[:, W:2 * W], um[:, 2 * W:3 * W]
    o = 3 * W
    xw, xa, xg = um[:, o:o + DLp], um[:, o + DLp:o + DLp + ALp], um[:, o + DLp + ALp:o + DLp + ALp + GLp]
    w_log = -_softplus(-(w0 + _bdot(jnp.tanh(xw), w2))) - 0.5
    decay = jnp.exp(-jnp.exp(w_log))
    a = _sigmoid(a0 + _bdot(xa, a2))
    g = _bdot(_sigmoid(xg), g2)
    kk = k * k_k
    nrm = jnp.sqrt(_xdot(_xdot(kk * kk, seg), segt))
    kk = kk / jnp.maximum(nrm, 1e-12)
    k2 = k * (1.0 + (a - 1.0) * k_a)
    return r, decay, k2, v, kk, kk * a, g


def _shift_mix(u, uprev, mu):
    first = lax.broadcasted_iota(jnp.int32, u.shape, 0) == 0
    sh = jnp.where(first, uprev, pltpu.roll(u, 1, 0))
    return u + (sh - u) * mu, sh


def _post_rwkv(y, r, k2, v, g, lnx_g, lnx_b, r_k, seg, segt):
    inv = 1.0 / HEAD
    mean = _xdot(_xdot(y, seg), segt) * inv
    yc = y - mean
    var = _xdot(_xdot(yc * yc, seg), segt) * inv
    yn = yc * lax.rsqrt(var + GN_EPS) * lnx_g + lnx_b
    bonus = _xdot(_xdot(r * k2 * r_k, seg), segt) * v
    return (yn + bonus) * g


def _prep_fox(uf, qg, kg, fb, seg, segt, *, dims):
    FW, FHp = dims
    q, k, v, f = uf[:, :FW], uf[:, FW:2 * FW], uf[:, 2 * FW:3 * FW], uf[:, 3 * FW:3 * FW + FHp]
    inv = 1.0 / HEAD
    qn = q * lax.rsqrt(_xdot(_xdot(q * q, seg), segt) * inv + RMS_EPS) * qg * (HEAD ** -0.5)
    kn = k * lax.rsqrt(_xdot(_xdot(k * k, seg), segt) * inv + RMS_EPS) * kg
    return qn, kn, v, -_softplus(-(f + fb))


def _tail(h2, pe, z, png, pgb):
    return h2 + _sigmoid(z + pgb) * _rms(pe, png)


def _swiglu(gt, up):
    return gt * _sigmoid(gt) * up


def _cumsum(x, *, reverse, name):
    T, C = x.shape
    tc = _pick(T, (256, 128))
    n = T // tc
    i0 = lax.broadcasted_iota(jnp.int32, (tc, tc), 0)
    i1 = lax.broadcasted_iota(jnp.int32, (tc, tc), 1)
    tri = ((i0 <= i1) if reverse else (i0 >= i1)).astype(BF16)

    def body(x_ref, tri_ref, o_ref, carry):
        i = pl.program_id(0)

        @pl.when(i == 0)
        def _():
            carry[...] = jnp.zeros_like(carry)

        v = x_ref[...]
        hi = v.astype(BF16)
        r1 = v - hi.astype(F32)
        mid = r1.astype(BF16)
        lo = (r1 - mid.astype(F32)).astype(BF16)
        t = tri_ref[...]
        d = lambda p: jnp.dot(t, p, preferred_element_type=F32)
        c = d(hi) + d(mid) + d(lo) + carry[0:1, :]
        o_ref[...] = c
        edge = c[0:1, :] if reverse else c[tc - 1:tc, :]
        carry[...] = jnp.broadcast_to(edge, carry.shape)

    blk = pl.BlockSpec((tc, C), (lambda i: (n - 1 - i, 0)) if reverse else (lambda i: (i, 0)))
    return pl.pallas_call(
        body, name=name, grid=(n,), in_specs=[blk, pl.BlockSpec((tc, tc), lambda i: (0, 0))], out_specs=blk,
        out_shape=jax.ShapeDtypeStruct((T, C), F32), scratch_shapes=[pltpu.VMEM((8, C), F32)],
        compiler_params=_cparams("arbitrary"),
    )(x, tri)


BWD_HEADS_PER_TRIP = 4
FWD_STEPS_PER_TRIP = 2
RECOMPUTE_STEPS_PER_TRIP = 4


def _steps(n, per_trip, step):
    def trip(i, carry):
        for j in range(per_trip):
            carry = step(i * per_trip + j, carry)
        return carry

    lax.fori_loop(0, n // per_trip, trip, 0)


def _col(tile, lane, t):
    return jnp.sum(jnp.where(lane == t, tile, 0.0), axis=1, keepdims=True)


def _scan_fwd(r, w, k, kk, kka, vT, *, hg, tc):
    H, T, N = r.shape
    nc = T // tc

    def body(r_ref, w_ref, k_ref, kk_ref, kka_ref, vT_ref, yT_ref, chk_ref, s_ref):
        @pl.when(pl.program_id(1) == 0)
        def _():
            s_ref[...] = jnp.zeros_like(s_ref)

        chk_ref[:, 0] = s_ref[...]
        yT_ref[...] = jnp.zeros_like(yT_ref)
        lane = lax.broadcasted_iota(jnp.int32, (N, tc), 1)

        def emit_y(S, h, t):
            y = jnp.sum(S * r_ref[h, pl.ds(jnp.maximum(t, 0), 1), :], axis=1, keepdims=True)
            yT_ref[h] = jnp.where(lane == t, y, yT_ref[h])

        def step(t, carry):
            for h in range(hg):
                row = lambda ref: ref[h, pl.ds(t, 1), :]
                S = s_ref[h]
                emit_y(S, h, t - 1)
                vcol = _col(vT_ref[h], lane, t)
                sa = -jnp.sum(S * row(kk_ref), axis=1, keepdims=True)
                s_ref[h] = S * row(w_ref) + sa * row(kka_ref) + vcol * row(k_ref)
            return carry

        _steps(tc, FWD_STEPS_PER_TRIP, step)
        for h in range(hg):
            emit_y(s_ref[h], h, tc - 1)

    rows = pl.BlockSpec((hg, tc, N), lambda g, c: (g, c, 0))
    cols = pl.BlockSpec((hg, N, tc), lambda g, c: (g, 0, c))
    return pl.pallas_call(
        body, name="scan_fwd", grid=(H // hg, nc), in_specs=[rows] * 5 + [cols],
        out_specs=[cols, pl.BlockSpec((hg, 1, N, N), lambda g, c: (g, c, 0, 0))],
        out_shape=[jax.ShapeDtypeStruct((H, N, T), F32), jax.ShapeDtypeStruct((H, nc, N, N), F32)],
        scratch_shapes=[pltpu.VMEM((hg, N, N), F32)],
        compiler_params=_cparams("arbitrary", "arbitrary"),
    )(r, w, k, kk, kka, vT)


def _scan_bwd(r, w, k, kk, kka, vT, dyT, chk, *, hg, tc):
    H, T, N = r.shape
    nc = T // tc

    def body(r_ref, w_ref, k_ref, kk_ref, kka_ref, vT_ref, dyT_ref, chk_ref,
             dr_ref, dw_ref, dk_ref, dkk_ref, dkka_ref, dvT_ref, sp_ref, ds_ref):
        @pl.when(pl.program_id(1) == 0)
        def _():
            ds_ref[...] = jnp.zeros_like(ds_ref)

        dvT_ref[...] = jnp.zeros_like(dvT_ref)
        lane = lax.broadcasted_iota(jnp.int32, (N, tc), 1)

        for h in range(hg):
            sp_ref[h, 0] = chk_ref[h, 0]

        def fstep(t, carry):
            for h in range(hg):
                row = lambda ref: ref[h, pl.ds(t, 1), :]
                S = sp_ref[h, t]
                vcol = _col(vT_ref[h], lane, t)
                sa = -jnp.sum(S * row(kk_ref), axis=1, keepdims=True)
                sp_ref[h, t + 1] = S * row(w_ref) + sa * row(kka_ref) + vcol * row(k_ref)
            return carry

        _steps(tc, RECOMPUTE_STEPS_PER_TRIP, fstep)

        def bstep(h0, i, carry):
            t = tc - 1 - i
            for h in range(h0, min(h0 + BWD_HEADS_PER_TRIP, hg)):
                row = lambda ref: ref[h, pl.ds(t, 1), :]
                rr, wr, kr, kkr, kkar = row(r_ref), row(w_ref), row(k_ref), row(kk_ref), row(kka_ref)
                Sp = sp_ref[h, t]
                Sn = sp_ref[h, t + 1]
                dycol = _col(dyT_ref[h], lane, t)
                vcol = _col(vT_ref[h], lane, t)
                dS = ds_ref[h]
                dSn = dS + dycol * rr
                dsa = jnp.sum(dS * kkar, axis=1, keepdims=True) + dycol * jnp.sum(rr * kkar, axis=1, keepdims=True)
                dr_ref[h, pl.ds(t, 1), :] = jnp.sum(Sn * dycol, axis=0, keepdims=True)
                sa = -jnp.sum(Sp * kkr, axis=1, keepdims=True)
                dw_ref[h, pl.ds(t, 1), :] = jnp.sum(dSn * Sp, axis=0, keepdims=True)
                dkka_ref[h, pl.ds(t, 1), :] = jnp.sum(dSn * sa, axis=0, keepdims=True)
                dvcol = jnp.sum(dSn * kr, axis=1, keepdims=True)
                dk_ref[h, pl.ds(t, 1), :] = jnp.sum(dSn * vcol, axis=0, keepdims=True)
                dkk_ref[h, pl.ds(t, 1), :] = -jnp.sum(Sp * dsa, axis=0, keepdims=True)
                ds_ref[h] = dSn * wr - dsa * kkr
                dvT_ref[h] = jnp.where(lane == t, dvcol, dvT_ref[h])
            return carry

        for h0 in range(0, hg, BWD_HEADS_PER_TRIP):
            lax.fori_loop(0, tc, functools.partial(bstep, h0), 0)

    rows = pl.BlockSpec((hg, tc, N), lambda g, c: (g, nc - 1 - c, 0))
    cols = pl.BlockSpec((hg, N, tc), lambda g, c: (g, 0, nc - 1 - c))
    return pl.pallas_call(
        body, name="scan_bwd", grid=(H // hg, nc),
        in_specs=[rows] * 5 + [cols, cols, pl.BlockSpec((hg, 1, N, N), lambda g, c: (g, nc - 1 - c, 0, 0))],
        out_specs=[rows] * 5 + [cols],
        out_shape=[jax.ShapeDtypeStruct((H, T, N), F32)] * 5 + [jax.ShapeDtypeStruct((H, N, T), F32)],
        scratch_shapes=[pltpu.VMEM((hg, tc + 1, N, N), F32), pltpu.VMEM((hg, N, N), F32)],
        compiler_params=_cparams("arbitrary", "arbitrary"),
    )(r, w, k, kk, kka, vT, dyT, chk)


_NT = (((1,), (1,)), ((), ()))
_TN = (((0,), (0,)), ((), ()))


def _scores(q, k, cc, cr, masked):
    s = lax.dot_general(q, k, _NT, preferred_element_type=F32) + cc - cr
    if masked:
        tb = s.shape[0]
        keep = lax.broadcasted_iota(jnp.int32, (tb, tb), 0) >= lax.broadcasted_iota(jnp.int32, (tb, tb), 1)
        s = jnp.where(keep, s, NEG)
    return s


def _attn_specs(T, N, tb):
    blk = pl.BlockSpec((1, tb, N), lambda h, i: (h, i, 0))
    whole = pl.BlockSpec((1, T, N), lambda h, i: (h, 0, 0))
    col = pl.BlockSpec((1, tb, 1), lambda h, i: (h, i, 0))
    wcol = pl.BlockSpec((1, T, 1), lambda h, i: (h, 0, 0))
    row = pl.BlockSpec((1, 1, tb), lambda h, i: (h, 0, i))
    wrow = pl.BlockSpec((1, 1, T), lambda h, i: (h, 0, 0))
    return blk, whole, col, wcol, row, wrow


def _call_carrying(body, name, grid, in_specs, out_specs, out_shape, scratch, args, comm):
    n_out = len(out_specs)
    if comm is not None:
        n = len(comm[0])
        body = _carrying(body, len(in_specs), n_out, grid, comm)
        in_specs, out_specs = in_specs + [_HBM] * n, out_specs + [_HBM] * n
        out_shape, scratch, args = out_shape + _exchange_shapes(*comm), scratch + _exchange_sems(n), args + list(comm[0])
    res = pl.pallas_call(
        body, name=name, grid=grid, in_specs=in_specs, out_specs=out_specs, out_shape=out_shape,
        scratch_shapes=scratch, compiler_params=_cparams(*["arbitrary"] * len(grid)),
    )(*args)
    return res[:n_out], res[n_out:]


def _attn_fwd(q, k, v, ccol, crow, *, tb, comm=None):
    H, T, N = q.shape

    def body(q_ref, k_ref, v_ref, cc_ref, cr_ref, o_ref, lse_ref, m_s, l_s, acc_s):
        qi = pl.program_id(1)
        m_s[...] = jnp.full_like(m_s, NEG)
        l_s[...] = jnp.zeros_like(l_s)
        acc_s[...] = jnp.zeros_like(acc_s)
        q_, cc = q_ref[0], cc_ref[0]

        def block(j, masked):
            at = pl.ds(pl.multiple_of(j * tb, tb), tb)
            s = _scores(q_, k_ref[0, at, :], cc, cr_ref[0, :, at], masked)
            m_new = jnp.maximum(m_s[...], jnp.max(s, axis=1, keepdims=True))
            p = jnp.exp(s - m_new)
            alpha = jnp.exp(m_s[...] - m_new)
            l_s[...] = alpha * l_s[...] + jnp.sum(p, axis=1, keepdims=True)
            acc_s[...] = alpha * acc_s[...] + jnp.dot(p.astype(BF16), v_ref[0, at, :], preferred_element_type=F32)
            m_s[...] = m_new

        def below(j, carry):
            block(j, False)
            return carry

        lax.fori_loop(0, qi, below, 0)
        block(qi, True)
        o_ref[0] = acc_s[...] / l_s[...]
        lse_ref[0] = m_s[...] + jnp.log(l_s[...])

    blk, whole, col, wcol, row, wrow = _attn_specs(T, N, tb)
    return _call_carrying(
        body, "fox_fwd", (H, T // tb), [blk, whole, whole, col, wrow], [blk, col],
        [jax.ShapeDtypeStruct((H, T, N), F32), jax.ShapeDtypeStruct((H, T, 1), F32)],
        [pltpu.VMEM((tb, 1), F32), pltpu.VMEM((tb, 1), F32), pltpu.VMEM((tb, N), F32)],
        [q, k, v, ccol, crow], comm)


def _attn_bwd_q(q, k, v, ccol, crow, o, lse, do, *, tb, comm=None):
    H, T, N = q.shape

    def body(q_ref, k_ref, v_ref, cc_ref, cr_ref, o_ref, lse_ref, do_ref, dq_ref, dc_ref, dl_ref, dq_s, dc_s):
        qi = pl.program_id(1)
        dq_s[...] = jnp.zeros_like(dq_s)
        dc_s[...] = jnp.zeros_like(dc_s)
        q_, cc, lse_, do_ = q_ref[0], cc_ref[0], lse_ref[0], do_ref[0]
        delta = jnp.sum(do_ * o_ref[0], axis=1, keepdims=True)
        dl_ref[0] = delta
        dob = do_.astype(BF16)

        def block(j, masked):
            at = pl.ds(pl.multiple_of(j * tb, tb), tb)
            kb = k_ref[0, at, :]
            p = jnp.exp(_scores(q_, kb, cc, cr_ref[0, :, at], masked) - lse_)
            dp = lax.dot_general(dob, v_ref[0, at, :], _NT, preferred_element_type=F32)
            ds = p * (dp - delta)
            dq_s[...] += jnp.dot(ds.astype(BF16), kb, preferred_element_type=F32)
            dc_s[...] += jnp.sum(ds, axis=1, keepdims=True)

        def below(j, carry):
            block(j, False)
            return carry

        lax.fori_loop(0, qi, below, 0)
        block(qi, True)
        dq_ref[0] = dq_s[...]
        dc_ref[0] = dc_s[...]

    blk, whole, col, wcol, row, wrow = _attn_specs(T, N, tb)
    return _call_carrying(
        body, "fox_bwd_q", (H, T // tb), [blk, whole, whole, col, wrow, blk, col, blk], [blk, col, col],
        [jax.ShapeDtypeStruct((H, T, N), F32)] + [jax.ShapeDtypeStruct((H, T, 1), F32)] * 2,
        [pltpu.VMEM((tb, N), F32), pltpu.VMEM((tb, 1), F32)],
        [q, k, v, ccol, crow, o, lse, do], comm)


def _attn_bwd_kv(q, k, v, ccol, crow, lse, delta, do, *, tb, comm=None):
    H, T, N = q.shape
    nb = T // tb

    def body(q_ref, k_ref, v_ref, cc_ref, cr_ref, lse_ref, dl_ref, do_ref, dk_ref, dv_ref, dc_ref, dk_s, dv_s, dc_s):
        ki = pl.program_id(1)
        dk_s[...] = jnp.zeros_like(dk_s)
        dv_s[...] = jnp.zeros_like(dv_s)
        dc_s[...] = jnp.zeros_like(dc_s)
        kb, vb, cr = k_ref[0], v_ref[0], cr_ref[0]

        def block(j, masked):
            at = pl.ds(pl.multiple_of(j * tb, tb), tb)
            qb, dob = q_ref[0, at, :], do_ref[0, at, :]
            p = jnp.exp(_scores(qb, kb, cc_ref[0, at, :], cr, masked) - lse_ref[0, at, :])
            dp = lax.dot_general(dob, vb, _NT, preferred_element_type=F32)
            ds = p * (dp - dl_ref[0, at, :])
            dv_s[...] += lax.dot_general(p.astype(BF16), dob, _TN, preferred_element_type=F32)
            dk_s[...] += lax.dot_general(ds.astype(BF16), qb, _TN, preferred_element_type=F32)
            dc_s[...] += jnp.sum(ds, axis=0, keepdims=True)

        def above(j, carry):
            block(j, False)
            return carry

        block(ki, True)
        lax.fori_loop(ki + 1, nb, above, 0)
        dk_ref[0] = dk_s[...]
        dv_ref[0] = dv_s[...]
        dc_ref[0] = dc_s[...]

    blk, whole, col, wcol, row, wrow = _attn_specs(T, N, tb)
    return _call_carrying(
        body, "fox_bwd_kv", (H, nb), [whole, blk, blk, wcol, row, wcol, wcol, whole], [blk, blk, row],
        [jax.ShapeDtypeStruct((H, T, N), F32)] * 2 + [jax.ShapeDtypeStruct((H, 1, T), F32)],
        [pltpu.VMEM((tb, N), F32), pltpu.VMEM((tb, N), F32), pltpu.VMEM((1, tb), F32)],
        [q, k, v, ccol, crow, lse, delta, do], comm)


def _heads(x):
    T = x.shape[0]
    return x.reshape(T, -1, HEAD).transpose(1, 0, 2)


def _headsT(x):
    T = x.shape[0]
    return x.reshape(T, -1, HEAD).transpose(1, 2, 0)


def _unheads(x):
    return x.transpose(1, 0, 2).reshape(x.shape[1], -1)


def _unheadsT(x):
    return x.transpose(2, 0, 1).reshape(x.shape[2], -1)


def _padc(x, n):
    return jnp.pad(x, ((0, 0), (0, n - x.shape[1])))


def _padr(x, n):
    return jnp.pad(x, ((0, n - x.shape[0]), (0, 0)))


class _Dims:
    def __init__(self, W, DL, AL, GL, FH):
        self.W, self.DL, self.AL, self.GL, self.FH = W, DL, AL, GL, FH
        self.DLp, self.ALp, self.GLp, self.FHp = _rup(DL, LANE), _rup(AL, LANE), _rup(GL, LANE), _rup(FH, LANE)
        self.FW = FH * HEAD
        self.RC = 3 * W + DL + AL + GL
        self.RP = 3 * W + self.DLp + self.ALp + self.GLp
        self.FC = 3 * self.FW + FH
        self.FP = 3 * self.FW + self.FHp

    def pad_r(self, a):
        W, o = self.W, 3 * self.W
        return jnp.concatenate([a[:, :o], _padc(a[:, o:o + self.DL], self.DLp),
                                _padc(a[:, o + self.DL:o + self.DL + self.AL], self.ALp),
                                _padc(a[:, o + self.DL + self.AL:self.RC], self.GLp)], axis=1)

    def unpad_r(self, a):
        o = 3 * self.W
        return jnp.concatenate([a[:, :o], a[:, o:o + self.DL], a[:, o + self.DLp:o + self.DLp + self.AL],
                                a[:, o + self.DLp + self.ALp:o + self.DLp + self.ALp + self.GL]], axis=1)

    def pad_f(self, a):
        return _padc(a, self.FP)

    def unpad_f(self, a):
        return a[:, :self.FC]


def _local_step(x, p, tgt, Wt, vec, d, late_shards=None):
    Wt = dict(Wt)
    T, D = x.shape
    W, FW = d.W, d.FW
    H = W // HEAD
    seg, segt = _seg_mats(W)
    segf, segft = _seg_mats(FW)
    T1 = 256
    rk_flat = vec["r_k"].reshape(1, W)
    mu = d.pad_r(vec["shift_mu"])
    qg = jnp.tile(vec["q_norm_g"], (1, d.FH))
    kg = jnp.tile(vec["k_norm_g"], (1, d.FH))
    fb = _padc(vec["fgate_b"], d.FHp)
    pdims = (W, d.DLp, d.ALp, d.GLp)
    fdims = (FW, d.FHp)

    (xn,) = _rowwise(lambda x_, g_: _rms(x_, g_), [x], [vec["attn_norm_g"]], [(D, BF16)], tile=T1, name="norm_attn")
    u_r = _mm(xn, Wt["w_in_r"], name="mm_in_r")
    u_f = _mm(xn, Wt["w_in_f"], name="mm_in_f")

    prep_consts = [mu, vec["w0"], Wt["w2"].astype(F32), vec["a0"], Wt["a2"].astype(F32), Wt["g2"].astype(F32), vec["k_k"], vec["k_a"], seg, segt]

    def prep_fwd(u_, up_, mu_, *cs):
        um, _ = _shift_mix(u_, up_, mu_)
        return _prep_rwkv(um, *cs, dims=pdims)

    r, dec, k2, v, kk, kka, g = _rowwise(prep_fwd, [u_r], prep_consts, [(W, F32)] * 7, tile=128, name="rwkv_prep", prev=[u_r])
    hg, tc = min(16, H), 128
    yT, chk = _scan_fwd(_heads(r), _heads(dec), _heads(k2), _heads(kk), _heads(kka), _headsT(v), hg=hg, tc=tc)
    y = _unheadsT(yT)
    post_consts = [vec["lnx_g"], vec["lnx_b"], rk_flat, seg, segt]
    (y_r,) = _rowwise(_post_rwkv, [y, r, k2, v, g], post_consts, [(W, BF16)], tile=T1, name="rwkv_post")

    fox_consts = [qg, kg, fb, segf, segft]
    qn, kn, vf, logf = _rowwise(functools.partial(_prep_fox, dims=fdims), [u_f], fox_consts,
                                [(FW, BF16), (FW, BF16), (FW, BF16), (d.FHp, F32)], tile=T1, name="fox_prep")
    c = _cumsum(logf, reverse=False, name="fox_cumsum")
    cT = c[:, :d.FH].T
    ccol, crow = cT[:, :, None], cT[:, None, :]
    tb = _pick(T, (512, 256, 128))
    qh, kh, vh = _heads(qn), _heads(kn), _heads(vf)
    (o, lse), gathered = _attn_fwd(qh, kh, vh, ccol, crow, tb=tb, comm=late_shards and (late_shards, True))
    Wt.update({n: _whole(n, g) for n, g in zip(_LATE, gathered)})
    y_f = _unheads(o)

    ycat = jnp.concatenate([y_r, y_f.astype(BF16)], axis=1)
    h1 = _mm(ycat, Wt["w_out"], add=x, name="mm_out")
    (hn,) = _rowwise(lambda h_, g_: _rms(h_, g_), [h1], [vec["ffn_norm_g"]], [(D, BF16)], tile=T1, name="norm_ffn")
    gt = _mm(hn, Wt["w_gate"], name="mm_gate")
    up = _mm(hn, Wt["w_up"], name="mm_up")
    (act,) = _rowwise(_swiglu, [gt, up], [], [(gt.shape[1], BF16)], tile=T1, name="swiglu")
    h2 = _mm(act, Wt["w_down"], add=h1, name="mm_down")
    (hg_,) = _rowwise(lambda h_, g_: _rms(h_, g_), [h2], [vec["ple_gate_norm_g"]], [(D, BF16)], tile=T1, name="norm_gate")
    pe = _mm(p, Wt["ple_proj"], name="mm_ple")
    z = _mm(hg_, Wt["ple_gate_w"], name="mm_pgate")

    def tail(h2_, pe_, z_, tg_, png_, pgb_):
        h3, f = jax.vjp(_tail, h2_, pe_, z_, png_, pgb_)
        err = h3 - tg_
        dh3 = err * (1.0 / D)
        lt = 0.5 * jnp.sum(jnp.sum(err * err, axis=1, keepdims=True) * (1.0 / D), axis=0, keepdims=True)
        dh2_, dpe_, dz_, dpng_, dpgb_ = f(dh3)
        return dh2_, dpe_, dz_, jnp.broadcast_to(lt, (1, LANE)), dpng_, dpgb_

    dh3, dpe, dz, loss, g_png, g_pgb = _rowwise(
        tail, [h2, pe, z, tgt], [vec["ple_norm_g"], vec["ple_gate_b"]], [(D, F32), (D, BF16), (D, BF16)],
        [(1, LANE), (1, D), (1, D)], tile=T1, name="tail")
    G = {}
    gv = {"ple_norm_g": g_png, "ple_gate_b": g_pgb}
    G["ple_gate_w"] = _mm(hg_, dz, ta=True, out_dtype=BF16, name="gw_pgate")
    G["ple_proj"] = _mm(p, dpe, ta=True, out_dtype=BF16, name="gw_ple")
    d_hg = _mm(dz, Wt["ple_gate_w"], tb=True, name="mmb_pgate")

    def norm_bwd(h_, dres_, dn_, g_):
        _, f = jax.vjp(_rms, h_, g_)
        dh_, dg_ = f(dn_)
        return dres_ + dh_, dg_

    dh2, gv["ple_gate_norm_g"] = _rowwise(norm_bwd, [h2, dh3, d_hg], [vec["ple_gate_norm_g"]], [(D, F32)], [(1, D)],
                                          tile=T1, name="norm_gate_bwd")
    G["w_down"] = _mm(act, dh2, ta=True, out_dtype=BF16, name="gw_down")
    d_act = _mm(dh2, Wt["w_down"], tb=True, name="mmb_down")

    def swiglu_bwd(gt_, up_, da_):
        _, f = jax.vjp(_swiglu, gt_, up_)
        return f(da_)

    d_gt, d_up = _rowwise(swiglu_bwd, [gt, up, d_act], [], [(gt.shape[1], BF16)] * 2, tile=T1, name="swiglu_bwd")
    G["w_gate"] = _mm(hn, d_gt, ta=True, out_dtype=BF16, name="gw_gate")
    G["w_up"] = _mm(hn, d_up, ta=True, out_dtype=BF16, name="gw_up")
    d_hn = _mm(d_gt, Wt["w_gate"], tb=True, name="mmb_gate")
    d_hn = _mm(d_up, Wt["w_up"], tb=True, add=d_hn, name="mmb_up")
    dh1, gv["ffn_norm_g"] = _rowwise(norm_bwd, [h1, dh2, d_hn], [vec["ffn_norm_g"]], [(D, F32)], [(1, D)],
                                     tile=T1, name="norm_ffn_bwd")
    G["w_out"] = _mm(ycat, dh1, ta=True, out_dtype=BF16, name="gw_out")
    d_ycat = _mm(dh1, Wt["w_out"], tb=True, name="mmb_out")

    def post_bwd(y_, r_, k2_, v_, g_, dy_, *cs):
        lg, lb, rk, sg, sgt = cs
        _, f = jax.vjp(lambda *a: _post_rwkv(*a, sg, sgt), y_, r_, k2_, v_, g_, lg, lb, rk)
        return f(dy_)

    dy, dr1, dk1, dv1, dg, gv["lnx_g"], gv["lnx_b"], g_rk = _rowwise(
        post_bwd, [y, r, k2, v, g, (d_ycat, W, 0)], post_consts, [(W, F32)] * 5, [(1, W)] * 3, tile=128, name="rwkv_post_bwd")
    gv["r_k"] = g_rk.reshape(H, HEAD)
    dr, ddec, dk2, dkk, dkka, dvT = _scan_bwd(_heads(r), _heads(dec), _heads(k2), _heads(kk), _heads(kka), _headsT(v),
                                              _headsT(dy), chk, hg=min(8, H), tc=tc)
    dr, ddec, dk2, dkk, dkka, dv = (_unheads(dr), _unheads(ddec), _unheads(dk2), _unheads(dkk), _unheads(dkka), _unheadsT(dvT))

    def prep_bwd(u_, dr_, dr1_, ddec_, dk2_, dk1_, dv_, dv1_, dkk_, dkka_, dg_, up_, mu_, *cs):
        um, sh = _shift_mix(u_, up_, mu_)
        cs_d, sg, sgt = cs[:7], cs[7], cs[8]
        _, f = jax.vjp(lambda um_, *c_: _prep_rwkv(um_, *c_, sg, sgt, dims=pdims), um, *cs_d)
        res = f((dr_ + dr1_, ddec_, dk2_ + dk1_, dv_ + dv1_, dkk_, dkka_, dg_))
        dum = res[0]
        dmu = jnp.sum(dum * (sh - u_), axis=0, keepdims=True)
        return (dum, dmu) + tuple(res[1:])

    LP = [Wt["w2"].shape, Wt["a2"].shape, Wt["g2"].shape]
    dum, g_mu, gv["w0"], g_w2, gv["a0"], g_a2, g_g2, gv["k_k"], gv["k_a"] = _rowwise(
        prep_bwd, [u_r, dr, dr1, ddec, dk2, dk1, dv, dv1, dkk, dkka, dg], prep_consts, [(d.RP, F32)],
        [(1, d.RP), (1, W), LP[0], (1, W), LP[1], LP[2], (1, W), (1, W)], tile=128, name="rwkv_prep_bwd", prev=[u_r])
    gv["shift_mu"] = d.unpad_r(g_mu)
    G["w2"], G["a2"], G["g2"] = g_w2, g_a2, g_g2
    (du_r,) = _rowwise(lambda a_, an_, mu_: a_ * (1.0 - mu_) + jnp.where(
        lax.broadcasted_iota(jnp.int32, a_.shape, 0) == a_.shape[0] - 1, an_, pltpu.roll(a_, a_.shape[0] - 1, 0)) * mu_,
        [dum], [mu], [(d.RP, BF16)], tile=T1, name="shift_bwd", nxt=[dum])

    do = _heads(d_ycat[:, W:])
    pieces = lambda names: late_shards and ([_pieces(n, G[n]).astype(BF16) for n in names], False)
    (dq, dcq, delta), recv_q = _attn_bwd_q(qh, kh, vh, ccol, crow, o, lse, do, tb=tb, comm=pieces(_CARRIED_BY_Q))
    (dk_, dv_, dck), recv_kv = _attn_bwd_kv(qh, kh, vh, ccol, crow, lse, delta, do.astype(BF16), tb=tb,
                                           comm=pieces(_CARRIED_BY_KV))
    recv = dict(zip(_CARRIED_BY_Q + _CARRIED_BY_KV, tuple(recv_q) + tuple(recv_kv)))
    dc = _padc((dcq[:, :, 0] - dck[:, 0, :]).T, d.FHp)
    dlogf = _cumsum(dc, reverse=True, name="fox_cumsum_bwd")

    def fox_bwd(uf_, dq_, dk__, dv__, dlf_, *cs):
        qg_, kg_, fb_, sg, sgt = cs
        _, f = jax.vjp(lambda uf__, a, b, c_: _prep_fox(uf__, a, b, c_, sg, sgt, dims=fdims), uf_, qg_, kg_, fb_)
        return f((dq_, dk__, dv__, dlf_))

    du_f, g_qg, g_kg, g_fb = _rowwise(fox_bwd, [u_f, _unheads(dq), _unheads(dk_), _unheads(dv_), dlogf], fox_consts,
                                      [(d.FP, BF16)], [(1, FW), (1, FW), (1, d.FHp)], tile=T1, name="fox_prep_bwd")
    gv["q_norm_g"] = g_qg.reshape(d.FH, HEAD).sum(0, keepdims=True)
    gv["k_norm_g"] = g_kg.reshape(d.FH, HEAD).sum(0, keepdims=True)
    gv["fgate_b"] = g_fb[:, :d.FH]

    G["w_in_r"] = _mm(xn, du_r, ta=True, out_dtype=BF16, name="gw_in_r")
    G["w_in_f"] = _mm(xn, du_f, ta=True, out_dtype=BF16, name="gw_in_f")
    d_xn = _mm(du_r, Wt["w_in_r"], tb=True, name="mmb_in_r")
    d_xn = _mm(du_f, Wt["w_in_f"], tb=True, add=d_xn, name="mmb_in_f")
    dx, gv["attn_norm_g"] = _rowwise(norm_bwd, [x, dh1, d_xn], [vec["attn_norm_g"]], [(D, F32)], [(1, D)],
                                     tile=T1, name="norm_attn_bwd")
    return loss, dx, G, gv, recv


_HBM = pl.BlockSpec(memory_space=pl.ANY)
_OTHER_CHIPS = ((0, 1), (1, 0), (1, 1))


def _flip(v, bit):
    return 1 - v if bit else v


def _exchange(arrs, *, gather, name):
    n = len(arrs)

    def body(*refs):
        copies = _exchange_copies(refs[:n], refs[n:2 * n], refs[2 * n:], gather)
        for cp in copies:
            cp.start()
        for cp in copies:
            cp.wait()

    return pl.pallas_call(
        body, name=name, in_specs=[_HBM] * n, out_specs=[_HBM] * n, out_shape=_exchange_shapes(arrs, gather),
        scratch_shapes=_exchange_sems(n),
    )(*arrs)


def _exchange_shapes(arrs, gather):
    return [jax.ShapeDtypeStruct(((N_CHIPS,) + a.shape) if gather else a.shape, a.dtype) for a in arrs]


def _exchange_sems(n):
    return [pltpu.SemaphoreType.DMA((3 * n,)), pltpu.SemaphoreType.DMA((3 * n,)), pltpu.SemaphoreType.DMA((n,))]


def _exchange_copies(ins, outs, sems, gather):
    send_sems, recv_sems, own_sems = sems
    x, y, c = lax.axis_index("x"), lax.axis_index("y"), lax.axis_index("c")
    me = 2 * x + y
    copies = []
    for a in range(len(ins)):
        copies.append(pltpu.make_async_copy(ins[a] if gather else ins[a].at[me], outs[a].at[me], own_sems.at[a]))
        for k, (dx, dy) in enumerate(_OTHER_CHIPS):
            px, py = _flip(x, dx), _flip(y, dy)
            copies.append(pltpu.make_async_remote_copy(
                src_ref=ins[a] if gather else ins[a].at[2 * px + py], dst_ref=outs[a].at[me],
                send_sem=send_sems.at[3 * a + k], recv_sem=recv_sems.at[3 * a + k],
                device_id=(px, py, c), device_id_type=MESH))
    return copies


def _carrying(body, n_in, n_out, grid, comm):
    arrs, gather = comm
    n = len(arrs)

    def wrapped(*refs):
        c_in = refs[n_in:n_in + n]
        c_out = refs[n_in + n + n_out:n_in + 2 * n + n_out]
        ids = [pl.program_id(a) for a in range(len(grid))]
        first = functools.reduce(jnp.logical_and, [i == 0 for i in ids])
        last = functools.reduce(jnp.logical_and, [i == g - 1 for i, g in zip(ids, grid)])

        @pl.when(first)
        def _():
            for cp in _exchange_copies(c_in, c_out, refs[-3:], gather):
                cp.start()

        body(*refs[:n_in], *refs[n_in + n:n_in + n + n_out], *refs[n_in + 2 * n + n_out:-3])

        @pl.when(last)
        def _():
            for cp in _exchange_copies(c_in, c_out, refs[-3:], gather):
                cp.wait()

    return wrapped


def _swap_cores(arrs, *, name):
    n = len(arrs)

    def body(*refs):
        ins, outs = refs[:n], refs[n:2 * n]
        send_sems, recv_sems = refs[2 * n:]
        peer = (lax.axis_index("x"), lax.axis_index("y"), 1 - lax.axis_index("c"))
        cps = [pltpu.make_async_remote_copy(src_ref=ins[a], dst_ref=outs[a], send_sem=send_sems.at[a],
                                            recv_sem=recv_sems.at[a], device_id=peer, device_id_type=MESH) for a in range(n)]
        for cp in cps:
            cp.start()
        for cp in cps:
            cp.wait()

    return pl.pallas_call(
        body, name=name, in_specs=[_HBM] * n, out_specs=[_HBM] * n,
        out_shape=[jax.ShapeDtypeStruct(a.shape, a.dtype) for a in arrs],
        scratch_shapes=[pltpu.SemaphoreType.DMA((n,)), pltpu.SemaphoreType.DMA((n,))],
    )(*arrs)


def _allreduce_small(pack, *, name):
    R, C = pack.shape

    def body(p_ref, o_ref, recv, send_sems, recv_sems):
        x, y, c = lax.axis_index("x"), lax.axis_index("y"), lax.axis_index("c")
        me = 4 * x + 2 * y + c
        recv[me] = p_ref[...]
        cps = []
        for k in range(1, N_DEV):
            peer = (_flip(x, k & 4), _flip(y, k & 2), _flip(c, k & 1))
            cp = pltpu.make_async_remote_copy(src_ref=p_ref, dst_ref=recv.at[me], send_sem=send_sems.at[k - 1],
                                              recv_sem=recv_sems.at[k - 1], device_id=peer, device_id_type=MESH)
            cp.start()
            cps.append(cp)
        for cp in cps:
            cp.wait()
        acc = recv[0]
        for s in range(1, N_DEV):
            acc = acc + recv[s]
        o_ref[...] = acc

    vm = pl.BlockSpec(memory_space=pltpu.VMEM)
    return pl.pallas_call(
        body, name=name, in_specs=[vm], out_specs=vm, out_shape=jax.ShapeDtypeStruct((R, C), F32),
        scratch_shapes=[pltpu.VMEM((N_DEV, R, C), F32), pltpu.SemaphoreType.DMA((N_DEV - 1,)), pltpu.SemaphoreType.DMA((N_DEV - 1,))],
    )(pack)


def _sum_slots(a, *, name):
    S, R, C = a.shape
    tr = _pick(R, (256, 128, 64, 32, 16, 8))

    def body(a_ref, o_ref):
        acc = a_ref[0].astype(F32)
        for s in range(1, S):
            acc = acc + a_ref[s].astype(F32)
        o_ref[...] = acc

    return pl.pallas_call(
        body, name=name, grid=(R // tr,), in_specs=[pl.BlockSpec((S, tr, C), lambda i: (0, i, 0))],
        out_specs=pl.BlockSpec((tr, C), lambda i: (i, 0)), out_shape=jax.ShapeDtypeStruct((R, C), F32),
        compiler_params=_cparams("parallel"),
    )(a)


def _adamw(w, m, v, gs, *, name):
    R, C = w.shape
    tile = _pick(R, (128, 96, 64, 32, 16, 8))

    def fn(w_, m_, v_, *g_):
        g = g_[0]
        for e in g_[1:]:
            g = g + e
        m2 = ADAM_B1 * m_ + (1.0 - ADAM_B1) * g
        v2 = ADAM_B2 * v_ + (1.0 - ADAM_B2) * jnp.square(g)
        m_hat = m2 / (1.0 - ADAM_B1 ** ADAM_STEP)
        v_hat = v2 / (1.0 - ADAM_B2 ** ADAM_STEP)
        delta = -ADAM_LR * (m_hat / (jnp.sqrt(v_hat) + ADAM_EPS) + ADAM_WD * w_)
        return g, delta, m2, v2

    return _rowwise(fn, [w, m, v, *gs], [], [(C, F32)] * 4, tile=tile, name=name)


_ARGS = "x, p, attn_norm_g, w_in, shift_mu, w0, w2, a0, a2, g2, k_k, k_a, r_k, lnx_g, lnx_b, q_norm_g, k_norm_g, fgate_b, w_out, ffn_norm_g, w_gate, w_up, w_down, ple_proj, ple_norm_g, ple_gate_norm_g, ple_gate_w, ple_gate_b, loss_target, m_attn_norm_g, m_w_in, m_shift_mu, m_w0, m_w2, m_a0, m_a2, m_g2, m_k_k, m_k_a, m_r_k, m_lnx_g, m_lnx_b, m_q_norm_g, m_k_norm_g, m_fgate_b, m_w_out, m_ffn_norm_g, m_w_gate, m_w_up, m_w_down, m_ple_proj, m_ple_norm_g, m_ple_gate_norm_g, m_ple_gate_w, m_ple_gate_b, v_attn_norm_g, v_w_in, v_shift_mu, v_w0, v_w2, v_a0, v_a2, v_g2, v_k_k, v_k_a, v_r_k, v_lnx_g, v_lnx_b, v_q_norm_g, v_k_norm_g, v_fgate_b, v_w_out, v_ffn_norm_g, v_w_gate, v_w_up, v_w_down, v_ple_proj, v_ple_norm_g, v_ple_gate_norm_g, v_ple_gate_w, v_ple_gate_b".split(", ")
_WEIGHTS = _ARGS[2:28]
_COL_SHARDED = ("w_in", "w2", "a2", "g2", "w_gate", "w_up", "ple_proj")
_ROW_SHARDED = ("w_out", "w_down", "ple_gate_w")
_MATRICES = _COL_SHARDED + _ROW_SHARDED
_EARLY = ("w_in", "w2", "a2", "g2")
_CARRIED_BY_Q = ("ple_gate_w", "ple_proj", "w_down")
_CARRIED_BY_KV = ("w_gate", "w_up", "w_out")
_LATE = _CARRIED_BY_Q + _CARRIED_BY_KV
_VECTORS = tuple(n for n in _WEIGHTS if n not in _MATRICES)


def _whole(name, g):
    if name in _COL_SHARDED:
        return g.transpose(1, 0, 2).reshape(g.shape[1], -1)
    return g.reshape(-1, g.shape[2])


def _pieces(name, a):
    if name in _COL_SHARDED:
        return a.reshape(a.shape[0], N_CHIPS, -1).transpose(1, 0, 2)
    return a.reshape(N_CHIPS, -1, a.shape[1])


def kernel(x, p, attn_norm_g, w_in, shift_mu, w0, w2, a0, a2, g2, k_k, k_a, r_k, lnx_g, lnx_b, q_norm_g, k_norm_g, fgate_b, w_out, ffn_norm_g, w_gate, w_up, w_down, ple_proj, ple_norm_g, ple_gate_norm_g, ple_gate_w, ple_gate_b, loss_target, m_attn_norm_g, m_w_in, m_shift_mu, m_w0, m_w2, m_a0, m_a2, m_g2, m_k_k, m_k_a, m_r_k, m_lnx_g, m_lnx_b, m_q_norm_g, m_k_norm_g, m_fgate_b, m_w_out, m_ffn_norm_g, m_w_gate, m_w_up, m_w_down, m_ple_proj, m_ple_norm_g, m_ple_gate_norm_g, m_ple_gate_w, m_ple_gate_b, v_attn_norm_g, v_w_in, v_shift_mu, v_w0, v_w2, v_a0, v_a2, v_g2, v_k_k, v_k_a, v_r_k, v_lnx_g, v_lnx_b, v_q_norm_g, v_k_norm_g, v_fgate_b, v_w_out, v_ffn_norm_g, v_w_gate, v_w_up, v_w_down, v_ple_proj, v_ple_norm_g, v_ple_gate_norm_g, v_ple_gate_w, v_ple_gate_b):
    A = dict(zip(_ARGS, (x, p, attn_norm_g, w_in, shift_mu, w0, w2, a0, a2, g2, k_k, k_a, r_k, lnx_g, lnx_b, q_norm_g, k_norm_g, fgate_b, w_out, ffn_norm_g, w_gate, w_up, w_down, ple_proj, ple_norm_g, ple_gate_norm_g, ple_gate_w, ple_gate_b, loss_target, m_attn_norm_g, m_w_in, m_shift_mu, m_w0, m_w2, m_a0, m_a2, m_g2, m_k_k, m_k_a, m_r_k, m_lnx_g, m_lnx_b, m_q_norm_g, m_k_norm_g, m_fgate_b, m_w_out, m_ffn_norm_g, m_w_gate, m_w_up, m_w_down, m_ple_proj, m_ple_norm_g, m_ple_gate_norm_g, m_ple_gate_w, m_ple_gate_b, v_attn_norm_g, v_w_in, v_shift_mu, v_w0, v_w2, v_a0, v_a2, v_g2, v_k_k, v_k_a, v_r_k, v_lnx_g, v_lnx_b, v_q_norm_g, v_k_norm_g, v_fgate_b, v_w_out, v_ffn_norm_g, v_w_gate, v_w_up, v_w_down, v_ple_proj, v_ple_norm_g, v_ple_gate_norm_g, v_ple_gate_w, v_ple_gate_b)))
    x, p, tgt = A["x"][0], A["p"][0, 0], A["loss_target"][0]
    d = _Dims(W=A["w0"].shape[-1], DL=A["w2"].shape[1], AL=A["a2"].shape[1], GL=A["g2"].shape[1], FH=A["fgate_b"].shape[-1])

    shard = lambda n: A[n][0].astype(BF16)
    gathered = _exchange([shard(n) for n in _EARLY], gather=True, name="gather_early")
    full = {n: _whole(n, g) for n, g in zip(_EARLY, gathered)}
    Wt = {"w_in_r": d.pad_r(full["w_in"][:, :d.RC]), "w_in_f": d.pad_f(full["w_in"][:, d.RC:]),
          "w2": _padr(full["w2"], d.DLp), "a2": _padr(full["a2"], d.ALp), "g2": _padr(full["g2"], d.GLp)}
    vec = {n: A[n].reshape(-1, A[n].shape[-1]) for n in _VECTORS}

    loss, dx, G, gv, recv = _local_step(x, p, tgt, Wt, vec, d, late_shards=[shard(n) for n in _LATE])

    gw = {"w_in": jnp.concatenate([d.unpad_r(G["w_in_r"]), d.unpad_f(G["w_in_f"])], axis=1),
          "w2": G["w2"][:d.DL], "a2": G["a2"][:d.AL], "g2": G["g2"][:d.GL]}
    recv.update(zip(_EARLY, _exchange([_pieces(n, gw[n]).astype(BF16) for n in _EARLY], gather=False, name="scatter_early")))
    part = [_sum_slots(recv[n], name="sum_" + n) for n in _MATRICES]
    sib = _swap_cores(part, name="swap_cores")

    sizes = [1] + [A[n].size for n in _VECTORS]
    rows = _rup(_rup(sum(sizes), LANE) // LANE, 8)

    def pack(items):
        flat = jnp.concatenate([i.reshape(-1) for i in items])
        return jnp.pad(flat, (0, rows * LANE - flat.shape[0])).reshape(rows, LANE)

    red = _allreduce_small(pack([loss[0, :1]] + [gv[n] for n in _VECTORS]), name="allreduce_vectors")
    zero = jnp.zeros((1,), F32)
    upd = _adamw(pack([zero] + [A[n] for n in _VECTORS]), pack([zero] + [A["m_" + n] for n in _VECTORS]),
                 pack([zero + 1.0] + [A["v_" + n] for n in _VECTORS]), [red], name="adamw_vectors")
    offs = [0]
    for s in sizes:
        offs.append(offs[-1] + s)
    unpack = lambda a, i, n: a.reshape(-1)[offs[i + 1]:offs[i + 2]].reshape(A[n].shape)

    out = {"grad": {}, "delta": {}, "new_m": {}, "new_v": {}}
    for i, n in enumerate(_VECTORS):
        for kind, a in zip(out, upd):
            out[kind][n] = unpack(a, i, n)
    for n, mine, other in zip(_MATRICES, part, sib):
        res = _adamw(A[n][0], A["m_" + n][0], A["v_" + n][0], [mine, other], name="adamw_" + n)
        for kind, a in zip(out, res):
            out[kind][n] = a[None]
    return (red[0, 0], dx[None], *[out[k][n] for k in out for n in _WEIGHTS])
```

```python
import functools

import jax
import jax.numpy as jnp
from jax import lax
from jax.experimental import pallas as pl
from jax.experimental.pallas import tpu as pltpu

F32 = jnp.float32
BF16 = jnp.bfloat16
LANE = 128
HEAD = 64
RMS_EPS = 1e-6
GN_EPS = 64e-5
ADAM_LR, ADAM_B1, ADAM_B2, ADAM_EPS, ADAM_WD, ADAM_STEP = 0.001, 0.9, 0.999, 1e-08, 0.01, 10
VMEM_LIMIT = 56 * 1024 * 1024
MM_TILE_BYTES = 40 * 1024 * 1024
NEG = -1e30
MESH = pl.DeviceIdType.MESH
N_CHIPS = 4
N_DEV = 8


def _rup(n, m):
    return -(-n // m) * m


def _pick(n, cands):
    for c in cands:
        if n % c == 0:
            return c
    return n


def _cparams(*sem):
    return pltpu.CompilerParams(dimension_semantics=sem, vmem_limit_bytes=VMEM_LIMIT)


def _mm(a, b, *, ta=False, tb=False, add=None, out_dtype=F32, name):
    M, K = (a.shape[1], a.shape[0]) if ta else a.shape
    N = b.shape[0] if tb else b.shape[1]
    tn = _pick(N, (512, 640, 256, 128))
    fits = lambda m, t: 2 * (m * t * a.dtype.itemsize + t * tn * b.dtype.itemsize + m * tn * 8) <= MM_TILE_BYTES
    tm, tk = next((m, t) for t in (K, 2048, 1024, 512, 640, 256, 128) for m in (1024, 512, 256, 128, M)
                  if K % t == 0 and M % m == 0 and fits(m, t))
    nk = K // tk
    dn = (((0 if ta else 1,), (1 if tb else 0,)), ((), ()))

    def body(*refs):
        if add is None:
            a_ref, b_ref, o_ref, acc = refs
        else:
            a_ref, b_ref, add_ref, o_ref, acc = refs
        ks = pl.program_id(2)
        part = lax.dot_general(a_ref[...].astype(BF16), b_ref[...].astype(BF16), dn, preferred_element_type=F32)
        if nk > 1:
            @pl.when(ks == 0)
            def _():
                acc[...] = jnp.zeros_like(acc)

            acc[...] += part

        @pl.when(ks == nk - 1)
        def _():
            res = acc[...] if nk > 1 else part
            if add is not None:
                res = res + add_ref[...].astype(F32)
            o_ref[...] = res.astype(out_dtype)

    a_spec = pl.BlockSpec((tk, tm), lambda i, j, k: (k, i)) if ta else pl.BlockSpec((tm, tk), lambda i, j, k: (i, k))
    b_spec = pl.BlockSpec((tn, tk), lambda i, j, k: (j, k)) if tb else pl.BlockSpec((tk, tn), lambda i, j, k: (k, j))
    o_spec = pl.BlockSpec((tm, tn), lambda i, j, k: (i, j))
    ins, specs = [a, b], [a_spec, b_spec]
    if add is not None:
        ins.append(add)
        specs.append(o_spec)
    return pl.pallas_call(
        body, name=name, grid=(M // tm, N // tn, nk), in_specs=specs, out_specs=o_spec,
        out_shape=jax.ShapeDtypeStruct((M, N), out_dtype),
        scratch_shapes=[pltpu.VMEM((tm, tn) if nk > 1 else (8, LANE), F32)],
        compiler_params=_cparams("parallel", "parallel", "arbitrary"),
    )(*ins)


def _rowwise(fn, rows, consts, outs, accs=(), *, tile, name, prev=(), nxt=()):
    rows = [r if isinstance(r, tuple) else (r, r.shape[1], 0) for r in rows]
    T = rows[0][0].shape[0]
    tile = min(tile, T)
    n = T // tile
    sub = 8
    nr, npv, nnx, ncst, no, na = len(rows), len(prev), len(nxt), len(consts), len(outs), len(accs)

    def body(*refs):
        i = pl.program_id(0)
        it = iter(refs)
        rv = [next(it)[...] for _ in range(nr)]
        pv = [jnp.where(i > 0, next(it)[sub - 1:sub, :], 0.0) for _ in range(npv)]
        nv = [jnp.where(i < n - 1, next(it)[0:1, :], 0.0) for _ in range(nnx)]
        cv = [next(it)[...] for _ in range(ncst)]
        o_refs = [next(it) for _ in range(no)]
        a_refs = [next(it) for _ in range(na)]
        res = fn(*rv, *pv, *nv, *cv)
        if not isinstance(res, (tuple, list)):
            res = (res,)
        for r, o in zip(o_refs, res[:no]):
            r[...] = o.astype(r.dtype)
        if na:
            @pl.when(i == 0)
            def _():
                for r in a_refs:
                    r[...] = jnp.zeros_like(r)
            for r, o in zip(a_refs, res[no:]):
                r[...] += o.astype(F32)

    in_specs = [pl.BlockSpec((tile, w), functools.partial(lambda cb, i: (i, cb), cb)) for _, w, cb in rows]
    in_specs += [pl.BlockSpec((sub, a.shape[1]), lambda i: (jnp.maximum(i * (tile // sub) - 1, 0), 0)) for a in prev]
    in_specs += [pl.BlockSpec((sub, a.shape[1]), lambda i: (jnp.minimum((i + 1) * (tile // sub), T // sub - 1), 0)) for a in nxt]
    in_specs += [pl.BlockSpec(c.shape, lambda i: (0, 0)) for c in consts]
    out_specs = [pl.BlockSpec((tile, c), lambda i: (i, 0)) for c, _ in outs]
    out_specs += [pl.BlockSpec(s, lambda i: (0, 0)) for s in accs]
    out_shape = [jax.ShapeDtypeStruct((T, c), d) for c, d in outs] + [jax.ShapeDtypeStruct(s, F32) for s in accs]
    res = pl.pallas_call(
        body, name=name, grid=(n,), in_specs=in_specs, out_specs=out_specs, out_shape=out_shape,
        compiler_params=_cparams("arbitrary"),
    )(*[r[0] for r in rows], *prev, *nxt, *consts)
    return res


@jax.custom_vjp
def _bdot(a, b):
    return jnp.dot(a.astype(BF16), b.astype(BF16), preferred_element_type=F32)


def _bdot_fwd(a, b):
    return _bdot(a, b), (a.astype(BF16), b.astype(BF16))


def _bdot_bwd(res, ct):
    a, b = res
    c = ct.astype(BF16)
    return (lax.dot_general(c, b, (((1,), (1,)), ((), ())), preferred_element_type=F32),
            lax.dot_general(a, c, (((0,), (0,)), ((), ())), preferred_element_type=F32))


_bdot.defvjp(_bdot_fwd, _bdot_bwd)


def _xdot(a, b):
    return jnp.dot(a, b, precision=lax.Precision.HIGHEST, preferred_element_type=F32)


def _rms(x, g, eps=RMS_EPS):
    return x * lax.rsqrt(jnp.mean(x * x, axis=-1, keepdims=True) + eps) * g


def _softplus(x):
    return jnp.maximum(x, 0.0) + jnp.log(1.0 + jnp.exp(-jnp.abs(x)))


def _sigmoid(x):
    return 1.0 / (1.0 + jnp.exp(-x))


def _seg_mats(width):
    h = lax.broadcasted_iota(jnp.int32, (width, LANE), 0) // HEAD
    j = lax.broadcasted_iota(jnp.int32, (width, LANE), 1)
    seg = (h == j).astype(F32)
    return seg, seg.T


def _prep_rwkv(um, w0, w2, a0, a2, g2, k_k, k_a, seg, segt, *, dims):
    W, DLp, ALp, GLp = dims
    r, k, v = um[:, :W], um[:, W:2 * W], um[:, 2 * W:3 * W]
    o = 3 * W
    xw, xa, xg = um[:, o:o + DLp], um[:, o + DLp:o + DLp + ALp], um[:, o + DLp + ALp:o + DLp + ALp + GLp]
    w_log = -_softplus(-(w0 + _bdot(jnp.tanh(xw), w2))) - 0.5
    decay = jnp.exp(-jnp.exp(w_log))
    a = _sigmoid(a0 + _bdot(xa, a2))
    g = _bdot(_sigmoid(xg), g2)
    kk = k * k_k
    nrm = jnp.sqrt(_xdot(_xdot(kk * kk, seg), segt))
    kk = kk / jnp.maximum(nrm, 1e-12)
    k2 = k * (1.0 + (a - 1.0) * k_a)
    return r, decay, k2, v, kk, kk * a, g


def _shift_mix(u, uprev, mu):
    first = lax.broadcasted_iota(jnp.int32, u.shape, 0) == 0
    sh = jnp.where(first, uprev, pltpu.roll(u, 1, 0))
    return u + (sh - u) * mu, sh


def _post_rwkv(y, r, k2, v, g, lnx_g, lnx_b, r_k, seg, segt):
    inv = 1.0 / HEAD
    mean = _xdot(_xdot(y, seg), segt) * inv
    yc = y - mean
    var = _xdot(_xdot(yc * yc, seg), segt) * inv
    yn = yc * lax.rsqrt(var + GN_EPS) * lnx_g + lnx_b
    bonus = _xdot(_xdot(r * k2 * r_k, seg), segt) * v
    return (yn + bonus) * g


def _prep_fox(uf, qg, kg, fb, seg, segt, *, dims):
    FW, FHp = dims
    q, k, v, f = uf[:, :FW], uf[:, FW:2 * FW], uf[:, 2 * FW:3 * FW], uf[:, 3 * FW:3 * FW + FHp]
    inv = 1.0 / HEAD
    qn = q * lax.rsqrt(_xdot(_xdot(q * q, seg), segt) * inv + RMS_EPS) * qg * (HEAD ** -0.5)
    kn = k * lax.rsqrt(_xdot(_xdot(k * k, seg), segt) * inv + RMS_EPS) * kg
    return qn, kn, v, -_softplus(-(f + fb))


def _tail(h2, pe, z, png, pgb):
    return h2 + _sigmoid(z + pgb) * _rms(pe, png)


def _swiglu(gt, up):
    return gt * _sigmoid(gt) * up


def _cumsum(x, *, reverse, name):
    T, C = x.shape
    tc = _pick(T, (256, 128))
    n = T // tc
    i0 = lax.broadcasted_iota(jnp.int32, (tc, tc), 0)
    i1 = lax.broadcasted_iota(jnp.int32, (tc, tc), 1)
    tri = ((i0 <= i1) if reverse else (i0 >= i1)).astype(BF16)

    def body(x_ref, tri_ref, o_ref, carry):
        i = pl.program_id(0)

        @pl.when(i == 0)
        def _():
            carry[...] = jnp.zeros_like(carry)

        v = x_ref[...]
        hi = v.astype(BF16)
        r1 = v - hi.astype(F32)
        mid = r1.astype(BF16)
        lo = (r1 - mid.astype(F32)).astype(BF16)
        t = tri_ref[...]
        d = lambda p: jnp.dot(t, p, preferred_element_type=F32)
        c = d(hi) + d(mid) + d(lo) + carry[0:1, :]
        o_ref[...] = c
        edge = c[0:1, :] if reverse else c[tc - 1:tc, :]
        carry[...] = jnp.broadcast_to(edge, carry.shape)

    blk = pl.BlockSpec((tc, C), (lambda i: (n - 1 - i, 0)) if reverse else (lambda i: (i, 0)))
    return pl.pallas_call(
        body, name=name, grid=(n,), in_specs=[blk, pl.BlockSpec((tc, tc), lambda i: (0, 0))], out_specs=blk,
        out_shape=jax.ShapeDtypeStruct((T, C), F32), scratch_shapes=[pltpu.VMEM((8, C), F32)],
        compiler_params=_cparams("arbitrary"),
    )(x, tri)


BWD_HEADS_PER_TRIP = 4
FWD_STEPS_PER_TRIP = 2
RECOMPUTE_STEPS_PER_TRIP = 4


def _steps(n, per_trip, step):
    def trip(i, carry):
        for j in range(per_trip):
            carry = step(i * per_trip + j, carry)
        return carry

    lax.fori_loop(0, n // per_trip, trip, 0)


def _col(tile, lane, t):
    return jnp.sum(jnp.where(lane == t, tile, 0.0), axis=1, keepdims=True)


def _scan_fwd(r, w, k, kk, kka, vT, *, hg, tc):
    H, T, N = r.shape
    nc = T // tc

    def body(r_ref, w_ref, k_ref, kk_ref, kka_ref, vT_ref, yT_ref, chk_ref, s_ref):
        @pl.when(pl.program_id(1) == 0)
        def _():
            s_ref[...] = jnp.zeros_like(s_ref)

        chk_ref[:, 0] = s_ref[...]
        yT_ref[...] = jnp.zeros_like(yT_ref)
        lane = lax.broadcasted_iota(jnp.int32, (N, tc), 1)

        def emit_y(S, h, t):
            y = jnp.sum(S * r_ref[h, pl.ds(jnp.maximum(t, 0), 1), :], axis=1, keepdims=True)
            yT_ref[h] = jnp.where(lane == t, y, yT_ref[h])

        def step(t, carry):
            for h in range(hg):
                row = lambda ref: ref[h, pl.ds(t, 1), :]
                S = s_ref[h]
                emit_y(S, h, t - 1)
                vcol = _col(vT_ref[h], lane, t)
                sa = -jnp.sum(S * row(kk_ref), axis=1, keepdims=True)
                s_ref[h] = S * row(w_ref) + sa * row(kka_ref) + vcol * row(k_ref)
            return carry

        _steps(tc, FWD_STEPS_PER_TRIP, step)
        for h in range(hg):
            emit_y(s_ref[h], h, tc - 1)

    rows = pl.BlockSpec((hg, tc, N), lambda g, c: (g, c, 0))
    cols = pl.BlockSpec((hg, N, tc), lambda g, c: (g, 0, c))
    return pl.pallas_call(
        body, name="scan_fwd", grid=(H // hg, nc), in_specs=[rows] * 5 + [cols],
        out_specs=[cols, pl.BlockSpec((hg, 1, N, N), lambda g, c: (g, c, 0, 0))],
        out_shape=[jax.ShapeDtypeStruct((H, N, T), F32), jax.ShapeDtypeStruct((H, nc, N, N), F32)],
        scratch_shapes=[pltpu.VMEM((hg, N, N), F32)],
        compiler_params=_cparams("arbitrary", "arbitrary"),
    )(r, w, k, kk, kka, vT)


def _scan_bwd(r, w, k, kk, kka, vT, dyT, chk, *, hg, tc):
    H, T, N = r.shape
    nc = T // tc

    def body(r_ref, w_ref, k_ref, kk_ref, kka_ref, vT_ref, dyT_ref, chk_ref,
             dr_ref, dw_ref, dk_ref, dkk_ref, dkka_ref, dvT_ref, sp_ref, ds_ref):
        @pl.when(pl.program_id(1) == 0)
        def _():
            ds_ref[...] = jnp.zeros_like(ds_ref)

        dvT_ref[...] = jnp.zeros_like(dvT_ref)
        lane = lax.broadcasted_iota(jnp.int32, (N, tc), 1)

        for h in range(hg):
            sp_ref[h, 0] = chk_ref[h, 0]

        def fstep(t, carry):
            for h in range(hg):
                row = lambda ref: ref[h, pl.ds(t, 1), :]
                S = sp_ref[h, t]
                vcol = _col(vT_ref[h], lane, t)
                sa = -jnp.sum(S * row(kk_ref), axis=1, keepdims=True)
                sp_ref[h, t + 1] = S * row(w_ref) + sa * row(kka_ref) + vcol * row(k_ref)
            return carry

        _steps(tc, RECOMPUTE_STEPS_PER_TRIP, fstep)

        def bstep(h0, i, carry):
            t = tc - 1 - i
            for h in range(h0, min(h0 + BWD_HEADS_PER_TRIP, hg)):
                row = lambda ref: ref[h, pl.ds(t, 1), :]
                rr, wr, kr, kkr, kkar = row(r_ref), row(w_ref), row(k_ref), row(kk_ref), row(kka_ref)
                Sp = sp_ref[h, t]
                Sn = sp_ref[h, t + 1]
                dycol = _col(dyT_ref[h], lane, t)
                vcol = _col(vT_ref[h], lane, t)
                dS = ds_ref[h]
                dSn = dS + dycol * rr
                dsa = jnp.sum(dS * kkar, axis=1, keepdims=True) + dycol * jnp.sum(rr * kkar, axis=1, keepdims=True)
                dr_ref[h, pl.ds(t, 1), :] = jnp.sum(Sn * dycol, axis=0, keepdims=True)
                sa = -jnp.sum(Sp * kkr, axis=1, keepdims=True)
                dw_ref[h, pl.ds(t, 1), :] = jnp.sum(dSn * Sp, axis=0, keepdims=True)
                dkka_ref[h, pl.ds(t, 1), :] = jnp.sum(dSn * sa, axis=0, keepdims=True)
                dvcol = jnp.sum(dSn * kr, axis=1, keepdims=True)
                dk_ref[h, pl.ds(t, 1), :] = jnp.sum(dSn * vcol, axis=0, keepdims=True)
                dkk_ref[h, pl.ds(t, 1), :] = -jnp.sum(Sp * dsa, axis=0, keepdims=True)
                ds_ref[h] = dSn * wr - dsa * kkr
                dvT_ref[h] = jnp.where(lane == t, dvcol, dvT_ref[h])
            return carry

        for h0 in range(0, hg, BWD_HEADS_PER_TRIP):
            lax.fori_loop(0, tc, functools.partial(bstep, h0), 0)

    rows = pl.BlockSpec((hg, tc, N), lambda g, c: (g, nc - 1 - c, 0))
    cols = pl.BlockSpec((hg, N, tc), lambda g, c: (g, 0, nc - 1 - c))
    return pl.pallas_call(
        body, name="scan_bwd", grid=(H // hg, nc),
        in_specs=[rows] * 5 + [cols, cols, pl.BlockSpec((hg, 1, N, N), lambda g, c: (g, nc - 1 - c, 0, 0))],
        out_specs=[rows] * 5 + [cols],
        out_shape=[jax.ShapeDtypeStruct((H, T, N), F32)] * 5 + [jax.ShapeDtypeStruct((H, N, T), F32)],
        scratch_shapes=[pltpu.VMEM((hg, tc + 1, N, N), F32), pltpu.VMEM((hg, N, N), F32)],
        compiler_params=_cparams("arbitrary", "arbitrary"),
    )(r, w, k, kk, kka, vT, dyT, chk)


_NT = (((1,), (1,)), ((), ()))
_TN = (((0,), (0,)), ((), ()))


def _scores(q, k, cc, cr, masked):
    s = lax.dot_general(q, k, _NT, preferred_element_type=F32) + cc - cr
    if masked:
        tb = s.shape[0]
        keep = lax.broadcasted_iota(jnp.int32, (tb, tb), 0) >= lax.broadcasted_iota(jnp.int32, (tb, tb), 1)
        s = jnp.where(keep, s, NEG)
    return s


def _attn_specs(T, N, tb):
    blk = pl.BlockSpec((1, tb, N), lambda h, i: (h, i, 0))
    whole = pl.BlockSpec((1, T, N), lambda h, i: (h, 0, 0))
    col = pl.BlockSpec((1, tb, 1), lambda h, i: (h, i, 0))
    wcol = pl.BlockSpec((1, T, 1), lambda h, i: (h, 0, 0))
    row = pl.BlockSpec((1, 1, tb), lambda h, i: (h, 0, i))
    wrow = pl.BlockSpec((1, 1, T), lambda h, i: (h, 0, 0))
    return blk, whole, col, wcol, row, wrow


def _call_carrying(body, name, grid, in_specs, out_specs, out_shape, scratch, args, comm):
    n_out = len(out_specs)
    if comm is not None:
        n = len(comm[0])
        body = _carrying(body, len(in_specs), n_out, grid, comm)
        in_specs, out_specs = in_specs + [_HBM] * n, out_specs + [_HBM] * n
        out_shape, scratch, args = out_shape + _exchange_shapes(*comm), scratch + _exchange_sems(n), args + list(comm[0])
    res = pl.pallas_call(
        body, name=name, grid=grid, in_specs=in_specs, out_specs=out_specs, out_shape=out_shape,
        scratch_shapes=scratch, compiler_params=_cparams(*["arbitrary"] * len(grid)),
    )(*args)
    return res[:n_out], res[n_out:]


def _attn_fwd(q, k, v, ccol, crow, *, tb, comm=None):
    H, T, N = q.shape

    def body(q_ref, k_ref, v_ref, cc_ref, cr_ref, o_ref, lse_ref, m_s, l_s, acc_s):
        qi = pl.program_id(1)
        m_s[...] = jnp.full_like(m_s, NEG)
        l_s[...] = jnp.zeros_like(l_s)
        acc_s[...] = jnp.zeros_like(acc_s)
        q_, cc = q_ref[0], cc_ref[0]

        def block(j, masked):
            at = pl.ds(pl.multiple_of(j * tb, tb), tb)
            s = _scores(q_, k_ref[0, at, :], cc, cr_ref[0, :, at], masked)
            m_new = jnp.maximum(m_s[...], jnp.max(s, axis=1, keepdims=True))
            p = jnp.exp(s - m_new)
            alpha = jnp.exp(m_s[...] - m_new)
            l_s[...] = alpha * l_s[...] + jnp.sum(p, axis=1, keepdims=True)
            acc_s[...] = alpha * acc_s[...] + jnp.dot(p.astype(BF16), v_ref[0, at, :], preferred_element_type=F32)
            m_s[...] = m_new

        def below(j, carry):
            block(j, False)
            return carry

        lax.fori_loop(0, qi, below, 0)
        block(qi, True)
        o_ref[0] = acc_s[...] / l_s[...]
        lse_ref[0] = m_s[...] + jnp.log(l_s[...])

    blk, whole, col, wcol, row, wrow = _attn_specs(T, N, tb)
    return _call_carrying(
        body, "fox_fwd", (H, T // tb), [blk, whole, whole, col, wrow], [blk, col],
        [jax.ShapeDtypeStruct((H, T, N), F32), jax.ShapeDtypeStruct((H, T, 1), F32)],
        [pltpu.VMEM((tb, 1), F32), pltpu.VMEM((tb, 1), F32), pltpu.VMEM((tb, N), F32)],
        [q, k, v, ccol, crow], comm)


def _attn_bwd_q(q, k, v, ccol, crow, o, lse, do, *, tb, comm=None):
    H, T, N = q.shape

    def body(q_ref, k_ref, v_ref, cc_ref, cr_ref, o_ref, lse_ref, do_ref, dq_ref, dc_ref, dl_ref, dq_s, dc_s):
        qi = pl.program_id(1)
        dq_s[...] = jnp.zeros_like(dq_s)
        dc_s[...] = jnp.zeros_like(dc_s)
        q_, cc, lse_, do_ = q_ref[0], cc_ref[0], lse_ref[0], do_ref[0]
        delta = jnp.sum(do_ * o_ref[0], axis=1, keepdims=True)
        dl_ref[0] = delta
        dob = do_.astype(BF16)

        def block(j, masked):
            at = pl.ds(pl.multiple_of(j * tb, tb), tb)
            kb = k_ref[0, at, :]
            p = jnp.exp(_scores(q_, kb, cc, cr_ref[0, :, at], masked) - lse_)
            dp = lax.dot_general(dob, v_ref[0, at, :], _NT, preferred_element_type=F32)
            ds = p * (dp - delta)
            dq_s[...] += jnp.dot(ds.astype(BF16), kb, preferred_element_type=F32)
            dc_s[...] += jnp.sum(ds, axis=1, keepdims=True)

        def below(j, carry):
            block(j, False)
            return carry

        lax.fori_loop(0, qi, below, 0)
        block(qi, True)
        dq_ref[0] = dq_s[...]
        dc_ref[0] = dc_s[...]

    blk, whole, col, wcol, row, wrow = _attn_specs(T, N, tb)
    return _call_carrying(
        body, "fox_bwd_q", (H, T // tb), [blk, whole, whole, col, wrow, blk, col, blk], [blk, col, col],
        [jax.ShapeDtypeStruct((H, T, N), F32)] + [jax.ShapeDtypeStruct((H, T, 1), F32)] * 2,
        [pltpu.VMEM((tb, N), F32), pltpu.VMEM((tb, 1), F32)],
        [q, k, v, ccol, crow, o, lse, do], comm)


def _attn_bwd_kv(q, k, v, ccol, crow, lse, delta, do, *, tb, comm=None):
    H, T, N = q.shape
    nb = T // tb

    def body(q_ref, k_ref, v_ref, cc_ref, cr_ref, lse_ref, dl_ref, do_ref, dk_ref, dv_ref, dc_ref, dk_s, dv_s, dc_s):
        ki = pl.program_id(1)
        dk_s[...] = jnp.zeros_like(dk_s)
        dv_s[...] = jnp.zeros_like(dv_s)
        dc_s[...] = jnp.zeros_like(dc_s)
        kb, vb, cr = k_ref[0], v_ref[0], cr_ref[0]

        def block(j, masked):
            at = pl.ds(pl.multiple_of(j * tb, tb), tb)
            qb, dob = q_ref[0, at, :], do_ref[0, at, :]
            p = jnp.exp(_scores(qb, kb, cc_ref[0, at, :], cr, masked) - lse_ref[0, at, :])
            dp = lax.dot_general(dob, vb, _NT, preferred_element_type=F32)
            ds = p * (dp - dl_ref[0, at, :])
            dv_s[...] += lax.dot_general(p.astype(BF16), dob, _TN, preferred_element_type=F32)
            dk_s[...] += lax.dot_general(ds.astype(BF16), qb, _TN, preferred_element_type=F32)
            dc_s[...] += jnp.sum(ds, axis=0, keepdims=True)

        def above(j, carry):
            block(j, False)
            return carry

        block(ki, True)
        lax.fori_loop(ki + 1, nb, above, 0)
        dk_ref[0] = dk_s[...]
        dv_ref[0] = dv_s[...]
        dc_ref[0] = dc_s[...]

    blk, whole, col, wcol, row, wrow = _attn_specs(T, N, tb)
    return _call_carrying(
        body, "fox_bwd_kv", (H, nb), [whole, blk, blk, wcol, row, wcol, wcol, whole], [blk, blk, row],
        [jax.ShapeDtypeStruct((H, T, N), F32)] * 2 + [jax.ShapeDtypeStruct((H, 1, T), F32)],
        [pltpu.VMEM((tb, N), F32), pltpu.VMEM((tb, N), F32), pltpu.VMEM((1, tb), F32)],
        [q, k, v, ccol, crow, lse, delta, do], comm)


def _heads(x):
    T = x.shape[0]
    return x.reshape(T, -1, HEAD).transpose(1, 0, 2)


def _headsT(x):
    T = x.shape[0]
    return x.reshape(T, -1, HEAD).transpose(1, 2, 0)


def _unheads(x):
    return x.transpose(1, 0, 2).reshape(x.shape[1], -1)


def _unheadsT(x):
    return x.transpose(2, 0, 1).reshape(x.shape[2], -1)


def _padc(x, n):
    return jnp.pad(x, ((0, 0), (0, n - x.shape[1])))


def _padr(x, n):
    return jnp.pad(x, ((0, n - x.shape[0]), (0, 0)))


class _Dims:
    def __init__(self, W, DL, AL, GL, FH):
        self.W, self.DL, self.AL, self.GL, self.FH = W, DL, AL, GL, FH
        self.DLp, self.ALp, self.GLp, self.FHp = _rup(DL, LANE), _rup(AL, LANE), _rup(GL, LANE), _rup(FH, LANE)
        self.FW = FH * HEAD
        self.RC = 3 * W + DL + AL + GL
        self.RP = 3 * W + self.DLp + self.ALp + self.GLp
        self.FC = 3 * self.FW + FH
        self.FP = 3 * self.FW + self.FHp

    def pad_r(self, a):
        W, o = self.W, 3 * self.W
        return jnp.concatenate([a[:, :o], _padc(a[:, o:o + self.DL], self.DLp),
                                _padc(a[:, o + self.DL:o + self.DL + self.AL], self.ALp),
                                _padc(a[:, o + self.DL + self.AL:self.RC], self.GLp)], axis=1)

    def unpad_r(self, a):
        o = 3 * self.W
        return jnp.concatenate([a[:, :o], a[:, o:o + self.DL], a[:, o + self.DLp:o + self.DLp + self.AL],
                                a[:, o + self.DLp + self.ALp:o + self.DLp + self.ALp + self.GL]], axis=1)

    def pad_f(self, a):
        return _padc(a, self.FP)

    def unpad_f(self, a):
        return a[:, :self.FC]


def _local_step(x, p, tgt, Wt, vec, d, late_shards=None):
    Wt = dict(Wt)
    T, D = x.shape
    W, FW = d.W, d.FW
    H = W // HEAD
    seg, segt = _seg_mats(W)
    segf, segft = _seg_mats(FW)
    T1 = 256
    rk_flat = vec["r_k"].reshape(1, W)
    mu = d.pad_r(vec["shift_mu"])
    qg = jnp.tile(vec["q_norm_g"], (1, d.FH))
    kg = jnp.tile(vec["k_norm_g"], (1, d.FH))
    fb = _padc(vec["fgate_b"], d.FHp)
    pdims = (W, d.DLp, d.ALp, d.GLp)
    fdims = (FW, d.FHp)

    (xn,) = _rowwise(lambda x_, g_: _rms(x_, g_), [x], [vec["attn_norm_g"]], [(D, BF16)], tile=T1, name="norm_attn")
    u_r = _mm(xn, Wt["w_in_r"], name="mm_in_r")
    u_f = _mm(xn, Wt["w_in_f"], name="mm_in_f")

    prep_consts = [mu, vec["w0"], Wt["w2"].astype(F32), vec["a0"], Wt["a2"].astype(F32), Wt["g2"].astype(F32), vec["k_k"], vec["k_a"], seg, segt]

    def prep_fwd(u_, up_, mu_, *cs):
        um, _ = _shift_mix(u_, up_, mu_)
        return _prep_rwkv(um, *cs, dims=pdims)

    r, dec, k2, v, kk, kka, g = _rowwise(prep_fwd, [u_r], prep_consts, [(W, F32)] * 7, tile=128, name="rwkv_prep", prev=[u_r])
    hg, tc = min(16, H), 128
    yT, chk = _scan_fwd(_heads(r), _heads(dec), _heads(k2), _heads(kk), _heads(kka), _headsT(v), hg=hg, tc=tc)
    y = _unheadsT(yT)
    post_consts = [vec["lnx_g"], vec["lnx_b"], rk_flat, seg, segt]
    (y_r,) = _rowwise(_post_rwkv, [y, r, k2, v, g], post_consts, [(W, BF16)], tile=T1, name="rwkv_post")

    fox_consts = [qg, kg, fb, segf, segft]
    qn, kn, vf, logf = _rowwise(functools.partial(_prep_fox, dims=fdims), [u_f], fox_consts,
                                [(FW, BF16), (FW, BF16), (FW, BF16), (d.FHp, F32)], tile=T1, name="fox_prep")
    c = _cumsum(logf, reverse=False, name="fox_cumsum")
    cT = c[:, :d.FH].T
    ccol, crow = cT[:, :, None], cT[:, None, :]
    tb = _pick(T, (512, 256, 128))
    qh, kh, vh = _heads(qn), _heads(kn), _heads(vf)
    (o, lse), gathered = _attn_fwd(qh, kh, vh, ccol, crow, tb=tb, comm=late_shards and (late_shards, True))
    Wt.update({n: _whole(n, g) for n, g in zip(_LATE, gathered)})
    y_f = _unheads(o)

    ycat = jnp.concatenate([y_r, y_f.astype(BF16)], axis=1)
    h1 = _mm(ycat, Wt["w_out"], add=x, name="mm_out")
    (hn,) = _rowwise(lambda h_, g_: _rms(h_, g_), [h1], [vec["ffn_norm_g"]], [(D, BF16)], tile=T1, name="norm_ffn")
    gt = _mm(hn, Wt["w_gate"], name="mm_gate")
    up = _mm(hn, Wt["w_up"], name="mm_up")
    (act,) = _rowwise(_swiglu, [gt, up], [], [(gt.shape[1], BF16)], tile=T1, name="swiglu")
    h2 = _mm(act, Wt["w_down"], add=h1, name="mm_down")
    (hg_,) = _rowwise(lambda h_, g_: _rms(h_, g_), [h2], [vec["ple_gate_norm_g"]], [(D, BF16)], tile=T1, name="norm_gate")
    pe = _mm(p, Wt["ple_proj"], name="mm_ple")
    z = _mm(hg_, Wt["ple_gate_w"], name="mm_pgate")

    def tail(h2_, pe_, z_, tg_, png_, pgb_):
        h3, f = jax.vjp(_tail, h2_, pe_, z_, png_, pgb_)
        err = h3 - tg_
        dh3 = err * (1.0 / D)
        lt = 0.5 * jnp.sum(jnp.sum(err * err, axis=1, keepdims=True) * (1.0 / D), axis=0, keepdims=True)
        dh2_, dpe_, dz_, dpng_, dpgb_ = f(dh3)
        return dh2_, dpe_, dz_, jnp.broadcast_to(lt, (1, LANE)), dpng_, dpgb_

    dh3, dpe, dz, loss, g_png, g_pgb = _rowwise(
        tail, [h2, pe, z, tgt], [vec["ple_norm_g"], vec["ple_gate_b"]], [(D, F32), (D, BF16), (D, BF16)],
        [(1, LANE), (1, D), (1, D)], tile=T1, name="tail")
    G = {}
    gv = {"ple_norm_g": g_png, "ple_gate_b": g_pgb}
    G["ple_gate_w"] = _mm(hg_, dz, ta=True, out_dtype=BF16, name="gw_pgate")
    G["ple_proj"] = _mm(p, dpe, ta=True, out_dtype=BF16, name="gw_ple")
    d_hg = _mm(dz, Wt["ple_gate_w"], tb=True, name="mmb_pgate")

    def norm_bwd(h_, dres_, dn_, g_):
        _, f = jax.vjp(_rms, h_, g_)
        dh_, dg_ = f(dn_)
        return dres_ + dh_, dg_

    dh2, gv["ple_gate_norm_g"] = _rowwise(norm_bwd, [h2, dh3, d_hg], [vec["ple_gate_norm_g"]], [(D, F32)], [(1, D)],
                                          tile=T1, name="norm_gate_bwd")
    G["w_down"] = _mm(act, dh2, ta=True, out_dtype=BF16, name="gw_down")
    d_act = _mm(dh2, Wt["w_down"], tb=True, name="mmb_down")

    def swiglu_bwd(gt_, up_, da_):
        _, f = jax.vjp(_swiglu, gt_, up_)
        return f(da_)

    d_gt, d_up = _rowwise(swiglu_bwd, [gt, up, d_act], [], [(gt.shape[1], BF16)] * 2, tile=T1, name="swiglu_bwd")
    G["w_gate"] = _mm(hn, d_gt, ta=True, out_dtype=BF16, name="gw_gate")
    G["w_up"] = _mm(hn, d_up, ta=True, out_dtype=BF16, name="gw_up")
    d_hn = _mm(d_gt, Wt["w_gate"], tb=True, name="mmb_gate")
    d_hn = _mm(d_up, Wt["w_up"], tb=True, add=d_hn, name="mmb_up")
    dh1, gv["ffn_norm_g"] = _rowwise(norm_bwd, [h1, dh2, d_hn], [vec["ffn_norm_g"]], [(D, F32)], [(1, D)],
                                     tile=T1, name="norm_ffn_bwd")
    G["w_out"] = _mm(ycat, dh1, ta=True, out_dtype=BF16, name="gw_out")
    d_ycat = _mm(dh1, Wt["w_out"], tb=True, name="mmb_out")

    def post_bwd(y_, r_, k2_, v_, g_, dy_, *cs):
        lg, lb, rk, sg, sgt = cs
        _, f = jax.vjp(lambda *a: _post_rwkv(*a, sg, sgt), y_, r_, k2_, v_, g_, lg, lb, rk)
        return f(dy_)

    dy, dr1, dk1, dv1, dg, gv["lnx_g"], gv["lnx_b"], g_rk = _rowwise(
        post_bwd, [y, r, k2, v, g, (d_ycat, W, 0)], post_consts, [(W, F32)] * 5, [(1, W)] * 3, tile=128, name="rwkv_post_bwd")
    gv["r_k"] = g_rk.reshape(H, HEAD)
    dr, ddec, dk2, dkk, dkka, dvT = _scan_bwd(_heads(r), _heads(dec), _heads(k2), _heads(kk), _heads(kka), _headsT(v),
                                              _headsT(dy), chk, hg=min(8, H), tc=tc)
    dr, ddec, dk2, dkk, dkka, dv = (_unheads(dr), _unheads(ddec), _unheads(dk2), _unheads(dkk), _unheads(dkka), _unheadsT(dvT))

    def prep_bwd(u_, dr_, dr1_, ddec_, dk2_, dk1_, dv_, dv1_, dkk_, dkka_, dg_, up_, mu_, *cs):
        um, sh = _shift_mix(u_, up_, mu_)
        cs_d, sg, sgt = cs[:7], cs[7], cs[8]
        _, f = jax.vjp(lambda um_, *c_: _prep_rwkv(um_, *c_, sg, sgt, dims=pdims), um, *cs_d)
        res = f((dr_ + dr1_, ddec_, dk2_ + dk1_, dv_ + dv1_, dkk_, dkka_, dg_))
        dum = res[0]
        dmu = jnp.sum(dum * (sh - u_), axis=0, keepdims=True)
        return (dum, dmu) + tuple(res[1:])

    LP = [Wt["w2"].shape, Wt["a2"].shape, Wt["g2"].shape]
    dum, g_mu, gv["w0"], g_w2, gv["a0"], g_a2, g_g2, gv["k_k"], gv["k_a"] = _rowwise(
        prep_bwd, [u_r, dr, dr1, ddec, dk2, dk1, dv, dv1, dkk, dkka, dg], prep_consts, [(d.RP, F32)],
        [(1, d.RP), (1, W), LP[0], (1, W), LP[1], LP[2], (1, W), (1, W)], tile=128, name="rwkv_prep_bwd", prev=[u_r])
    gv["shift_mu"] = d.unpad_r(g_mu)
    G["w2"], G["a2"], G["g2"] = g_w2, g_a2, g_g2
    (du_r,) = _rowwise(lambda a_, an_, mu_: a_ * (1.0 - mu_) + jnp.where(
        lax.broadcasted_iota(jnp.int32, a_.shape, 0) == a_.shape[0] - 1, an_, pltpu.roll(a_, a_.shape[0] - 1, 0)) * mu_,
        [dum], [mu], [(d.RP, BF16)], tile=T1, name="shift_bwd", nxt=[dum])

    do = _heads(d_ycat[:, W:])
    pieces = lambda names: late_shards and ([_pieces(n, G[n]).astype(BF16) for n in names], False)
    (dq, dcq, delta), recv_q = _attn_bwd_q(qh, kh, vh, ccol, crow, o, lse, do, tb=tb, comm=pieces(_CARRIED_BY_Q))
    (dk_, dv_, dck), recv_kv = _attn_bwd_kv(qh, kh, vh, ccol, crow, lse, delta, do.astype(BF16), tb=tb,
                                           comm=pieces(_CARRIED_BY_KV))
    recv = dict(zip(_CARRIED_BY_Q + _CARRIED_BY_KV, tuple(recv_q) + tuple(recv_kv)))
    dc = _padc((dcq[:, :, 0] - dck[:, 0, :]).T, d.FHp)
    dlogf = _cumsum(dc, reverse=True, name="fox_cumsum_bwd")

    def fox_bwd(uf_, dq_, dk__, dv__, dlf_, *cs):
        qg_, kg_, fb_, sg, sgt = cs
        _, f = jax.vjp(lambda uf__, a, b, c_: _prep_fox(uf__, a, b, c_, sg, sgt, dims=fdims), uf_, qg_, kg_, fb_)
        return f((dq_, dk__, dv__, dlf_))

    du_f, g_qg, g_kg, g_fb = _rowwise(fox_bwd, [u_f, _unheads(dq), _unheads(dk_), _unheads(dv_), dlogf], fox_consts,
                                      [(d.FP, BF16)], [(1, FW), (1, FW), (1, d.FHp)], tile=T1, name="fox_prep_bwd")
    gv["q_norm_g"] = g_qg.reshape(d.FH, HEAD).sum(0, keepdims=True)
    gv["k_norm_g"] = g_kg.reshape(d.FH, HEAD).sum(0, keepdims=True)
    gv["fgate_b"] = g_fb[:, :d.FH]

    G["w_in_r"] = _mm(xn, du_r, ta=True, out_dtype=BF16, name="gw_in_r")
    G["w_in_f"] = _mm(xn, du_f, ta=True, out_dtype=BF16, name="gw_in_f")
    d_xn = _mm(du_r, Wt["w_in_r"], tb=True, name="mmb_in_r")
    d_xn = _mm(du_f, Wt["w_in_f"], tb=True, add=d_xn, name="mmb_in_f")
    dx, gv["attn_norm_g"] = _rowwise(norm_bwd, [x, dh1, d_xn], [vec["attn_norm_g"]], [(D, F32)], [(1, D)],
                                     tile=T1, name="norm_attn_bwd")
    return loss, dx, G, gv, recv


_HBM = pl.BlockSpec(memory_space=pl.ANY)
_OTHER_CHIPS = ((0, 1), (1, 0), (1, 1))


def _flip(v, bit):
    return 1 - v if bit else v


def _exchange(arrs, *, gather, name):
    n = len(arrs)

    def body(*refs):
        copies = _exchange_copies(refs[:n], refs[n:2 * n], refs[2 * n:], gather)
        for cp in copies:
            cp.start()
        for cp in copies:
            cp.wait()

    return pl.pallas_call(
        body, name=name, in_specs=[_HBM] * n, out_specs=[_HBM] * n, out_shape=_exchange_shapes(arrs, gather),
        scratch_shapes=_exchange_sems(n),
    )(*arrs)


def _exchange_shapes(arrs, gather):
    return [jax.ShapeDtypeStruct(((N_CHIPS,) + a.shape) if gather else a.shape, a.dtype) for a in arrs]


def _exchange_sems(n):
    return [pltpu.SemaphoreType.DMA((3 * n,)), pltpu.SemaphoreType.DMA((3 * n,)), pltpu.SemaphoreType.DMA((n,))]


def _exchange_copies(ins, outs, sems, gather):
    send_sems, recv_sems, own_sems = sems
    x, y, c = lax.axis_index("x"), lax.axis_index("y"), lax.axis_index("c")
    me = 2 * x + y
    copies = []
    for a in range(len(ins)):
        copies.append(pltpu.make_async_copy(ins[a] if gather else ins[a].at[me], outs[a].at[me], own_sems.at[a]))
        for k, (dx, dy) in enumerate(_OTHER_CHIPS):
            px, py = _flip(x, dx), _flip(y, dy)
            copies.append(pltpu.make_async_remote_copy(
                src_ref=ins[a] if gather else ins[a].at[2 * px + py], dst_ref=outs[a].at[me],
                send_sem=send_sems.at[3 * a + k], recv_sem=recv_sems.at[3 * a + k],
                device_id=(px, py, c), device_id_type=MESH))
    return copies


def _carrying(body, n_in, n_out, grid, comm):
    arrs, gather = comm
    n = len(arrs)

    def wrapped(*refs):
        c_in = refs[n_in:n_in + n]
        c_out = refs[n_in + n + n_out:n_in + 2 * n + n_out]
        ids = [pl.program_id(a) for a in range(len(grid))]
        first = functools.reduce(jnp.logical_and, [i == 0 for i in ids])
        last = functools.reduce(jnp.logical_and, [i == g - 1 for i, g in zip(ids, grid)])

        @pl.when(first)
        def _():
            for cp in _exchange_copies(c_in, c_out, refs[-3:], gather):
                cp.start()

        body(*refs[:n_in], *refs[n_in + n:n_in + n + n_out], *refs[n_in + 2 * n + n_out:-3])

        @pl.when(last)
        def _():
            for cp in _exchange_copies(c_in, c_out, refs[-3:], gather):
                cp.wait()

    return wrapped


def _swap_cores(arrs, *, name):
    n = len(arrs)

    def body(*refs):
        ins, outs = refs[:n], refs[n:2 * n]
        send_sems, recv_sems = refs[2 * n:]
        peer = (lax.axis_index("x"), lax.axis_index("y"), 1 - lax.axis_index("c"))
        cps = [pltpu.make_async_remote_copy(src_ref=ins[a], dst_ref=outs[a], send_sem=send_sems.at[a],
                                            recv_sem=recv_sems.at[a], device_id=peer, device_id_type=MESH) for a in range(n)]
        for cp in cps:
            cp.start()
        for cp in cps:
            cp.wait()

    return pl.pallas_call(
        body, name=name, in_specs=[_HBM] * n, out_specs=[_HBM] * n,
        out_shape=[jax.ShapeDtypeStruct(a.shape, a.dtype) for a in arrs],
        scratch_shapes=[pltpu.SemaphoreType.DMA((n,)), pltpu.SemaphoreType.DMA((n,))],
    )(*arrs)


def _allreduce_small(pack, *, name):
    R, C = pack.shape

    def body(p_ref, o_ref, recv, send_sems, recv_sems):
        x, y, c = lax.axis_index("x"), lax.axis_index("y"), lax.axis_index("c")
        me = 4 * x + 2 * y + c
        recv[me] = p_ref[...]
        cps = []
        for k in range(1, N_DEV):
            peer = (_flip(x, k & 4), _flip(y, k & 2), _flip(c, k & 1))
            cp = pltpu.make_async_remote_copy(src_ref=p_ref, dst_ref=recv.at[me], send_sem=send_sems.at[k - 1],
                                              recv_sem=recv_sems.at[k - 1], device_id=peer, device_id_type=MESH)
            cp.start()
            cps.append(cp)
        for cp in cps:
            cp.wait()
        acc = recv[0]
        for s in range(1, N_DEV):
            acc = acc + recv[s]
        o_ref[...] = acc

    vm = pl.BlockSpec(memory_space=pltpu.VMEM)
    return pl.pallas_call(
        body, name=name, in_specs=[vm], out_specs=vm, out_shape=jax.ShapeDtypeStruct((R, C), F32),
        scratch_shapes=[pltpu.VMEM((N_DEV, R, C), F32), pltpu.SemaphoreType.DMA((N_DEV - 1,)), pltpu.SemaphoreType.DMA((N_DEV - 1,))],
    )(pack)


def _sum_slots(a, *, name):
    S, R, C = a.shape
    tr = _pick(R, (256, 128, 64, 32, 16, 8))

    def body(a_ref, o_ref):
        acc = a_ref[0].astype(F32)
        for s in range(1, S):
            acc = acc + a_ref[s].astype(F32)
        o_ref[...] = acc

    return pl.pallas_call(
        body, name=name, grid=(R // tr,), in_specs=[pl.BlockSpec((S, tr, C), lambda i: (0, i, 0))],
        out_specs=pl.BlockSpec((tr, C), lambda i: (i, 0)), out_shape=jax.ShapeDtypeStruct((R, C), F32),
        compiler_params=_cparams("parallel"),
    )(a)


def _adamw(w, m, v, gs, *, name):
    R, C = w.shape
    tile = _pick(R, (128, 96, 64, 32, 16, 8))

    def fn(w_, m_, v_, *g_):
        g = g_[0]
        for e in g_[1:]:
            g = g + e
        m2 = ADAM_B1 * m_ + (1.0 - ADAM_B1) * g
        v2 = ADAM_B2 * v_ + (1.0 - ADAM_B2) * jnp.square(g)
        m_hat = m2 / (1.0 - ADAM_B1 ** ADAM_STEP)
        v_hat = v2 / (1.0 - ADAM_B2 ** ADAM_STEP)
        delta = -ADAM_LR * (m_hat / (jnp.sqrt(v_hat) + ADAM_EPS) + ADAM_WD * w_)
        return g, delta, m2, v2

    return _rowwise(fn, [w, m, v, *gs], [], [(C, F32)] * 4, tile=tile, name=name)


_ARGS = "x, p, attn_norm_g, w_in, shift_mu, w0, w2, a0, a2, g2, k_k, k_a, r_k, lnx_g, lnx_b, q_norm_g, k_norm_g, fgate_b, w_out, ffn_norm_g, w_gate, w_up, w_down, ple_proj, ple_norm_g, ple_gate_norm_g, ple_gate_w, ple_gate_b, loss_target, m_attn_norm_g, m_w_in, m_shift_mu, m_w0, m_w2, m_a0, m_a2, m_g2, m_k_k, m_k_a, m_r_k, m_lnx_g, m_lnx_b, m_q_norm_g, m_k_norm_g, m_fgate_b, m_w_out, m_ffn_norm_g, m_w_gate, m_w_up, m_w_down, m_ple_proj, m_ple_norm_g, m_ple_gate_norm_g, m_ple_gate_w, m_ple_gate_b, v_attn_norm_g, v_w_in, v_shift_mu, v_w0, v_w2, v_a0, v_a2, v_g2, v_k_k, v_k_a, v_r_k, v_lnx_g, v_lnx_b, v_q_norm_g, v_k_norm_g, v_fgate_b, v_w_out, v_ffn_norm_g, v_w_gate, v_w_up, v_w_down, v_ple_proj, v_ple_norm_g, v_ple_gate_norm_g, v_ple_gate_w, v_ple_gate_b".split(", ")
_WEIGHTS = _ARGS[2:28]
_COL_SHARDED = ("w_in", "w2", "a2", "g2", "w_gate", "w_up", "ple_proj")
_ROW_SHARDED = ("w_out", "w_down", "ple_gate_w")
_MATRICES = _COL_SHARDED + _ROW_SHARDED
_EARLY = ("w_in", "w2", "a2", "g2")
_CARRIED_BY_Q = ("ple_gate_w", "ple_proj", "w_down")
_CARRIED_BY_KV = ("w_gate", "w_up", "w_out")
_LATE = _CARRIED_BY_Q + _CARRIED_BY_KV
_VECTORS = tuple(n for n in _WEIGHTS if n not in _MATRICES)


def _whole(name, g):
    if name in _COL_SHARDED:
        return g.transpose(1, 0, 2).reshape(g.shape[1], -1)
    return g.reshape(-1, g.shape[2])


def _pieces(name, a):
    if name in _COL_SHARDED:
        return a.reshape(a.shape[0], N_CHIPS, -1).transpose(1, 0, 2)
    return a.reshape(N_CHIPS, -1, a.shape[1])


def kernel(x, p, attn_norm_g, w_in, shift_mu, w0, w2, a0, a2, g2, k_k, k_a, r_k, lnx_g, lnx_b, q_norm_g, k_norm_g, fgate_b, w_out, ffn_norm_g, w_gate, w_up, w_down, ple_proj, ple_norm_g, ple_gate_norm_g, ple_gate_w, ple_gate_b, loss_target, m_attn_norm_g, m_w_in, m_shift_mu, m_w0, m_w2, m_a0, m_a2, m_g2, m_k_k, m_k_a, m_r_k, m_lnx_g, m_lnx_b, m_q_norm_g, m_k_norm_g, m_fgate_b, m_w_out, m_ffn_norm_g, m_w_gate, m_w_up, m_w_down, m_ple_proj, m_ple_norm_g, m_ple_gate_norm_g, m_ple_gate_w, m_ple_gate_b, v_attn_norm_g, v_w_in, v_shift_mu, v_w0, v_w2, v_a0, v_a2, v_g2, v_k_k, v_k_a, v_r_k, v_lnx_g, v_lnx_b, v_q_norm_g, v_k_norm_g, v_fgate_b, v_w_out, v_ffn_norm_g, v_w_gate, v_w_up, v_w_down, v_ple_proj, v_ple_norm_g, v_ple_gate_norm_g, v_ple_gate_w, v_ple_gate_b):
    A = dict(zip(_ARGS, (x, p, attn_norm_g, w_in, shift_mu, w0, w2, a0, a2, g2, k_k, k_a, r_k, lnx_g, lnx_b, q_norm_g, k_norm_g, fgate_b, w_out, ffn_norm_g, w_gate, w_up, w_down, ple_proj, ple_norm_g, ple_gate_norm_g, ple_gate_w, ple_gate_b, loss_target, m_attn_norm_g, m_w_in, m_shift_mu, m_w0, m_w2, m_a0, m_a2, m_g2, m_k_k, m_k_a, m_r_k, m_lnx_g, m_lnx_b, m_q_norm_g, m_k_norm_g, m_fgate_b, m_w_out, m_ffn_norm_g, m_w_gate, m_w_up, m_w_down, m_ple_proj, m_ple_norm_g, m_ple_gate_norm_g, m_ple_gate_w, m_ple_gate_b, v_attn_norm_g, v_w_in, v_shift_mu, v_w0, v_w2, v_a0, v_a2, v_g2, v_k_k, v_k_a, v_r_k, v_lnx_g, v_lnx_b, v_q_norm_g, v_k_norm_g, v_fgate_b, v_w_out, v_ffn_norm_g, v_w_gate, v_w_up, v_w_down, v_ple_proj, v_ple_norm_g, v_ple_gate_norm_g, v_ple_gate_w, v_ple_gate_b)))
    x, p, tgt = A["x"][0], A["p"][0, 0], A["loss_target"][0]
    d = _Dims(W=A["w0"].shape[-1], DL=A["w2"].shape[1], AL=A["a2"].shape[1], GL=A["g2"].shape[1], FH=A["fgate_b"].shape[-1])

    shard = lambda n: A[n][0].astype(BF16)
    gathered = _exchange([shard(n) for n in _EARLY], gather=True, name="gather_early")
    full = {n: _whole(n, g) for n, g in zip(_EARLY, gathered)}
    Wt = {"w_in_r": d.pad_r(full["w_in"][:, :d.RC]), "w_in_f": d.pad_f(full["w_in"][:, d.RC:]),
          "w2": _padr(full["w2"], d.DLp), "a2": _padr(full["a2"], d.ALp), "g2": _padr(full["g2"], d.GLp)}
    vec = {n: A[n].reshape(-1, A[n].shape[-1]) for n in _VECTORS}

    loss, dx, G, gv, recv = _local_step(x, p, tgt, Wt, vec, d, late_shards=[shard(n) for n in _LATE])

    gw = {"w_in": jnp.concatenate([d.unpad_r(G["w_in_r"]), d.unpad_f(G["w_in_f"])], axis=1),
          "w2": G["w2"][:d.DL], "a2": G["a2"][:d.AL], "g2": G["g2"][:d.GL]}
    recv.update(zip(_EARLY, _exchange([_pieces(n, gw[n]).astype(BF16) for n in _EARLY], gather=False, name="scatter_early")))
    part = [_sum_slots(recv[n], name="sum_" + n) for n in _MATRICES]
    sib = _swap_cores(part, name="swap_cores")

    sizes = [1] + [A[n].size for n in _VECTORS]
    rows = _rup(_rup(sum(sizes), LANE) // LANE, 8)

    def pack(items):
        flat = jnp.concatenate([i.reshape(-1) for i in items])
        return jnp.pad(flat, (0, rows * LANE - flat.shape[0])).reshape(rows, LANE)

    red = _allreduce_small(pack([loss[0, :1]] + [gv[n] for n in _VECTORS]), name="allreduce_vectors")
    zero = jnp.zeros((1,), F32)
    upd = _adamw(pack([zero] + [A[n] for n in _VECTORS]), pack([zero] + [A["m_" + n] for n in _VECTORS]),
                 pack([zero + 1.0] + [A["v_" + n] for n in _VECTORS]), [red], name="adamw_vectors")
    offs = [0]
    for s in sizes:
        offs.append(offs[-1] + s)
    unpack = lambda a, i, n: a.reshape(-1)[offs[i + 1]:offs[i + 2]].reshape(A[n].shape)

    out = {"grad": {}, "delta": {}, "new_m": {}, "new_v": {}}
    for i, n in enumerate(_VECTORS):
        for kind, a in zip(out, upd):
            out[kind][n] = unpack(a, i, n)
    for n, mine, other in zip(_MATRICES, part, sib):
        res = _adamw(A[n][0], A["m_" + n][0], A["v_" + n][0], [mine, other], name="adamw_" + n)
        for kind, a in zip(out, res):
            out[kind][n] = a[None]
    return (red[0, 0], dx[None], *[out[k][n] for k in out for n in _WEIGHTS])
```

```python
import functools

import jax
import jax.numpy as jnp
from jax import lax
from jax.experimental import pallas as pl
from jax.experimental.pallas import tpu as pltpu

F32 = jnp.float32
BF16 = jnp.bfloat16
LANE = 128
HEAD = 64
RMS_EPS = 1e-6
GN_EPS = 64e-5
ADAM_LR, ADAM_B1, ADAM_B2, ADAM_EPS, ADAM_WD, ADAM_STEP = 0.001, 0.9, 0.999, 1e-08, 0.01, 10
VMEM_LIMIT = 56 * 1024 * 1024
MM_TILE_BYTES = 40 * 1024 * 1024
NEG = -1e30
MESH = pl.DeviceIdType.MESH
N_CHIPS = 4
N_DEV = 8


def _rup(n, m):
    return -(-n // m) * m


def _pick(n, cands):
    for c in cands:
        if n % c == 0:
            return c
    return n


def _cparams(*sem):
    return pltpu.CompilerParams(dimension_semantics=sem, vmem_limit_bytes=VMEM_LIMIT)


def _mm(a, b, *, ta=False, tb=False, add=None, out_dtype=F32, name):
    M, K = (a.shape[1], a.shape[0]) if ta else a.shape
    N = b.shape[0] if tb else b.shape[1]
    tn = _pick(N, (512, 640, 256, 128))
    fits = lambda m, t: 2 * (m * t * a.dtype.itemsize + t * tn * b.dtype.itemsize + m * tn * 8) <= MM_TILE_BYTES
    tm, tk = next((m, t) for t in (K, 2048, 1024, 512, 640, 256, 128) for m in (1024, 512, 256, 128, M)
                  if K % t == 0 and M % m == 0 and fits(m, t))
    nk = K // tk
    dn = (((0 if ta else 1,), (1 if tb else 0,)), ((), ()))

    def body(*refs):
        if add is None:
            a_ref, b_ref, o_ref, acc = refs
        else:
            a_ref, b_ref, add_ref, o_ref, acc = refs
        ks = pl.program_id(2)
        part = lax.dot_general(a_ref[...].astype(BF16), b_ref[...].astype(BF16), dn, preferred_element_type=F32)
        if nk > 1:
            @pl.when(ks == 0)
            def _():
                acc[...] = jnp.zeros_like(acc)

            acc[...] += part

        @pl.when(ks == nk - 1)
        def _():
            res = acc[...] if nk > 1 else part
            if add is not None:
                res = res + add_ref[...].astype(F32)
            o_ref[...] = res.astype(out_dtype)

    a_spec = pl.BlockSpec((tk, tm), lambda i, j, k: (k, i)) if ta else pl.BlockSpec((tm, tk), lambda i, j, k: (i, k))
    b_spec = pl.BlockSpec((tn, tk), lambda i, j, k: (j, k)) if tb else pl.BlockSpec((tk, tn), lambda i, j, k: (k, j))
    o_spec = pl.BlockSpec((tm, tn), lambda i, j, k: (i, j))
    ins, specs = [a, b], [a_spec, b_spec]
    if add is not None:
        ins.append(add)
        specs.append(o_spec)
    return pl.pallas_call(
        body, name=name, grid=(M // tm, N // tn, nk), in_specs=specs, out_specs=o_spec,
        out_shape=jax.ShapeDtypeStruct((M, N), out_dtype),
        scratch_shapes=[pltpu.VMEM((tm, tn) if nk > 1 else (8, LANE), F32)],
        compiler_params=_cparams("parallel", "parallel", "arbitrary"),
    )(*ins)


def _rowwise(fn, rows, consts, outs, accs=(), *, tile, name, prev=(), nxt=()):
    rows = [r if isinstance(r, tuple) else (r, r.shape[1], 0) for r in rows]
    T = rows[0][0].shape[0]
    tile = min(tile, T)
    n = T // tile
    sub = 8
    nr, npv, nnx, ncst, no, na = len(rows), len(prev), len(nxt), len(consts), len(outs), len(accs)

    def body(*refs):
        i = pl.program_id(0)
        it = iter(refs)
        rv = [next(it)[...] for _ in range(nr)]
        pv = [jnp.where(i > 0, next(it)[sub - 1:sub, :], 0.0) for _ in range(npv)]
        nv = [jnp.where(i < n - 1, next(it)[0:1, :], 0.0) for _ in range(nnx)]
        cv = [next(it)[...] for _ in range(ncst)]
        o_refs = [next(it) for _ in range(no)]
        a_refs = [next(it) for _ in range(na)]
        res = fn(*rv, *pv, *nv, *cv)
        if not isinstance(res, (tuple, list)):
            res = (res,)
        for r, o in zip(o_refs, res[:no]):
            r[...] = o.astype(r.dtype)
        if na:
            @pl.when(i == 0)
            def _():
                for r in a_refs:
                    r[...] = jnp.zeros_like(r)
            for r, o in zip(a_refs, res[no:]):
                r[...] += o.astype(F32)

    in_specs = [pl.BlockSpec((tile, w), functools.partial(lambda cb, i: (i, cb), cb)) for _, w, cb in rows]
    in_specs += [pl.BlockSpec((sub, a.shape[1]), lambda i: (jnp.maximum(i * (tile // sub) - 1, 0), 0)) for a in prev]
    in_specs += [pl.BlockSpec((sub, a.shape[1]), lambda i: (jnp.minimum((i + 1) * (tile // sub), T // sub - 1), 0)) for a in nxt]
    in_specs += [pl.BlockSpec(c.shape, lambda i: (0, 0)) for c in consts]
    out_specs = [pl.BlockSpec((tile, c), lambda i: (i, 0)) for c, _ in outs]
    out_specs += [pl.BlockSpec(s, lambda i: (0, 0)) for s in accs]
    out_shape = [jax.ShapeDtypeStruct((T, c), d) for c, d in outs] + [jax.ShapeDtypeStruct(s, F32) for s in accs]
    res = pl.pallas_call(
        body, name=name, grid=(n,), in_specs=in_specs, out_specs=out_specs, out_shape=out_shape,
        compiler_params=_cparams("arbitrary"),
    )(*[r[0] for r in rows], *prev, *nxt, *consts)
    return res


@jax.custom_vjp
def _bdot(a, b):
    return jnp.dot(a.astype(BF16), b.astype(BF16), preferred_element_type=F32)


def _bdot_fwd(a, b):
    return _bdot(a, b), (a.astype(BF16), b.astype(BF16))


def _bdot_bwd(res, ct):
    a, b = res
    c = ct.astype(BF16)
    return (lax.dot_general(c, b, (((1,), (1,)), ((), ())), preferred_element_type=F32),
            lax.dot_general(a, c, (((0,), (0,)), ((), ())), preferred_element_type=F32))


_bdot.defvjp(_bdot_fwd, _bdot_bwd)


def _xdot(a, b):
    return jnp.dot(a, b, precision=lax.Precision.HIGHEST, preferred_element_type=F32)


def _rms(x, g, eps=RMS_EPS):
    return x * lax.rsqrt(jnp.mean(x * x, axis=-1, keepdims=True) + eps) * g


def _softplus(x):
    return jnp.maximum(x, 0.0) + jnp.log(1.0 + jnp.exp(-jnp.abs(x)))


def _sigmoid(x):
    return 1.0 / (1.0 + jnp.exp(-x))


def _seg_mats(width):
    h = lax.broadcasted_iota(jnp.int32, (width, LANE), 0) // HEAD
    j = lax.broadcasted_iota(jnp.int32, (width, LANE), 1)
    seg = (h == j).astype(F32)
    return seg, seg.T


def _prep_rwkv(um, w0, w2, a0, a2, g2, k_k, k_a, seg, segt, *, dims):
    W, DLp, ALp, GLp = dims
    r, k, v = um[:, :W], um[:, W:2 * W], um[:, 2 * W:3 * W]
    o = 3 * W
    xw, xa, xg = um[:, o:o + DLp], um[:, o + DLp:o + DLp + ALp], um[:, o + DLp + ALp:o + DLp + ALp + GLp]
    w_log = -_softplus(-(w0 + _bdot(jnp.tanh(xw), w2))) - 0.5
    decay = jnp.exp(-jnp.exp(w_log))
    a = _sigmoid(a0 + _bdot(xa, a2))
    g = _bdot(_sigmoid(xg), g2)
    kk = k * k_k
    nrm = jnp.sqrt(_xdot(_xdot(kk * kk, seg), segt))
    kk = kk / jnp.maximum(nrm, 1e-12)
    k2 = k * (1.0 + (a - 1.0) * k_a)
    return r, decay, k2, v, kk, kk * a, g


def _shift_mix(u, uprev, mu):
    first = lax.broadcasted_iota(jnp.int32, u.shape, 0) == 0
    sh = jnp.where(first, uprev, pltpu.roll(u, 1, 0))
    return u + (sh - u) * mu, sh


def _post_rwkv(y, r, k2, v, g, lnx_g, lnx_b, r_k, seg, segt):
    inv = 1.0 / HEAD
    mean = _xdot(_xdot(y, seg), segt) * inv
    yc = y - mean
    var = _xdot(_xdot(yc * yc, seg), segt) * inv
    yn = yc * lax.rsqrt(var + GN_EPS) * lnx_g + lnx_b
    bonus = _xdot(_xdot(r * k2 * r_k, seg), segt) * v
    return (yn + bonus) * g


def _prep_fox(uf, qg, kg, fb, seg, segt, *, dims):
    FW, FHp = dims
    q, k, v, f = uf[:, :FW], uf[:, FW:2 * FW], uf[:, 2 * FW:3 * FW], uf[:, 3 * FW:3 * FW + FHp]
    inv = 1.0 / HEAD
    qn = q * lax.rsqrt(_xdot(_xdot(q * q, seg), segt) * inv + RMS_EPS) * qg * (HEAD ** -0.5)
    kn = k * lax.rsqrt(_xdot(_xdot(k * k, seg), segt) * inv + RMS_EPS) * kg
    return qn, kn, v, -_softplus(-(f + fb))


def _tail(h2, pe, z, png, pgb):
    return h2 + _sigmoid(z + pgb) * _rms(pe, png)


def _swiglu(gt, up):
    return gt * _sigmoid(gt) * up


def _cumsum(x, *, reverse, name):
    T, C = x.shape
    tc = _pick(T, (256, 128))
    n = T // tc
    i0 = lax.broadcasted_iota(jnp.int32, (tc, tc), 0)
    i1 = lax.broadcasted_iota(jnp.int32, (tc, tc), 1)
    tri = ((i0 <= i1) if reverse else (i0 >= i1)).astype(BF16)

    def body(x_ref, tri_ref, o_ref, carry):
        i = pl.program_id(0)

        @pl.when(i == 0)
        def _():
            carry[...] = jnp.zeros_like(carry)

        v = x_ref[...]
        hi = v.astype(BF16)
        r1 = v - hi.astype(F32)
        mid = r1.astype(BF16)
        lo = (r1 - mid.astype(F32)).astype(BF16)
        t = tri_ref[...]
        d = lambda p: jnp.dot(t, p, preferred_element_type=F32)
        c = d(hi) + d(mid) + d(lo) + carry[0:1, :]
        o_ref[...] = c
        edge = c[0:1, :] if reverse else c[tc - 1:tc, :]
        carry[...] = jnp.broadcast_to(edge, carry.shape)

    blk = pl.BlockSpec((tc, C), (lambda i: (n - 1 - i, 0)) if reverse else (lambda i: (i, 0)))
    return pl.pallas_call(
        body, name=name, grid=(n,), in_specs=[blk, pl.BlockSpec((tc, tc), lambda i: (0, 0))], out_specs=blk,
        out_shape=jax.ShapeDtypeStruct((T, C), F32), scratch_shapes=[pltpu.VMEM((8, C), F32)],
        compiler_params=_cparams("arbitrary"),
    )(x, tri)


BWD_HEADS_PER_TRIP = 4
FWD_STEPS_PER_TRIP = 2
RECOMPUTE_STEPS_PER_TRIP = 4


def _steps(n, per_trip, step):
    def trip(i, carry):
        for j in range(per_trip):
            carry = step(i * per_trip + j, carry)
        return carry

    lax.fori_loop(0, n // per_trip, trip, 0)


def _col(tile, lane, t):
    return jnp.sum(jnp.where(lane == t, tile, 0.0), axis=1, keepdims=True)


def _scan_fwd(r, w, k, kk, kka, vT, *, hg, tc, comm=None):
    H, T, N = r.shape
    nc = T // tc

    def body(r_ref, w_ref, k_ref, kk_ref, kka_ref, vT_ref, yT_ref, chk_ref, s_ref):
        @pl.when(pl.program_id(1) == 0)
        def _():
            s_ref[...] = jnp.zeros_like(s_ref)

        chk_ref[:, 0] = s_ref[...]
        yT_ref[...] = jnp.zeros_like(yT_ref)
        lane = lax.broadcasted_iota(jnp.int32, (N, tc), 1)

        def emit_y(S, h, t):
            y = jnp.sum(S * r_ref[h, pl.ds(jnp.maximum(t, 0), 1), :], axis=1, keepdims=True)
            yT_ref[h] = jnp.where(lane == t, y, yT_ref[h])

        def step(t, carry):
            for h in range(hg):
                row = lambda ref: ref[h, pl.ds(t, 1), :]
                S = s_ref[h]
                emit_y(S, h, t - 1)
                vcol = _col(vT_ref[h], lane, t)
                sa = -jnp.sum(S * row(kk_ref), axis=1, keepdims=True)
                s_ref[h] = S * row(w_ref) + sa * row(kka_ref) + vcol * row(k_ref)
            return carry

        _steps(tc, FWD_STEPS_PER_TRIP, step)
        for h in range(hg):
            emit_y(s_ref[h], h, tc - 1)

    rows = pl.BlockSpec((hg, tc, N), lambda g, c: (g, c, 0))
    cols = pl.BlockSpec((hg, N, tc), lambda g, c: (g, 0, c))
    return _call_carrying(
        body, "scan_fwd", (H // hg, nc), [rows] * 5 + [cols],
        [cols, pl.BlockSpec((hg, 1, N, N), lambda g, c: (g, c, 0, 0))],
        [jax.ShapeDtypeStruct((H, N, T), F32), jax.ShapeDtypeStruct((H, nc, N, N), F32)],
        [pltpu.VMEM((hg, N, N), F32)], [r, w, k, kk, kka, vT], comm)


def _scan_bwd(r, w, k, kk, kka, vT, dyT, chk, *, hg, tc, comm=None):
    H, T, N = r.shape
    nc = T // tc

    def body(r_ref, w_ref, k_ref, kk_ref, kka_ref, vT_ref, dyT_ref, chk_ref,
             dr_ref, dw_ref, dk_ref, dkk_ref, dkka_ref, dvT_ref, sp_ref, ds_ref):
        @pl.when(pl.program_id(1) == 0)
        def _():
            ds_ref[...] = jnp.zeros_like(ds_ref)

        dvT_ref[...] = jnp.zeros_like(dvT_ref)
        lane = lax.broadcasted_iota(jnp.int32, (N, tc), 1)

        for h in range(hg):
            sp_ref[h, 0] = chk_ref[h, 0]

        def fstep(t, carry):
            for h in range(hg):
                row = lambda ref: ref[h, pl.ds(t, 1), :]
                S = sp_ref[h, t]
                vcol = _col(vT_ref[h], lane, t)
                sa = -jnp.sum(S * row(kk_ref), axis=1, keepdims=True)
                sp_ref[h, t + 1] = S * row(w_ref) + sa * row(kka_ref) + vcol * row(k_ref)
            return carry

        _steps(tc, RECOMPUTE_STEPS_PER_TRIP, fstep)

        def bstep(h0, i, carry):
            t = tc - 1 - i
            for h in range(h0, min(h0 + BWD_HEADS_PER_TRIP, hg)):
                row = lambda ref: ref[h, pl.ds(t, 1), :]
                rr, wr, kr, kkr, kkar = row(r_ref), row(w_ref), row(k_ref), row(kk_ref), row(kka_ref)
                Sp = sp_ref[h, t]
                Sn = sp_ref[h, t + 1]
                dycol = _col(dyT_ref[h], lane, t)
                vcol = _col(vT_ref[h], lane, t)
                dS = ds_ref[h]
                dSn = dS + dycol * rr
                dsa = jnp.sum(dS * kkar, axis=1, keepdims=True) + dycol * jnp.sum(rr * kkar, axis=1, keepdims=True)
                dr_ref[h, pl.ds(t, 1), :] = jnp.sum(Sn * dycol, axis=0, keepdims=True)
                sa = -jnp.sum(Sp * kkr, axis=1, keepdims=True)
                dw_ref[h, pl.ds(t, 1), :] = jnp.sum(dSn * Sp, axis=0, keepdims=True)
                dkka_ref[h, pl.ds(t, 1), :] = jnp.sum(dSn * sa, axis=0, keepdims=True)
                dvcol = jnp.sum(dSn * kr, axis=1, keepdims=True)
                dk_ref[h, pl.ds(t, 1), :] = jnp.sum(dSn * vcol, axis=0, keepdims=True)
                dkk_ref[h, pl.ds(t, 1), :] = -jnp.sum(Sp * dsa, axis=0, keepdims=True)
                ds_ref[h] = dSn * wr - dsa * kkr
                dvT_ref[h] = jnp.where(lane == t, dvcol, dvT_ref[h])
            return carry

        for h0 in range(0, hg, BWD_HEADS_PER_TRIP):
            lax.fori_loop(0, tc, functools.partial(bstep, h0), 0)

    rows = pl.BlockSpec((hg, tc, N), lambda g, c: (g, nc - 1 - c, 0))
    cols = pl.BlockSpec((hg, N, tc), lambda g, c: (g, 0, nc - 1 - c))
    return _call_carrying(
        body, "scan_bwd", (H // hg, nc),
        [rows] * 5 + [cols, cols, pl.BlockSpec((hg, 1, N, N), lambda g, c: (g, nc - 1 - c, 0, 0))], [rows] * 5 + [cols],
        [jax.ShapeDtypeStruct((H, T, N), F32)] * 5 + [jax.ShapeDtypeStruct((H, N, T), F32)],
        [pltpu.VMEM((hg, tc + 1, N, N), F32), pltpu.VMEM((hg, N, N), F32)], [r, w, k, kk, kka, vT, dyT, chk], comm)


_NT = (((1,), (1,)), ((), ()))
_TN = (((0,), (0,)), ((), ()))


def _scores(q, k, cc, cr, masked):
    s = lax.dot_general(q, k, _NT, preferred_element_type=F32) + cc - cr
    if masked:
        tb = s.shape[0]
        keep = lax.broadcasted_iota(jnp.int32, (tb, tb), 0) >= lax.broadcasted_iota(jnp.int32, (tb, tb), 1)
        s = jnp.where(keep, s, NEG)
    return s


def _attn_specs(T, N, tb):
    blk = pl.BlockSpec((1, tb, N), lambda h, i: (h, i, 0))
    whole = pl.BlockSpec((1, T, N), lambda h, i: (h, 0, 0))
    col = pl.BlockSpec((1, tb, 1), lambda h, i: (h, i, 0))
    wcol = pl.BlockSpec((1, T, 1), lambda h, i: (h, 0, 0))
    row = pl.BlockSpec((1, 1, tb), lambda h, i: (h, 0, i))
    wrow = pl.BlockSpec((1, 1, T), lambda h, i: (h, 0, 0))
    return blk, whole, col, wcol, row, wrow


def _call_carrying(body, name, grid, in_specs, out_specs, out_shape, scratch, args, comm):
    n_out = len(out_specs)
    if comm is not None:
        n = len(comm[0])
        body = _carrying(body, len(in_specs), n_out, grid, comm)
        in_specs, out_specs = in_specs + [_HBM] * n, out_specs + [_HBM] * n
        out_shape, scratch, args = out_shape + _exchange_shapes(*comm), scratch + _exchange_sems(n), args + list(comm[0])
    res = pl.pallas_call(
        body, name=name, grid=grid, in_specs=in_specs, out_specs=out_specs, out_shape=out_shape,
        scratch_shapes=scratch, compiler_params=_cparams(*["arbitrary"] * len(grid)),
    )(*args)
    return res[:n_out], res[n_out:]


def _attn_fwd(q, k, v, ccol, crow, *, tb, comm=None):
    H, T, N = q.shape

    def body(q_ref, k_ref, v_ref, cc_ref, cr_ref, o_ref, lse_ref, m_s, l_s, acc_s):
        qi = pl.program_id(1)
        m_s[...] = jnp.full_like(m_s, NEG)
        l_s[...] = jnp.zeros_like(l_s)
        acc_s[...] = jnp.zeros_like(acc_s)
        q_, cc = q_ref[0], cc_ref[0]

        def block(j, masked):
            at = pl.ds(pl.multiple_of(j * tb, tb), tb)
            s = _scores(q_, k_ref[0, at, :], cc, cr_ref[0, :, at], masked)
            m_new = jnp.maximum(m_s[...], jnp.max(s, axis=1, keepdims=True))
            p = jnp.exp(s - m_new)
            alpha = jnp.exp(m_s[...] - m_new)
            l_s[...] = alpha * l_s[...] + jnp.sum(p, axis=1, keepdims=True)
            acc_s[...] = alpha * acc_s[...] + jnp.dot(p.astype(BF16), v_ref[0, at, :], preferred_element_type=F32)
            m_s[...] = m_new

        def below(j, carry):
            block(j, False)
            return carry

        lax.fori_loop(0, qi, below, 0)
        block(qi, True)
        o_ref[0] = acc_s[...] / l_s[...]
        lse_ref[0] = m_s[...] + jnp.log(l_s[...])

    blk, whole, col, wcol, row, wrow = _attn_specs(T, N, tb)
    return _call_carrying(
        body, "fox_fwd", (H, T // tb), [blk, whole, whole, col, wrow], [blk, col],
        [jax.ShapeDtypeStruct((H, T, N), F32), jax.ShapeDtypeStruct((H, T, 1), F32)],
        [pltpu.VMEM((tb, 1), F32), pltpu.VMEM((tb, 1), F32), pltpu.VMEM((tb, N), F32)],
        [q, k, v, ccol, crow], comm)


def _attn_bwd(q, k, v, ccol, crow, o, lse, do, *, tb, comm=None):
    H, T, N = q.shape
    nb = T // tb

    def body(q_ref, k_ref, v_ref, cc_ref, cr_ref, o_ref, lse_ref, do_ref,
             dq_ref, dcq_ref, dk_ref, dv_ref, dck_ref, dq_s, dcq_s, dk_s, dv_s, dck_s):
        qi = pl.program_id(1)

        @pl.when(qi == 0)
        def _():
            dk_s[...] = jnp.zeros_like(dk_s)
            dv_s[...] = jnp.zeros_like(dv_s)
            dck_s[...] = jnp.zeros_like(dck_s)

        dq_s[...] = jnp.zeros_like(dq_s)
        dcq_s[...] = jnp.zeros_like(dcq_s)
        q_, cc, lse_, do_ = q_ref[0], cc_ref[0], lse_ref[0], do_ref[0]
        delta = jnp.sum(do_ * o_ref[0], axis=1, keepdims=True)
        dob = do_.astype(BF16)

        def block(j, masked):
            at = pl.ds(pl.multiple_of(j * tb, tb), tb)
            kb = k_ref[0, at, :]
            p = jnp.exp(_scores(q_, kb, cc, cr_ref[0, :, at], masked) - lse_)
            dp = lax.dot_general(dob, v_ref[0, at, :], _NT, preferred_element_type=F32)
            ds = p * (dp - delta)
            dsb = ds.astype(BF16)
            dq_s[...] += jnp.dot(dsb, kb, preferred_element_type=F32)
            dcq_s[...] += jnp.sum(ds, axis=1, keepdims=True)
            dv_s[at, :] += lax.dot_general(p.astype(BF16), dob, _TN, preferred_element_type=F32)
            dk_s[at, :] += lax.dot_general(dsb, q_, _TN, preferred_element_type=F32)
            dck_s[:, at] += jnp.sum(ds, axis=0, keepdims=True)

        def below(j, carry):
            block(j, False)
            return carry

        lax.fori_loop(0, qi, below, 0)
        block(qi, True)
        dq_ref[0] = dq_s[...]
        dcq_ref[0] = dcq_s[...]

        @pl.when(qi == nb - 1)
        def _():
            dk_ref[0] = dk_s[...]
            dv_ref[0] = dv_s[...]
            dck_ref[0] = dck_s[...]

    blk, whole, col, wcol, row, wrow = _attn_specs(T, N, tb)
    return _call_carrying(
        body, "fox_bwd", (H, nb), [blk, whole, whole, col, wrow, blk, col, blk], [blk, col, whole, whole, wrow],
        [jax.ShapeDtypeStruct((H, T, N), F32), jax.ShapeDtypeStruct((H, T, 1), F32), jax.ShapeDtypeStruct((H, T, N), F32),
         jax.ShapeDtypeStruct((H, T, N), F32), jax.ShapeDtypeStruct((H, 1, T), F32)],
        [pltpu.VMEM((tb, N), F32), pltpu.VMEM((tb, 1), F32), pltpu.VMEM((T, N), F32), pltpu.VMEM((T, N), F32),
         pltpu.VMEM((1, T), F32)],
        [q, k, v, ccol, crow, o, lse, do], comm)


def _heads(x):
    T = x.shape[0]
    return x.reshape(T, -1, HEAD).transpose(1, 0, 2)


def _headsT(x):
    T = x.shape[0]
    return x.reshape(T, -1, HEAD).transpose(1, 2, 0)


def _unheads(x):
    return x.transpose(1, 0, 2).reshape(x.shape[1], -1)


def _unheadsT(x):
    return x.transpose(2, 0, 1).reshape(x.shape[2], -1)


def _padc(x, n):
    return jnp.pad(x, ((0, 0), (0, n - x.shape[1])))


def _padr(x, n):
    return jnp.pad(x, ((0, n - x.shape[0]), (0, 0)))


class _Dims:
    def __init__(self, W, DL, AL, GL, FH):
        self.W, self.DL, self.AL, self.GL, self.FH = W, DL, AL, GL, FH
        self.DLp, self.ALp, self.GLp, self.FHp = _rup(DL, LANE), _rup(AL, LANE), _rup(GL, LANE), _rup(FH, LANE)
        self.FW = FH * HEAD
        self.RC = 3 * W + DL + AL + GL
        self.RP = 3 * W + self.DLp + self.ALp + self.GLp
        self.FC = 3 * self.FW + FH
        self.FP = 3 * self.FW + self.FHp

    def pad_r(self, a):
        W, o = self.W, 3 * self.W
        return jnp.concatenate([a[:, :o], _padc(a[:, o:o + self.DL], self.DLp),
                                _padc(a[:, o + self.DL:o + self.DL + self.AL], self.ALp),
                                _padc(a[:, o + self.DL + self.AL:self.RC], self.GLp)], axis=1)

    def unpad_r(self, a):
        o = 3 * self.W
        return jnp.concatenate([a[:, :o], a[:, o:o + self.DL], a[:, o + self.DLp:o + self.DLp + self.AL],
                                a[:, o + self.DLp + self.ALp:o + self.DLp + self.ALp + self.GL]], axis=1)

    def pad_f(self, a):
        return _padc(a, self.FP)

    def unpad_f(self, a):
        return a[:, :self.FC]


def _local_step(x, p, tgt, Wt, vec, d, late_shards=None):
    Wt = dict(Wt)
    T, D = x.shape
    W, FW = d.W, d.FW
    H = W // HEAD
    seg, segt = _seg_mats(W)
    segf, segft = _seg_mats(FW)
    T1 = 256
    rk_flat = vec["r_k"].reshape(1, W)
    mu = d.pad_r(vec["shift_mu"])
    qg = jnp.tile(vec["q_norm_g"], (1, d.FH))
    kg = jnp.tile(vec["k_norm_g"], (1, d.FH))
    fb = _padc(vec["fgate_b"], d.FHp)
    pdims = (W, d.DLp, d.ALp, d.GLp)
    fdims = (FW, d.FHp)

    (xn,) = _rowwise(lambda x_, g_: _rms(x_, g_), [x], [vec["attn_norm_g"]], [(D, BF16)], tile=T1, name="norm_attn")
    u_r = _mm(xn, Wt["w_in_r"], name="mm_in_r")
    u_f = _mm(xn, Wt["w_in_f"], name="mm_in_f")

    prep_consts = [mu, vec["w0"], Wt["w2"].astype(F32), vec["a0"], Wt["a2"].astype(F32), Wt["g2"].astype(F32), vec["k_k"], vec["k_a"], seg, segt]

    def prep_fwd(u_, up_, mu_, *cs):
        um, _ = _shift_mix(u_, up_, mu_)
        return _prep_rwkv(um, *cs, dims=pdims)

    r, dec, k2, v, kk, kka, g = _rowwise(prep_fwd, [u_r], prep_consts, [(W, F32)] * 7, tile=128, name="rwkv_prep", prev=[u_r])
    hg, tc = min(16, H), 128
    (yT, chk), gathered = _scan_fwd(_heads(r), _heads(dec), _heads(k2), _heads(kk), _heads(kka), _headsT(v), hg=hg, tc=tc,
                                    comm=late_shards and (late_shards, True))
    Wt.update({n: _whole(n, g) for n, g in zip(_LATE, gathered)})
    y = _unheadsT(yT)
    post_consts = [vec["lnx_g"], vec["lnx_b"], rk_flat, seg, segt]
    (y_r,) = _rowwise(_post_rwkv, [y, r, k2, v, g], post_consts, [(W, BF16)], tile=T1, name="rwkv_post")

    fox_consts = [qg, kg, fb, segf, segft]
    qn, kn, vf, logf = _rowwise(functools.partial(_prep_fox, dims=fdims), [u_f], fox_consts,
                                [(FW, BF16), (FW, BF16), (FW, BF16), (d.FHp, F32)], tile=T1, name="fox_prep")
    c = _cumsum(logf, reverse=False, name="fox_cumsum")
    cT = c[:, :d.FH].T
    ccol, crow = cT[:, :, None], cT[:, None, :]
    tb = _pick(T, (1024, 512, 256, 128))
    qh, kh, vh = _heads(qn), _heads(kn), _heads(vf)
    (o, lse), _ = _attn_fwd(qh, kh, vh, ccol, crow, tb=tb)
    y_f = _unheads(o)

    ycat = jnp.concatenate([y_r, y_f.astype(BF16)], axis=1)
    h1 = _mm(ycat, Wt["w_out"], add=x, name="mm_out")
    (hn,) = _rowwise(lambda h_, g_: _rms(h_, g_), [h1], [vec["ffn_norm_g"]], [(D, BF16)], tile=T1, name="norm_ffn")
    gt = _mm(hn, Wt["w_gate"], name="mm_gate")
    up = _mm(hn, Wt["w_up"], name="mm_up")
    (act,) = _rowwise(_swiglu, [gt, up], [], [(gt.shape[1], BF16)], tile=T1, name="swiglu")
    h2 = _mm(act, Wt["w_down"], add=h1, name="mm_down")
    (hg_,) = _rowwise(lambda h_, g_: _rms(h_, g_), [h2], [vec["ple_gate_norm_g"]], [(D, BF16)], tile=T1, name="norm_gate")
    pe = _mm(p, Wt["ple_proj"], name="mm_ple")
    z = _mm(hg_, Wt["ple_gate_w"], name="mm_pgate")

    def tail(h2_, pe_, z_, tg_, png_, pgb_):
        h3, f = jax.vjp(_tail, h2_, pe_, z_, png_, pgb_)
        err = h3 - tg_
        dh3 = err * (1.0 / D)
        lt = 0.5 * jnp.sum(jnp.sum(err * err, axis=1, keepdims=True) * (1.0 / D), axis=0, keepdims=True)
        dh2_, dpe_, dz_, dpng_, dpgb_ = f(dh3)
        return dh2_, dpe_, dz_, jnp.broadcast_to(lt, (1, LANE)), dpng_, dpgb_

    dh3, dpe, dz, loss, g_png, g_pgb = _rowwise(
        tail, [h2, pe, z, tgt], [vec["ple_norm_g"], vec["ple_gate_b"]], [(D, F32), (D, BF16), (D, BF16)],
        [(1, LANE), (1, D), (1, D)], tile=T1, name="tail")
    G = {}
    gv = {"ple_norm_g": g_png, "ple_gate_b": g_pgb}
    G["ple_gate_w"] = _mm(hg_, dz, ta=True, out_dtype=BF16, name="gw_pgate")
    G["ple_proj"] = _mm(p, dpe, ta=True, out_dtype=BF16, name="gw_ple")
    d_hg = _mm(dz, Wt["ple_gate_w"], tb=True, name="mmb_pgate")

    def norm_bwd(h_, dres_, dn_, g_):
        _, f = jax.vjp(_rms, h_, g_)
        dh_, dg_ = f(dn_)
        return dres_ + dh_, dg_

    dh2, gv["ple_gate_norm_g"] = _rowwise(norm_bwd, [h2, dh3, d_hg], [vec["ple_gate_norm_g"]], [(D, F32)], [(1, D)],
                                          tile=T1, name="norm_gate_bwd")
    G["w_down"] = _mm(act, dh2, ta=True, out_dtype=BF16, name="gw_down")
    d_act = _mm(dh2, Wt["w_down"], tb=True, name="mmb_down")

    def swiglu_bwd(gt_, up_, da_):
        _, f = jax.vjp(_swiglu, gt_, up_)
        return f(da_)

    d_gt, d_up = _rowwise(swiglu_bwd, [gt, up, d_act], [], [(gt.shape[1], BF16)] * 2, tile=T1, name="swiglu_bwd")
    G["w_gate"] = _mm(hn, d_gt, ta=True, out_dtype=BF16, name="gw_gate")
    G["w_up"] = _mm(hn, d_up, ta=True, out_dtype=BF16, name="gw_up")
    d_hn = _mm(d_gt, Wt["w_gate"], tb=True, name="mmb_gate")
    d_hn = _mm(d_up, Wt["w_up"], tb=True, add=d_hn, name="mmb_up")
    dh1, gv["ffn_norm_g"] = _rowwise(norm_bwd, [h1, dh2, d_hn], [vec["ffn_norm_g"]], [(D, F32)], [(1, D)],
                                     tile=T1, name="norm_ffn_bwd")
    G["w_out"] = _mm(ycat, dh1, ta=True, out_dtype=BF16, name="gw_out")
    d_ycat = _mm(dh1, Wt["w_out"], tb=True, name="mmb_out")

    def post_bwd(y_, r_, k2_, v_, g_, dy_, *cs):
        lg, lb, rk, sg, sgt = cs
        _, f = jax.vjp(lambda *a: _post_rwkv(*a, sg, sgt), y_, r_, k2_, v_, g_, lg, lb, rk)
        return f(dy_)

    dy, dr1, dk1, dv1, dg, gv["lnx_g"], gv["lnx_b"], g_rk = _rowwise(
        post_bwd, [y, r, k2, v, g, (d_ycat, W, 0)], post_consts, [(W, F32)] * 5, [(1, W)] * 3, tile=128, name="rwkv_post_bwd")
    gv["r_k"] = g_rk.reshape(H, HEAD)
    pieces = late_shards and ([_pieces(n, G[n]).astype(BF16) for n in _LATE], False)
    (dr, ddec, dk2, dkk, dkka, dvT), recv = _scan_bwd(_heads(r), _heads(dec), _heads(k2), _heads(kk), _heads(kka),
                                                      _headsT(v), _headsT(dy), chk, hg=min(8, H), tc=tc, comm=pieces)
    recv = dict(zip(_LATE, recv))
    dr, ddec, dk2, dkk, dkka, dv = (_unheads(dr), _unheads(ddec), _unheads(dk2), _unheads(dkk), _unheads(dkka), _unheadsT(dvT))

    def prep_bwd(u_, dr_, dr1_, ddec_, dk2_, dk1_, dv_, dv1_, dkk_, dkka_, dg_, up_, mu_, *cs):
        um, sh = _shift_mix(u_, up_, mu_)
        cs_d, sg, sgt = cs[:7], cs[7], cs[8]
        _, f = jax.vjp(lambda um_, *c_: _prep_rwkv(um_, *c_, sg, sgt, dims=pdims), um, *cs_d)
        res = f((dr_ + dr1_, ddec_, dk2_ + dk1_, dv_ + dv1_, dkk_, dkka_, dg_))
        dum = res[0]
        dmu = jnp.sum(dum * (sh - u_), axis=0, keepdims=True)
        return (dum, dmu) + tuple(res[1:])

    LP = [Wt["w2"].shape, Wt["a2"].shape, Wt["g2"].shape]
    dum, g_mu, gv["w0"], g_w2, gv["a0"], g_a2, g_g2, gv["k_k"], gv["k_a"] = _rowwise(
        prep_bwd, [u_r, dr, dr1, ddec, dk2, dk1, dv, dv1, dkk, dkka, dg], prep_consts, [(d.RP, F32)],
        [(1, d.RP), (1, W), LP[0], (1, W), LP[1], LP[2], (1, W), (1, W)], tile=128, name="rwkv_prep_bwd", prev=[u_r])
    gv["shift_mu"] = d.unpad_r(g_mu)
    G["w2"], G["a2"], G["g2"] = g_w2, g_a2, g_g2
    (du_r,) = _rowwise(lambda a_, an_, mu_: a_ * (1.0 - mu_) + jnp.where(
        lax.broadcasted_iota(jnp.int32, a_.shape, 0) == a_.shape[0] - 1, an_, pltpu.roll(a_, a_.shape[0] - 1, 0)) * mu_,
        [dum], [mu], [(d.RP, BF16)], tile=T1, name="shift_bwd", nxt=[dum])

    do = _heads(d_ycat[:, W:])
    (dq, dcq, dk_, dv_, dck), _ = _attn_bwd(qh, kh, vh, ccol, crow, o, lse, do, tb=tb)
    dc = _padc((dcq[:, :, 0] - dck[:, 0, :]).T, d.FHp)
    dlogf = _cumsum(dc, reverse=True, name="fox_cumsum_bwd")

    def fox_bwd(uf_, dq_, dk__, dv__, dlf_, *cs):
        qg_, kg_, fb_, sg, sgt = cs
        _, f = jax.vjp(lambda uf__, a, b, c_: _prep_fox(uf__, a, b, c_, sg, sgt, dims=fdims), uf_, qg_, kg_, fb_)
        return f((dq_, dk__, dv__, dlf_))

    du_f, g_qg, g_kg, g_fb = _rowwise(fox_bwd, [u_f, _unheads(dq), _unheads(dk_), _unheads(dv_), dlogf], fox_consts,
                                      [(d.FP, BF16)], [(1, FW), (1, FW), (1, d.FHp)], tile=T1, name="fox_prep_bwd")
    gv["q_norm_g"] = g_qg.reshape(d.FH, HEAD).sum(0, keepdims=True)
    gv["k_norm_g"] = g_kg.reshape(d.FH, HEAD).sum(0, keepdims=True)
    gv["fgate_b"] = g_fb[:, :d.FH]

    G["w_in_r"] = _mm(xn, du_r, ta=True, out_dtype=BF16, name="gw_in_r")
    G["w_in_f"] = _mm(xn, du_f, ta=True, out_dtype=BF16, name="gw_in_f")
    d_xn = _mm(du_r, Wt["w_in_r"], tb=True, name="mmb_in_r")
    d_xn = _mm(du_f, Wt["w_in_f"], tb=True, add=d_xn, name="mmb_in_f")
    dx, gv["attn_norm_g"] = _rowwise(norm_bwd, [x, dh1, d_xn], [vec["attn_norm_g"]], [(D, F32)], [(1, D)],
                                     tile=T1, name="norm_attn_bwd")
    return loss, dx, G, gv, recv


_HBM = pl.BlockSpec(memory_space=pl.ANY)
_OTHER_CHIPS = ((0, 1), (1, 0), (1, 1))


def _flip(v, bit):
    return 1 - v if bit else v


def _exchange(arrs, *, gather, name):
    n = len(arrs)

    def body(*refs):
        copies = _exchange_copies(refs[:n], refs[n:2 * n], refs[2 * n:], gather)
        for cp in copies:
            cp.start()
        for cp in copies:
            cp.wait()

    return pl.pallas_call(
        body, name=name, in_specs=[_HBM] * n, out_specs=[_HBM] * n, out_shape=_exchange_shapes(arrs, gather),
        scratch_shapes=_exchange_sems(n),
    )(*arrs)


def _exchange_shapes(arrs, gather):
    return [jax.ShapeDtypeStruct(((N_CHIPS,) + a.shape) if gather else a.shape, a.dtype) for a in arrs]


def _exchange_sems(n):
    return [pltpu.SemaphoreType.DMA((3 * n,)), pltpu.SemaphoreType.DMA((3 * n,)), pltpu.SemaphoreType.DMA((n,))]


def _exchange_copies(ins, outs, sems, gather):
    send_sems, recv_sems, own_sems = sems
    x, y, c = lax.axis_index("x"), lax.axis_index("y"), lax.axis_index("c")
    me = 2 * x + y
    copies = []
    for a in range(len(ins)):
        copies.append(pltpu.make_async_copy(ins[a] if gather else ins[a].at[me], outs[a].at[me], own_sems.at[a]))
        for k, (dx, dy) in enumerate(_OTHER_CHIPS):
            px, py = _flip(x, dx), _flip(y, dy)
            copies.append(pltpu.make_async_remote_copy(
                src_ref=ins[a] if gather else ins[a].at[2 * px + py], dst_ref=outs[a].at[me],
                send_sem=send_sems.at[3 * a + k], recv_sem=recv_sems.at[3 * a + k],
                device_id=(px, py, c), device_id_type=MESH))
    return copies


def _carrying(body, n_in, n_out, grid, comm):
    arrs, gather = comm
    n = len(arrs)

    def wrapped(*refs):
        c_in = refs[n_in:n_in + n]
        c_out = refs[n_in + n + n_out:n_in + 2 * n + n_out]
        ids = [pl.program_id(a) for a in range(len(grid))]
        first = functools.reduce(jnp.logical_and, [i == 0 for i in ids])
        last = functools.reduce(jnp.logical_and, [i == g - 1 for i, g in zip(ids, grid)])

        @pl.when(first)
        def _():
            for cp in _exchange_copies(c_in, c_out, refs[-3:], gather):
                cp.start()

        body(*refs[:n_in], *refs[n_in + n:n_in + n + n_out], *refs[n_in + 2 * n + n_out:-3])

        @pl.when(last)
        def _():
            for cp in _exchange_copies(c_in, c_out, refs[-3:], gather):
                cp.wait()

    return wrapped


def _swap_cores(arrs, *, name):
    n = len(arrs)

    def body(*refs):
        ins, outs = refs[:n], refs[n:2 * n]
        send_sems, recv_sems = refs[2 * n:]
        peer = (lax.axis_index("x"), lax.axis_index("y"), 1 - lax.axis_index("c"))
        cps = [pltpu.make_async_remote_copy(src_ref=ins[a], dst_ref=outs[a], send_sem=send_sems.at[a],
                                            recv_sem=recv_sems.at[a], device_id=peer, device_id_type=MESH) for a in range(n)]
        for cp in cps:
            cp.start()
        for cp in cps:
            cp.wait()

    return pl.pallas_call(
        body, name=name, in_specs=[_HBM] * n, out_specs=[_HBM] * n,
        out_shape=[jax.ShapeDtypeStruct(a.shape, a.dtype) for a in arrs],
        scratch_shapes=[pltpu.SemaphoreType.DMA((n,)), pltpu.SemaphoreType.DMA((n,))],
    )(*arrs)


def _allreduce_small(pack, *, name):
    R, C = pack.shape

    def body(p_ref, o_ref, recv, send_sems, recv_sems):
        x, y, c = lax.axis_index("x"), lax.axis_index("y"), lax.axis_index("c")
        me = 4 * x + 2 * y + c
        recv[me] = p_ref[...]
        cps = []
        for k in range(1, N_DEV):
            peer = (_flip(x, k & 4), _flip(y, k & 2), _flip(c, k & 1))
            cp = pltpu.make_async_remote_copy(src_ref=p_ref, dst_ref=recv.at[me], send_sem=send_sems.at[k - 1],
                                              recv_sem=recv_sems.at[k - 1], device_id=peer, device_id_type=MESH)
            cp.start()
            cps.append(cp)
        for cp in cps:
            cp.wait()
        acc = recv[0]
        for s in range(1, N_DEV):
            acc = acc + recv[s]
        o_ref[...] = acc

    vm = pl.BlockSpec(memory_space=pltpu.VMEM)
    return pl.pallas_call(
        body, name=name, in_specs=[vm], out_specs=vm, out_shape=jax.ShapeDtypeStruct((R, C), F32),
        scratch_shapes=[pltpu.VMEM((N_DEV, R, C), F32), pltpu.SemaphoreType.DMA((N_DEV - 1,)), pltpu.SemaphoreType.DMA((N_DEV - 1,))],
    )(pack)


def _sum_slots(a, *, name):
    S, R, C = a.shape
    tr = _pick(R, (256, 128, 64, 32, 16, 8))

    def body(a_ref, o_ref):
        acc = a_ref[0].astype(F32)
        for s in range(1, S):
            acc = acc + a_ref[s].astype(F32)
        o_ref[...] = acc

    return pl.pallas_call(
        body, name=name, grid=(R // tr,), in_specs=[pl.BlockSpec((S, tr, C), lambda i: (0, i, 0))],
        out_specs=pl.BlockSpec((tr, C), lambda i: (i, 0)), out_shape=jax.ShapeDtypeStruct((R, C), F32),
        compiler_params=_cparams("parallel"),
    )(a)


def _adamw(w, m, v, gs, *, name):
    R, C = w.shape
    tile = _pick(R, (128, 96, 64, 32, 16, 8))

    def fn(w_, m_, v_, *g_):
        g = g_[0]
        for e in g_[1:]:
            g = g + e
        m2 = ADAM_B1 * m_ + (1.0 - ADAM_B1) * g
        v2 = ADAM_B2 * v_ + (1.0 - ADAM_B2) * jnp.square(g)
        m_hat = m2 / (1.0 - ADAM_B1 ** ADAM_STEP)
        v_hat = v2 / (1.0 - ADAM_B2 ** ADAM_STEP)
        delta = -ADAM_LR * (m_hat / (jnp.sqrt(v_hat) + ADAM_EPS) + ADAM_WD * w_)
        return g, delta, m2, v2

    return _rowwise(fn, [w, m, v, *gs], [], [(C, F32)] * 4, tile=tile, name=name)


_ARGS = "x, p, attn_norm_g, w_in, shift_mu, w0, w2, a0, a2, g2, k_k, k_a, r_k, lnx_g, lnx_b, q_norm_g, k_norm_g, fgate_b, w_out, ffn_norm_g, w_gate, w_up, w_down, ple_proj, ple_norm_g, ple_gate_norm_g, ple_gate_w, ple_gate_b, loss_target, m_attn_norm_g, m_w_in, m_shift_mu, m_w0, m_w2, m_a0, m_a2, m_g2, m_k_k, m_k_a, m_r_k, m_lnx_g, m_lnx_b, m_q_norm_g, m_k_norm_g, m_fgate_b, m_w_out, m_ffn_norm_g, m_w_gate, m_w_up, m_w_down, m_ple_proj, m_ple_norm_g, m_ple_gate_norm_g, m_ple_gate_w, m_ple_gate_b, v_attn_norm_g, v_w_in, v_shift_mu, v_w0, v_w2, v_a0, v_a2, v_g2, v_k_k, v_k_a, v_r_k, v_lnx_g, v_lnx_b, v_q_norm_g, v_k_norm_g, v_fgate_b, v_w_out, v_ffn_norm_g, v_w_gate, v_w_up, v_w_down, v_ple_proj, v_ple_norm_g, v_ple_gate_norm_g, v_ple_gate_w, v_ple_gate_b".split(", ")
_WEIGHTS = _ARGS[2:28]
_COL_SHARDED = ("w_in", "w2", "a2", "g2", "w_gate", "w_up", "ple_proj")
_ROW_SHARDED = ("w_out", "w_down", "ple_gate_w")
_MATRICES = _COL_SHARDED + _ROW_SHARDED
_EARLY = ("w_in", "w2", "a2", "g2")
_LATE = tuple(n for n in _MATRICES if n not in _EARLY)
_VECTORS = tuple(n for n in _WEIGHTS if n not in _MATRICES)


def _whole(name, g):
    if name in _COL_SHARDED:
        return g.transpose(1, 0, 2).reshape(g.shape[1], -1)
    return g.reshape(-1, g.shape[2])


def _pieces(name, a):
    if name in _COL_SHARDED:
        return a.reshape(a.shape[0], N_CHIPS, -1).transpose(1, 0, 2)
    return a.reshape(N_CHIPS, -1, a.shape[1])


def kernel(x, p, attn_norm_g, w_in, shift_mu, w0, w2, a0, a2, g2, k_k, k_a, r_k, lnx_g, lnx_b, q_norm_g, k_norm_g, fgate_b, w_out, ffn_norm_g, w_gate, w_up, w_down, ple_proj, ple_norm_g, ple_gate_norm_g, ple_gate_w, ple_gate_b, loss_target, m_attn_norm_g, m_w_in, m_shift_mu, m_w0, m_w2, m_a0, m_a2, m_g2, m_k_k, m_k_a, m_r_k, m_lnx_g, m_lnx_b, m_q_norm_g, m_k_norm_g, m_fgate_b, m_w_out, m_ffn_norm_g, m_w_gate, m_w_up, m_w_down, m_ple_proj, m_ple_norm_g, m_ple_gate_norm_g, m_ple_gate_w, m_ple_gate_b, v_attn_norm_g, v_w_in, v_shift_mu, v_w0, v_w2, v_a0, v_a2, v_g2, v_k_k, v_k_a, v_r_k, v_lnx_g, v_lnx_b, v_q_norm_g, v_k_norm_g, v_fgate_b, v_w_out, v_ffn_norm_g, v_w_gate, v_w_up, v_w_down, v_ple_proj, v_ple_norm_g, v_ple_gate_norm_g, v_ple_gate_w, v_ple_gate_b):
    A = dict(zip(_ARGS, (x, p, attn_norm_g, w_in, shift_mu, w0, w2, a0, a2, g2, k_k, k_a, r_k, lnx_g, lnx_b, q_norm_g, k_norm_g, fgate_b, w_out, ffn_norm_g, w_gate, w_up, w_down, ple_proj, ple_norm_g, ple_gate_norm_g, ple_gate_w, ple_gate_b, loss_target, m_attn_norm_g, m_w_in, m_shift_mu, m_w0, m_w2, m_a0, m_a2, m_g2, m_k_k, m_k_a, m_r_k, m_lnx_g, m_lnx_b, m_q_norm_g, m_k_norm_g, m_fgate_b, m_w_out, m_ffn_norm_g, m_w_gate, m_w_up, m_w_down, m_ple_proj, m_ple_norm_g, m_ple_gate_norm_g, m_ple_gate_w, m_ple_gate_b, v_attn_norm_g, v_w_in, v_shift_mu, v_w0, v_w2, v_a0, v_a2, v_g2, v_k_k, v_k_a, v_r_k, v_lnx_g, v_lnx_b, v_q_norm_g, v_k_norm_g, v_fgate_b, v_w_out, v_ffn_norm_g, v_w_gate, v_w_up, v_w_down, v_ple_proj, v_ple_norm_g, v_ple_gate_norm_g, v_ple_gate_w, v_ple_gate_b)))
    x, p, tgt = A["x"][0], A["p"][0, 0], A["loss_target"][0]
    d = _Dims(W=A["w0"].shape[-1], DL=A["w2"].shape[1], AL=A["a2"].shape[1], GL=A["g2"].shape[1], FH=A["fgate_b"].shape[-1])

    shard = lambda n: A[n][0].astype(BF16)
    gathered = _exchange([shard(n) for n in _EARLY], gather=True, name="gather_early")
    full = {n: _whole(n, g) for n, g in zip(_EARLY, gathered)}
    Wt = {"w_in_r": d.pad_r(full["w_in"][:, :d.RC]), "w_in_f": d.pad_f(full["w_in"][:, d.RC:]),
          "w2": _padr(full["w2"], d.DLp), "a2": _padr(full["a2"], d.ALp), "g2": _padr(full["g2"], d.GLp)}
    vec = {n: A[n].reshape(-1, A[n].shape[-1]) for n in _VECTORS}

    loss, dx, G, gv, recv = _local_step(x, p, tgt, Wt, vec, d, late_shards=[shard(n) for n in _LATE])

    gw = {"w_in": jnp.concatenate([d.unpad_r(G["w_in_r"]), d.unpad_f(G["w_in_f"])], axis=1),
          "w2": G["w2"][:d.DL], "a2": G["a2"][:d.AL], "g2": G["g2"][:d.GL]}
    recv.update(zip(_EARLY, _exchange([_pieces(n, gw[n]).astype(BF16) for n in _EARLY], gather=False, name="scatter_early")))
    part = [_sum_slots(recv[n], name="sum_" + n) for n in _MATRICES]
    sib = _swap_cores(part, name="swap_cores")

    sizes = [1] + [A[n].size for n in _VECTORS]
    rows = _rup(_rup(sum(sizes), LANE) // LANE, 8)

    def pack(items):
        flat = jnp.concatenate([i.reshape(-1) for i in items])
        return jnp.pad(flat, (0, rows * LANE - flat.shape[0])).reshape(rows, LANE)

    red = _allreduce_small(pack([loss[0, :1]] + [gv[n] for n in _VECTORS]), name="allreduce_vectors")
    zero = jnp.zeros((1,), F32)
    upd = _adamw(pack([zero] + [A[n] for n in _VECTORS]), pack([zero] + [A["m_" + n] for n in _VECTORS]),
                 pack([zero + 1.0] + [A["v_" + n] for n in _VECTORS]), [red], name="adamw_vectors")
    offs = [0]
    for s in sizes:
        offs.append(offs[-1] + s)
    unpack = lambda a, i, n: a.reshape(-1)[offs[i + 1]:offs[i + 2]].reshape(A[n].shape)

    out = {"grad": {}, "delta": {}, "new_m": {}, "new_v": {}}
    for i, n in enumerate(_VECTORS):
        for kind, a in zip(out, upd):
            out[kind][n] = unpack(a, i, n)
    for n, mine, other in zip(_MATRICES, part, sib):
        res = _adamw(A[n][0], A["m_" + n][0], A["v_" + n][0], [mine, other], name="adamw_" + n)
        for kind, a in zip(out, res):
            out[kind][n] = a[None]
    return (red[0, 0], dx[None], *[out[k][n] for k in out for n in _WEIGHTS])
```

```python
import functools

import jax
import jax.numpy as jnp
from jax import lax
from jax.experimental import pallas as pl
from jax.experimental.pallas import tpu as pltpu

F32 = jnp.float32
BF16 = jnp.bfloat16
LANE = 128
HEAD = 64
RMS_EPS = 1e-6
GN_EPS = 64e-5
ADAM_LR, ADAM_B1, ADAM_B2, ADAM_EPS, ADAM_WD, ADAM_STEP = 0.001, 0.9, 0.999, 1e-08, 0.01, 10
VMEM_LIMIT = 56 * 1024 * 1024
MM_TILE_BYTES = 40 * 1024 * 1024
NEG = -1e30
MESH = pl.DeviceIdType.MESH
N_CHIPS = 4
N_DEV = 8


def _rup(n, m):
    return -(-n // m) * m


def _pick(n, cands):
    for c in cands:
        if n % c == 0:
            return c
    return n


def _cparams(*sem):
    return pltpu.CompilerParams(dimension_semantics=sem, vmem_limit_bytes=VMEM_LIMIT)


def _mm(a, b, *, ta=False, tb=False, add=None, out_dtype=F32, name):
    M, K = (a.shape[1], a.shape[0]) if ta else a.shape
    N = b.shape[0] if tb else b.shape[1]
    tn = _pick(N, (512, 640, 256, 128))
    fits = lambda m, t: 2 * (m * t * a.dtype.itemsize + t * tn * b.dtype.itemsize + m * tn * 8) <= MM_TILE_BYTES
    tm, tk = next((m, t) for t in (K, 2048, 1024, 512, 640, 256, 128) for m in (1024, 512, 256, 128, M)
                  if K % t == 0 and M % m == 0 and fits(m, t))
    nk = K // tk
    dn = (((0 if ta else 1,), (1 if tb else 0,)), ((), ()))

    def body(*refs):
        if add is None:
            a_ref, b_ref, o_ref, acc = refs
        else:
            a_ref, b_ref, add_ref, o_ref, acc = refs
        ks = pl.program_id(2)
        part = lax.dot_general(a_ref[...].astype(BF16), b_ref[...].astype(BF16), dn, preferred_element_type=F32)
        if nk > 1:
            @pl.when(ks == 0)
            def _():
                acc[...] = jnp.zeros_like(acc)

            acc[...] += part

        @pl.when(ks == nk - 1)
        def _():
            res = acc[...] if nk > 1 else part
            if add is not None:
                res = res + add_ref[...].astype(F32)
            o_ref[...] = res.astype(out_dtype)

    a_spec = pl.BlockSpec((tk, tm), lambda i, j, k: (k, i)) if ta else pl.BlockSpec((tm, tk), lambda i, j, k: (i, k))
    b_spec = pl.BlockSpec((tn, tk), lambda i, j, k: (j, k)) if tb else pl.BlockSpec((tk, tn), lambda i, j, k: (k, j))
    o_spec = pl.BlockSpec((tm, tn), lambda i, j, k: (i, j))
    ins, specs = [a, b], [a_spec, b_spec]
    if add is not None:
        ins.append(add)
        specs.append(o_spec)
    return pl.pallas_call(
        body, name=name, grid=(M // tm, N // tn, nk), in_specs=specs, out_specs=o_spec,
        out_shape=jax.ShapeDtypeStruct((M, N), out_dtype),
        scratch_shapes=[pltpu.VMEM((tm, tn) if nk > 1 else (8, LANE), F32)],
        compiler_params=_cparams("parallel", "parallel", "arbitrary"),
    )(*ins)


def _rowwise(fn, rows, consts, outs, accs=(), *, tile, name, prev=(), nxt=()):
    rows = [r if isinstance(r, tuple) else (r, r.shape[1], 0) for r in rows]
    T = rows[0][0].shape[0]
    tile = min(tile, T)
    n = T // tile
    sub = 8
    nr, npv, nnx, ncst, no, na = len(rows), len(prev), len(nxt), len(consts), len(outs), len(accs)

    def body(*refs):
        i = pl.program_id(0)
        it = iter(refs)
        rv = [next(it)[...] for _ in range(nr)]
        pv = [jnp.where(i > 0, next(it)[sub - 1:sub, :], 0.0) for _ in range(npv)]
        nv = [jnp.where(i < n - 1, next(it)[0:1, :], 0.0) for _ in range(nnx)]
        cv = [next(it)[...] for _ in range(ncst)]
        o_refs = [next(it) for _ in range(no)]
        a_refs = [next(it) for _ in range(na)]
        res = fn(*rv, *pv, *nv, *cv)
        if not isinstance(res, (tuple, list)):
            res = (res,)
        for r, o in zip(o_refs, res[:no]):
            r[...] = o.astype(r.dtype)
        if na:
            @pl.when(i == 0)
            def _():
                for r in a_refs:
                    r[...] = jnp.zeros_like(r)
            for r, o in zip(a_refs, res[no:]):
                r[...] += o.astype(F32)

    in_specs = [pl.BlockSpec((tile, w), functools.partial(lambda cb, i: (i, cb), cb)) for _, w, cb in rows]
    in_specs += [pl.BlockSpec((sub, a.shape[1]), lambda i: (jnp.maximum(i * (tile // sub) - 1, 0), 0)) for a in prev]
    in_specs += [pl.BlockSpec((sub, a.shape[1]), lambda i: (jnp.minimum((i + 1) * (tile // sub), T // sub - 1), 0)) for a in nxt]
    in_specs += [pl.BlockSpec(c.shape, lambda i: (0, 0)) for c in consts]
    out_specs = [pl.BlockSpec((tile, c), lambda i: (i, 0)) for c, _ in outs]
    out_specs += [pl.BlockSpec(s, lambda i: (0, 0)) for s in accs]
    out_shape = [jax.ShapeDtypeStruct((T, c), d) for c, d in outs] + [jax.ShapeDtypeStruct(s, F32) for s in accs]
    res = pl.pallas_call(
        body, name=name, grid=(n,), in_specs=in_specs, out_specs=out_specs, out_shape=out_shape,
        compiler_params=_cparams("arbitrary"),
    )(*[r[0] for r in rows], *prev, *nxt, *consts)
    return res


@jax.custom_vjp
def _bdot(a, b):
    return jnp.dot(a.astype(BF16), b.astype(BF16), preferred_element_type=F32)


def _bdot_fwd(a, b):
    return _bdot(a, b), (a.astype(BF16), b.astype(BF16))


def _bdot_bwd(res, ct):
    a, b = res
    c = ct.astype(BF16)
    return (lax.dot_general(c, b, (((1,), (1,)), ((), ())), preferred_element_type=F32),
            lax.dot_general(a, c, (((0,), (0,)), ((), ())), preferred_element_type=F32))


_bdot.defvjp(_bdot_fwd, _bdot_bwd)


def _xdot(a, b):
    return jnp.dot(a, b, precision=lax.Precision.HIGHEST, preferred_element_type=F32)


def _rms(x, g, eps=RMS_EPS):
    return x * lax.rsqrt(jnp.mean(x * x, axis=-1, keepdims=True) + eps) * g


def _softplus(x):
    return jnp.maximum(x, 0.0) + jnp.log(1.0 + jnp.exp(-jnp.abs(x)))


def _sigmoid(x):
    return 1.0 / (1.0 + jnp.exp(-x))


def _seg_mats(width):
    h = lax.broadcasted_iota(jnp.int32, (width, LANE), 0) // HEAD
    j = lax.broadcasted_iota(jnp.int32, (width, LANE), 1)
    seg = (h == j).astype(F32)
    return seg, seg.T


def _prep_rwkv(um, w0, w2, a0, a2, g2, k_k, k_a, seg, segt, *, dims):
    W, DLp, ALp, GLp = dims
    r, k, v = um[:, :W], um[:, W:2 * W], um[:, 2 * W:3 * W]
    o = 3 * W
    xw, xa, xg = um[:, o:o + DLp], um[:, o + DLp:o + DLp + ALp], um[:, o + DLp + ALp:o + DLp + ALp + GLp]
    w_log = -_softplus(-(w0 + _bdot(jnp.tanh(xw), w2))) - 0.5
    decay = jnp.exp(-jnp.exp(w_log))
    a = _sigmoid(a0 + _bdot(xa, a2))
    g = _bdot(_sigmoid(xg), g2)
    kk = k * k_k
    nrm = jnp.sqrt(_xdot(_xdot(kk * kk, seg), segt))
    kk = kk / jnp.maximum(nrm, 1e-12)
    k2 = k * (1.0 + (a - 1.0) * k_a)
    return r, decay, k2, v, kk, kk * a, g


def _shift_mix(u, uprev, mu):
    first = lax.broadcasted_iota(jnp.int32, u.shape, 0) == 0
    sh = jnp.where(first, uprev, pltpu.roll(u, 1, 0))
    return u + (sh - u) * mu, sh


def _post_rwkv(y, r, k2, v, g, lnx_g, lnx_b, r_k, seg, segt):
    inv = 1.0 / HEAD
    mean = _xdot(_xdot(y, seg), segt) * inv
    yc = y - mean
    var = _xdot(_xdot(yc * yc, seg), segt) * inv
    yn = yc * lax.rsqrt(var + GN_EPS) * lnx_g + lnx_b
    bonus = _xdot(_xdot(r * k2 * r_k, seg), segt) * v
    return (yn + bonus) * g


def _prep_fox(uf, qg, kg, fb, seg, segt, *, dims):
    FW, FHp = dims
    q, k, v, f = uf[:, :FW], uf[:, FW:2 * FW], uf[:, 2 * FW:3 * FW], uf[:, 3 * FW:3 * FW + FHp]
    inv = 1.0 / HEAD
    qn = q * lax.rsqrt(_xdot(_xdot(q * q, seg), segt) * inv + RMS_EPS) * qg * (HEAD ** -0.5)
    kn = k * lax.rsqrt(_xdot(_xdot(k * k, seg), segt) * inv + RMS_EPS) * kg
    return qn, kn, v, -_softplus(-(f + fb))


def _tail(h2, pe, z, png, pgb):
    return h2 + _sigmoid(z + pgb) * _rms(pe, png)


def _swiglu(gt, up):
    return gt * _sigmoid(gt) * up


def _cumsum(x, *, reverse, name):
    T, C = x.shape
    tc = _pick(T, (256, 128))
    n = T // tc
    i0 = lax.broadcasted_iota(jnp.int32, (tc, tc), 0)
    i1 = lax.broadcasted_iota(jnp.int32, (tc, tc), 1)
    tri = ((i0 <= i1) if reverse else (i0 >= i1)).astype(BF16)

    def body(x_ref, tri_ref, o_ref, carry):
        i = pl.program_id(0)

        @pl.when(i == 0)
        def _():
            carry[...] = jnp.zeros_like(carry)

        v = x_ref[...]
        hi = v.astype(BF16)
        r1 = v - hi.astype(F32)
        mid = r1.astype(BF16)
        lo = (r1 - mid.astype(F32)).astype(BF16)
        t = tri_ref[...]
        d = lambda p: jnp.dot(t, p, preferred_element_type=F32)
        c = d(hi) + d(mid) + d(lo) + carry[0:1, :]
        o_ref[...] = c
        edge = c[0:1, :] if reverse else c[tc - 1:tc, :]
        carry[...] = jnp.broadcast_to(edge, carry.shape)

    blk = pl.BlockSpec((tc, C), (lambda i: (n - 1 - i, 0)) if reverse else (lambda i: (i, 0)))
    return pl.pallas_call(
        body, name=name, grid=(n,), in_specs=[blk, pl.BlockSpec((tc, tc), lambda i: (0, 0))], out_specs=blk,
        out_shape=jax.ShapeDtypeStruct((T, C), F32), scratch_shapes=[pltpu.VMEM((8, C), F32)],
        compiler_params=_cparams("arbitrary"),
    )(x, tri)


BWD_PAIRS_PER_TRIP = 8
FWD_STEPS_PER_TRIP = 2
RECOMPUTE_STEPS_PER_TRIP = 4


def _steps(n, per_trip, step):
    def trip(i, carry):
        for j in range(per_trip):
            carry = step(i * per_trip + j, carry)
        return carry

    lax.fori_loop(0, n // per_trip, trip, 0)


def _col(tile, lane, t):
    return jnp.sum(jnp.where(lane == t, tile, 0.0), axis=1, keepdims=True)


def _scan_fwd(r, w, k, kk, kka, vT, *, hg, tc, comm=None):
    H, T, N = r.shape
    nc = T // tc

    def body(r_ref, w_ref, k_ref, kk_ref, kka_ref, vT_ref, yT_ref, chk_ref, s_ref):
        @pl.when(pl.program_id(1) == 0)
        def _():
            s_ref[...] = jnp.zeros_like(s_ref)

        chk_ref[:, 0] = s_ref[...]
        yT_ref[...] = jnp.zeros_like(yT_ref)
        lane = lax.broadcasted_iota(jnp.int32, (N, tc), 1)

        def emit_y(S, h, t):
            y = jnp.sum(S * r_ref[h, pl.ds(jnp.maximum(t, 0), 1), :], axis=1, keepdims=True)
            yT_ref[h] = jnp.where(lane == t, y, yT_ref[h])

        def step(t, carry):
            for h in range(hg):
                row = lambda ref: ref[h, pl.ds(t, 1), :]
                S = s_ref[h]
                emit_y(S, h, t - 1)
                vcol = _col(vT_ref[h], lane, t)
                sa = -jnp.sum(S * row(kk_ref), axis=1, keepdims=True)
                s_ref[h] = S * row(w_ref) + sa * row(kka_ref) + vcol * row(k_ref)
            return carry

        _steps(tc, FWD_STEPS_PER_TRIP, step)
        for h in range(hg):
            emit_y(s_ref[h], h, tc - 1)

    rows = pl.BlockSpec((hg, tc, N), lambda g, c: (g, c, 0))
    cols = pl.BlockSpec((hg, N, tc), lambda g, c: (g, 0, c))
    return _call_carrying(
        body, "scan_fwd", (H // hg, nc), [rows] * 5 + [cols],
        [cols, pl.BlockSpec((hg, 1, N, N), lambda g, c: (g, c, 0, 0))],
        [jax.ShapeDtypeStruct((H, N, T), F32), jax.ShapeDtypeStruct((H, nc, N, N), F32)],
        [pltpu.VMEM((hg, N, N), F32)], [r, w, k, kk, kka, vT], comm)


def _scan_bwd(r, w, k, kk, kka, vT, dyT, chk, *, hg, tc, comm=None):
    H, N, T = vT.shape
    nc = T // tc
    hp = hg // 2
    L = 2 * N

    def body(r_ref, w_ref, k_ref, kk_ref, kka_ref, vT_ref, dyT_ref, chk_ref,
             dr_ref, dw_ref, dk_ref, dkk_ref, dkka_ref, dvT_ref, sp_ref, ds_ref):
        @pl.when(pl.program_id(1) == 0)
        def _():
            ds_ref[...] = jnp.zeros_like(ds_ref)

        dvT_ref[...] = jnp.zeros_like(dvT_ref)
        lane = lax.broadcasted_iota(jnp.int32, (N, tc), 1)
        left = lax.broadcasted_iota(jnp.int32, (N, L), 1) < N
        left_row = lax.broadcasted_iota(jnp.int32, (1, L), 1) < N

        def halves(x, lf):
            a = jnp.sum(jnp.where(lf, x, 0.0), axis=1, keepdims=True)
            b = jnp.sum(jnp.where(lf, 0.0, x), axis=1, keepdims=True)
            return a, b, jnp.where(lf, a, b)

        def pair_col(ref, p, t):
            return jnp.where(left, _col(ref[2 * p], lane, t), _col(ref[2 * p + 1], lane, t))

        for p in range(hp):
            sp_ref[p, 0] = jnp.concatenate([chk_ref[2 * p, 0], chk_ref[2 * p + 1, 0]], axis=1)

        def fstep(t, carry):
            for p in range(hp):
                row = lambda ref: ref[p, pl.ds(t, 1), :]
                S = sp_ref[p, t]
                sa = halves(S * row(kk_ref), left)[2]
                sp_ref[p, t + 1] = S * row(w_ref) - sa * row(kka_ref) + pair_col(vT_ref, p, t) * row(k_ref)
            return carry

        _steps(tc, RECOMPUTE_STEPS_PER_TRIP, fstep)

        def bstep(p0, i, carry):
            t = tc - 1 - i
            for p in range(p0, min(p0 + BWD_PAIRS_PER_TRIP, hp)):
                row = lambda ref: ref[p, pl.ds(t, 1), :]
                rr, wr, kr, kkr, kkar = row(r_ref), row(w_ref), row(k_ref), row(kk_ref), row(kka_ref)
                Sp = sp_ref[p, t]
                Sn = sp_ref[p, t + 1]
                dycol, vcol = pair_col(dyT_ref, p, t), pair_col(vT_ref, p, t)
                dS = ds_ref[p]
                dSn = dS + dycol * rr
                dsa = halves(dS * kkar, left)[2] + dycol * halves(rr * kkar, left_row)[2]
                dr_ref[p, pl.ds(t, 1), :] = jnp.sum(Sn * dycol, axis=0, keepdims=True)
                sa = halves(Sp * kkr, left)[2]
                dw_ref[p, pl.ds(t, 1), :] = jnp.sum(dSn * Sp, axis=0, keepdims=True)
                dkka_ref[p, pl.ds(t, 1), :] = -jnp.sum(dSn * sa, axis=0, keepdims=True)
                dva, dvb, _ = halves(dSn * kr, left)
                dk_ref[p, pl.ds(t, 1), :] = jnp.sum(dSn * vcol, axis=0, keepdims=True)
                dkk_ref[p, pl.ds(t, 1), :] = -jnp.sum(Sp * dsa, axis=0, keepdims=True)
                ds_ref[p] = dSn * wr - dsa * kkr
                dvT_ref[2 * p] = jnp.where(lane == t, dva, dvT_ref[2 * p])
                dvT_ref[2 * p + 1] = jnp.where(lane == t, dvb, dvT_ref[2 * p + 1])
            return carry

        for p0 in range(0, hp, BWD_PAIRS_PER_TRIP):
            lax.fori_loop(0, tc, functools.partial(bstep, p0), 0)

    rows = pl.BlockSpec((hp, tc, L), lambda g, c: (g, nc - 1 - c, 0))
    cols = pl.BlockSpec((hg, N, tc), lambda g, c: (g, 0, nc - 1 - c))
    return _call_carrying(
        body, "scan_bwd", (H // hg, nc),
        [rows] * 5 + [cols, cols, pl.BlockSpec((hg, 1, N, N), lambda g, c: (g, nc - 1 - c, 0, 0))], [rows] * 5 + [cols],
        [jax.ShapeDtypeStruct((H // 2, T, L), F32)] * 5 + [jax.ShapeDtypeStruct((H, N, T), F32)],
        [pltpu.VMEM((hp, tc + 1, N, L), F32), pltpu.VMEM((hp, N, L), F32)], [r, w, k, kk, kka, vT, dyT, chk], comm)


_NT = (((1,), (1,)), ((), ()))
_TN = (((0,), (0,)), ((), ()))


def _scores(q, k, cc, cr, masked):
    s = lax.dot_general(q, k, _NT, preferred_element_type=F32) + cc - cr
    if masked:
        tb = s.shape[0]
        keep = lax.broadcasted_iota(jnp.int32, (tb, tb), 0) >= lax.broadcasted_iota(jnp.int32, (tb, tb), 1)
        s = jnp.where(keep, s, NEG)
    return s


def _attn_specs(T, N, tb):
    blk = pl.BlockSpec((1, tb, N), lambda h, i: (h, i, 0))
    whole = pl.BlockSpec((1, T, N), lambda h, i: (h, 0, 0))
    col = pl.BlockSpec((1, tb, 1), lambda h, i: (h, i, 0))
    wcol = pl.BlockSpec((1, T, 1), lambda h, i: (h, 0, 0))
    row = pl.BlockSpec((1, 1, tb), lambda h, i: (h, 0, i))
    wrow = pl.BlockSpec((1, 1, T), lambda h, i: (h, 0, 0))
    return blk, whole, col, wcol, row, wrow


def _call_carrying(body, name, grid, in_specs, out_specs, out_shape, scratch, args, comm):
    n_out = len(out_specs)
    if comm is not None:
        n = len(comm[0])
        body = _carrying(body, len(in_specs), n_out, grid, comm)
        in_specs, out_specs = in_specs + [_HBM] * n, out_specs + [_HBM] * n
        out_shape, scratch, args = out_shape + _exchange_shapes(*comm), scratch + _exchange_sems(n), args + list(comm[0])
    res = pl.pallas_call(
        body, name=name, grid=grid, in_specs=in_specs, out_specs=out_specs, out_shape=out_shape,
        scratch_shapes=scratch, compiler_params=_cparams(*["arbitrary"] * len(grid)),
    )(*args)
    return res[:n_out], res[n_out:]


def _attn_fwd(q, k, v, ccol, crow, *, tb, comm=None):
    H, T, N = q.shape

    def body(q_ref, k_ref, v_ref, cc_ref, cr_ref, o_ref, lse_ref, m_s, l_s, acc_s):
        qi = pl.program_id(1)
        m_s[...] = jnp.full_like(m_s, NEG)
        l_s[...] = jnp.zeros_like(l_s)
        acc_s[...] = jnp.zeros_like(acc_s)
        q_, cc = q_ref[0], cc_ref[0]

        def block(j, masked):
            at = pl.ds(pl.multiple_of(j * tb, tb), tb)
            s = _scores(q_, k_ref[0, at, :], cc, cr_ref[0, :, at], masked)
            m_new = jnp.maximum(m_s[...], jnp.max(s, axis=1, keepdims=True))
            p = jnp.exp(s - m_new)
            alpha = jnp.exp(m_s[...] - m_new)
            l_s[...] = alpha * l_s[...] + jnp.sum(p, axis=1, keepdims=True)
            acc_s[...] = alpha * acc_s[...] + jnp.dot(p.astype(BF16), v_ref[0, at, :], preferred_element_type=F32)
            m_s[...] = m_new

        def below(j, carry):
            block(j, False)
            return carry

        lax.fori_loop(0, qi, below, 0)
        block(qi, True)
        o_ref[0] = acc_s[...] / l_s[...]
        lse_ref[0] = m_s[...] + jnp.log(l_s[...])

    blk, whole, col, wcol, row, wrow = _attn_specs(T, N, tb)
    return _call_carrying(
        body, "fox_fwd", (H, T // tb), [blk, whole, whole, col, wrow], [blk, col],
        [jax.ShapeDtypeStruct((H, T, N), F32), jax.ShapeDtypeStruct((H, T, 1), F32)],
        [pltpu.VMEM((tb, 1), F32), pltpu.VMEM((tb, 1), F32), pltpu.VMEM((tb, N), F32)],
        [q, k, v, ccol, crow], comm)


def _attn_bwd(q, k, v, ccol, crow, o, lse, do, *, tb, comm=None):
    H, T, N = q.shape
    nb = T // tb

    def body(q_ref, k_ref, v_ref, cc_ref, cr_ref, o_ref, lse_ref, do_ref,
             dq_ref, dcq_ref, dk_ref, dv_ref, dck_ref, dq_s, dcq_s, dk_s, dv_s, dck_s):
        qi = pl.program_id(1)

        @pl.when(qi == 0)
        def _():
            dk_s[...] = jnp.zeros_like(dk_s)
            dv_s[...] = jnp.zeros_like(dv_s)
            dck_s[...] = jnp.zeros_like(dck_s)

        dq_s[...] = jnp.zeros_like(dq_s)
        dcq_s[...] = jnp.zeros_like(dcq_s)
        q_, cc, lse_, do_ = q_ref[0], cc_ref[0], lse_ref[0], do_ref[0]
        delta = jnp.sum(do_ * o_ref[0], axis=1, keepdims=True)
        dob = do_.astype(BF16)

        def block(j, masked):
            at = pl.ds(pl.multiple_of(j * tb, tb), tb)
            kb = k_ref[0, at, :]
            p = jnp.exp(_scores(q_, kb, cc, cr_ref[0, :, at], masked) - lse_)
            dp = lax.dot_general(dob, v_ref[0, at, :], _NT, preferred_element_type=F32)
            ds = p * (dp - delta)
            dsb = ds.astype(BF16)
            dq_s[...] += jnp.dot(dsb, kb, preferred_element_type=F32)
            dcq_s[...] += jnp.sum(ds, axis=1, keepdims=True)
            dv_s[at, :] += lax.dot_general(p.astype(BF16), dob, _TN, preferred_element_type=F32)
            dk_s[at, :] += lax.dot_general(dsb, q_, _TN, preferred_element_type=F32)
            dck_s[:, at] += jnp.sum(ds, axis=0, keepdims=True)

        def below(j, carry):
            block(j, False)
            return carry

        lax.fori_loop(0, qi, below, 0)
        block(qi, True)
        dq_ref[0] = dq_s[...]
        dcq_ref[0] = dcq_s[...]

        @pl.when(qi == nb - 1)
        def _():
            dk_ref[0] = dk_s[...]
            dv_ref[0] = dv_s[...]
            dck_ref[0] = dck_s[...]

    blk, whole, col, wcol, row, wrow = _attn_specs(T, N, tb)
    return _call_carrying(
        body, "fox_bwd", (H, nb), [blk, whole, whole, col, wrow, blk, col, blk], [blk, col, whole, whole, wrow],
        [jax.ShapeDtypeStruct((H, T, N), F32), jax.ShapeDtypeStruct((H, T, 1), F32), jax.ShapeDtypeStruct((H, T, N), F32),
         jax.ShapeDtypeStruct((H, T, N), F32), jax.ShapeDtypeStruct((H, 1, T), F32)],
        [pltpu.VMEM((tb, N), F32), pltpu.VMEM((tb, 1), F32), pltpu.VMEM((T, N), F32), pltpu.VMEM((T, N), F32),
         pltpu.VMEM((1, T), F32)],
        [q, k, v, ccol, crow, o, lse, do], comm)


def _heads(x):
    T = x.shape[0]
    return x.reshape(T, -1, HEAD).transpose(1, 0, 2)


def _pairs(x):
    T = x.shape[0]
    return x.reshape(T, -1, 2 * HEAD).transpose(1, 0, 2)


def _headsT(x):
    T = x.shape[0]
    return x.reshape(T, -1, HEAD).transpose(1, 2, 0)


def _unheads(x):
    return x.transpose(1, 0, 2).reshape(x.shape[1], -1)


def _unheadsT(x):
    return x.transpose(2, 0, 1).reshape(x.shape[2], -1)


def _padc(x, n):
    return jnp.pad(x, ((0, 0), (0, n - x.shape[1])))


def _padr(x, n):
    return jnp.pad(x, ((0, n - x.shape[0]), (0, 0)))


class _Dims:
    def __init__(self, W, DL, AL, GL, FH):
        self.W, self.DL, self.AL, self.GL, self.FH = W, DL, AL, GL, FH
        self.DLp, self.ALp, self.GLp, self.FHp = _rup(DL, LANE), _rup(AL, LANE), _rup(GL, LANE), _rup(FH, LANE)
        self.FW = FH * HEAD
        self.RC = 3 * W + DL + AL + GL
        self.RP = 3 * W + self.DLp + self.ALp + self.GLp
        self.FC = 3 * self.FW + FH
        self.FP = 3 * self.FW + self.FHp

    def pad_r(self, a):
        W, o = self.W, 3 * self.W
        return jnp.concatenate([a[:, :o], _padc(a[:, o:o + self.DL], self.DLp),
                                _padc(a[:, o + self.DL:o + self.DL + self.AL], self.ALp),
                                _padc(a[:, o + self.DL + self.AL:self.RC], self.GLp)], axis=1)

    def unpad_r(self, a):
        o = 3 * self.W
        return jnp.concatenate([a[:, :o], a[:, o:o + self.DL], a[:, o + self.DLp:o + self.DLp + self.AL],
                                a[:, o + self.DLp + self.ALp:o + self.DLp + self.ALp + self.GL]], axis=1)

    def pad_f(self, a):
        return _padc(a, self.FP)

    def unpad_f(self, a):
        return a[:, :self.FC]


def _local_step(x, p, tgt, Wt, vec, d, late_shards=None):
    Wt = dict(Wt)
    T, D = x.shape
    W, FW = d.W, d.FW
    H = W // HEAD
    seg, segt = _seg_mats(W)
    segf, segft = _seg_mats(FW)
    T1 = 256
    rk_flat = vec["r_k"].reshape(1, W)
    mu = d.pad_r(vec["shift_mu"])
    qg = jnp.tile(vec["q_norm_g"], (1, d.FH))
    kg = jnp.tile(vec["k_norm_g"], (1, d.FH))
    fb = _padc(vec["fgate_b"], d.FHp)
    pdims = (W, d.DLp, d.ALp, d.GLp)
    fdims = (FW, d.FHp)

    (xn,) = _rowwise(lambda x_, g_: _rms(x_, g_), [x], [vec["attn_norm_g"]], [(D, BF16)], tile=T1, name="norm_attn")
    u_r = _mm(xn, Wt["w_in_r"], name="mm_in_r")
    u_f = _mm(xn, Wt["w_in_f"], name="mm_in_f")

    prep_consts = [mu, vec["w0"], Wt["w2"].astype(F32), vec["a0"], Wt["a2"].astype(F32), Wt["g2"].astype(F32), vec["k_k"], vec["k_a"], seg, segt]

    def prep_fwd(u_, up_, mu_, *cs):
        um, _ = _shift_mix(u_, up_, mu_)
        return _prep_rwkv(um, *cs, dims=pdims)

    r, dec, k2, v, kk, kka, g = _rowwise(prep_fwd, [u_r], prep_consts, [(W, F32)] * 7, tile=128, name="rwkv_prep", prev=[u_r])
    hg, tc = min(16, H), 128
    (yT, chk), gathered = _scan_fwd(_heads(r), _heads(dec), _heads(k2), _heads(kk), _heads(kka), _headsT(v), hg=hg, tc=tc,
                                    comm=late_shards and (late_shards, True))
    Wt.update({n: _whole(n, g) for n, g in zip(_LATE, gathered)})
    y = _unheadsT(yT)
    post_consts = [vec["lnx_g"], vec["lnx_b"], rk_flat, seg, segt]
    (y_r,) = _rowwise(_post_rwkv, [y, r, k2, v, g], post_consts, [(W, BF16)], tile=T1, name="rwkv_post")

    fox_consts = [qg, kg, fb, segf, segft]
    qn, kn, vf, logf = _rowwise(functools.partial(_prep_fox, dims=fdims), [u_f], fox_consts,
                                [(FW, BF16), (FW, BF16), (FW, BF16), (d.FHp, F32)], tile=T1, name="fox_prep")
    c = _cumsum(logf, reverse=False, name="fox_cumsum")
    cT = c[:, :d.FH].T
    ccol, crow = cT[:, :, None], cT[:, None, :]
    tb = _pick(T, (1024, 512, 256, 128))
    qh, kh, vh = _heads(qn), _heads(kn), _heads(vf)
    (o, lse), _ = _attn_fwd(qh, kh, vh, ccol, crow, tb=tb)
    y_f = _unheads(o)

    ycat = jnp.concatenate([y_r, y_f.astype(BF16)], axis=1)
    h1 = _mm(ycat, Wt["w_out"], add=x, name="mm_out")
    (hn,) = _rowwise(lambda h_, g_: _rms(h_, g_), [h1], [vec["ffn_norm_g"]], [(D, BF16)], tile=T1, name="norm_ffn")
    gt = _mm(hn, Wt["w_gate"], name="mm_gate")
    up = _mm(hn, Wt["w_up"], name="mm_up")
    (act,) = _rowwise(_swiglu, [gt, up], [], [(gt.shape[1], BF16)], tile=T1, name="swiglu")
    h2 = _mm(act, Wt["w_down"], add=h1, name="mm_down")
    (hg_,) = _rowwise(lambda h_, g_: _rms(h_, g_), [h2], [vec["ple_gate_norm_g"]], [(D, BF16)], tile=T1, name="norm_gate")
    pe = _mm(p, Wt["ple_proj"], name="mm_ple")
    z = _mm(hg_, Wt["ple_gate_w"], name="mm_pgate")

    def tail(h2_, pe_, z_, tg_, png_, pgb_):
        h3, f = jax.vjp(_tail, h2_, pe_, z_, png_, pgb_)
        err = h3 - tg_
        dh3 = err * (1.0 / D)
        lt = 0.5 * jnp.sum(jnp.sum(err * err, axis=1, keepdims=True) * (1.0 / D), axis=0, keepdims=True)
        dh2_, dpe_, dz_, dpng_, dpgb_ = f(dh3)
        return dh2_, dpe_, dz_, jnp.broadcast_to(lt, (1, LANE)), dpng_, dpgb_

    dh3, dpe, dz, loss, g_png, g_pgb = _rowwise(
        tail, [h2, pe, z, tgt], [vec["ple_norm_g"], vec["ple_gate_b"]], [(D, F32), (D, BF16), (D, BF16)],
        [(1, LANE), (1, D), (1, D)], tile=T1, name="tail")
    G = {}
    gv = {"ple_norm_g": g_png, "ple_gate_b": g_pgb}
    G["ple_gate_w"] = _mm(hg_, dz, ta=True, out_dtype=BF16, name="gw_pgate")
    G["ple_proj"] = _mm(p, dpe, ta=True, out_dtype=BF16, name="gw_ple")
    d_hg = _mm(dz, Wt["ple_gate_w"], tb=True, name="mmb_pgate")

    def norm_bwd(h_, dres_, dn_, g_):
        _, f = jax.vjp(_rms, h_, g_)
        dh_, dg_ = f(dn_)
        return dres_ + dh_, dg_

    dh2, gv["ple_gate_norm_g"] = _rowwise(norm_bwd, [h2, dh3, d_hg], [vec["ple_gate_norm_g"]], [(D, F32)], [(1, D)],
                                          tile=T1, name="norm_gate_bwd")
    G["w_down"] = _mm(act, dh2, ta=True, out_dtype=BF16, name="gw_down")
    d_act = _mm(dh2, Wt["w_down"], tb=True, name="mmb_down")

    def swiglu_bwd(gt_, up_, da_):
        _, f = jax.vjp(_swiglu, gt_, up_)
        return f(da_)

    d_gt, d_up = _rowwise(swiglu_bwd, [gt, up, d_act], [], [(gt.shape[1], BF16)] * 2, tile=T1, name="swiglu_bwd")
    G["w_gate"] = _mm(hn, d_gt, ta=True, out_dtype=BF16, name="gw_gate")
    G["w_up"] = _mm(hn, d_up, ta=True, out_dtype=BF16, name="gw_up")
    d_hn = _mm(d_gt, Wt["w_gate"], tb=True, name="mmb_gate")
    d_hn = _mm(d_up, Wt["w_up"], tb=True, add=d_hn, name="mmb_up")
    dh1, gv["ffn_norm_g"] = _rowwise(norm_bwd, [h1, dh2, d_hn], [vec["ffn_norm_g"]], [(D, F32)], [(1, D)],
                                     tile=T1, name="norm_ffn_bwd")
    G["w_out"] = _mm(ycat, dh1, ta=True, out_dtype=BF16, name="gw_out")
    d_ycat = _mm(dh1, Wt["w_out"], tb=True, name="mmb_out")

    def post_bwd(y_, r_, k2_, v_, g_, dy_, *cs):
        lg, lb, rk, sg, sgt = cs
        _, f = jax.vjp(lambda *a: _post_rwkv(*a, sg, sgt), y_, r_, k2_, v_, g_, lg, lb, rk)
        return f(dy_)

    dy, dr1, dk1, dv1, dg, gv["lnx_g"], gv["lnx_b"], g_rk = _rowwise(
        post_bwd, [y, r, k2, v, g, (d_ycat, W, 0)], post_consts, [(W, F32)] * 5, [(1, W)] * 3, tile=128, name="rwkv_post_bwd")
    gv["r_k"] = g_rk.reshape(H, HEAD)
    pieces = late_shards and ([_pieces(n, G[n]).astype(BF16) for n in _LATE], False)
    (dr, ddec, dk2, dkk, dkka, dvT), recv = _scan_bwd(_pairs(r), _pairs(dec), _pairs(k2), _pairs(kk), _pairs(kka),
                                                      _headsT(v), _headsT(dy), chk, hg=hg, tc=tc, comm=pieces)
    recv = dict(zip(_LATE, recv))
    dr, ddec, dk2, dkk, dkka, dv = (_unheads(dr), _unheads(ddec), _unheads(dk2), _unheads(dkk), _unheads(dkka), _unheadsT(dvT))

    def prep_bwd(u_, dr_, dr1_, ddec_, dk2_, dk1_, dv_, dv1_, dkk_, dkka_, dg_, up_, mu_, *cs):
        um, sh = _shift_mix(u_, up_, mu_)
        cs_d, sg, sgt = cs[:7], cs[7], cs[8]
        _, f = jax.vjp(lambda um_, *c_: _prep_rwkv(um_, *c_, sg, sgt, dims=pdims), um, *cs_d)
        res = f((dr_ + dr1_, ddec_, dk2_ + dk1_, dv_ + dv1_, dkk_, dkka_, dg_))
        dum = res[0]
        dmu = jnp.sum(dum * (sh - u_), axis=0, keepdims=True)
        return (dum, dmu) + tuple(res[1:])

    LP = [Wt["w2"].shape, Wt["a2"].shape, Wt["g2"].shape]
    dum, g_mu, gv["w0"], g_w2, gv["a0"], g_a2, g_g2, gv["k_k"], gv["k_a"] = _rowwise(
        prep_bwd, [u_r, dr, dr1, ddec, dk2, dk1, dv, dv1, dkk, dkka, dg], prep_consts, [(d.RP, F32)],
        [(1, d.RP), (1, W), LP[0], (1, W), LP[1], LP[2], (1, W), (1, W)], tile=128, name="rwkv_prep_bwd", prev=[u_r])
    gv["shift_mu"] = d.unpad_r(g_mu)
    G["w2"], G["a2"], G["g2"] = g_w2, g_a2, g_g2
    (du_r,) = _rowwise(lambda a_, an_, mu_: a_ * (1.0 - mu_) + jnp.where(
        lax.broadcasted_iota(jnp.int32, a_.shape, 0) == a_.shape[0] - 1, an_, pltpu.roll(a_, a_.shape[0] - 1, 0)) * mu_,
        [dum], [mu], [(d.RP, BF16)], tile=T1, name="shift_bwd", nxt=[dum])

    do = _heads(d_ycat[:, W:])
    (dq, dcq, dk_, dv_, dck), _ = _attn_bwd(qh, kh, vh, ccol, crow, o, lse, do, tb=tb)
    dc = _padc((dcq[:, :, 0] - dck[:, 0, :]).T, d.FHp)
    dlogf = _cumsum(dc, reverse=True, name="fox_cumsum_bwd")

    def fox_bwd(uf_, dq_, dk__, dv__, dlf_, *cs):
        qg_, kg_, fb_, sg, sgt = cs
        _, f = jax.vjp(lambda uf__, a, b, c_: _prep_fox(uf__, a, b, c_, sg, sgt, dims=fdims), uf_, qg_, kg_, fb_)
        return f((dq_, dk__, dv__, dlf_))

    du_f, g_qg, g_kg, g_fb = _rowwise(fox_bwd, [u_f, _unheads(dq), _unheads(dk_), _unheads(dv_), dlogf], fox_consts,
                                      [(d.FP, BF16)], [(1, FW), (1, FW), (1, d.FHp)], tile=T1, name="fox_prep_bwd")
    gv["q_norm_g"] = g_qg.reshape(d.FH, HEAD).sum(0, keepdims=True)
    gv["k_norm_g"] = g_kg.reshape(d.FH, HEAD).sum(0, keepdims=True)
    gv["fgate_b"] = g_fb[:, :d.FH]

    G["w_in_r"] = _mm(xn, du_r, ta=True, out_dtype=BF16, name="gw_in_r")
    G["w_in_f"] = _mm(xn, du_f, ta=True, out_dtype=BF16, name="gw_in_f")
    d_xn = _mm(du_r, Wt["w_in_r"], tb=True, name="mmb_in_r")
    d_xn = _mm(du_f, Wt["w_in_f"], tb=True, add=d_xn, name="mmb_in_f")
    dx, gv["attn_norm_g"] = _rowwise(norm_bwd, [x, dh1, d_xn], [vec["attn_norm_g"]], [(D, F32)], [(1, D)],
                                     tile=T1, name="norm_attn_bwd")
    return loss, dx, G, gv, recv


_HBM = pl.BlockSpec(memory_space=pl.ANY)
_OTHER_CHIPS = ((0, 1), (1, 0), (1, 1))


def _flip(v, bit):
    return 1 - v if bit else v


def _exchange(arrs, *, gather, name):
    n = len(arrs)

    def body(*refs):
        copies = _exchange_copies(refs[:n], refs[n:2 * n], refs[2 * n:], gather)
        for cp in copies:
            cp.start()
        for cp in copies:
            cp.wait()

    return pl.pallas_call(
        body, name=name, in_specs=[_HBM] * n, out_specs=[_HBM] * n, out_shape=_exchange_shapes(arrs, gather),
        scratch_shapes=_exchange_sems(n),
    )(*arrs)


def _exchange_shapes(arrs, gather):
    return [jax.ShapeDtypeStruct(((N_CHIPS,) + a.shape) if gather else a.shape, a.dtype) for a in arrs]


def _exchange_sems(n):
    return [pltpu.SemaphoreType.DMA((3 * n,)), pltpu.SemaphoreType.DMA((3 * n,)), pltpu.SemaphoreType.DMA((n,))]


def _exchange_copies(ins, outs, sems, gather):
    send_sems, recv_sems, own_sems = sems
    x, y, c = lax.axis_index("x"), lax.axis_index("y"), lax.axis_index("c")
    me = 2 * x + y
    copies = []
    for a in range(len(ins)):
        copies.append(pltpu.make_async_copy(ins[a] if gather else ins[a].at[me], outs[a].at[me], own_sems.at[a]))
        for k, (dx, dy) in enumerate(_OTHER_CHIPS):
            px, py = _flip(x, dx), _flip(y, dy)
            copies.append(pltpu.make_async_remote_copy(
                src_ref=ins[a] if gather else ins[a].at[2 * px + py], dst_ref=outs[a].at[me],
                send_sem=send_sems.at[3 * a + k], recv_sem=recv_sems.at[3 * a + k],
                device_id=(px, py, c), device_id_type=MESH))
    return copies


def _carrying(body, n_in, n_out, grid, comm):
    arrs, gather = comm
    n = len(arrs)

    def wrapped(*refs):
        c_in = refs[n_in:n_in + n]
        c_out = refs[n_in + n + n_out:n_in + 2 * n + n_out]
        ids = [pl.program_id(a) for a in range(len(grid))]
        first = functools.reduce(jnp.logical_and, [i == 0 for i in ids])
        last = functools.reduce(jnp.logical_and, [i == g - 1 for i, g in zip(ids, grid)])

        @pl.when(first)
        def _():
            for cp in _exchange_copies(c_in, c_out, refs[-3:], gather):
                cp.start()

        body(*refs[:n_in], *refs[n_in + n:n_in + n + n_out], *refs[n_in + 2 * n + n_out:-3])

        @pl.when(last)
        def _():
            for cp in _exchange_copies(c_in, c_out, refs[-3:], gather):
                cp.wait()

    return wrapped


def _swap_cores(arrs, *, name):
    n = len(arrs)

    def body(*refs):
        ins, outs = refs[:n], refs[n:2 * n]
        send_sems, recv_sems = refs[2 * n:]
        peer = (lax.axis_index("x"), lax.axis_index("y"), 1 - lax.axis_index("c"))
        cps = [pltpu.make_async_remote_copy(src_ref=ins[a], dst_ref=outs[a], send_sem=send_sems.at[a],
                                            recv_sem=recv_sems.at[a], device_id=peer, device_id_type=MESH) for a in range(n)]
        for cp in cps:
            cp.start()
        for cp in cps:
            cp.wait()

    return pl.pallas_call(
        body, name=name, in_specs=[_HBM] * n, out_specs=[_HBM] * n,
        out_shape=[jax.ShapeDtypeStruct(a.shape, a.dtype) for a in arrs],
        scratch_shapes=[pltpu.SemaphoreType.DMA((n,)), pltpu.SemaphoreType.DMA((n,))],
    )(*arrs)


def _allreduce_small(pack, *, name):
    R, C = pack.shape

    def body(p_ref, o_ref, recv, send_sems, recv_sems):
        x, y, c = lax.axis_index("x"), lax.axis_index("y"), lax.axis_index("c")
        me = 4 * x + 2 * y + c
        recv[me] = p_ref[...]
        cps = []
        for k in range(1, N_DEV):
            peer = (_flip(x, k & 4), _flip(y, k & 2), _flip(c, k & 1))
            cp = pltpu.make_async_remote_copy(src_ref=p_ref, dst_ref=recv.at[me], send_sem=send_sems.at[k - 1],
                                              recv_sem=recv_sems.at[k - 1], device_id=peer, device_id_type=MESH)
            cp.start()
            cps.append(cp)
        for cp in cps:
            cp.wait()
        acc = recv[0]
        for s in range(1, N_DEV):
            acc = acc + recv[s]
        o_ref[...] = acc

    vm = pl.BlockSpec(memory_space=pltpu.VMEM)
    return pl.pallas_call(
        body, name=name, in_specs=[vm], out_specs=vm, out_shape=jax.ShapeDtypeStruct((R, C), F32),
        scratch_shapes=[pltpu.VMEM((N_DEV, R, C), F32), pltpu.SemaphoreType.DMA((N_DEV - 1,)), pltpu.SemaphoreType.DMA((N_DEV - 1,))],
    )(pack)


def _sum_slots(a, *, name):
    S, R, C = a.shape
    tr = _pick(R, (256, 128, 64, 32, 16, 8))

    def body(a_ref, o_ref):
        acc = a_ref[0].astype(F32)
        for s in range(1, S):
            acc = acc + a_ref[s].astype(F32)
        o_ref[...] = acc

    return pl.pallas_call(
        body, name=name, grid=(R // tr,), in_specs=[pl.BlockSpec((S, tr, C), lambda i: (0, i, 0))],
        out_specs=pl.BlockSpec((tr, C), lambda i: (i, 0)), out_shape=jax.ShapeDtypeStruct((R, C), F32),
        compiler_params=_cparams("parallel"),
    )(a)


def _adamw(w, m, v, gs, *, name):
    R, C = w.shape
    tile = _pick(R, (128, 96, 64, 32, 16, 8))

    def fn(w_, m_, v_, *g_):
        g = g_[0]
        for e in g_[1:]:
            g = g + e
        m2 = ADAM_B1 * m_ + (1.0 - ADAM_B1) * g
        v2 = ADAM_B2 * v_ + (1.0 - ADAM_B2) * jnp.square(g)
        m_hat = m2 / (1.0 - ADAM_B1 ** ADAM_STEP)
        v_hat = v2 / (1.0 - ADAM_B2 ** ADAM_STEP)
        delta = -ADAM_LR * (m_hat / (jnp.sqrt(v_hat) + ADAM_EPS) + ADAM_WD * w_)
        return g, delta, m2, v2

    return _rowwise(fn, [w, m, v, *gs], [], [(C, F32)] * 4, tile=tile, name=name)


_ARGS = "x, p, attn_norm_g, w_in, shift_mu, w0, w2, a0, a2, g2, k_k, k_a, r_k, lnx_g, lnx_b, q_norm_g, k_norm_g, fgate_b, w_out, ffn_norm_g, w_gate, w_up, w_down, ple_proj, ple_norm_g, ple_gate_norm_g, ple_gate_w, ple_gate_b, loss_target, m_attn_norm_g, m_w_in, m_shift_mu, m_w0, m_w2, m_a0, m_a2, m_g2, m_k_k, m_k_a, m_r_k, m_lnx_g, m_lnx_b, m_q_norm_g, m_k_norm_g, m_fgate_b, m_w_out, m_ffn_norm_g, m_w_gate, m_w_up, m_w_down, m_ple_proj, m_ple_norm_g, m_ple_gate_norm_g, m_ple_gate_w, m_ple_gate_b, v_attn_norm_g, v_w_in, v_shift_mu, v_w0, v_w2, v_a0, v_a2, v_g2, v_k_k, v_k_a, v_r_k, v_lnx_g, v_lnx_b, v_q_norm_g, v_k_norm_g, v_fgate_b, v_w_out, v_ffn_norm_g, v_w_gate, v_w_up, v_w_down, v_ple_proj, v_ple_norm_g, v_ple_gate_norm_g, v_ple_gate_w, v_ple_gate_b".split(", ")
_WEIGHTS = _ARGS[2:28]
_COL_SHARDED = ("w_in", "w2", "a2", "g2", "w_gate", "w_up", "ple_proj")
_ROW_SHARDED = ("w_out", "w_down", "ple_gate_w")
_MATRICES = _COL_SHARDED + _ROW_SHARDED
_EARLY = ("w_in", "w2", "a2", "g2")
_LATE = tuple(n for n in _MATRICES if n not in _EARLY)
_VECTORS = tuple(n for n in _WEIGHTS if n not in _MATRICES)


def _whole(name, g):
    if name in _COL_SHARDED:
        return g.transpose(1, 0, 2).reshape(g.shape[1], -1)
    return g.reshape(-1, g.shape[2])


def _pieces(name, a):
    if name in _COL_SHARDED:
        return a.reshape(a.shape[0], N_CHIPS, -1).transpose(1, 0, 2)
    return a.reshape(N_CHIPS, -1, a.shape[1])


def kernel(x, p, attn_norm_g, w_in, shift_mu, w0, w2, a0, a2, g2, k_k, k_a, r_k, lnx_g, lnx_b, q_norm_g, k_norm_g, fgate_b, w_out, ffn_norm_g, w_gate, w_up, w_down, ple_proj, ple_norm_g, ple_gate_norm_g, ple_gate_w, ple_gate_b, loss_target, m_attn_norm_g, m_w_in, m_shift_mu, m_w0, m_w2, m_a0, m_a2, m_g2, m_k_k, m_k_a, m_r_k, m_lnx_g, m_lnx_b, m_q_norm_g, m_k_norm_g, m_fgate_b, m_w_out, m_ffn_norm_g, m_w_gate, m_w_up, m_w_down, m_ple_proj, m_ple_norm_g, m_ple_gate_norm_g, m_ple_gate_w, m_ple_gate_b, v_attn_norm_g, v_w_in, v_shift_mu, v_w0, v_w2, v_a0, v_a2, v_g2, v_k_k, v_k_a, v_r_k, v_lnx_g, v_lnx_b, v_q_norm_g, v_k_norm_g, v_fgate_b, v_w_out, v_ffn_norm_g, v_w_gate, v_w_up, v_w_down, v_ple_proj, v_ple_norm_g, v_ple_gate_norm_g, v_ple_gate_w, v_ple_gate_b):
    A = dict(zip(_ARGS, (x, p, attn_norm_g, w_in, shift_mu, w0, w2, a0, a2, g2, k_k, k_a, r_k, lnx_g, lnx_b, q_norm_g, k_norm_g, fgate_b, w_out, ffn_norm_g, w_gate, w_up, w_down, ple_proj, ple_norm_g, ple_gate_norm_g, ple_gate_w, ple_gate_b, loss_target, m_attn_norm_g, m_w_in, m_shift_mu, m_w0, m_w2, m_a0, m_a2, m_g2, m_k_k, m_k_a, m_r_k, m_lnx_g, m_lnx_b, m_q_norm_g, m_k_norm_g, m_fgate_b, m_w_out, m_ffn_norm_g, m_w_gate, m_w_up, m_w_down, m_ple_proj, m_ple_norm_g, m_ple_gate_norm_g, m_ple_gate_w, m_ple_gate_b, v_attn_norm_g, v_w_in, v_shift_mu, v_w0, v_w2, v_a0, v_a2, v_g2, v_k_k, v_k_a, v_r_k, v_lnx_g, v_lnx_b, v_q_norm_g, v_k_norm_g, v_fgate_b, v_w_out, v_ffn_norm_g, v_w_gate, v_w_up, v_w_down, v_ple_proj, v_ple_norm_g, v_ple_gate_norm_g, v_ple_gate_w, v_ple_gate_b)))
    x, p, tgt = A["x"][0], A["p"][0, 0], A["loss_target"][0]
    d = _Dims(W=A["w0"].shape[-1], DL=A["w2"].shape[1], AL=A["a2"].shape[1], GL=A["g2"].shape[1], FH=A["fgate_b"].shape[-1])

    shard = lambda n: A[n][0].astype(BF16)
    gathered = _exchange([shard(n) for n in _EARLY], gather=True, name="gather_early")
    full = {n: _whole(n, g) for n, g in zip(_EARLY, gathered)}
    Wt = {"w_in_r": d.pad_r(full["w_in"][:, :d.RC]), "w_in_f": d.pad_f(full["w_in"][:, d.RC:]),
          "w2": _padr(full["w2"], d.DLp), "a2": _padr(full["a2"], d.ALp), "g2": _padr(full["g2"], d.GLp)}
    vec = {n: A[n].reshape(-1, A[n].shape[-1]) for n in _VECTORS}

    loss, dx, G, gv, recv = _local_step(x, p, tgt, Wt, vec, d, late_shards=[shard(n) for n in _LATE])

    gw = {"w_in": jnp.concatenate([d.unpad_r(G["w_in_r"]), d.unpad_f(G["w_in_f"])], axis=1),
          "w2": G["w2"][:d.DL], "a2": G["a2"][:d.AL], "g2": G["g2"][:d.GL]}
    recv.update(zip(_EARLY, _exchange([_pieces(n, gw[n]).astype(BF16) for n in _EARLY], gather=False, name="scatter_early")))
    part = [_sum_slots(recv[n], name="sum_" + n) for n in _MATRICES]
    sib = _swap_cores(part, name="swap_cores")

    sizes = [1] + [A[n].size for n in _VECTORS]
    rows = _rup(_rup(sum(sizes), LANE) // LANE, 8)

    def pack(items):
        flat = jnp.concatenate([i.reshape(-1) for i in items])
        return jnp.pad(flat, (0, rows * LANE - flat.shape[0])).reshape(rows, LANE)

    red = _allreduce_small(pack([loss[0, :1]] + [gv[n] for n in _VECTORS]), name="allreduce_vectors")
    zero = jnp.zeros((1,), F32)
    upd = _adamw(pack([zero] + [A[n] for n in _VECTORS]), pack([zero] + [A["m_" + n] for n in _VECTORS]),
                 pack([zero + 1.0] + [A["v_" + n] for n in _VECTORS]), [red], name="adamw_vectors")
    offs = [0]
    for s in sizes:
        offs.append(offs[-1] + s)
    unpack = lambda a, i, n: a.reshape(-1)[offs[i + 1]:offs[i + 2]].reshape(A[n].shape)

    out = {"grad": {}, "delta": {}, "new_m": {}, "new_v": {}}
    for i, n in enumerate(_VECTORS):
        for kind, a in zip(out, upd):
            out[kind][n] = unpack(a, i, n)
    for n, mine, other in zip(_MATRICES, part, sib):
        res = _adamw(A[n][0], A["m_" + n][0], A["v_" + n][0], [mine, other], name="adamw_" + n)
        for kind, a in zip(out, res):
            out[kind][n] = a[None]
    return (red[0, 0], dx[None], *[out[k][n] for k in out for n in _WEIGHTS])
```

```python
import functools

import jax
import jax.numpy as jnp
from jax import lax
from jax.experimental import pallas as pl
from jax.experimental.pallas import tpu as pltpu

F32 = jnp.float32
BF16 = jnp.bfloat16
LANE = 128
HEAD = 64
RMS_EPS = 1e-6
GN_EPS = 64e-5
ADAM_LR, ADAM_B1, ADAM_B2, ADAM_EPS, ADAM_WD, ADAM_STEP = 0.001, 0.9, 0.999, 1e-08, 0.01, 10
VMEM_LIMIT = 56 * 1024 * 1024
MM_TILE_BYTES = 40 * 1024 * 1024
NEG = -1e30
MESH = pl.DeviceIdType.MESH
N_CHIPS = 4
N_DEV = 8


def _rup(n, m):
    return -(-n // m) * m


def _pick(n, cands):
    for c in cands:
        if n % c == 0:
            return c
    return n


def _cparams(*sem):
    return pltpu.CompilerParams(dimension_semantics=sem, vmem_limit_bytes=VMEM_LIMIT)


def _mm(a, b, *, ta=False, tb=False, add=None, out_dtype=F32, name):
    M, K = (a.shape[1], a.shape[0]) if ta else a.shape
    N = b.shape[0] if tb else b.shape[1]
    tn = _pick(N, (512, 640, 256, 128))
    fits = lambda m, t: 2 * (m * t * a.dtype.itemsize + t * tn * b.dtype.itemsize + m * tn * 8) <= MM_TILE_BYTES
    tm, tk = next((m, t) for t in (K, 2048, 1024, 512, 640, 256, 128) for m in (1024, 512, 256, 128, M)
                  if K % t == 0 and M % m == 0 and fits(m, t))
    nk = K // tk
    dn = (((0 if ta else 1,), (1 if tb else 0,)), ((), ()))

    def body(*refs):
        if add is None:
            a_ref, b_ref, o_ref, acc = refs
        else:
            a_ref, b_ref, add_ref, o_ref, acc = refs
        ks = pl.program_id(2)
        part = lax.dot_general(a_ref[...].astype(BF16), b_ref[...].astype(BF16), dn, preferred_element_type=F32)
        if nk > 1:
            @pl.when(ks == 0)
            def _():
                acc[...] = jnp.zeros_like(acc)

            acc[...] += part

        @pl.when(ks == nk - 1)
        def _():
            res = acc[...] if nk > 1 else part
            if add is not None:
                res = res + add_ref[...].astype(F32)
            o_ref[...] = res.astype(out_dtype)

    a_spec = pl.BlockSpec((tk, tm), lambda i, j, k: (k, i)) if ta else pl.BlockSpec((tm, tk), lambda i, j, k: (i, k))
    b_spec = pl.BlockSpec((tn, tk), lambda i, j, k: (j, k)) if tb else pl.BlockSpec((tk, tn), lambda i, j, k: (k, j))
    o_spec = pl.BlockSpec((tm, tn), lambda i, j, k: (i, j))
    ins, specs = [a, b], [a_spec, b_spec]
    if add is not None:
        ins.append(add)
        specs.append(o_spec)
    return pl.pallas_call(
        body, name=name, grid=(M // tm, N // tn, nk), in_specs=specs, out_specs=o_spec,
        out_shape=jax.ShapeDtypeStruct((M, N), out_dtype),
        scratch_shapes=[pltpu.VMEM((tm, tn) if nk > 1 else (8, LANE), F32)],
        compiler_params=_cparams("parallel", "parallel", "arbitrary"),
    )(*ins)


def _rowwise(fn, rows, consts, outs, accs=(), *, tile, name, prev=(), nxt=()):
    rows = [r if isinstance(r, tuple) else (r, r.shape[1], 0) for r in rows]
    T = rows[0][0].shape[0]
    tile = min(tile, T)
    n = T // tile
    sub = 8
    nr, npv, nnx, ncst, no, na = len(rows), len(prev), len(nxt), len(consts), len(outs), len(accs)

    def body(*refs):
        i = pl.program_id(0)
        it = iter(refs)
        rv = [next(it)[...] for _ in range(nr)]
        pv = [jnp.where(i > 0, next(it)[sub - 1:sub, :], 0.0) for _ in range(npv)]
        nv = [jnp.where(i < n - 1, next(it)[0:1, :], 0.0) for _ in range(nnx)]
        cv = [next(it)[...] for _ in range(ncst)]
        o_refs = [next(it) for _ in range(no)]
        a_refs = [next(it) for _ in range(na)]
        res = fn(*rv, *pv, *nv, *cv)
        if not isinstance(res, (tuple, list)):
            res = (res,)
        for r, o in zip(o_refs, res[:no]):
            r[...] = o.astype(r.dtype)
        if na:
            @pl.when(i == 0)
            def _():
                for r in a_refs:
                    r[...] = jnp.zeros_like(r)
            for r, o in zip(a_refs, res[no:]):
                r[...] += o.astype(F32)

    in_specs = [pl.BlockSpec((tile, w), functools.partial(lambda cb, i: (i, cb), cb)) for _, w, cb in rows]
    in_specs += [pl.BlockSpec((sub, a.shape[1]), lambda i: (jnp.maximum(i * (tile // sub) - 1, 0), 0)) for a in prev]
    in_specs += [pl.BlockSpec((sub, a.shape[1]), lambda i: (jnp.minimum((i + 1) * (tile // sub), T // sub - 1), 0)) for a in nxt]
    in_specs += [pl.BlockSpec(c.shape, lambda i: (0, 0)) for c in consts]
    out_specs = [pl.BlockSpec((tile, c), lambda i: (i, 0)) for c, _ in outs]
    out_specs += [pl.BlockSpec(s, lambda i: (0, 0)) for s in accs]
    out_shape = [jax.ShapeDtypeStruct((T, c), d) for c, d in outs] + [jax.ShapeDtypeStruct(s, F32) for s in accs]
    res = pl.pallas_call(
        body, name=name, grid=(n,), in_specs=in_specs, out_specs=out_specs, out_shape=out_shape,
        compiler_params=_cparams("arbitrary"),
    )(*[r[0] for r in rows], *prev, *nxt, *consts)
    return res


@jax.custom_vjp
def _bdot(a, b):
    return jnp.dot(a.astype(BF16), b.astype(BF16), preferred_element_type=F32)


def _bdot_fwd(a, b):
    return _bdot(a, b), (a.astype(BF16), b.astype(BF16))


def _bdot_bwd(res, ct):
    a, b = res
    c = ct.astype(BF16)
    return (lax.dot_general(c, b, (((1,), (1,)), ((), ())), preferred_element_type=F32),
            lax.dot_general(a, c, (((0,), (0,)), ((), ())), preferred_element_type=F32))


_bdot.defvjp(_bdot_fwd, _bdot_bwd)


def _split3(x):
    hi = x.astype(BF16)
    r1 = x - hi.astype(F32)
    mid = r1.astype(BF16)
    return hi, mid, (r1 - mid.astype(F32)).astype(BF16)


def _dot_exact(a, b):
    return sum(jnp.dot(p, b, preferred_element_type=F32) for p in _split3(a))


@jax.custom_vjp
def _head_sums(x, seg, segt):
    return _dot_exact(_dot_exact(x, seg), segt)


def _head_sums_fwd(x, seg, segt):
    return _head_sums(x, seg, segt), (seg, segt)


def _head_sums_bwd(res, ct):
    seg, segt = res
    return _head_sums(ct, seg, segt), jnp.zeros_like(seg), jnp.zeros_like(segt)


_head_sums.defvjp(_head_sums_fwd, _head_sums_bwd)


def _rms(x, g, eps=RMS_EPS):
    return x * lax.rsqrt(jnp.mean(x * x, axis=-1, keepdims=True) + eps) * g


def _softplus(x):
    return jnp.maximum(x, 0.0) + jnp.log(1.0 + jnp.exp(-jnp.abs(x)))


def _sigmoid(x):
    return 1.0 / (1.0 + jnp.exp(-x))


def _seg_mats(width):
    h = lax.broadcasted_iota(jnp.int32, (width, LANE), 0) // HEAD
    j = lax.broadcasted_iota(jnp.int32, (width, LANE), 1)
    seg = (h == j).astype(BF16)
    return seg, seg.T


def _prep_rwkv(um, w0, w2, a0, a2, g2, k_k, k_a, seg, segt, *, dims):
    W, DLp, ALp, GLp = dims
    r, k, v = um[:, :W], um[:, W:2 * W], um[:, 2 * W:3 * W]
    o = 3 * W
    xw, xa, xg = um[:, o:o + DLp], um[:, o + DLp:o + DLp + ALp], um[:, o + DLp + ALp:o + DLp + ALp + GLp]
    w_log = -_softplus(-(w0 + _bdot(jnp.tanh(xw), w2))) - 0.5
    decay = jnp.exp(-jnp.exp(w_log))
    a = _sigmoid(a0 + _bdot(xa, a2))
    g = _bdot(_sigmoid(xg), g2)
    kk = k * k_k
    nrm = jnp.sqrt(_head_sums(kk * kk, seg, segt))
    kk = kk / jnp.maximum(nrm, 1e-12)
    k2 = k * (1.0 + (a - 1.0) * k_a)
    return r, decay, k2, v, kk, kk * a, g


def _shift_mix(u, uprev, mu):
    first = lax.broadcasted_iota(jnp.int32, u.shape, 0) == 0
    sh = jnp.where(first, uprev, pltpu.roll(u, 1, 0))
    return u + (sh - u) * mu, sh


def _post_rwkv(y, r, k2, v, g, lnx_g, lnx_b, r_k, seg, segt):
    inv = 1.0 / HEAD
    mean = _head_sums(y, seg, segt) * inv
    yc = y - mean
    var = _head_sums(yc * yc, seg, segt) * inv
    yn = yc * lax.rsqrt(var + GN_EPS) * lnx_g + lnx_b
    bonus = _head_sums(r * k2 * r_k, seg, segt) * v
    return (yn + bonus) * g


def _prep_fox(uf, qg, kg, fb, seg, segt, *, dims):
    FW, FHp = dims
    q, k, v, f = uf[:, :FW], uf[:, FW:2 * FW], uf[:, 2 * FW:3 * FW], uf[:, 3 * FW:3 * FW + FHp]
    inv = 1.0 / HEAD
    qn = q * lax.rsqrt(_head_sums(q * q, seg, segt) * inv + RMS_EPS) * qg * (HEAD ** -0.5)
    kn = k * lax.rsqrt(_head_sums(k * k, seg, segt) * inv + RMS_EPS) * kg
    return qn, kn, v, -_softplus(-(f + fb))


def _tail(h2, pe, z, png, pgb):
    return h2 + _sigmoid(z + pgb) * _rms(pe, png)


def _swiglu(gt, up):
    return gt * _sigmoid(gt) * up


def _cumsum(x, *, reverse, name):
    T, C = x.shape
    tc = _pick(T, (256, 128))
    n = T // tc
    i0 = lax.broadcasted_iota(jnp.int32, (tc, tc), 0)
    i1 = lax.broadcasted_iota(jnp.int32, (tc, tc), 1)
    tri = ((i0 <= i1) if reverse else (i0 >= i1)).astype(BF16)

    def body(x_ref, tri_ref, o_ref, carry):
        i = pl.program_id(0)

        @pl.when(i == 0)
        def _():
            carry[...] = jnp.zeros_like(carry)

        v = x_ref[...]
        hi = v.astype(BF16)
        r1 = v - hi.astype(F32)
        mid = r1.astype(BF16)
        lo = (r1 - mid.astype(F32)).astype(BF16)
        t = tri_ref[...]
        d = lambda p: jnp.dot(t, p, preferred_element_type=F32)
        c = d(hi) + d(mid) + d(lo) + carry[0:1, :]
        o_ref[...] = c
        edge = c[0:1, :] if reverse else c[tc - 1:tc, :]
        carry[...] = jnp.broadcast_to(edge, carry.shape)

    blk = pl.BlockSpec((tc, C), (lambda i: (n - 1 - i, 0)) if reverse else (lambda i: (i, 0)))
    return pl.pallas_call(
        body, name=name, grid=(n,), in_specs=[blk, pl.BlockSpec((tc, tc), lambda i: (0, 0))], out_specs=blk,
        out_shape=jax.ShapeDtypeStruct((T, C), F32), scratch_shapes=[pltpu.VMEM((8, C), F32)],
        compiler_params=_cparams("arbitrary"),
    )(x, tri)


BWD_PAIRS_PER_TRIP = 8
FWD_STEPS_PER_TRIP = 8
RECOMPUTE_STEPS_PER_TRIP = 8


def _steps(n, per_trip, step):
    def trip(i, carry):
        for j in range(per_trip):
            carry = step(i * per_trip + j, carry)
        return carry

    lax.fori_loop(0, n // per_trip, trip, 0)


def _col(tile, lane, t):
    return jnp.sum(jnp.where(lane == t, tile, 0.0), axis=1, keepdims=True)


def _scan_fwd(r, w, k, kk, kka, vT, *, hg, tc, comm=None):
    H, T, N = r.shape
    nc = T // tc

    def body(r_ref, w_ref, k_ref, kk_ref, kka_ref, vT_ref, yT_ref, chk_ref, s_ref):
        @pl.when(pl.program_id(1) == 0)
        def _():
            s_ref[...] = jnp.zeros_like(s_ref)

        chk_ref[:, 0] = s_ref[...]
        yT_ref[...] = jnp.zeros_like(yT_ref)
        lane = lax.broadcasted_iota(jnp.int32, (N, tc), 1)

        def emit_y(S, h, t):
            y = jnp.sum(S * r_ref[h, pl.ds(jnp.maximum(t, 0), 1), :], axis=1, keepdims=True)
            yT_ref[h] = jnp.where(lane == t, y, yT_ref[h])

        def step(t, carry):
            for h in range(hg):
                row = lambda ref: ref[h, pl.ds(t, 1), :]
                S = s_ref[h]
                emit_y(S, h, t - 1)
                vcol = _col(vT_ref[h], lane, t)
                sa = -jnp.sum(S * row(kk_ref), axis=1, keepdims=True)
                s_ref[h] = S * row(w_ref) + sa * row(kka_ref) + vcol * row(k_ref)
            return carry

        _steps(tc, FWD_STEPS_PER_TRIP, step)
        for h in range(hg):
            emit_y(s_ref[h], h, tc - 1)

    rows = pl.BlockSpec((hg, tc, N), lambda g, c: (g, c, 0))
    cols = pl.BlockSpec((hg, N, tc), lambda g, c: (g, 0, c))
    return _call_carrying(
        body, "scan_fwd", (H // hg, nc), [rows] * 5 + [cols],
        [cols, pl.BlockSpec((hg, 1, N, N), lambda g, c: (g, c, 0, 0))],
        [jax.ShapeDtypeStruct((H, N, T), F32), jax.ShapeDtypeStruct((H, nc, N, N), F32)],
        [pltpu.VMEM((hg, N, N), F32)], [r, w, k, kk, kka, vT], comm)


def _scan_bwd(r, w, k, kk, kka, vT, dyT, chk, *, hg, tc, comm=None):
    H, N, T = vT.shape
    nc = T // tc
    hp = hg // 2
    L = 2 * N

    def body(r_ref, w_ref, k_ref, kk_ref, kka_ref, vT_ref, dyT_ref, chk_ref,
             dr_ref, dw_ref, dk_ref, dkk_ref, dkka_ref, dvT_ref, sp_ref, ds_ref):
        @pl.when(pl.program_id(1) == 0)
        def _():
            ds_ref[...] = jnp.zeros_like(ds_ref)

        dvT_ref[...] = jnp.zeros_like(dvT_ref)
        lane = lax.broadcasted_iota(jnp.int32, (N, tc), 1)
        left = lax.broadcasted_iota(jnp.int32, (N, L), 1) < N
        left_row = lax.broadcasted_iota(jnp.int32, (1, L), 1) < N

        def halves(x, lf):
            a = jnp.sum(jnp.where(lf, x, 0.0), axis=1, keepdims=True)
            b = jnp.sum(jnp.where(lf, 0.0, x), axis=1, keepdims=True)
            return a, b, jnp.where(lf, a, b)

        def pair_col(ref, p, t):
            return jnp.where(left, _col(ref[2 * p], lane, t), _col(ref[2 * p + 1], lane, t))

        for p in range(hp):
            sp_ref[p, 0] = jnp.concatenate([chk_ref[2 * p, 0], chk_ref[2 * p + 1, 0]], axis=1)

        def fstep(t, carry):
            for p in range(hp):
                row = lambda ref: ref[p, pl.ds(t, 1), :]
                S = sp_ref[p, t]
                sa = halves(S * row(kk_ref), left)[2]
                sp_ref[p, t + 1] = S * row(w_ref) - sa * row(kka_ref) + pair_col(vT_ref, p, t) * row(k_ref)
            return carry

        _steps(tc, RECOMPUTE_STEPS_PER_TRIP, fstep)

        def bstep(p0, i, carry):
            t = tc - 1 - i
            for p in range(p0, min(p0 + BWD_PAIRS_PER_TRIP, hp)):
                row = lambda ref: ref[p, pl.ds(t, 1), :]
                rr, wr, kr, kkr, kkar = row(r_ref), row(w_ref), row(k_ref), row(kk_ref), row(kka_ref)
                Sp = sp_ref[p, t]
                Sn = sp_ref[p, t + 1]
                dycol, vcol = pair_col(dyT_ref, p, t), pair_col(vT_ref, p, t)
                dS = ds_ref[p]
                dSn = dS + dycol * rr
                dsa = halves(dS * kkar, left)[2] + dycol * halves(rr * kkar, left_row)[2]
                dr_ref[p, pl.ds(t, 1), :] = jnp.sum(Sn * dycol, axis=0, keepdims=True)
                sa = halves(Sp * kkr, left)[2]
                dw_ref[p, pl.ds(t, 1), :] = jnp.sum(dSn * Sp, axis=0, keepdims=True)
                dkka_ref[p, pl.ds(t, 1), :] = -jnp.sum(dSn * sa, axis=0, keepdims=True)
                dva, dvb, _ = halves(dSn * kr, left)
                dk_ref[p, pl.ds(t, 1), :] = jnp.sum(dSn * vcol, axis=0, keepdims=True)
                dkk_ref[p, pl.ds(t, 1), :] = -jnp.sum(Sp * dsa, axis=0, keepdims=True)
                ds_ref[p] = dSn * wr - dsa * kkr
                dvT_ref[2 * p] = jnp.where(lane == t, dva, dvT_ref[2 * p])
                dvT_ref[2 * p + 1] = jnp.where(lane == t, dvb, dvT_ref[2 * p + 1])
            return carry

        for p0 in range(0, hp, BWD_PAIRS_PER_TRIP):
            lax.fori_loop(0, tc, functools.partial(bstep, p0), 0)

    rows = pl.BlockSpec((hp, tc, L), lambda g, c: (g, nc - 1 - c, 0))
    cols = pl.BlockSpec((hg, N, tc), lambda g, c: (g, 0, nc - 1 - c))
    return _call_carrying(
        body, "scan_bwd", (H // hg, nc),
        [rows] * 5 + [cols, cols, pl.BlockSpec((hg, 1, N, N), lambda g, c: (g, nc - 1 - c, 0, 0))], [rows] * 5 + [cols],
        [jax.ShapeDtypeStruct((H // 2, T, L), F32)] * 5 + [jax.ShapeDtypeStruct((H, N, T), F32)],
        [pltpu.VMEM((hp, tc + 1, N, L), F32), pltpu.VMEM((hp, N, L), F32)], [r, w, k, kk, kka, vT, dyT, chk], comm)


_NT = (((1,), (1,)), ((), ()))
_TN = (((0,), (0,)), ((), ()))


def _scores(q, k, cc, cr, masked):
    s = lax.dot_general(q, k, _NT, preferred_element_type=F32) + cc - cr
    if masked:
        tb = s.shape[0]
        keep = lax.broadcasted_iota(jnp.int32, (tb, tb), 0) >= lax.broadcasted_iota(jnp.int32, (tb, tb), 1)
        s = jnp.where(keep, s, NEG)
    return s


def _attn_specs(T, N, tb):
    blk = pl.BlockSpec((1, tb, N), lambda h, i: (h, i, 0))
    whole = pl.BlockSpec((1, T, N), lambda h, i: (h, 0, 0))
    col = pl.BlockSpec((1, tb, 1), lambda h, i: (h, i, 0))
    wcol = pl.BlockSpec((1, T, 1), lambda h, i: (h, 0, 0))
    row = pl.BlockSpec((1, 1, tb), lambda h, i: (h, 0, i))
    wrow = pl.BlockSpec((1, 1, T), lambda h, i: (h, 0, 0))
    return blk, whole, col, wcol, row, wrow


def _call_carrying(body, name, grid, in_specs, out_specs, out_shape, scratch, args, comm):
    n_out = len(out_specs)
    if comm is not None:
        n = len(comm[0])
        body = _carrying(body, len(in_specs), n_out, grid, comm)
        in_specs, out_specs = in_specs + [_HBM] * n, out_specs + [_HBM] * n
        out_shape, scratch, args = out_shape + _exchange_shapes(*comm), scratch + _exchange_sems(n), args + list(comm[0])
    res = pl.pallas_call(
        body, name=name, grid=grid, in_specs=in_specs, out_specs=out_specs, out_shape=out_shape,
        scratch_shapes=scratch, compiler_params=_cparams(*["arbitrary"] * len(grid)),
    )(*args)
    return res[:n_out], res[n_out:]


def _attn_fwd(q, k, v, ccol, crow, *, tb, comm=None):
    H, T, N = q.shape

    def body(q_ref, k_ref, v_ref, cc_ref, cr_ref, o_ref, lse_ref, m_s, l_s, acc_s):
        qi = pl.program_id(1)
        m_s[...] = jnp.full_like(m_s, NEG)
        l_s[...] = jnp.zeros_like(l_s)
        acc_s[...] = jnp.zeros_like(acc_s)
        q_, cc = q_ref[0], cc_ref[0]

        def block(j, masked):
            at = pl.ds(pl.multiple_of(j * tb, tb), tb)
            s = _scores(q_, k_ref[0, at, :], cc, cr_ref[0, :, at], masked)
            m_new = jnp.maximum(m_s[...], jnp.max(s, axis=1, keepdims=True))
            p = jnp.exp(s - m_new)
            alpha = jnp.exp(m_s[...] - m_new)
            l_s[...] = alpha * l_s[...] + jnp.sum(p, axis=1, keepdims=True)
            acc_s[...] = alpha * acc_s[...] + jnp.dot(p.astype(BF16), v_ref[0, at, :], preferred_element_type=F32)
            m_s[...] = m_new

        def below(j, carry):
            block(j, False)
            return carry

        lax.fori_loop(0, qi, below, 0)
        block(qi, True)
        o_ref[0] = acc_s[...] / l_s[...]
        lse_ref[0] = m_s[...] + jnp.log(l_s[...])

    blk, whole, col, wcol, row, wrow = _attn_specs(T, N, tb)
    return _call_carrying(
        body, "fox_fwd", (H, T // tb), [blk, whole, whole, col, wrow], [blk, col],
        [jax.ShapeDtypeStruct((H, T, N), F32), jax.ShapeDtypeStruct((H, T, 1), F32)],
        [pltpu.VMEM((tb, 1), F32), pltpu.VMEM((tb, 1), F32), pltpu.VMEM((tb, N), F32)],
        [q, k, v, ccol, crow], comm)


def _attn_bwd(q, k, v, ccol, crow, o, lse, do, *, tb, comm=None):
    H, T, N = q.shape
    nb = T // tb

    def body(q_ref, k_ref, v_ref, cc_ref, cr_ref, o_ref, lse_ref, do_ref,
             dq_ref, dcq_ref, dk_ref, dv_ref, dck_ref, dq_s, dcq_s, dk_s, dv_s, dck_s):
        qi = pl.program_id(1)

        @pl.when(qi == 0)
        def _():
            dk_s[...] = jnp.zeros_like(dk_s)
            dv_s[...] = jnp.zeros_like(dv_s)
            dck_s[...] = jnp.zeros_like(dck_s)

        dq_s[...] = jnp.zeros_like(dq_s)
        dcq_s[...] = jnp.zeros_like(dcq_s)
        q_, cc, lse_, do_ = q_ref[0], cc_ref[0], lse_ref[0], do_ref[0]
        delta = jnp.sum(do_ * o_ref[0], axis=1, keepdims=True)
        dob = do_.astype(BF16)

        def block(j, masked):
            at = pl.ds(pl.multiple_of(j * tb, tb), tb)
            kb = k_ref[0, at, :]
            p = jnp.exp(_scores(q_, kb, cc, cr_ref[0, :, at], masked) - lse_)
            dp = lax.dot_general(dob, v_ref[0, at, :], _NT, preferred_element_type=F32)
            ds = p * (dp - delta)
            dsb = ds.astype(BF16)
            dq_s[...] += jnp.dot(dsb, kb, preferred_element_type=F32)
            dcq_s[...] += jnp.sum(ds, axis=1, keepdims=True)
            dv_s[at, :] += lax.dot_general(p.astype(BF16), dob, _TN, preferred_element_type=F32)
            dk_s[at, :] += lax.dot_general(dsb, q_, _TN, preferred_element_type=F32)
            dck_s[:, at] += jnp.sum(ds, axis=0, keepdims=True)

        def below(j, carry):
            block(j, False)
            return carry

        lax.fori_loop(0, qi, below, 0)
        block(qi, True)
        dq_ref[0] = dq_s[...]
        dcq_ref[0] = dcq_s[...]

        @pl.when(qi == nb - 1)
        def _():
            dk_ref[0] = dk_s[...]
            dv_ref[0] = dv_s[...]
            dck_ref[0] = dck_s[...]

    blk, whole, col, wcol, row, wrow = _attn_specs(T, N, tb)
    return _call_carrying(
        body, "fox_bwd", (H, nb), [blk, whole, whole, col, wrow, blk, col, blk], [blk, col, whole, whole, wrow],
        [jax.ShapeDtypeStruct((H, T, N), F32), jax.ShapeDtypeStruct((H, T, 1), F32), jax.ShapeDtypeStruct((H, T, N), F32),
         jax.ShapeDtypeStruct((H, T, N), F32), jax.ShapeDtypeStruct((H, 1, T), F32)],
        [pltpu.VMEM((tb, N), F32), pltpu.VMEM((tb, 1), F32), pltpu.VMEM((T, N), F32), pltpu.VMEM((T, N), F32),
         pltpu.VMEM((1, T), F32)],
        [q, k, v, ccol, crow, o, lse, do], comm)


def _heads(x):
    T = x.shape[0]
    return x.reshape(T, -1, HEAD).transpose(1, 0, 2)


def _pairs(x):
    T = x.shape[0]
    return x.reshape(T, -1, 2 * HEAD).transpose(1, 0, 2)


def _headsT(x):
    T = x.shape[0]
    return x.reshape(T, -1, HEAD).transpose(1, 2, 0)


def _unheads(x):
    return x.transpose(1, 0, 2).reshape(x.shape[1], -1)


def _unheadsT(x):
    return x.transpose(2, 0, 1).reshape(x.shape[2], -1)


def _padc(x, n):
    return jnp.pad(x, ((0, 0), (0, n - x.shape[1])))


def _padr(x, n):
    return jnp.pad(x, ((0, n - x.shape[0]), (0, 0)))


class _Dims:
    def __init__(self, W, DL, AL, GL, FH):
        self.W, self.DL, self.AL, self.GL, self.FH = W, DL, AL, GL, FH
        self.DLp, self.ALp, self.GLp, self.FHp = _rup(DL, LANE), _rup(AL, LANE), _rup(GL, LANE), _rup(FH, LANE)
        self.FW = FH * HEAD
        self.RC = 3 * W + DL + AL + GL
        self.RP = 3 * W + self.DLp + self.ALp + self.GLp
        self.FC = 3 * self.FW + FH
        self.FP = 3 * self.FW + self.FHp

    def pad_r(self, a):
        W, o = self.W, 3 * self.W
        return jnp.concatenate([a[:, :o], _padc(a[:, o:o + self.DL], self.DLp),
                                _padc(a[:, o + self.DL:o + self.DL + self.AL], self.ALp),
                                _padc(a[:, o + self.DL + self.AL:self.RC], self.GLp)], axis=1)

    def unpad_r(self, a):
        o = 3 * self.W
        return jnp.concatenate([a[:, :o], a[:, o:o + self.DL], a[:, o + self.DLp:o + self.DLp + self.AL],
                                a[:, o + self.DLp + self.ALp:o + self.DLp + self.ALp + self.GL]], axis=1)

    def pad_f(self, a):
        return _padc(a, self.FP)

    def unpad_f(self, a):
        return a[:, :self.FC]


def _local_step(x, p, tgt, Wt, vec, d, late_shards=None):
    Wt = dict(Wt)
    T, D = x.shape
    W, FW = d.W, d.FW
    H = W // HEAD
    seg, segt = _seg_mats(W)
    segf, segft = _seg_mats(FW)
    T1 = 256
    rk_flat = vec["r_k"].reshape(1, W)
    mu = d.pad_r(vec["shift_mu"])
    qg = jnp.tile(vec["q_norm_g"], (1, d.FH))
    kg = jnp.tile(vec["k_norm_g"], (1, d.FH))
    fb = _padc(vec["fgate_b"], d.FHp)
    pdims = (W, d.DLp, d.ALp, d.GLp)
    fdims = (FW, d.FHp)

    (xn,) = _rowwise(lambda x_, g_: _rms(x_, g_), [x], [vec["attn_norm_g"]], [(D, BF16)], tile=T1, name="norm_attn")
    u_r = _mm(xn, Wt["w_in_r"], name="mm_in_r")
    u_f = _mm(xn, Wt["w_in_f"], name="mm_in_f")

    prep_consts = [mu, vec["w0"], Wt["w2"].astype(F32), vec["a0"], Wt["a2"].astype(F32), Wt["g2"].astype(F32), vec["k_k"], vec["k_a"], seg, segt]

    def prep_fwd(u_, up_, mu_, *cs):
        um, _ = _shift_mix(u_, up_, mu_)
        return _prep_rwkv(um, *cs, dims=pdims)

    r, dec, k2, v, kk, kka, g = _rowwise(prep_fwd, [u_r], prep_consts, [(W, F32)] * 7, tile=128, name="rwkv_prep", prev=[u_r])
    hg, tc = min(16, H), 128
    (yT, chk), gathered = _scan_fwd(_heads(r), _heads(dec), _heads(k2), _heads(kk), _heads(kka), _headsT(v), hg=hg, tc=tc,
                                    comm=late_shards and (late_shards, True))
    Wt.update({n: _whole(n, g) for n, g in zip(_LATE, gathered)})
    y = _unheadsT(yT)
    post_consts = [vec["lnx_g"], vec["lnx_b"], rk_flat, seg, segt]
    (y_r,) = _rowwise(_post_rwkv, [y, r, k2, v, g], post_consts, [(W, BF16)], tile=T1, name="rwkv_post")

    fox_consts = [qg, kg, fb, segf, segft]
    qn, kn, vf, logf = _rowwise(functools.partial(_prep_fox, dims=fdims), [u_f], fox_consts,
                                [(FW, BF16), (FW, BF16), (FW, BF16), (d.FHp, F32)], tile=T1, name="fox_prep")
    c = _cumsum(logf, reverse=False, name="fox_cumsum")
    cT = c[:, :d.FH].T
    ccol, crow = cT[:, :, None], cT[:, None, :]
    tb = _pick(T, (1024, 512, 256, 128))
    qh, kh, vh = _heads(qn), _heads(kn), _heads(vf)
    (o, lse), _ = _attn_fwd(qh, kh, vh, ccol, crow, tb=tb)
    y_f = _unheads(o)

    ycat = jnp.concatenate([y_r, y_f.astype(BF16)], axis=1)
    h1 = _mm(ycat, Wt["w_out"], add=x, name="mm_out")
    (hn,) = _rowwise(lambda h_, g_: _rms(h_, g_), [h1], [vec["ffn_norm_g"]], [(D, BF16)], tile=T1, name="norm_ffn")
    gt = _mm(hn, Wt["w_gate"], name="mm_gate")
    up = _mm(hn, Wt["w_up"], name="mm_up")
    (act,) = _rowwise(_swiglu, [gt, up], [], [(gt.shape[1], BF16)], tile=T1, name="swiglu")
    h2 = _mm(act, Wt["w_down"], add=h1, name="mm_down")
    (hg_,) = _rowwise(lambda h_, g_: _rms(h_, g_), [h2], [vec["ple_gate_norm_g"]], [(D, BF16)], tile=T1, name="norm_gate")
    pe = _mm(p, Wt["ple_proj"], name="mm_ple")
    z = _mm(hg_, Wt["ple_gate_w"], name="mm_pgate")

    def tail(h2_, pe_, z_, tg_, png_, pgb_):
        h3, f = jax.vjp(_tail, h2_, pe_, z_, png_, pgb_)
        err = h3 - tg_
        dh3 = err * (1.0 / D)
        lt = 0.5 * jnp.sum(jnp.sum(err * err, axis=1, keepdims=True) * (1.0 / D), axis=0, keepdims=True)
        dh2_, dpe_, dz_, dpng_, dpgb_ = f(dh3)
        return dh2_, dpe_, dz_, jnp.broadcast_to(lt, (1, LANE)), dpng_, dpgb_

    dh3, dpe, dz, loss, g_png, g_pgb = _rowwise(
        tail, [h2, pe, z, tgt], [vec["ple_norm_g"], vec["ple_gate_b"]], [(D, F32), (D, BF16), (D, BF16)],
        [(1, LANE), (1, D), (1, D)], tile=T1, name="tail")
    G = {}
    gv = {"ple_norm_g": g_png, "ple_gate_b": g_pgb}
    G["ple_gate_w"] = _mm(hg_, dz, ta=True, out_dtype=BF16, name="gw_pgate")
    G["ple_proj"] = _mm(p, dpe, ta=True, out_dtype=BF16, name="gw_ple")
    d_hg = _mm(dz, Wt["ple_gate_w"], tb=True, name="mmb_pgate")

    def norm_bwd(h_, dres_, dn_, g_):
        _, f = jax.vjp(_rms, h_, g_)
        dh_, dg_ = f(dn_)
        return dres_ + dh_, dg_

    dh2, gv["ple_gate_norm_g"] = _rowwise(norm_bwd, [h2, dh3, d_hg], [vec["ple_gate_norm_g"]], [(D, F32)], [(1, D)],
                                          tile=T1, name="norm_gate_bwd")
    G["w_down"] = _mm(act, dh2, ta=True, out_dtype=BF16, name="gw_down")
    d_act = _mm(dh2, Wt["w_down"], tb=True, name="mmb_down")

    def swiglu_bwd(gt_, up_, da_):
        _, f = jax.vjp(_swiglu, gt_, up_)
        return f(da_)

    d_gt, d_up = _rowwise(swiglu_bwd, [gt, up, d_act], [], [(gt.shape[1], BF16)] * 2, tile=T1, name="swiglu_bwd")
    G["w_gate"] = _mm(hn, d_gt, ta=True, out_dtype=BF16, name="gw_gate")
    G["w_up"] = _mm(hn, d_up, ta=True, out_dtype=BF16, name="gw_up")
    d_hn = _mm(d_gt, Wt["w_gate"], tb=True, name="mmb_gate")
    d_hn = _mm(d_up, Wt["w_up"], tb=True, add=d_hn, name="mmb_up")
    dh1, gv["ffn_norm_g"] = _rowwise(norm_bwd, [h1, dh2, d_hn], [vec["ffn_norm_g"]], [(D, F32)], [(1, D)],
                                     tile=T1, name="norm_ffn_bwd")
    G["w_out"] = _mm(ycat, dh1, ta=True, out_dtype=BF16, name="gw_out")
    d_ycat = _mm(dh1, Wt["w_out"], tb=True, name="mmb_out")

    def post_bwd(y_, r_, k2_, v_, g_, dy_, *cs):
        lg, lb, rk, sg, sgt = cs
        _, f = jax.vjp(lambda *a: _post_rwkv(*a, sg, sgt), y_, r_, k2_, v_, g_, lg, lb, rk)
        return f(dy_)

    dy, dr1, dk1, dv1, dg, gv["lnx_g"], gv["lnx_b"], g_rk = _rowwise(
        post_bwd, [y, r, k2, v, g, (d_ycat, W, 0)], post_consts, [(W, F32)] * 5, [(1, W)] * 3, tile=128, name="rwkv_post_bwd")
    gv["r_k"] = g_rk.reshape(H, HEAD)
    pieces = late_shards and ([_pieces(n, G[n]).astype(BF16) for n in _LATE], False)
    (dr, ddec, dk2, dkk, dkka, dvT), recv = _scan_bwd(_pairs(r), _pairs(dec), _pairs(k2), _pairs(kk), _pairs(kka),
                                                      _headsT(v), _headsT(dy), chk, hg=hg, tc=tc, comm=pieces)
    recv = dict(zip(_LATE, recv))
    dr, ddec, dk2, dkk, dkka, dv = (_unheads(dr), _unheads(ddec), _unheads(dk2), _unheads(dkk), _unheads(dkka), _unheadsT(dvT))

    def prep_bwd(u_, dr_, dr1_, ddec_, dk2_, dk1_, dv_, dv1_, dkk_, dkka_, dg_, up_, mu_, *cs):
        um, sh = _shift_mix(u_, up_, mu_)
        cs_d, sg, sgt = cs[:7], cs[7], cs[8]
        _, f = jax.vjp(lambda um_, *c_: _prep_rwkv(um_, *c_, sg, sgt, dims=pdims), um, *cs_d)
        res = f((dr_ + dr1_, ddec_, dk2_ + dk1_, dv_ + dv1_, dkk_, dkka_, dg_))
        dum = res[0]
        dmu = jnp.sum(dum * (sh - u_), axis=0, keepdims=True)
        return (dum, dmu) + tuple(res[1:])

    LP = [Wt["w2"].shape, Wt["a2"].shape, Wt["g2"].shape]
    dum, g_mu, gv["w0"], g_w2, gv["a0"], g_a2, g_g2, gv["k_k"], gv["k_a"] = _rowwise(
        prep_bwd, [u_r, dr, dr1, ddec, dk2, dk1, dv, dv1, dkk, dkka, dg], prep_consts, [(d.RP, F32)],
        [(1, d.RP), (1, W), LP[0], (1, W), LP[1], LP[2], (1, W), (1, W)], tile=128, name="rwkv_prep_bwd", prev=[u_r])
    gv["shift_mu"] = d.unpad_r(g_mu)
    G["w2"], G["a2"], G["g2"] = g_w2, g_a2, g_g2
    (du_r,) = _rowwise(lambda a_, an_, mu_: a_ * (1.0 - mu_) + jnp.where(
        lax.broadcasted_iota(jnp.int32, a_.shape, 0) == a_.shape[0] - 1, an_, pltpu.roll(a_, a_.shape[0] - 1, 0)) * mu_,
        [dum], [mu], [(d.RP, BF16)], tile=T1, name="shift_bwd", nxt=[dum])

    do = _heads(d_ycat[:, W:])
    (dq, dcq, dk_, dv_, dck), _ = _attn_bwd(qh, kh, vh, ccol, crow, o, lse, do, tb=tb)
    dc = _padc((dcq[:, :, 0] - dck[:, 0, :]).T, d.FHp)
    dlogf = _cumsum(dc, reverse=True, name="fox_cumsum_bwd")

    def fox_bwd(uf_, dq_, dk__, dv__, dlf_, *cs):
        qg_, kg_, fb_, sg, sgt = cs
        _, f = jax.vjp(lambda uf__, a, b, c_: _prep_fox(uf__, a, b, c_, sg, sgt, dims=fdims), uf_, qg_, kg_, fb_)
        return f((dq_, dk__, dv__, dlf_))

    du_f, g_qg, g_kg, g_fb = _rowwise(fox_bwd, [u_f, _unheads(dq), _unheads(dk_), _unheads(dv_), dlogf], fox_consts,
                                      [(d.FP, BF16)], [(1, FW), (1, FW), (1, d.FHp)], tile=T1, name="fox_prep_bwd")
    gv["q_norm_g"] = g_qg.reshape(d.FH, HEAD).sum(0, keepdims=True)
    gv["k_norm_g"] = g_kg.reshape(d.FH, HEAD).sum(0, keepdims=True)
    gv["fgate_b"] = g_fb[:, :d.FH]

    G["w_in_r"] = _mm(xn, du_r, ta=True, out_dtype=BF16, name="gw_in_r")
    G["w_in_f"] = _mm(xn, du_f, ta=True, out_dtype=BF16, name="gw_in_f")
    d_xn = _mm(du_r, Wt["w_in_r"], tb=True, name="mmb_in_r")
    d_xn = _mm(du_f, Wt["w_in_f"], tb=True, add=d_xn, name="mmb_in_f")
    dx, gv["attn_norm_g"] = _rowwise(norm_bwd, [x, dh1, d_xn], [vec["attn_norm_g"]], [(D, F32)], [(1, D)],
                                     tile=T1, name="norm_attn_bwd")
    return loss, dx, G, gv, recv


_HBM = pl.BlockSpec(memory_space=pl.ANY)
_OTHER_CHIPS = ((0, 1), (1, 0), (1, 1))


def _flip(v, bit):
    return 1 - v if bit else v


def _exchange(arrs, *, gather, name):
    n = len(arrs)

    def body(*refs):
        copies = _exchange_copies(refs[:n], refs[n:2 * n], refs[2 * n:], gather)
        for cp in copies:
            cp.start()
        for cp in copies:
            cp.wait()

    return pl.pallas_call(
        body, name=name, in_specs=[_HBM] * n, out_specs=[_HBM] * n, out_shape=_exchange_shapes(arrs, gather),
        scratch_shapes=_exchange_sems(n),
    )(*arrs)


def _exchange_shapes(arrs, gather):
    return [jax.ShapeDtypeStruct(((N_CHIPS,) + a.shape) if gather else a.shape, a.dtype) for a in arrs]


def _exchange_sems(n):
    return [pltpu.SemaphoreType.DMA((3 * n,)), pltpu.SemaphoreType.DMA((3 * n,)), pltpu.SemaphoreType.DMA((n,))]


def _exchange_copies(ins, outs, sems, gather):
    send_sems, recv_sems, own_sems = sems
    x, y, c = lax.axis_index("x"), lax.axis_index("y"), lax.axis_index("c")
    me = 2 * x + y
    copies = []
    for a in range(len(ins)):
        copies.append(pltpu.make_async_copy(ins[a] if gather else ins[a].at[me], outs[a].at[me], own_sems.at[a]))
        for k, (dx, dy) in enumerate(_OTHER_CHIPS):
            px, py = _flip(x, dx), _flip(y, dy)
            copies.append(pltpu.make_async_remote_copy(
                src_ref=ins[a] if gather else ins[a].at[2 * px + py], dst_ref=outs[a].at[me],
                send_sem=send_sems.at[3 * a + k], recv_sem=recv_sems.at[3 * a + k],
                device_id=(px, py, c), device_id_type=MESH))
    return copies


def _carrying(body, n_in, n_out, grid, comm):
    arrs, gather = comm
    n = len(arrs)

    def wrapped(*refs):
        c_in = refs[n_in:n_in + n]
        c_out = refs[n_in + n + n_out:n_in + 2 * n + n_out]
        ids = [pl.program_id(a) for a in range(len(grid))]
        first = functools.reduce(jnp.logical_and, [i == 0 for i in ids])
        last = functools.reduce(jnp.logical_and, [i == g - 1 for i, g in zip(ids, grid)])

        @pl.when(first)
        def _():
            for cp in _exchange_copies(c_in, c_out, refs[-3:], gather):
                cp.start()

        body(*refs[:n_in], *refs[n_in + n:n_in + n + n_out], *refs[n_in + 2 * n + n_out:-3])

        @pl.when(last)
        def _():
            for cp in _exchange_copies(c_in, c_out, refs[-3:], gather):
                cp.wait()

    return wrapped


def _swap_cores(arrs, *, name):
    n = len(arrs)

    def body(*refs):
        ins, outs = refs[:n], refs[n:2 * n]
        send_sems, recv_sems = refs[2 * n:]
        peer = (lax.axis_index("x"), lax.axis_index("y"), 1 - lax.axis_index("c"))
        cps = [pltpu.make_async_remote_copy(src_ref=ins[a], dst_ref=outs[a], send_sem=send_sems.at[a],
                                            recv_sem=recv_sems.at[a], device_id=peer, device_id_type=MESH) for a in range(n)]
        for cp in cps:
            cp.start()
        for cp in cps:
            cp.wait()

    return pl.pallas_call(
        body, name=name, in_specs=[_HBM] * n, out_specs=[_HBM] * n,
        out_shape=[jax.ShapeDtypeStruct(a.shape, a.dtype) for a in arrs],
        scratch_shapes=[pltpu.SemaphoreType.DMA((n,)), pltpu.SemaphoreType.DMA((n,))],
    )(*arrs)


def _allreduce_small(pack, *, name):
    R, C = pack.shape

    def body(p_ref, o_ref, recv, send_sems, recv_sems):
        x, y, c = lax.axis_index("x"), lax.axis_index("y"), lax.axis_index("c")
        me = 4 * x + 2 * y + c
        recv[me] = p_ref[...]
        cps = []
        for k in range(1, N_DEV):
            peer = (_flip(x, k & 4), _flip(y, k & 2), _flip(c, k & 1))
            cp = pltpu.make_async_remote_copy(src_ref=p_ref, dst_ref=recv.at[me], send_sem=send_sems.at[k - 1],
                                              recv_sem=recv_sems.at[k - 1], device_id=peer, device_id_type=MESH)
            cp.start()
            cps.append(cp)
        for cp in cps:
            cp.wait()
        acc = recv[0]
        for s in range(1, N_DEV):
            acc = acc + recv[s]
        o_ref[...] = acc

    vm = pl.BlockSpec(memory_space=pltpu.VMEM)
    return pl.pallas_call(
        body, name=name, in_specs=[vm], out_specs=vm, out_shape=jax.ShapeDtypeStruct((R, C), F32),
        scratch_shapes=[pltpu.VMEM((N_DEV, R, C), F32), pltpu.SemaphoreType.DMA((N_DEV - 1,)), pltpu.SemaphoreType.DMA((N_DEV - 1,))],
    )(pack)


def _sum_slots(a, *, name):
    S, R, C = a.shape
    tr = _pick(R, (256, 128, 64, 32, 16, 8))

    def body(a_ref, o_ref):
        acc = a_ref[0].astype(F32)
        for s in range(1, S):
            acc = acc + a_ref[s].astype(F32)
        o_ref[...] = acc

    return pl.pallas_call(
        body, name=name, grid=(R // tr,), in_specs=[pl.BlockSpec((S, tr, C), lambda i: (0, i, 0))],
        out_specs=pl.BlockSpec((tr, C), lambda i: (i, 0)), out_shape=jax.ShapeDtypeStruct((R, C), F32),
        compiler_params=_cparams("parallel"),
    )(a)


def _adamw(w, m, v, gs, *, name):
    R, C = w.shape
    tile = _pick(R, (128, 96, 64, 32, 16, 8))

    def fn(w_, m_, v_, *g_):
        g = g_[0]
        for e in g_[1:]:
            g = g + e
        m2 = ADAM_B1 * m_ + (1.0 - ADAM_B1) * g
        v2 = ADAM_B2 * v_ + (1.0 - ADAM_B2) * jnp.square(g)
        m_hat = m2 / (1.0 - ADAM_B1 ** ADAM_STEP)
        v_hat = v2 / (1.0 - ADAM_B2 ** ADAM_STEP)
        delta = -ADAM_LR * (m_hat / (jnp.sqrt(v_hat) + ADAM_EPS) + ADAM_WD * w_)
        return g, delta, m2, v2

    return _rowwise(fn, [w, m, v, *gs], [], [(C, F32)] * 4, tile=tile, name=name)


_ARGS = "x, p, attn_norm_g, w_in, shift_mu, w0, w2, a0, a2, g2, k_k, k_a, r_k, lnx_g, lnx_b, q_norm_g, k_norm_g, fgate_b, w_out, ffn_norm_g, w_gate, w_up, w_down, ple_proj, ple_norm_g, ple_gate_norm_g, ple_gate_w, ple_gate_b, loss_target, m_attn_norm_g, m_w_in, m_shift_mu, m_w0, m_w2, m_a0, m_a2, m_g2, m_k_k, m_k_a, m_r_k, m_lnx_g, m_lnx_b, m_q_norm_g, m_k_norm_g, m_fgate_b, m_w_out, m_ffn_norm_g, m_w_gate, m_w_up, m_w_down, m_ple_proj, m_ple_norm_g, m_ple_gate_norm_g, m_ple_gate_w, m_ple_gate_b, v_attn_norm_g, v_w_in, v_shift_mu, v_w0, v_w2, v_a0, v_a2, v_g2, v_k_k, v_k_a, v_r_k, v_lnx_g, v_lnx_b, v_q_norm_g, v_k_norm_g, v_fgate_b, v_w_out, v_ffn_norm_g, v_w_gate, v_w_up, v_w_down, v_ple_proj, v_ple_norm_g, v_ple_gate_norm_g, v_ple_gate_w, v_ple_gate_b".split(", ")
_WEIGHTS = _ARGS[2:28]
_COL_SHARDED = ("w_in", "w2", "a2", "g2", "w_gate", "w_up", "ple_proj")
_ROW_SHARDED = ("w_out", "w_down", "ple_gate_w")
_MATRICES = _COL_SHARDED + _ROW_SHARDED
_EARLY = ("w_in", "w2", "a2", "g2")
_LATE = tuple(n for n in _MATRICES if n not in _EARLY)
_VECTORS = tuple(n for n in _WEIGHTS if n not in _MATRICES)


def _whole(name, g):
    if name in _COL_SHARDED:
        return g.transpose(1, 0, 2).reshape(g.shape[1], -1)
    return g.reshape(-1, g.shape[2])


def _pieces(name, a):
    if name in _COL_SHARDED:
        return a.reshape(a.shape[0], N_CHIPS, -1).transpose(1, 0, 2)
    return a.reshape(N_CHIPS, -1, a.shape[1])


def kernel(x, p, attn_norm_g, w_in, shift_mu, w0, w2, a0, a2, g2, k_k, k_a, r_k, lnx_g, lnx_b, q_norm_g, k_norm_g, fgate_b, w_out, ffn_norm_g, w_gate, w_up, w_down, ple_proj, ple_norm_g, ple_gate_norm_g, ple_gate_w, ple_gate_b, loss_target, m_attn_norm_g, m_w_in, m_shift_mu, m_w0, m_w2, m_a0, m_a2, m_g2, m_k_k, m_k_a, m_r_k, m_lnx_g, m_lnx_b, m_q_norm_g, m_k_norm_g, m_fgate_b, m_w_out, m_ffn_norm_g, m_w_gate, m_w_up, m_w_down, m_ple_proj, m_ple_norm_g, m_ple_gate_norm_g, m_ple_gate_w, m_ple_gate_b, v_attn_norm_g, v_w_in, v_shift_mu, v_w0, v_w2, v_a0, v_a2, v_g2, v_k_k, v_k_a, v_r_k, v_lnx_g, v_lnx_b, v_q_norm_g, v_k_norm_g, v_fgate_b, v_w_out, v_ffn_norm_g, v_w_gate, v_w_up, v_w_down, v_ple_proj, v_ple_norm_g, v_ple_gate_norm_g, v_ple_gate_w, v_ple_gate_b):
    A = dict(zip(_ARGS, (x, p, attn_norm_g, w_in, shift_mu, w0, w2, a0, a2, g2, k_k, k_a, r_k, lnx_g, lnx_b, q_norm_g, k_norm_g, fgate_b, w_out, ffn_norm_g, w_gate, w_up, w_down, ple_proj, ple_norm_g, ple_gate_norm_g, ple_gate_w, ple_gate_b, loss_target, m_attn_norm_g, m_w_in, m_shift_mu, m_w0, m_w2, m_a0, m_a2, m_g2, m_k_k, m_k_a, m_r_k, m_lnx_g, m_lnx_b, m_q_norm_g, m_k_norm_g, m_fgate_b, m_w_out, m_ffn_norm_g, m_w_gate, m_w_up, m_w_down, m_ple_proj, m_ple_norm_g, m_ple_gate_norm_g, m_ple_gate_w, m_ple_gate_b, v_attn_norm_g, v_w_in, v_shift_mu, v_w0, v_w2, v_a0, v_a2, v_g2, v_k_k, v_k_a, v_r_k, v_lnx_g, v_lnx_b, v_q_norm_g, v_k_norm_g, v_fgate_b, v_w_out, v_ffn_norm_g, v_w_gate, v_w_up, v_w_down, v_ple_proj, v_ple_norm_g, v_ple_gate_norm_g, v_ple_gate_w, v_ple_gate_b)))
    x, p, tgt = A["x"][0], A["p"][0, 0], A["loss_target"][0]
    d = _Dims(W=A["w0"].shape[-1], DL=A["w2"].shape[1], AL=A["a2"].shape[1], GL=A["g2"].shape[1], FH=A["fgate_b"].shape[-1])

    shard = lambda n: A[n][0].astype(BF16)
    gathered = _exchange([shard(n) for n in _EARLY], gather=True, name="gather_early")
    full = {n: _whole(n, g) for n, g in zip(_EARLY, gathered)}
    Wt = {"w_in_r": d.pad_r(full["w_in"][:, :d.RC]), "w_in_f": d.pad_f(full["w_in"][:, d.RC:]),
          "w2": _padr(full["w2"], d.DLp), "a2": _padr(full["a2"], d.ALp), "g2": _padr(full["g2"], d.GLp)}
    vec = {n: A[n].reshape(-1, A[n].shape[-1]) for n in _VECTORS}

    loss, dx, G, gv, recv = _local_step(x, p, tgt, Wt, vec, d, late_shards=[shard(n) for n in _LATE])

    gw = {"w_in": jnp.concatenate([d.unpad_r(G["w_in_r"]), d.unpad_f(G["w_in_f"])], axis=1),
          "w2": G["w2"][:d.DL], "a2": G["a2"][:d.AL], "g2": G["g2"][:d.GL]}
    recv.update(zip(_EARLY, _exchange([_pieces(n, gw[n]).astype(BF16) for n in _EARLY], gather=False, name="scatter_early")))
    part = [_sum_slots(recv[n], name="sum_" + n) for n in _MATRICES]
    sib = _swap_cores(part, name="swap_cores")

    sizes = [1] + [A[n].size for n in _VECTORS]
    rows = _rup(_rup(sum(sizes), LANE) // LANE, 8)

    def pack(items):
        flat = jnp.concatenate([i.reshape(-1) for i in items])
        return jnp.pad(flat, (0, rows * LANE - flat.shape[0])).reshape(rows, LANE)

    red = _allreduce_small(pack([loss[0, :1]] + [gv[n] for n in _VECTORS]), name="allreduce_vectors")
    zero = jnp.zeros((1,), F32)
    upd = _adamw(pack([zero] + [A[n] for n in _VECTORS]), pack([zero] + [A["m_" + n] for n in _VECTORS]),
                 pack([zero + 1.0] + [A["v_" + n] for n in _VECTORS]), [red], name="adamw_vectors")
    offs = [0]
    for s in sizes:
        offs.append(offs[-1] + s)
    unpack = lambda a, i, n: a.reshape(-1)[offs[i + 1]:offs[i + 2]].reshape(A[n].shape)

    out = {"grad": {}, "delta": {}, "new_m": {}, "new_v": {}}
    for i, n in enumerate(_VECTORS):
        for kind, a in zip(out, upd):
            out[kind][n] = unpack(a, i, n)
    for n, mine, other in zip(_MATRICES, part, sib):
        res = _adamw(A[n][0], A["m_" + n][0], A["v_" + n][0], [mine, other], name="adamw_" + n)
        for kind, a in zip(out, res):
            out[kind][n] = a[None]
    return (red[0, 0], dx[None], *[out[k][n] for k in out for n in _WEIGHTS])
```

```python
import functools

import jax
import jax.numpy as jnp
from jax import lax
from jax.experimental import pallas as pl
from jax.experimental.pallas import tpu as pltpu

F32 = jnp.float32
BF16 = jnp.bfloat16
LANE = 128
HEAD = 64
RMS_EPS = 1e-6
GN_EPS = 64e-5
ADAM_LR, ADAM_B1, ADAM_B2, ADAM_EPS, ADAM_WD, ADAM_STEP = 0.001, 0.9, 0.999, 1e-08, 0.01, 10
VMEM_LIMIT = 56 * 1024 * 1024
MM_TILE_BYTES = 40 * 1024 * 1024
NEG = -1e30
MESH = pl.DeviceIdType.MESH
N_CHIPS = 4
N_DEV = 8


def _rup(n, m):
    return -(-n // m) * m


def _pick(n, cands):
    for c in cands:
        if n % c == 0:
            return c
    return n


def _cparams(*sem):
    return pltpu.CompilerParams(dimension_semantics=sem, vmem_limit_bytes=VMEM_LIMIT)


def _mm(a, b, *, ta=False, tb=False, add=None, out_dtype=F32, name):
    M, K = (a.shape[1], a.shape[0]) if ta else a.shape
    N = b.shape[0] if tb else b.shape[1]
    tn = _pick(N, (512, 640, 256, 128))
    fits = lambda m, t: 2 * (m * t * a.dtype.itemsize + t * tn * b.dtype.itemsize + m * tn * 8) <= MM_TILE_BYTES
    tm, tk = next((m, t) for t in (K, 2048, 1024, 512, 640, 256, 128) for m in (1024, 512, 256, 128, M)
                  if K % t == 0 and M % m == 0 and fits(m, t))
    nk = K // tk
    dn = (((0 if ta else 1,), (1 if tb else 0,)), ((), ()))

    def body(*refs):
        if add is None:
            a_ref, b_ref, o_ref, acc = refs
        else:
            a_ref, b_ref, add_ref, o_ref, acc = refs
        ks = pl.program_id(2)
        part = lax.dot_general(a_ref[...].astype(BF16), b_ref[...].astype(BF16), dn, preferred_element_type=F32)
        if nk > 1:
            @pl.when(ks == 0)
            def _():
                acc[...] = jnp.zeros_like(acc)

            acc[...] += part

        @pl.when(ks == nk - 1)
        def _():
            res = acc[...] if nk > 1 else part
            if add is not None:
                res = res + add_ref[...].astype(F32)
            o_ref[...] = res.astype(out_dtype)

    a_spec = pl.BlockSpec((tk, tm), lambda i, j, k: (k, i)) if ta else pl.BlockSpec((tm, tk), lambda i, j, k: (i, k))
    b_spec = pl.BlockSpec((tn, tk), lambda i, j, k: (j, k)) if tb else pl.BlockSpec((tk, tn), lambda i, j, k: (k, j))
    o_spec = pl.BlockSpec((tm, tn), lambda i, j, k: (i, j))
    ins, specs = [a, b], [a_spec, b_spec]
    if add is not None:
        ins.append(add)
        specs.append(o_spec)
    return pl.pallas_call(
        body, name=name, grid=(M // tm, N // tn, nk), in_specs=specs, out_specs=o_spec,
        out_shape=jax.ShapeDtypeStruct((M, N), out_dtype),
        scratch_shapes=[pltpu.VMEM((tm, tn) if nk > 1 else (8, LANE), F32)],
        compiler_params=_cparams("parallel", "parallel", "arbitrary"),
    )(*ins)


def _rowwise(fn, rows, consts, outs, accs=(), *, tile, name, prev=(), nxt=()):
    paired = [not isinstance(r, tuple) and r.ndim == 3 for r in rows]
    rows = [r if isinstance(r, tuple) else (r, r.shape[-1], 0) for r in rows]
    T = rows[0][0].shape[-2]
    tile = min(tile, T)
    n = T // tile
    sub = 8
    nr, npv, nnx, ncst, no, na = len(rows), len(prev), len(nxt), len(consts), len(outs), len(accs)
    out_paired = [len(o) == 3 for o in outs]

    def body(*refs):
        i = pl.program_id(0)
        it = iter(refs)
        rv = [next(it) for _ in range(nr)]
        rv = [jnp.concatenate([r[g] for g in range(r.shape[0])], axis=1) if pr else r[...] for r, pr in zip(rv, paired)]
        pv = [jnp.where(i > 0, next(it)[sub - 1:sub, :], 0.0) for _ in range(npv)]
        nv = [jnp.where(i < n - 1, next(it)[0:1, :], 0.0) for _ in range(nnx)]
        cv = [next(it)[...] for _ in range(ncst)]
        o_refs = [next(it) for _ in range(no)]
        a_refs = [next(it) for _ in range(na)]
        res = fn(*rv, *pv, *nv, *cv)
        if not isinstance(res, (tuple, list)):
            res = (res,)
        for r, o, pr in zip(o_refs, res[:no], out_paired):
            if pr:
                for g in range(r.shape[0]):
                    r[g] = o[:, g * LANE:(g + 1) * LANE].astype(r.dtype)
            else:
                r[...] = o.astype(r.dtype)
        if na:
            @pl.when(i == 0)
            def _():
                for r in a_refs:
                    r[...] = jnp.zeros_like(r)
            for r, o in zip(a_refs, res[no:]):
                r[...] += o.astype(F32)

    in_specs = [pl.BlockSpec((a.shape[0], tile, w), lambda i: (0, i, 0)) if pr else
                pl.BlockSpec((tile, w), functools.partial(lambda cb, i: (i, cb), cb)) for (a, w, cb), pr in zip(rows, paired)]
    in_specs += [pl.BlockSpec((sub, a.shape[1]), lambda i: (jnp.maximum(i * (tile // sub) - 1, 0), 0)) for a in prev]
    in_specs += [pl.BlockSpec((sub, a.shape[1]), lambda i: (jnp.minimum((i + 1) * (tile // sub), T // sub - 1), 0)) for a in nxt]
    in_specs += [pl.BlockSpec(c.shape, lambda i: (0, 0)) for c in consts]
    out_specs = [pl.BlockSpec((o[0] // LANE, tile, LANE), lambda i: (0, i, 0)) if pr else
                 pl.BlockSpec((tile, o[0]), lambda i: (i, 0)) for o, pr in zip(outs, out_paired)]
    out_specs += [pl.BlockSpec(s, lambda i: (0, 0)) for s in accs]
    out_shape = [jax.ShapeDtypeStruct((o[0] // LANE, T, LANE) if pr else (T, o[0]), o[1]) for o, pr in zip(outs, out_paired)]
    out_shape += [jax.ShapeDtypeStruct(s, F32) for s in accs]
    res = pl.pallas_call(
        body, name=name, grid=(n,), in_specs=in_specs, out_specs=out_specs, out_shape=out_shape,
        compiler_params=_cparams("arbitrary"),
    )(*[r[0] for r in rows], *prev, *nxt, *consts)
    return res


@jax.custom_vjp
def _bdot(a, b):
    return jnp.dot(a.astype(BF16), b.astype(BF16), preferred_element_type=F32)


def _bdot_fwd(a, b):
    return _bdot(a, b), (a.astype(BF16), b.astype(BF16))


def _bdot_bwd(res, ct):
    a, b = res
    c = ct.astype(BF16)
    return (lax.dot_general(c, b, (((1,), (1,)), ((), ())), preferred_element_type=F32),
            lax.dot_general(a, c, (((0,), (0,)), ((), ())), preferred_element_type=F32))


_bdot.defvjp(_bdot_fwd, _bdot_bwd)


def _split3(x):
    hi = x.astype(BF16)
    r1 = x - hi.astype(F32)
    mid = r1.astype(BF16)
    return hi, mid, (r1 - mid.astype(F32)).astype(BF16)


def _dot_exact(a, b):
    return sum(jnp.dot(p, b, preferred_element_type=F32) for p in _split3(a))


@jax.custom_vjp
def _head_sums(x, seg, segt):
    return _dot_exact(_dot_exact(x, seg), segt)


def _head_sums_fwd(x, seg, segt):
    return _head_sums(x, seg, segt), (seg, segt)


def _head_sums_bwd(res, ct):
    seg, segt = res
    return _head_sums(ct, seg, segt), jnp.zeros_like(seg), jnp.zeros_like(segt)


_head_sums.defvjp(_head_sums_fwd, _head_sums_bwd)


def _rms(x, g, eps=RMS_EPS):
    return x * lax.rsqrt(jnp.mean(x * x, axis=-1, keepdims=True) + eps) * g


def _softplus(x):
    return jnp.maximum(x, 0.0) + jnp.log(1.0 + jnp.exp(-jnp.abs(x)))


def _sigmoid(x):
    return 1.0 / (1.0 + jnp.exp(-x))


def _seg_mats(width):
    h = lax.broadcasted_iota(jnp.int32, (width, LANE), 0) // HEAD
    j = lax.broadcasted_iota(jnp.int32, (width, LANE), 1)
    seg = (h == j).astype(BF16)
    return seg, seg.T


def _prep_rwkv(um, w0, w2, a0, a2, g2, k_k, k_a, seg, segt, *, dims):
    W, DLp, ALp, GLp = dims
    r, k, v = um[:, :W], um[:, W:2 * W], um[:, 2 * W:3 * W]
    o = 3 * W
    xw, xa, xg = um[:, o:o + DLp], um[:, o + DLp:o + DLp + ALp], um[:, o + DLp + ALp:o + DLp + ALp + GLp]
    w_log = -_softplus(-(w0 + _bdot(jnp.tanh(xw), w2))) - 0.5
    decay = jnp.exp(-jnp.exp(w_log))
    a = _sigmoid(a0 + _bdot(xa, a2))
    g = _bdot(_sigmoid(xg), g2)
    kk = k * k_k
    nrm = jnp.sqrt(_head_sums(kk * kk, seg, segt))
    kk = kk / jnp.maximum(nrm, 1e-12)
    k2 = k * (1.0 + (a - 1.0) * k_a)
    return r, decay, k2, v, kk, kk * a, g


def _shift_mix(u, uprev, mu):
    first = lax.broadcasted_iota(jnp.int32, u.shape, 0) == 0
    sh = jnp.where(first, uprev, pltpu.roll(u, 1, 0))
    return u + (sh - u) * mu, sh


def _post_rwkv(y, r, k2, v, g, lnx_g, lnx_b, r_k, seg, segt):
    inv = 1.0 / HEAD
    mean = _head_sums(y, seg, segt) * inv
    yc = y - mean
    var = _head_sums(yc * yc, seg, segt) * inv
    yn = yc * lax.rsqrt(var + GN_EPS) * lnx_g + lnx_b
    bonus = _head_sums(r * k2 * r_k, seg, segt) * v
    return (yn + bonus) * g


def _prep_fox(uf, qg, kg, fb, seg, segt, *, dims):
    FW, FHp = dims
    q, k, v, f = uf[:, :FW], uf[:, FW:2 * FW], uf[:, 2 * FW:3 * FW], uf[:, 3 * FW:3 * FW + FHp]
    inv = 1.0 / HEAD
    qn = q * lax.rsqrt(_head_sums(q * q, seg, segt) * inv + RMS_EPS) * qg * (HEAD ** -0.5)
    kn = k * lax.rsqrt(_head_sums(k * k, seg, segt) * inv + RMS_EPS) * kg
    return qn, kn, v, -_softplus(-(f + fb))


def _tail(h2, pe, z, png, pgb):
    return h2 + _sigmoid(z + pgb) * _rms(pe, png)


def _swiglu(gt, up):
    return gt * _sigmoid(gt) * up


def _cumsum(x, *, reverse, name):
    T, C = x.shape
    tc = _pick(T, (256, 128))
    n = T // tc
    i0 = lax.broadcasted_iota(jnp.int32, (tc, tc), 0)
    i1 = lax.broadcasted_iota(jnp.int32, (tc, tc), 1)
    tri = ((i0 <= i1) if reverse else (i0 >= i1)).astype(BF16)

    def body(x_ref, tri_ref, o_ref, carry):
        i = pl.program_id(0)

        @pl.when(i == 0)
        def _():
            carry[...] = jnp.zeros_like(carry)

        v = x_ref[...]
        hi = v.astype(BF16)
        r1 = v - hi.astype(F32)
        mid = r1.astype(BF16)
        lo = (r1 - mid.astype(F32)).astype(BF16)
        t = tri_ref[...]
        d = lambda p: jnp.dot(t, p, preferred_element_type=F32)
        c = d(hi) + d(mid) + d(lo) + carry[0:1, :]
        o_ref[...] = c
        edge = c[0:1, :] if reverse else c[tc - 1:tc, :]
        carry[...] = jnp.broadcast_to(edge, carry.shape)

    blk = pl.BlockSpec((tc, C), (lambda i: (n - 1 - i, 0)) if reverse else (lambda i: (i, 0)))
    return pl.pallas_call(
        body, name=name, grid=(n,), in_specs=[blk, pl.BlockSpec((tc, tc), lambda i: (0, 0))], out_specs=blk,
        out_shape=jax.ShapeDtypeStruct((T, C), F32), scratch_shapes=[pltpu.VMEM((8, C), F32)],
        compiler_params=_cparams("arbitrary"),
    )(x, tri)


BWD_PAIRS_PER_TRIP = 8
FWD_STEPS_PER_TRIP = 8
RECOMPUTE_STEPS_PER_TRIP = 8


def _steps(n, per_trip, step):
    def trip(i, carry):
        for j in range(per_trip):
            carry = step(i * per_trip + j, carry)
        return carry

    lax.fori_loop(0, n // per_trip, trip, 0)


def _col(tile, lane, t):
    return jnp.sum(jnp.where(lane == t, tile, 0.0), axis=1, keepdims=True)


def _halves(x, left):
    a = jnp.sum(jnp.where(left, x, 0.0), axis=1, keepdims=True)
    b = jnp.sum(jnp.where(left, 0.0, x), axis=1, keepdims=True)
    return a, b, jnp.where(left, a, b)


def _pair_col(ref, p, lane, left, t):
    return jnp.where(left, _col(ref[2 * p], lane, t), _col(ref[2 * p + 1], lane, t))


def _scan_fwd(r, w, k, kk, kka, vT, *, hg, tc, comm=None):
    H, N, T = vT.shape
    nc = T // tc
    hp = hg // 2
    L = 2 * N

    def body(r_ref, w_ref, k_ref, kk_ref, kka_ref, vT_ref, yT_ref, chk_ref, s_ref):
        @pl.when(pl.program_id(1) == 0)
        def _():
            s_ref[...] = jnp.zeros_like(s_ref)

        chk_ref[:, 0] = s_ref[...]
        yT_ref[...] = jnp.zeros_like(yT_ref)
        lane = lax.broadcasted_iota(jnp.int32, (N, tc), 1)
        left = lax.broadcasted_iota(jnp.int32, (N, L), 1) < N

        def emit_y(S, p, t):
            ya, yb, _ = _halves(S * r_ref[p, pl.ds(jnp.maximum(t, 0), 1), :], left)
            yT_ref[2 * p] = jnp.where(lane == t, ya, yT_ref[2 * p])
            yT_ref[2 * p + 1] = jnp.where(lane == t, yb, yT_ref[2 * p + 1])

        def step(t, carry):
            for p in range(hp):
                row = lambda ref: ref[p, pl.ds(t, 1), :]
                S = s_ref[p]
                emit_y(S, p, t - 1)
                sa = _halves(S * row(kk_ref), left)[2]
                s_ref[p] = S * row(w_ref) - sa * row(kka_ref) + _pair_col(vT_ref, p, lane, left, t) * row(k_ref)
            return carry

        _steps(tc, FWD_STEPS_PER_TRIP, step)
        for p in range(hp):
            emit_y(s_ref[p], p, tc - 1)

    rows = pl.BlockSpec((hp, tc, L), lambda g, c: (g, c, 0))
    cols = pl.BlockSpec((hg, N, tc), lambda g, c: (g, 0, c))
    return _call_carrying(
        body, "scan_fwd", (H // hg, nc), [rows] * 5 + [cols],
        [cols, pl.BlockSpec((hp, 1, N, L), lambda g, c: (g, c, 0, 0))],
        [jax.ShapeDtypeStruct((H, N, T), F32), jax.ShapeDtypeStruct((H // 2, nc, N, L), F32)],
        [pltpu.VMEM((hp, N, L), F32)], [r, w, k, kk, kka, vT], comm)


def _scan_bwd(r, w, k, kk, kka, vT, dyT, chk, *, hg, tc, comm=None):
    H, N, T = vT.shape
    nc = T // tc
    hp = hg // 2
    L = 2 * N

    def body(r_ref, w_ref, k_ref, kk_ref, kka_ref, vT_ref, dyT_ref, chk_ref,
             dr_ref, dw_ref, dk_ref, dkk_ref, dkka_ref, dvT_ref, sp_ref, ds_ref):
        @pl.when(pl.program_id(1) == 0)
        def _():
            ds_ref[...] = jnp.zeros_like(ds_ref)

        dvT_ref[...] = jnp.zeros_like(dvT_ref)
        lane = lax.broadcasted_iota(jnp.int32, (N, tc), 1)
        left = lax.broadcasted_iota(jnp.int32, (N, L), 1) < N
        left_row = lax.broadcasted_iota(jnp.int32, (1, L), 1) < N

        halves = _halves
        pair_col = lambda ref, p, t: _pair_col(ref, p, lane, left, t)
        for p in range(hp):
            sp_ref[p, 0] = chk_ref[p, 0]

        def fstep(t, carry):
            for p in range(hp):
                row = lambda ref: ref[p, pl.ds(t, 1), :]
                S = sp_ref[p, t]
                sa = halves(S * row(kk_ref), left)[2]
                sp_ref[p, t + 1] = S * row(w_ref) - sa * row(kka_ref) + pair_col(vT_ref, p, t) * row(k_ref)
            return carry

        _steps(tc, RECOMPUTE_STEPS_PER_TRIP, fstep)

        def bstep(p0, i, carry):
            t = tc - 1 - i
            for p in range(p0, min(p0 + BWD_PAIRS_PER_TRIP, hp)):
                row = lambda ref: ref[p, pl.ds(t, 1), :]
                rr, wr, kr, kkr, kkar = row(r_ref), row(w_ref), row(k_ref), row(kk_ref), row(kka_ref)
                Sp = sp_ref[p, t]
                Sn = sp_ref[p, t + 1]
                dycol, vcol = pair_col(dyT_ref, p, t), pair_col(vT_ref, p, t)
                dS = ds_ref[p]
                dSn = dS + dycol * rr
                dsa = halves(dS * kkar, left)[2] + dycol * halves(rr * kkar, left_row)[2]
                dr_ref[p, pl.ds(t, 1), :] = jnp.sum(Sn * dycol, axis=0, keepdims=True)
                sa = halves(Sp * kkr, left)[2]
                dw_ref[p, pl.ds(t, 1), :] = jnp.sum(dSn * Sp, axis=0, keepdims=True)
                dkka_ref[p, pl.ds(t, 1), :] = -jnp.sum(dSn * sa, axis=0, keepdims=True)
                dva, dvb, _ = halves(dSn * kr, left)
                dk_ref[p, pl.ds(t, 1), :] = jnp.sum(dSn * vcol, axis=0, keepdims=True)
                dkk_ref[p, pl.ds(t, 1), :] = -jnp.sum(Sp * dsa, axis=0, keepdims=True)
                ds_ref[p] = dSn * wr - dsa * kkr
                dvT_ref[2 * p] = jnp.where(lane == t, dva, dvT_ref[2 * p])
                dvT_ref[2 * p + 1] = jnp.where(lane == t, dvb, dvT_ref[2 * p + 1])
            return carry

        for p0 in range(0, hp, BWD_PAIRS_PER_TRIP):
            lax.fori_loop(0, tc, functools.partial(bstep, p0), 0)

    rows = pl.BlockSpec((hp, tc, L), lambda g, c: (g, nc - 1 - c, 0))
    cols = pl.BlockSpec((hg, N, tc), lambda g, c: (g, 0, nc - 1 - c))
    return _call_carrying(
        body, "scan_bwd", (H // hg, nc),
        [rows] * 5 + [cols, cols, pl.BlockSpec((hp, 1, N, L), lambda g, c: (g, nc - 1 - c, 0, 0))], [rows] * 5 + [cols],
        [jax.ShapeDtypeStruct((H // 2, T, L), F32)] * 5 + [jax.ShapeDtypeStruct((H, N, T), F32)],
        [pltpu.VMEM((hp, tc + 1, N, L), F32), pltpu.VMEM((hp, N, L), F32)], [r, w, k, kk, kka, vT, dyT, chk], comm)


_NT = (((1,), (1,)), ((), ()))
_TN = (((0,), (0,)), ((), ()))


def _scores(q, k, cc, cr, masked):
    s = lax.dot_general(q, k, _NT, preferred_element_type=F32) + cc - cr
    if masked:
        tb = s.shape[0]
        keep = lax.broadcasted_iota(jnp.int32, (tb, tb), 0) >= lax.broadcasted_iota(jnp.int32, (tb, tb), 1)
        s = jnp.where(keep, s, NEG)
    return s


def _attn_specs(T, N, tb):
    blk = pl.BlockSpec((1, tb, N), lambda h, i: (h, i, 0))
    whole = pl.BlockSpec((1, T, N), lambda h, i: (h, 0, 0))
    col = pl.BlockSpec((1, tb, 1), lambda h, i: (h, i, 0))
    wcol = pl.BlockSpec((1, T, 1), lambda h, i: (h, 0, 0))
    row = pl.BlockSpec((1, 1, tb), lambda h, i: (h, 0, i))
    wrow = pl.BlockSpec((1, 1, T), lambda h, i: (h, 0, 0))
    return blk, whole, col, wcol, row, wrow


def _call_carrying(body, name, grid, in_specs, out_specs, out_shape, scratch, args, comm):
    n_out = len(out_specs)
    if comm is not None:
        n = len(comm[0])
        body = _carrying(body, len(in_specs), n_out, grid, comm)
        in_specs, out_specs = in_specs + [_HBM] * n, out_specs + [_HBM] * n
        out_shape, scratch, args = out_shape + _exchange_shapes(*comm), scratch + _exchange_sems(n), args + list(comm[0])
    res = pl.pallas_call(
        body, name=name, grid=grid, in_specs=in_specs, out_specs=out_specs, out_shape=out_shape,
        scratch_shapes=scratch, compiler_params=_cparams(*["arbitrary"] * len(grid)),
    )(*args)
    return res[:n_out], res[n_out:]


def _attn_fwd(q, k, v, ccol, crow, *, tb, comm=None):
    H, T, N = q.shape

    def body(q_ref, k_ref, v_ref, cc_ref, cr_ref, o_ref, lse_ref, m_s, l_s, acc_s):
        qi = pl.program_id(1)
        m_s[...] = jnp.full_like(m_s, NEG)
        l_s[...] = jnp.zeros_like(l_s)
        acc_s[...] = jnp.zeros_like(acc_s)
        q_, cc = q_ref[0], cc_ref[0]

        def block(j, masked):
            at = pl.ds(pl.multiple_of(j * tb, tb), tb)
            s = _scores(q_, k_ref[0, at, :], cc, cr_ref[0, :, at], masked)
            m_new = jnp.maximum(m_s[...], jnp.max(s, axis=1, keepdims=True))
            p = jnp.exp(s - m_new)
            alpha = jnp.exp(m_s[...] - m_new)
            l_s[...] = alpha * l_s[...] + jnp.sum(p, axis=1, keepdims=True)
            acc_s[...] = alpha * acc_s[...] + jnp.dot(p.astype(BF16), v_ref[0, at, :], preferred_element_type=F32)
            m_s[...] = m_new

        def below(j, carry):
            block(j, False)
            return carry

        lax.fori_loop(0, qi, below, 0)
        block(qi, True)
        o_ref[0] = acc_s[...] / l_s[...]
        lse_ref[0] = m_s[...] + jnp.log(l_s[...])

    blk, whole, col, wcol, row, wrow = _attn_specs(T, N, tb)
    return _call_carrying(
        body, "fox_fwd", (H, T // tb), [blk, whole, whole, col, wrow], [blk, col],
        [jax.ShapeDtypeStruct((H, T, N), F32), jax.ShapeDtypeStruct((H, T, 1), F32)],
        [pltpu.VMEM((tb, 1), F32), pltpu.VMEM((tb, 1), F32), pltpu.VMEM((tb, N), F32)],
        [q, k, v, ccol, crow], comm)


def _attn_bwd(q, k, v, ccol, crow, o, lse, do, *, tb, comm=None):
    H, T, N = q.shape
    nb = T // tb

    def body(q_ref, k_ref, v_ref, cc_ref, cr_ref, o_ref, lse_ref, do_ref,
             dq_ref, dcq_ref, dk_ref, dv_ref, dck_ref, dq_s, dcq_s, dk_s, dv_s, dck_s):
        qi = pl.program_id(1)

        @pl.when(qi == 0)
        def _():
            dk_s[...] = jnp.zeros_like(dk_s)
            dv_s[...] = jnp.zeros_like(dv_s)
            dck_s[...] = jnp.zeros_like(dck_s)

        dq_s[...] = jnp.zeros_like(dq_s)
        dcq_s[...] = jnp.zeros_like(dcq_s)
        q_, cc, lse_, do_ = q_ref[0], cc_ref[0], lse_ref[0], do_ref[0]
        delta = jnp.sum(do_ * o_ref[0], axis=1, keepdims=True)
        dob = do_.astype(BF16)

        def block(j, masked):
            at = pl.ds(pl.multiple_of(j * tb, tb), tb)
            kb = k_ref[0, at, :]
            p = jnp.exp(_scores(q_, kb, cc, cr_ref[0, :, at], masked) - lse_)
            dp = lax.dot_general(dob, v_ref[0, at, :], _NT, preferred_element_type=F32)
            ds = p * (dp - delta)
            dsb = ds.astype(BF16)
            dq_s[...] += jnp.dot(dsb, kb, preferred_element_type=F32)
            dcq_s[...] += jnp.sum(ds, axis=1, keepdims=True)
            dv_s[at, :] += lax.dot_general(p.astype(BF16), dob, _TN, preferred_element_type=F32)
            dk_s[at, :] += lax.dot_general(dsb, q_, _TN, preferred_element_type=F32)
            dck_s[:, at] += jnp.sum(ds, axis=0, keepdims=True)

        def below(j, carry):
            block(j, False)
            return carry

        lax.fori_loop(0, qi, below, 0)
        block(qi, True)
        dq_ref[0] = dq_s[...]
        dcq_ref[0] = dcq_s[...]

        @pl.when(qi == nb - 1)
        def _():
            dk_ref[0] = dk_s[...]
            dv_ref[0] = dv_s[...]
            dck_ref[0] = dck_s[...]

    blk, whole, col, wcol, row, wrow = _attn_specs(T, N, tb)
    return _call_carrying(
        body, "fox_bwd", (H, nb), [blk, whole, whole, col, wrow, blk, col, blk], [blk, col, whole, whole, wrow],
        [jax.ShapeDtypeStruct((H, T, N), F32), jax.ShapeDtypeStruct((H, T, 1), F32), jax.ShapeDtypeStruct((H, T, N), F32),
         jax.ShapeDtypeStruct((H, T, N), F32), jax.ShapeDtypeStruct((H, 1, T), F32)],
        [pltpu.VMEM((tb, N), F32), pltpu.VMEM((tb, 1), F32), pltpu.VMEM((T, N), F32), pltpu.VMEM((T, N), F32),
         pltpu.VMEM((1, T), F32)],
        [q, k, v, ccol, crow, o, lse, do], comm)


def _heads(x):
    T = x.shape[0]
    return x.reshape(T, -1, HEAD).transpose(1, 0, 2)


def _headsT(x):
    T = x.shape[0]
    return x.reshape(T, -1, HEAD).transpose(1, 2, 0)


def _unheads(x):
    return x.transpose(1, 0, 2).reshape(x.shape[1], -1)


def _unheadsT(x):
    return x.transpose(2, 0, 1).reshape(x.shape[2], -1)


def _padc(x, n):
    return jnp.pad(x, ((0, 0), (0, n - x.shape[1])))


def _padr(x, n):
    return jnp.pad(x, ((0, n - x.shape[0]), (0, 0)))


class _Dims:
    def __init__(self, W, DL, AL, GL, FH):
        self.W, self.DL, self.AL, self.GL, self.FH = W, DL, AL, GL, FH
        self.DLp, self.ALp, self.GLp, self.FHp = _rup(DL, LANE), _rup(AL, LANE), _rup(GL, LANE), _rup(FH, LANE)
        self.FW = FH * HEAD
        self.RC = 3 * W + DL + AL + GL
        self.RP = 3 * W + self.DLp + self.ALp + self.GLp
        self.FC = 3 * self.FW + FH
        self.FP = 3 * self.FW + self.FHp

    def pad_r(self, a):
        W, o = self.W, 3 * self.W
        return jnp.concatenate([a[:, :o], _padc(a[:, o:o + self.DL], self.DLp),
                                _padc(a[:, o + self.DL:o + self.DL + self.AL], self.ALp),
                                _padc(a[:, o + self.DL + self.AL:self.RC], self.GLp)], axis=1)

    def unpad_r(self, a):
        o = 3 * self.W
        return jnp.concatenate([a[:, :o], a[:, o:o + self.DL], a[:, o + self.DLp:o + self.DLp + self.AL],
                                a[:, o + self.DLp + self.ALp:o + self.DLp + self.ALp + self.GL]], axis=1)

    def pad_f(self, a):
        return _padc(a, self.FP)

    def unpad_f(self, a):
        return a[:, :self.FC]


def _local_step(x, p, tgt, Wt, vec, d, late_shards=None):
    Wt = dict(Wt)
    T, D = x.shape
    W, FW = d.W, d.FW
    H = W // HEAD
    seg, segt = _seg_mats(W)
    segf, segft = _seg_mats(FW)
    T1 = 256
    rk_flat = vec["r_k"].reshape(1, W)
    mu = d.pad_r(vec["shift_mu"])
    qg = jnp.tile(vec["q_norm_g"], (1, d.FH))
    kg = jnp.tile(vec["k_norm_g"], (1, d.FH))
    fb = _padc(vec["fgate_b"], d.FHp)
    pdims = (W, d.DLp, d.ALp, d.GLp)
    fdims = (FW, d.FHp)

    (xn,) = _rowwise(lambda x_, g_: _rms(x_, g_), [x], [vec["attn_norm_g"]], [(D, BF16)], tile=T1, name="norm_attn")
    u_r = _mm(xn, Wt["w_in_r"], name="mm_in_r")
    u_f = _mm(xn, Wt["w_in_f"], name="mm_in_f")

    prep_consts = [mu, vec["w0"], Wt["w2"].astype(F32), vec["a0"], Wt["a2"].astype(F32), Wt["g2"].astype(F32), vec["k_k"], vec["k_a"], seg, segt]

    def prep_fwd(u_, up_, mu_, *cs):
        um, _ = _shift_mix(u_, up_, mu_)
        return _prep_rwkv(um, *cs, dims=pdims)

    pw, tw = (W, F32, "pairs"), (W, F32)
    r, dec, k2, v, kk, kka, g = _rowwise(prep_fwd, [u_r], prep_consts, [pw, pw, pw, tw, pw, pw, tw], tile=128,
                                         name="rwkv_prep", prev=[u_r])
    hg, tc = min(16, H), 128
    (yT, chk), gathered = _scan_fwd(r, dec, k2, kk, kka, _headsT(v), hg=hg, tc=tc,
                                    comm=late_shards and (late_shards, True))
    Wt.update({n: _whole(n, g) for n, g in zip(_LATE, gathered)})
    y = _unheadsT(yT)
    post_consts = [vec["lnx_g"], vec["lnx_b"], rk_flat, seg, segt]
    (y_r,) = _rowwise(_post_rwkv, [y, r, k2, v, g], post_consts, [(W, BF16)], tile=T1, name="rwkv_post")

    fox_consts = [qg, kg, fb, segf, segft]
    qn, kn, vf, logf = _rowwise(functools.partial(_prep_fox, dims=fdims), [u_f], fox_consts,
                                [(FW, BF16), (FW, BF16), (FW, BF16), (d.FHp, F32)], tile=T1, name="fox_prep")
    c = _cumsum(logf, reverse=False, name="fox_cumsum")
    cT = c[:, :d.FH].T
    ccol, crow = cT[:, :, None], cT[:, None, :]
    tb = _pick(T, (1024, 512, 256, 128))
    qh, kh, vh = _heads(qn), _heads(kn), _heads(vf)
    (o, lse), _ = _attn_fwd(qh, kh, vh, ccol, crow, tb=tb)
    y_f = _unheads(o)

    ycat = jnp.concatenate([y_r, y_f.astype(BF16)], axis=1)
    h1 = _mm(ycat, Wt["w_out"], add=x, name="mm_out")
    (hn,) = _rowwise(lambda h_, g_: _rms(h_, g_), [h1], [vec["ffn_norm_g"]], [(D, BF16)], tile=T1, name="norm_ffn")
    gt = _mm(hn, Wt["w_gate"], name="mm_gate")
    up = _mm(hn, Wt["w_up"], name="mm_up")
    (act,) = _rowwise(_swiglu, [gt, up], [], [(gt.shape[1], BF16)], tile=T1, name="swiglu")
    h2 = _mm(act, Wt["w_down"], add=h1, name="mm_down")
    (hg_,) = _rowwise(lambda h_, g_: _rms(h_, g_), [h2], [vec["ple_gate_norm_g"]], [(D, BF16)], tile=T1, name="norm_gate")
    pe = _mm(p, Wt["ple_proj"], name="mm_ple")
    z = _mm(hg_, Wt["ple_gate_w"], name="mm_pgate")

    def tail(h2_, pe_, z_, tg_, png_, pgb_):
        h3, f = jax.vjp(_tail, h2_, pe_, z_, png_, pgb_)
        err = h3 - tg_
        dh3 = err * (1.0 / D)
        lt = 0.5 * jnp.sum(jnp.sum(err * err, axis=1, keepdims=True) * (1.0 / D), axis=0, keepdims=True)
        dh2_, dpe_, dz_, dpng_, dpgb_ = f(dh3)
        return dh2_, dpe_, dz_, jnp.broadcast_to(lt, (1, LANE)), dpng_, dpgb_

    dh3, dpe, dz, loss, g_png, g_pgb = _rowwise(
        tail, [h2, pe, z, tgt], [vec["ple_norm_g"], vec["ple_gate_b"]], [(D, F32), (D, BF16), (D, BF16)],
        [(1, LANE), (1, D), (1, D)], tile=T1, name="tail")
    G = {}
    gv = {"ple_norm_g": g_png, "ple_gate_b": g_pgb}
    G["ple_gate_w"] = _mm(hg_, dz, ta=True, out_dtype=BF16, name="gw_pgate")
    G["ple_proj"] = _mm(p, dpe, ta=True, out_dtype=BF16, name="gw_ple")
    d_hg = _mm(dz, Wt["ple_gate_w"], tb=True, name="mmb_pgate")

    def norm_bwd(h_, dres_, dn_, g_):
        _, f = jax.vjp(_rms, h_, g_)
        dh_, dg_ = f(dn_)
        return dres_ + dh_, dg_

    dh2, gv["ple_gate_norm_g"] = _rowwise(norm_bwd, [h2, dh3, d_hg], [vec["ple_gate_norm_g"]], [(D, F32)], [(1, D)],
                                          tile=T1, name="norm_gate_bwd")
    G["w_down"] = _mm(act, dh2, ta=True, out_dtype=BF16, name="gw_down")
    d_act = _mm(dh2, Wt["w_down"], tb=True, name="mmb_down")

    def swiglu_bwd(gt_, up_, da_):
        _, f = jax.vjp(_swiglu, gt_, up_)
        return f(da_)

    d_gt, d_up = _rowwise(swiglu_bwd, [gt, up, d_act], [], [(gt.shape[1], BF16)] * 2, tile=T1, name="swiglu_bwd")
    G["w_gate"] = _mm(hn, d_gt, ta=True, out_dtype=BF16, name="gw_gate")
    G["w_up"] = _mm(hn, d_up, ta=True, out_dtype=BF16, name="gw_up")
    d_hn = _mm(d_gt, Wt["w_gate"], tb=True, name="mmb_gate")
    d_hn = _mm(d_up, Wt["w_up"], tb=True, add=d_hn, name="mmb_up")
    dh1, gv["ffn_norm_g"] = _rowwise(norm_bwd, [h1, dh2, d_hn], [vec["ffn_norm_g"]], [(D, F32)], [(1, D)],
                                     tile=T1, name="norm_ffn_bwd")
    G["w_out"] = _mm(ycat, dh1, ta=True, out_dtype=BF16, name="gw_out")
    d_ycat = _mm(dh1, Wt["w_out"], tb=True, name="mmb_out")

    def post_bwd(y_, r_, k2_, v_, g_, dy_, *cs):
        lg, lb, rk, sg, sgt = cs
        _, f = jax.vjp(lambda *a: _post_rwkv(*a, sg, sgt), y_, r_, k2_, v_, g_, lg, lb, rk)
        return f(dy_)

    dy, dr1, dk1, dv1, dg, gv["lnx_g"], gv["lnx_b"], g_rk = _rowwise(
        post_bwd, [y, r, k2, v, g, (d_ycat, W, 0)], post_consts, [(W, F32)] * 5, [(1, W)] * 3, tile=128, name="rwkv_post_bwd")
    gv["r_k"] = g_rk.reshape(H, HEAD)
    pieces = late_shards and ([_pieces(n, G[n]).astype(BF16) for n in _LATE], False)
    (dr, ddec, dk2, dkk, dkka, dvT), recv = _scan_bwd(r, dec, k2, kk, kka, _headsT(v), _headsT(dy), chk, hg=hg, tc=tc,
                                                      comm=pieces)
    recv = dict(zip(_LATE, recv))
    dv = _unheadsT(dvT)

    def prep_bwd(u_, dr_, dr1_, ddec_, dk2_, dk1_, dv_, dv1_, dkk_, dkka_, dg_, up_, mu_, *cs):
        um, sh = _shift_mix(u_, up_, mu_)
        cs_d, sg, sgt = cs[:7], cs[7], cs[8]
        _, f = jax.vjp(lambda um_, *c_: _prep_rwkv(um_, *c_, sg, sgt, dims=pdims), um, *cs_d)
        res = f((dr_ + dr1_, ddec_, dk2_ + dk1_, dv_ + dv1_, dkk_, dkka_, dg_))
        dum = res[0]
        dmu = jnp.sum(dum * (sh - u_), axis=0, keepdims=True)
        return (dum, dmu) + tuple(res[1:])

    LP = [Wt["w2"].shape, Wt["a2"].shape, Wt["g2"].shape]
    dum, g_mu, gv["w0"], g_w2, gv["a0"], g_a2, g_g2, gv["k_k"], gv["k_a"] = _rowwise(
        prep_bwd, [u_r, dr, dr1, ddec, dk2, dk1, dv, dv1, dkk, dkka, dg], prep_consts, [(d.RP, F32)],
        [(1, d.RP), (1, W), LP[0], (1, W), LP[1], LP[2], (1, W), (1, W)], tile=128, name="rwkv_prep_bwd", prev=[u_r])
    gv["shift_mu"] = d.unpad_r(g_mu)
    G["w2"], G["a2"], G["g2"] = g_w2, g_a2, g_g2
    (du_r,) = _rowwise(lambda a_, an_, mu_: a_ * (1.0 - mu_) + jnp.where(
        lax.broadcasted_iota(jnp.int32, a_.shape, 0) == a_.shape[0] - 1, an_, pltpu.roll(a_, a_.shape[0] - 1, 0)) * mu_,
        [dum], [mu], [(d.RP, BF16)], tile=T1, name="shift_bwd", nxt=[dum])

    do = _heads(d_ycat[:, W:])
    (dq, dcq, dk_, dv_, dck), _ = _attn_bwd(qh, kh, vh, ccol, crow, o, lse, do, tb=tb)
    dc = _padc((dcq[:, :, 0] - dck[:, 0, :]).T, d.FHp)
    dlogf = _cumsum(dc, reverse=True, name="fox_cumsum_bwd")

    def fox_bwd(uf_, dq_, dk__, dv__, dlf_, *cs):
        qg_, kg_, fb_, sg, sgt = cs
        _, f = jax.vjp(lambda uf__, a, b, c_: _prep_fox(uf__, a, b, c_, sg, sgt, dims=fdims), uf_, qg_, kg_, fb_)
        return f((dq_, dk__, dv__, dlf_))

    du_f, g_qg, g_kg, g_fb = _rowwise(fox_bwd, [u_f, _unheads(dq), _unheads(dk_), _unheads(dv_), dlogf], fox_consts,
                                      [(d.FP, BF16)], [(1, FW), (1, FW), (1, d.FHp)], tile=T1, name="fox_prep_bwd")
    gv["q_norm_g"] = g_qg.reshape(d.FH, HEAD).sum(0, keepdims=True)
    gv["k_norm_g"] = g_kg.reshape(d.FH, HEAD).sum(0, keepdims=True)
    gv["fgate_b"] = g_fb[:, :d.FH]

    G["w_in_r"] = _mm(xn, du_r, ta=True, out_dtype=BF16, name="gw_in_r")
    G["w_in_f"] = _mm(xn, du_f, ta=True, out_dtype=BF16, name="gw_in_f")
    d_xn = _mm(du_r, Wt["w_in_r"], tb=True, name="mmb_in_r")
    d_xn = _mm(du_f, Wt["w_in_f"], tb=True, add=d_xn, name="mmb_in_f")
    dx, gv["attn_norm_g"] = _rowwise(norm_bwd, [x, dh1, d_xn], [vec["attn_norm_g"]], [(D, F32)], [(1, D)],
                                     tile=T1, name="norm_attn_bwd")
    return loss, dx, G, gv, recv


_HBM = pl.BlockSpec(memory_space=pl.ANY)
_OTHER_CHIPS = ((0, 1), (1, 0), (1, 1))


def _flip(v, bit):
    return 1 - v if bit else v


def _exchange(arrs, *, gather, name):
    n = len(arrs)

    def body(*refs):
        copies = _exchange_copies(refs[:n], refs[n:2 * n], refs[2 * n:], gather)
        for cp in copies:
            cp.start()
        for cp in copies:
            cp.wait()

    return pl.pallas_call(
        body, name=name, in_specs=[_HBM] * n, out_specs=[_HBM] * n, out_shape=_exchange_shapes(arrs, gather),
        scratch_shapes=_exchange_sems(n),
    )(*arrs)


def _exchange_shapes(arrs, gather):
    return [jax.ShapeDtypeStruct(((N_CHIPS,) + a.shape) if gather else a.shape, a.dtype) for a in arrs]


def _exchange_sems(n):
    return [pltpu.SemaphoreType.DMA((3 * n,)), pltpu.SemaphoreType.DMA((3 * n,)), pltpu.SemaphoreType.DMA((n,))]


def _exchange_copies(ins, outs, sems, gather):
    send_sems, recv_sems, own_sems = sems
    x, y, c = lax.axis_index("x"), lax.axis_index("y"), lax.axis_index("c")
    me = 2 * x + y
    copies = []
    for a in range(len(ins)):
        copies.append(pltpu.make_async_copy(ins[a] if gather else ins[a].at[me], outs[a].at[me], own_sems.at[a]))
        for k, (dx, dy) in enumerate(_OTHER_CHIPS):
            px, py = _flip(x, dx), _flip(y, dy)
            copies.append(pltpu.make_async_remote_copy(
                src_ref=ins[a] if gather else ins[a].at[2 * px + py], dst_ref=outs[a].at[me],
                send_sem=send_sems.at[3 * a + k], recv_sem=recv_sems.at[3 * a + k],
                device_id=(px, py, c), device_id_type=MESH))
    return copies


def _carrying(body, n_in, n_out, grid, comm):
    arrs, gather = comm
    n = len(arrs)

    def wrapped(*refs):
        c_in = refs[n_in:n_in + n]
        c_out = refs[n_in + n + n_out:n_in + 2 * n + n_out]
        ids = [pl.program_id(a) for a in range(len(grid))]
        first = functools.reduce(jnp.logical_and, [i == 0 for i in ids])
        last = functools.reduce(jnp.logical_and, [i == g - 1 for i, g in zip(ids, grid)])

        @pl.when(first)
        def _():
            for cp in _exchange_copies(c_in, c_out, refs[-3:], gather):
                cp.start()

        body(*refs[:n_in], *refs[n_in + n:n_in + n + n_out], *refs[n_in + 2 * n + n_out:-3])

        @pl.when(last)
        def _():
            for cp in _exchange_copies(c_in, c_out, refs[-3:], gather):
                cp.wait()

    return wrapped


def _swap_cores(arrs, *, name):
    n = len(arrs)

    def body(*refs):
        ins, outs = refs[:n], refs[n:2 * n]
        send_sems, recv_sems = refs[2 * n:]
        peer = (lax.axis_index("x"), lax.axis_index("y"), 1 - lax.axis_index("c"))
        cps = [pltpu.make_async_remote_copy(src_ref=ins[a], dst_ref=outs[a], send_sem=send_sems.at[a],
                                            recv_sem=recv_sems.at[a], device_id=peer, device_id_type=MESH) for a in range(n)]
        for cp in cps:
            cp.start()
        for cp in cps:
            cp.wait()

    return pl.pallas_call(
        body, name=name, in_specs=[_HBM] * n, out_specs=[_HBM] * n,
        out_shape=[jax.ShapeDtypeStruct(a.shape, a.dtype) for a in arrs],
        scratch_shapes=[pltpu.SemaphoreType.DMA((n,)), pltpu.SemaphoreType.DMA((n,))],
    )(*arrs)


def _allreduce_small(pack, *, name):
    R, C = pack.shape

    def body(p_ref, o_ref, recv, send_sems, recv_sems):
        x, y, c = lax.axis_index("x"), lax.axis_index("y"), lax.axis_index("c")
        me = 4 * x + 2 * y + c
        recv[me] = p_ref[...]
        cps = []
        for k in range(1, N_DEV):
            peer = (_flip(x, k & 4), _flip(y, k & 2), _flip(c, k & 1))
            cp = pltpu.make_async_remote_copy(src_ref=p_ref, dst_ref=recv.at[me], send_sem=send_sems.at[k - 1],
                                              recv_sem=recv_sems.at[k - 1], device_id=peer, device_id_type=MESH)
            cp.start()
            cps.append(cp)
        for cp in cps:
            cp.wait()
        acc = recv[0]
        for s in range(1, N_DEV):
            acc = acc + recv[s]
        o_ref[...] = acc

    vm = pl.BlockSpec(memory_space=pltpu.VMEM)
    return pl.pallas_call(
        body, name=name, in_specs=[vm], out_specs=vm, out_shape=jax.ShapeDtypeStruct((R, C), F32),
        scratch_shapes=[pltpu.VMEM((N_DEV, R, C), F32), pltpu.SemaphoreType.DMA((N_DEV - 1,)), pltpu.SemaphoreType.DMA((N_DEV - 1,))],
    )(pack)


def _sum_slots(a, *, name):
    S, R, C = a.shape
    tr = _pick(R, (256, 128, 64, 32, 16, 8))

    def body(a_ref, o_ref):
        acc = a_ref[0].astype(F32)
        for s in range(1, S):
            acc = acc + a_ref[s].astype(F32)
        o_ref[...] = acc

    return pl.pallas_call(
        body, name=name, grid=(R // tr,), in_specs=[pl.BlockSpec((S, tr, C), lambda i: (0, i, 0))],
        out_specs=pl.BlockSpec((tr, C), lambda i: (i, 0)), out_shape=jax.ShapeDtypeStruct((R, C), F32),
        compiler_params=_cparams("parallel"),
    )(a)


def _adamw(w, m, v, gs, *, name):
    R, C = w.shape
    tile = _pick(R, (128, 96, 64, 32, 16, 8))

    def fn(w_, m_, v_, *g_):
        g = g_[0]
        for e in g_[1:]:
            g = g + e
        m2 = ADAM_B1 * m_ + (1.0 - ADAM_B1) * g
        v2 = ADAM_B2 * v_ + (1.0 - ADAM_B2) * jnp.square(g)
        m_hat = m2 / (1.0 - ADAM_B1 ** ADAM_STEP)
        v_hat = v2 / (1.0 - ADAM_B2 ** ADAM_STEP)
        delta = -ADAM_LR * (m_hat / (jnp.sqrt(v_hat) + ADAM_EPS) + ADAM_WD * w_)
        return g, delta, m2, v2

    return _rowwise(fn, [w, m, v, *gs], [], [(C, F32)] * 4, tile=tile, name=name)


_ARGS = "x, p, attn_norm_g, w_in, shift_mu, w0, w2, a0, a2, g2, k_k, k_a, r_k, lnx_g, lnx_b, q_norm_g, k_norm_g, fgate_b, w_out, ffn_norm_g, w_gate, w_up, w_down, ple_proj, ple_norm_g, ple_gate_norm_g, ple_gate_w, ple_gate_b, loss_target, m_attn_norm_g, m_w_in, m_shift_mu, m_w0, m_w2, m_a0, m_a2, m_g2, m_k_k, m_k_a, m_r_k, m_lnx_g, m_lnx_b, m_q_norm_g, m_k_norm_g, m_fgate_b, m_w_out, m_ffn_norm_g, m_w_gate, m_w_up, m_w_down, m_ple_proj, m_ple_norm_g, m_ple_gate_norm_g, m_ple_gate_w, m_ple_gate_b, v_attn_norm_g, v_w_in, v_shift_mu, v_w0, v_w2, v_a0, v_a2, v_g2, v_k_k, v_k_a, v_r_k, v_lnx_g, v_lnx_b, v_q_norm_g, v_k_norm_g, v_fgate_b, v_w_out, v_ffn_norm_g, v_w_gate, v_w_up, v_w_down, v_ple_proj, v_ple_norm_g, v_ple_gate_norm_g, v_ple_gate_w, v_ple_gate_b".split(", ")
_WEIGHTS = _ARGS[2:28]
_COL_SHARDED = ("w_in", "w2", "a2", "g2", "w_gate", "w_up", "ple_proj")
_ROW_SHARDED = ("w_out", "w_down", "ple_gate_w")
_MATRICES = _COL_SHARDED + _ROW_SHARDED
_EARLY = ("w_in", "w2", "a2", "g2")
_LATE = tuple(n for n in _MATRICES if n not in _EARLY)
_VECTORS = tuple(n for n in _WEIGHTS if n not in _MATRICES)


def _whole(name, g):
    if name in _COL_SHARDED:
        return g.transpose(1, 0, 2).reshape(g.shape[1], -1)
    return g.reshape(-1, g.shape[2])


def _pieces(name, a):
    if name in _COL_SHARDED:
        return a.reshape(a.shape[0], N_CHIPS, -1).transpose(1, 0, 2)
    return a.reshape(N_CHIPS, -1, a.shape[1])


def kernel(x, p, attn_norm_g, w_in, shift_mu, w0, w2, a0, a2, g2, k_k, k_a, r_k, lnx_g, lnx_b, q_norm_g, k_norm_g, fgate_b, w_out, ffn_norm_g, w_gate, w_up, w_down, ple_proj, ple_norm_g, ple_gate_norm_g, ple_gate_w, ple_gate_b, loss_target, m_attn_norm_g, m_w_in, m_shift_mu, m_w0, m_w2, m_a0, m_a2, m_g2, m_k_k, m_k_a, m_r_k, m_lnx_g, m_lnx_b, m_q_norm_g, m_k_norm_g, m_fgate_b, m_w_out, m_ffn_norm_g, m_w_gate, m_w_up, m_w_down, m_ple_proj, m_ple_norm_g, m_ple_gate_norm_g, m_ple_gate_w, m_ple_gate_b, v_attn_norm_g, v_w_in, v_shift_mu, v_w0, v_w2, v_a0, v_a2, v_g2, v_k_k, v_k_a, v_r_k, v_lnx_g, v_lnx_b, v_q_norm_g, v_k_norm_g, v_fgate_b, v_w_out, v_ffn_norm_g, v_w_gate, v_w_up, v_w_down, v_ple_proj, v_ple_norm_g, v_ple_gate_norm_g, v_ple_gate_w, v_ple_gate_b):
    A = dict(zip(_ARGS, (x, p, attn_norm_g, w_in, shift_mu, w0, w2, a0, a2, g2, k_k, k_a, r_k, lnx_g, lnx_b, q_norm_g, k_norm_g, fgate_b, w_out, ffn_norm_g, w_gate, w_up, w_down, ple_proj, ple_norm_g, ple_gate_norm_g, ple_gate_w, ple_gate_b, loss_target, m_attn_norm_g, m_w_in, m_shift_mu, m_w0, m_w2, m_a0, m_a2, m_g2, m_k_k, m_k_a, m_r_k, m_lnx_g, m_lnx_b, m_q_norm_g, m_k_norm_g, m_fgate_b, m_w_out, m_ffn_norm_g, m_w_gate, m_w_up, m_w_down, m_ple_proj, m_ple_norm_g, m_ple_gate_norm_g, m_ple_gate_w, m_ple_gate_b, v_attn_norm_g, v_w_in, v_shift_mu, v_w0, v_w2, v_a0, v_a2, v_g2, v_k_k, v_k_a, v_r_k, v_lnx_g, v_lnx_b, v_q_norm_g, v_k_norm_g, v_fgate_b, v_w_out, v_ffn_norm_g, v_w_gate, v_w_up, v_w_down, v_ple_proj, v_ple_norm_g, v_ple_gate_norm_g, v_ple_gate_w, v_ple_gate_b)))
    x, p, tgt = A["x"][0], A["p"][0, 0], A["loss_target"][0]
    d = _Dims(W=A["w0"].shape[-1], DL=A["w2"].shape[1], AL=A["a2"].shape[1], GL=A["g2"].shape[1], FH=A["fgate_b"].shape[-1])

    shard = lambda n: A[n][0].astype(BF16)
    gathered = _exchange([shard(n) for n in _EARLY], gather=True, name="gather_early")
    full = {n: _whole(n, g) for n, g in zip(_EARLY, gathered)}
    Wt = {"w_in_r": d.pad_r(full["w_in"][:, :d.RC]), "w_in_f": d.pad_f(full["w_in"][:, d.RC:]),
          "w2": _padr(full["w2"], d.DLp), "a2": _padr(full["a2"], d.ALp), "g2": _padr(full["g2"], d.GLp)}
    vec = {n: A[n].reshape(-1, A[n].shape[-1]) for n in _VECTORS}

    loss, dx, G, gv, recv = _local_step(x, p, tgt, Wt, vec, d, late_shards=[shard(n) for n in _LATE])

    gw = {"w_in": jnp.concatenate([d.unpad_r(G["w_in_r"]), d.unpad_f(G["w_in_f"])], axis=1),
          "w2": G["w2"][:d.DL], "a2": G["a2"][:d.AL], "g2": G["g2"][:d.GL]}
    recv.update(zip(_EARLY, _exchange([_pieces(n, gw[n]).astype(BF16) for n in _EARLY], gather=False, name="scatter_early")))
    part = [_sum_slots(recv[n], name="sum_" + n) for n in _MATRICES]
    sib = _swap_cores(part, name="swap_cores")

    sizes = [1] + [A[n].size for n in _VECTORS]
    rows = _rup(_rup(sum(sizes), LANE) // LANE, 8)

    def pack(items):
        flat = jnp.concatenate([i.reshape(-1) for i in items])
        return jnp.pad(flat, (0, rows * LANE - flat.shape[0])).reshape(rows, LANE)

    red = _allreduce_small(pack([loss[0, :1]] + [gv[n] for n in _VECTORS]), name="allreduce_vectors")
    zero = jnp.zeros((1,), F32)
    upd = _adamw(pack([zero] + [A[n] for n in _VECTORS]), pack([zero] + [A["m_" + n] for n in _VECTORS]),
                 pack([zero + 1.0] + [A["v_" + n] for n in _VECTORS]), [red], name="adamw_vectors")
    offs = [0]
    for s in sizes:
        offs.append(offs[-1] + s)
    unpack = lambda a, i, n: a.reshape(-1)[offs[i + 1]:offs[i + 2]].reshape(A[n].shape)

    out = {"grad": {}, "delta": {}, "new_m": {}, "new_v": {}}
    for i, n in enumerate(_VECTORS):
        for kind, a in zip(out, upd):
            out[kind][n] = unpack(a, i, n)
    for n, mine, other in zip(_MATRICES, part, sib):
        res = _adamw(A[n][0], A["m_" + n][0], A["v_" + n][0], [mine, other], name="adamw_" + n)
        for kind, a in zip(out, res):
            out[kind][n] = a[None]
    return (red[0, 0], dx[None], *[out[k][n] for k in out for n in _WEIGHTS])
```

```python
import functools

import jax
import jax.numpy as jnp
from jax import lax
from jax.experimental import pallas as pl
from jax.experimental.pallas import tpu as pltpu

F32 = jnp.float32
BF16 = jnp.bfloat16
LANE = 128
HEAD = 64
RMS_EPS = 1e-6
GN_EPS = 64e-5
ADAM_LR, ADAM_B1, ADAM_B2, ADAM_EPS, ADAM_WD, ADAM_STEP = 0.001, 0.9, 0.999, 1e-08, 0.01, 10
VMEM_LIMIT = 56 * 1024 * 1024
MM_TILE_BYTES = 40 * 1024 * 1024
NEG = -1e30
MESH = pl.DeviceIdType.MESH
N_CHIPS = 4
N_DEV = 8


def _rup(n, m):
    return -(-n // m) * m


def _pick(n, cands):
    for c in cands:
        if n % c == 0:
            return c
    return n


def _cparams(*sem):
    return pltpu.CompilerParams(dimension_semantics=sem, vmem_limit_bytes=VMEM_LIMIT)


def _mm(a, b, *, ta=False, tb=False, add=None, out_dtype=F32, name):
    M, K = (a.shape[1], a.shape[0]) if ta else a.shape
    N = b.shape[0] if tb else b.shape[1]
    tn = _pick(N, (512, 640, 256, 128))
    fits = lambda m, t: 2 * (m * t * a.dtype.itemsize + t * tn * b.dtype.itemsize + m * tn * 8) <= MM_TILE_BYTES
    tm, tk = next((m, t) for t in (K, 2048, 1024, 512, 640, 256, 128) for m in (1024, 512, 256, 128, M)
                  if K % t == 0 and M % m == 0 and fits(m, t))
    nk = K // tk
    dn = (((0 if ta else 1,), (1 if tb else 0,)), ((), ()))

    def body(*refs):
        if add is None:
            a_ref, b_ref, o_ref, acc = refs
        else:
            a_ref, b_ref, add_ref, o_ref, acc = refs
        ks = pl.program_id(2)
        part = lax.dot_general(a_ref[...].astype(BF16), b_ref[...].astype(BF16), dn, preferred_element_type=F32)
        if nk > 1:
            @pl.when(ks == 0)
            def _():
                acc[...] = jnp.zeros_like(acc)

            acc[...] += part

        @pl.when(ks == nk - 1)
        def _():
            res = acc[...] if nk > 1 else part
            if add is not None:
                res = res + add_ref[...].astype(F32)
            o_ref[...] = res.astype(out_dtype)

    a_spec = pl.BlockSpec((tk, tm), lambda i, j, k: (k, i)) if ta else pl.BlockSpec((tm, tk), lambda i, j, k: (i, k))
    b_spec = pl.BlockSpec((tn, tk), lambda i, j, k: (j, k)) if tb else pl.BlockSpec((tk, tn), lambda i, j, k: (k, j))
    o_spec = pl.BlockSpec((tm, tn), lambda i, j, k: (i, j))
    ins, specs = [a, b], [a_spec, b_spec]
    if add is not None:
        ins.append(add)
        specs.append(o_spec)
    return pl.pallas_call(
        body, name=name, grid=(M // tm, N // tn, nk), in_specs=specs, out_specs=o_spec,
        out_shape=jax.ShapeDtypeStruct((M, N), out_dtype),
        scratch_shapes=[pltpu.VMEM((tm, tn) if nk > 1 else (8, LANE), F32)],
        compiler_params=_cparams("parallel", "parallel", "arbitrary"),
    )(*ins)


def _rowwise(fn, rows, consts, outs, accs=(), *, tile, name, prev=(), nxt=()):
    paired = [not isinstance(r, tuple) and r.ndim == 3 for r in rows]
    rows = [r if isinstance(r, tuple) else (r, r.shape[-1], 0) for r in rows]
    T = rows[0][0].shape[-2]
    tile = min(tile, T)
    n = T // tile
    sub = 8
    nr, npv, nnx, ncst, no, na = len(rows), len(prev), len(nxt), len(consts), len(outs), len(accs)
    out_paired = [len(o) == 3 for o in outs]

    def body(*refs):
        i = pl.program_id(0)
        it = iter(refs)
        rv = [next(it) for _ in range(nr)]
        rv = [jnp.concatenate([r[g] for g in range(r.shape[0])], axis=1) if pr else r[...] for r, pr in zip(rv, paired)]
        pv = [jnp.where(i > 0, next(it)[sub - 1:sub, :], 0.0) for _ in range(npv)]
        nv = [jnp.where(i < n - 1, next(it)[0:1, :], 0.0) for _ in range(nnx)]
        cv = [next(it)[...] for _ in range(ncst)]
        o_refs = [next(it) for _ in range(no)]
        a_refs = [next(it) for _ in range(na)]
        res = fn(*rv, *pv, *nv, *cv)
        if not isinstance(res, (tuple, list)):
            res = (res,)
        for r, o, pr in zip(o_refs, res[:no], out_paired):
            if pr:
                for g in range(r.shape[0]):
                    r[g] = o[:, g * LANE:(g + 1) * LANE].astype(r.dtype)
            else:
                r[...] = o.astype(r.dtype)
        if na:
            @pl.when(i == 0)
            def _():
                for r in a_refs:
                    r[...] = jnp.zeros_like(r)
            for r, o in zip(a_refs, res[no:]):
                r[...] += o.astype(F32)

    in_specs = [pl.BlockSpec((a.shape[0], tile, w), lambda i: (0, i, 0)) if pr else
                pl.BlockSpec((tile, w), functools.partial(lambda cb, i: (i, cb), cb)) for (a, w, cb), pr in zip(rows, paired)]
    in_specs += [pl.BlockSpec((sub, a.shape[1]), lambda i: (jnp.maximum(i * (tile // sub) - 1, 0), 0)) for a in prev]
    in_specs += [pl.BlockSpec((sub, a.shape[1]), lambda i: (jnp.minimum((i + 1) * (tile // sub), T // sub - 1), 0)) for a in nxt]
    in_specs += [pl.BlockSpec(c.shape, lambda i: (0, 0)) for c in consts]
    out_specs = [pl.BlockSpec((o[0] // LANE, tile, LANE), lambda i: (0, i, 0)) if pr else
                 pl.BlockSpec((tile, o[0]), lambda i: (i, 0)) for o, pr in zip(outs, out_paired)]
    out_specs += [pl.BlockSpec(s, lambda i: (0, 0)) for s in accs]
    out_shape = [jax.ShapeDtypeStruct((o[0] // LANE, T, LANE) if pr else (T, o[0]), o[1]) for o, pr in zip(outs, out_paired)]
    out_shape += [jax.ShapeDtypeStruct(s, F32) for s in accs]
    res = pl.pallas_call(
        body, name=name, grid=(n,), in_specs=in_specs, out_specs=out_specs, out_shape=out_shape,
        compiler_params=_cparams("arbitrary"),
    )(*[r[0] for r in rows], *prev, *nxt, *consts)
    return res


@jax.custom_vjp
def _bdot(a, b):
    return jnp.dot(a.astype(BF16), b.astype(BF16), preferred_element_type=F32)


def _bdot_fwd(a, b):
    return _bdot(a, b), (a.astype(BF16), b.astype(BF16))


def _bdot_bwd(res, ct):
    a, b = res
    c = ct.astype(BF16)
    return (lax.dot_general(c, b, (((1,), (1,)), ((), ())), preferred_element_type=F32),
            lax.dot_general(a, c, (((0,), (0,)), ((), ())), preferred_element_type=F32))


_bdot.defvjp(_bdot_fwd, _bdot_bwd)


def _split3(x):
    hi = x.astype(BF16)
    r1 = x - hi.astype(F32)
    mid = r1.astype(BF16)
    return hi, mid, (r1 - mid.astype(F32)).astype(BF16)


def _dot_exact(a, b):
    return sum(jnp.dot(p, b, preferred_element_type=F32) for p in _split3(a))


@jax.custom_vjp
def _head_sums(x, seg, segt):
    return _dot_exact(_dot_exact(x, seg), segt)


def _head_sums_fwd(x, seg, segt):
    return _head_sums(x, seg, segt), (seg, segt)


def _head_sums_bwd(res, ct):
    seg, segt = res
    return _head_sums(ct, seg, segt), jnp.zeros_like(seg), jnp.zeros_like(segt)


_head_sums.defvjp(_head_sums_fwd, _head_sums_bwd)


def _rms(x, g, eps=RMS_EPS):
    return x * lax.rsqrt(jnp.mean(x * x, axis=-1, keepdims=True) + eps) * g


def _softplus(x):
    return jnp.maximum(x, 0.0) + jnp.log(1.0 + jnp.exp(-jnp.abs(x)))


def _sigmoid(x):
    return 1.0 / (1.0 + jnp.exp(-x))


def _seg_mats(width):
    h = lax.broadcasted_iota(jnp.int32, (width, LANE), 0) // HEAD
    j = lax.broadcasted_iota(jnp.int32, (width, LANE), 1)
    seg = (h == j).astype(BF16)
    return seg, seg.T


def _prep_rwkv(um, w0, w2, a0, a2, g2, k_k, k_a, seg, segt, *, dims):
    W, DLp, ALp, GLp = dims
    r, k, v = um[:, :W], um[:, W:2 * W], um[:, 2 * W:3 * W]
    o = 3 * W
    xw, xa, xg = um[:, o:o + DLp], um[:, o + DLp:o + DLp + ALp], um[:, o + DLp + ALp:o + DLp + ALp + GLp]
    w_log = -_softplus(-(w0 + _bdot(jnp.tanh(xw), w2))) - 0.5
    decay = jnp.exp(-jnp.exp(w_log))
    a = _sigmoid(a0 + _bdot(xa, a2))
    g = _bdot(_sigmoid(xg), g2)
    kk = k * k_k
    nrm = jnp.sqrt(_head_sums(kk * kk, seg, segt))
    kk = kk / jnp.maximum(nrm, 1e-12)
    k2 = k * (1.0 + (a - 1.0) * k_a)
    return r, decay, k2, v, kk, kk * a, g


def _shift_mix(u, uprev, mu):
    first = lax.broadcasted_iota(jnp.int32, u.shape, 0) == 0
    sh = jnp.where(first, uprev, pltpu.roll(u, 1, 0))
    return u + (sh - u) * mu, sh


def _post_rwkv(y, r, k2, v, g, lnx_g, lnx_b, r_k, seg, segt):
    inv = 1.0 / HEAD
    mean = _head_sums(y, seg, segt) * inv
    yc = y - mean
    var = _head_sums(yc * yc, seg, segt) * inv
    yn = yc * lax.rsqrt(var + GN_EPS) * lnx_g + lnx_b
    bonus = _head_sums(r * k2 * r_k, seg, segt) * v
    return (yn + bonus) * g


def _prep_fox(uf, qg, kg, fb, seg, segt, *, dims):
    FW, FHp = dims
    q, k, v, f = uf[:, :FW], uf[:, FW:2 * FW], uf[:, 2 * FW:3 * FW], uf[:, 3 * FW:3 * FW + FHp]
    inv = 1.0 / HEAD
    qn = q * lax.rsqrt(_head_sums(q * q, seg, segt) * inv + RMS_EPS) * qg * (HEAD ** -0.5)
    kn = k * lax.rsqrt(_head_sums(k * k, seg, segt) * inv + RMS_EPS) * kg
    return qn, kn, v, -_softplus(-(f + fb))


def _tail(h2, pe, z, png, pgb):
    return h2 + _sigmoid(z + pgb) * _rms(pe, png)


def _swiglu(gt, up):
    return gt * _sigmoid(gt) * up


def _cumsum(x, *, reverse, name):
    T, C = x.shape
    tc = _pick(T, (256, 128))
    n = T // tc
    i0 = lax.broadcasted_iota(jnp.int32, (tc, tc), 0)
    i1 = lax.broadcasted_iota(jnp.int32, (tc, tc), 1)
    tri = ((i0 <= i1) if reverse else (i0 >= i1)).astype(BF16)

    def body(x_ref, tri_ref, o_ref, carry):
        i = pl.program_id(0)

        @pl.when(i == 0)
        def _():
            carry[...] = jnp.zeros_like(carry)

        v = x_ref[...]
        hi = v.astype(BF16)
        r1 = v - hi.astype(F32)
        mid = r1.astype(BF16)
        lo = (r1 - mid.astype(F32)).astype(BF16)
        t = tri_ref[...]
        d = lambda p: jnp.dot(t, p, preferred_element_type=F32)
        c = d(hi) + d(mid) + d(lo) + carry[0:1, :]
        o_ref[...] = c
        edge = c[0:1, :] if reverse else c[tc - 1:tc, :]
        carry[...] = jnp.broadcast_to(edge, carry.shape)

    blk = pl.BlockSpec((tc, C), (lambda i: (n - 1 - i, 0)) if reverse else (lambda i: (i, 0)))
    return pl.pallas_call(
        body, name=name, grid=(n,), in_specs=[blk, pl.BlockSpec((tc, tc), lambda i: (0, 0))], out_specs=blk,
        out_shape=jax.ShapeDtypeStruct((T, C), F32), scratch_shapes=[pltpu.VMEM((8, C), F32)],
        compiler_params=_cparams("arbitrary"),
    )(x, tri)


BWD_PAIRS_PER_TRIP = 8
BWD_STEPS_PER_TRIP = 4
FWD_STEPS_PER_TRIP = 8
RECOMPUTE_STEPS_PER_TRIP = 8


def _steps(n, per_trip, step):
    def trip(i, carry):
        for j in range(per_trip):
            carry = step(i * per_trip + j, carry)
        return carry

    lax.fori_loop(0, n // per_trip, trip, 0)


def _col(tile, lane, t):
    return jnp.sum(jnp.where(lane == t, tile, 0.0), axis=1, keepdims=True)


def _halves(x, left):
    a = jnp.sum(jnp.where(left, x, 0.0), axis=1, keepdims=True)
    b = jnp.sum(jnp.where(left, 0.0, x), axis=1, keepdims=True)
    return a, b, jnp.where(left, a, b)


def _pair_col(ref, p, lane, left, t):
    return jnp.where(left, _col(ref[2 * p], lane, t), _col(ref[2 * p + 1], lane, t))


def _scan_fwd(r, w, k, kk, kka, vT, *, hg, tc, comm=None):
    H, N, T = vT.shape
    nc = T // tc
    hp = hg // 2
    L = 2 * N

    def body(r_ref, w_ref, k_ref, kk_ref, kka_ref, vT_ref, yT_ref, chk_ref, s_ref):
        @pl.when(pl.program_id(1) == 0)
        def _():
            s_ref[...] = jnp.zeros_like(s_ref)

        chk_ref[:, 0] = s_ref[...]
        yT_ref[...] = jnp.zeros_like(yT_ref)
        lane = lax.broadcasted_iota(jnp.int32, (N, tc), 1)
        left = lax.broadcasted_iota(jnp.int32, (N, L), 1) < N

        def emit_y(S, p, t):
            ya, yb, _ = _halves(S * r_ref[p, pl.ds(jnp.maximum(t, 0), 1), :], left)
            yT_ref[2 * p] = jnp.where(lane == t, ya, yT_ref[2 * p])
            yT_ref[2 * p + 1] = jnp.where(lane == t, yb, yT_ref[2 * p + 1])

        def step(t, carry):
            for p in range(hp):
                row = lambda ref: ref[p, pl.ds(t, 1), :]
                S = s_ref[p]
                emit_y(S, p, t - 1)
                sa = _halves(S * row(kk_ref), left)[2]
                s_ref[p] = S * row(w_ref) - sa * row(kka_ref) + _pair_col(vT_ref, p, lane, left, t) * row(k_ref)
            return carry

        _steps(tc, FWD_STEPS_PER_TRIP, step)
        for p in range(hp):
            emit_y(s_ref[p], p, tc - 1)

    rows = pl.BlockSpec((hp, tc, L), lambda g, c: (g, c, 0))
    cols = pl.BlockSpec((hg, N, tc), lambda g, c: (g, 0, c))
    return _call_carrying(
        body, "scan_fwd", (H // hg, nc), [rows] * 5 + [cols],
        [cols, pl.BlockSpec((hp, 1, N, L), lambda g, c: (g, c, 0, 0))],
        [jax.ShapeDtypeStruct((H, N, T), F32), jax.ShapeDtypeStruct((H // 2, nc, N, L), F32)],
        [pltpu.VMEM((hp, N, L), F32)], [r, w, k, kk, kka, vT], comm)


def _scan_bwd(r, w, k, kk, kka, vT, dyT, chk, *, hg, tc, comm=None):
    H, N, T = vT.shape
    nc = T // tc
    hp = hg // 2
    L = 2 * N

    def body(r_ref, w_ref, k_ref, kk_ref, kka_ref, vT_ref, dyT_ref, chk_ref,
             dr_ref, dw_ref, dk_ref, dkk_ref, dkka_ref, dvT_ref, sp_ref, ds_ref):
        @pl.when(pl.program_id(1) == 0)
        def _():
            ds_ref[...] = jnp.zeros_like(ds_ref)

        dvT_ref[...] = jnp.zeros_like(dvT_ref)
        lane = lax.broadcasted_iota(jnp.int32, (N, tc), 1)
        left = lax.broadcasted_iota(jnp.int32, (N, L), 1) < N
        left_row = lax.broadcasted_iota(jnp.int32, (1, L), 1) < N

        halves = _halves
        pair_col = lambda ref, p, t: _pair_col(ref, p, lane, left, t)
        for p in range(hp):
            sp_ref[p, 0] = chk_ref[p, 0]

        def fstep(t, carry):
            for p in range(hp):
                row = lambda ref: ref[p, pl.ds(t, 1), :]
                S = sp_ref[p, t]
                sa = halves(S * row(kk_ref), left)[2]
                sp_ref[p, t + 1] = S * row(w_ref) - sa * row(kka_ref) + pair_col(vT_ref, p, t) * row(k_ref)
            return carry

        _steps(tc, RECOMPUTE_STEPS_PER_TRIP, fstep)

        def bstep(p0, i, carry):
            for j in range(BWD_STEPS_PER_TRIP):
                back_one(p0, tc - 1 - (i * BWD_STEPS_PER_TRIP + j))
            return carry

        def back_one(p0, t):
            for p in range(p0, min(p0 + BWD_PAIRS_PER_TRIP, hp)):
                row = lambda ref: ref[p, pl.ds(t, 1), :]
                rr, wr, kr, kkr, kkar = row(r_ref), row(w_ref), row(k_ref), row(kk_ref), row(kka_ref)
                Sp = sp_ref[p, t]
                Sn = sp_ref[p, t + 1]
                dycol, vcol = pair_col(dyT_ref, p, t), pair_col(vT_ref, p, t)
                dS = ds_ref[p]
                dSn = dS + dycol * rr
                dsa = halves(dS * kkar, left)[2] + dycol * halves(rr * kkar, left_row)[2]
                dr_ref[p, pl.ds(t, 1), :] = jnp.sum(Sn * dycol, axis=0, keepdims=True)
                sa = halves(Sp * kkr, left)[2]
                dw_ref[p, pl.ds(t, 1), :] = jnp.sum(dSn * Sp, axis=0, keepdims=True)
                dkka_ref[p, pl.ds(t, 1), :] = -jnp.sum(dSn * sa, axis=0, keepdims=True)
                dva, dvb, _ = halves(dSn * kr, left)
                dk_ref[p, pl.ds(t, 1), :] = jnp.sum(dSn * vcol, axis=0, keepdims=True)
                dkk_ref[p, pl.ds(t, 1), :] = -jnp.sum(Sp * dsa, axis=0, keepdims=True)
                ds_ref[p] = dSn * wr - dsa * kkr
                dvT_ref[2 * p] = jnp.where(lane == t, dva, dvT_ref[2 * p])
                dvT_ref[2 * p + 1] = jnp.where(lane == t, dvb, dvT_ref[2 * p + 1])

        for p0 in range(0, hp, BWD_PAIRS_PER_TRIP):
            lax.fori_loop(0, tc // BWD_STEPS_PER_TRIP, functools.partial(bstep, p0), 0)

    rows = pl.BlockSpec((hp, tc, L), lambda g, c: (g, nc - 1 - c, 0))
    cols = pl.BlockSpec((hg, N, tc), lambda g, c: (g, 0, nc - 1 - c))
    return _call_carrying(
        body, "scan_bwd", (H // hg, nc),
        [rows] * 5 + [cols, cols, pl.BlockSpec((hp, 1, N, L), lambda g, c: (g, nc - 1 - c, 0, 0))], [rows] * 5 + [cols],
        [jax.ShapeDtypeStruct((H // 2, T, L), F32)] * 5 + [jax.ShapeDtypeStruct((H, N, T), F32)],
        [pltpu.VMEM((hp, tc + 1, N, L), F32), pltpu.VMEM((hp, N, L), F32)], [r, w, k, kk, kka, vT, dyT, chk], comm)


_NT = (((1,), (1,)), ((), ()))
_TN = (((0,), (0,)), ((), ()))


def _scores(q, k, cc, cr, masked):
    s = lax.dot_general(q, k, _NT, preferred_element_type=F32) + cc - cr
    if masked:
        tb = s.shape[0]
        keep = lax.broadcasted_iota(jnp.int32, (tb, tb), 0) >= lax.broadcasted_iota(jnp.int32, (tb, tb), 1)
        s = jnp.where(keep, s, NEG)
    return s


def _attn_specs(T, N, tb):
    blk = pl.BlockSpec((1, tb, N), lambda h, i: (h, i, 0))
    whole = pl.BlockSpec((1, T, N), lambda h, i: (h, 0, 0))
    col = pl.BlockSpec((1, tb, 1), lambda h, i: (h, i, 0))
    wcol = pl.BlockSpec((1, T, 1), lambda h, i: (h, 0, 0))
    row = pl.BlockSpec((1, 1, tb), lambda h, i: (h, 0, i))
    wrow = pl.BlockSpec((1, 1, T), lambda h, i: (h, 0, 0))
    return blk, whole, col, wcol, row, wrow


def _call_carrying(body, name, grid, in_specs, out_specs, out_shape, scratch, args, comm):
    n_out = len(out_specs)
    if comm is not None:
        n = len(comm[0])
        body = _carrying(body, len(in_specs), n_out, grid, comm)
        in_specs, out_specs = in_specs + [_HBM] * n, out_specs + [_HBM] * n
        out_shape, scratch, args = out_shape + _exchange_shapes(*comm), scratch + _exchange_sems(n), args + list(comm[0])
    res = pl.pallas_call(
        body, name=name, grid=grid, in_specs=in_specs, out_specs=out_specs, out_shape=out_shape,
        scratch_shapes=scratch, compiler_params=_cparams(*["arbitrary"] * len(grid)),
    )(*args)
    return res[:n_out], res[n_out:]


def _attn_fwd(q, k, v, ccol, crow, *, tb, comm=None):
    H, T, N = q.shape

    def body(q_ref, k_ref, v_ref, cc_ref, cr_ref, o_ref, lse_ref, m_s, l_s, acc_s):
        qi = pl.program_id(1)
        m_s[...] = jnp.full_like(m_s, NEG)
        l_s[...] = jnp.zeros_like(l_s)
        acc_s[...] = jnp.zeros_like(acc_s)
        q_, cc = q_ref[0], cc_ref[0]

        def block(j, masked):
            at = pl.ds(pl.multiple_of(j * tb, tb), tb)
            s = _scores(q_, k_ref[0, at, :], cc, cr_ref[0, :, at], masked)
            m_new = jnp.maximum(m_s[...], jnp.max(s, axis=1, keepdims=True))
            p = jnp.exp(s - m_new)
            alpha = jnp.exp(m_s[...] - m_new)
            l_s[...] = alpha * l_s[...] + jnp.sum(p, axis=1, keepdims=True)
            acc_s[...] = alpha * acc_s[...] + jnp.dot(p.astype(BF16), v_ref[0, at, :], preferred_element_type=F32)
            m_s[...] = m_new

        def below(j, carry):
            block(j, False)
            return carry

        lax.fori_loop(0, qi, below, 0)
        block(qi, True)
        o_ref[0] = acc_s[...] / l_s[...]
        lse_ref[0] = m_s[...] + jnp.log(l_s[...])

    blk, whole, col, wcol, row, wrow = _attn_specs(T, N, tb)
    return _call_carrying(
        body, "fox_fwd", (H, T // tb), [blk, whole, whole, col, wrow], [blk, col],
        [jax.ShapeDtypeStruct((H, T, N), F32), jax.ShapeDtypeStruct((H, T, 1), F32)],
        [pltpu.VMEM((tb, 1), F32), pltpu.VMEM((tb, 1), F32), pltpu.VMEM((tb, N), F32)],
        [q, k, v, ccol, crow], comm)


def _attn_bwd(q, k, v, ccol, crow, o, lse, do, *, tb, comm=None):
    H, T, N = q.shape
    nb = T // tb

    def body(q_ref, k_ref, v_ref, cc_ref, cr_ref, o_ref, lse_ref, do_ref,
             dq_ref, dcq_ref, dk_ref, dv_ref, dck_ref, dq_s, dcq_s, dk_s, dv_s, dck_s):
        qi = pl.program_id(1)

        @pl.when(qi == 0)
        def _():
            dk_s[...] = jnp.zeros_like(dk_s)
            dv_s[...] = jnp.zeros_like(dv_s)
            dck_s[...] = jnp.zeros_like(dck_s)

        dq_s[...] = jnp.zeros_like(dq_s)
        dcq_s[...] = jnp.zeros_like(dcq_s)
        q_, cc, lse_, do_ = q_ref[0], cc_ref[0], lse_ref[0], do_ref[0]
        delta = jnp.sum(do_ * o_ref[0], axis=1, keepdims=True)
        dob = do_.astype(BF16)

        def block(j, masked):
            at = pl.ds(pl.multiple_of(j * tb, tb), tb)
            kb = k_ref[0, at, :]
            p = jnp.exp(_scores(q_, kb, cc, cr_ref[0, :, at], masked) - lse_)
            dp = lax.dot_general(dob, v_ref[0, at, :], _NT, preferred_element_type=F32)
            ds = p * (dp - delta)
            dsb = ds.astype(BF16)
            dq_s[...] += jnp.dot(dsb, kb, preferred_element_type=F32)
            dcq_s[...] += jnp.sum(ds, axis=1, keepdims=True)
            dv_s[at, :] += lax.dot_general(p.astype(BF16), dob, _TN, preferred_element_type=F32)
            dk_s[at, :] += lax.dot_general(dsb, q_, _TN, preferred_element_type=F32)
            dck_s[:, at] += jnp.sum(ds, axis=0, keepdims=True)

        def below(j, carry):
            block(j, False)
            return carry

        lax.fori_loop(0, qi, below, 0)
        block(qi, True)
        dq_ref[0] = dq_s[...]
        dcq_ref[0] = dcq_s[...]

        @pl.when(qi == nb - 1)
        def _():
            dk_ref[0] = dk_s[...]
            dv_ref[0] = dv_s[...]
            dck_ref[0] = dck_s[...]

    blk, whole, col, wcol, row, wrow = _attn_specs(T, N, tb)
    return _call_carrying(
        body, "fox_bwd", (H, nb), [blk, whole, whole, col, wrow, blk, col, blk], [blk, col, whole, whole, wrow],
        [jax.ShapeDtypeStruct((H, T, N), F32), jax.ShapeDtypeStruct((H, T, 1), F32), jax.ShapeDtypeStruct((H, T, N), F32),
         jax.ShapeDtypeStruct((H, T, N), F32), jax.ShapeDtypeStruct((H, 1, T), F32)],
        [pltpu.VMEM((tb, N), F32), pltpu.VMEM((tb, 1), F32), pltpu.VMEM((T, N), F32), pltpu.VMEM((T, N), F32),
         pltpu.VMEM((1, T), F32)],
        [q, k, v, ccol, crow, o, lse, do], comm)


def _heads(x):
    T = x.shape[0]
    return x.reshape(T, -1, HEAD).transpose(1, 0, 2)


def _headsT(x):
    T = x.shape[0]
    return x.reshape(T, -1, HEAD).transpose(1, 2, 0)


def _unheads(x):
    return x.transpose(1, 0, 2).reshape(x.shape[1], -1)


def _unheadsT(x):
    return x.transpose(2, 0, 1).reshape(x.shape[2], -1)


def _padc(x, n):
    return jnp.pad(x, ((0, 0), (0, n - x.shape[1])))


def _padr(x, n):
    return jnp.pad(x, ((0, n - x.shape[0]), (0, 0)))


class _Dims:
    def __init__(self, W, DL, AL, GL, FH):
        self.W, self.DL, self.AL, self.GL, self.FH = W, DL, AL, GL, FH
        self.DLp, self.ALp, self.GLp, self.FHp = _rup(DL, LANE), _rup(AL, LANE), _rup(GL, LANE), _rup(FH, LANE)
        self.FW = FH * HEAD
        self.RC = 3 * W + DL + AL + GL
        self.RP = 3 * W + self.DLp + self.ALp + self.GLp
        self.FC = 3 * self.FW + FH
        self.FP = 3 * self.FW + self.FHp

    def pad_r(self, a):
        W, o = self.W, 3 * self.W
        return jnp.concatenate([a[:, :o], _padc(a[:, o:o + self.DL], self.DLp),
                                _padc(a[:, o + self.DL:o + self.DL + self.AL], self.ALp),
                                _padc(a[:, o + self.DL + self.AL:self.RC], self.GLp)], axis=1)

    def unpad_r(self, a):
        o = 3 * self.W
        return jnp.concatenate([a[:, :o], a[:, o:o + self.DL], a[:, o + self.DLp:o + self.DLp + self.AL],
                                a[:, o + self.DLp + self.ALp:o + self.DLp + self.ALp + self.GL]], axis=1)

    def pad_f(self, a):
        return _padc(a, self.FP)

    def unpad_f(self, a):
        return a[:, :self.FC]


def _local_step(x, p, tgt, Wt, vec, d, late_shards=None):
    Wt = dict(Wt)
    T, D = x.shape
    W, FW = d.W, d.FW
    H = W // HEAD
    seg, segt = _seg_mats(W)
    segf, segft = _seg_mats(FW)
    T1 = 256
    rk_flat = vec["r_k"].reshape(1, W)
    mu = d.pad_r(vec["shift_mu"])
    qg = jnp.tile(vec["q_norm_g"], (1, d.FH))
    kg = jnp.tile(vec["k_norm_g"], (1, d.FH))
    fb = _padc(vec["fgate_b"], d.FHp)
    pdims = (W, d.DLp, d.ALp, d.GLp)
    fdims = (FW, d.FHp)

    (xn,) = _rowwise(lambda x_, g_: _rms(x_, g_), [x], [vec["attn_norm_g"]], [(D, BF16)], tile=T1, name="norm_attn")
    u_r = _mm(xn, Wt["w_in_r"], name="mm_in_r")
    u_f = _mm(xn, Wt["w_in_f"], name="mm_in_f")

    prep_consts = [mu, vec["w0"], Wt["w2"].astype(F32), vec["a0"], Wt["a2"].astype(F32), Wt["g2"].astype(F32), vec["k_k"], vec["k_a"], seg, segt]

    def prep_fwd(u_, up_, mu_, *cs):
        um, _ = _shift_mix(u_, up_, mu_)
        return _prep_rwkv(um, *cs, dims=pdims)

    pw, tw = (W, F32, "pairs"), (W, F32)
    r, dec, k2, v, kk, kka, g = _rowwise(prep_fwd, [u_r], prep_consts, [pw, pw, pw, tw, pw, pw, tw], tile=128,
                                         name="rwkv_prep", prev=[u_r])
    hg, tc = min(16, H), 128
    (yT, chk), gathered = _scan_fwd(r, dec, k2, kk, kka, _headsT(v), hg=hg, tc=tc,
                                    comm=late_shards and (late_shards, True))
    Wt.update({n: _whole(n, g) for n, g in zip(_LATE, gathered)})
    y = _unheadsT(yT)
    post_consts = [vec["lnx_g"], vec["lnx_b"], rk_flat, seg, segt]
    (y_r,) = _rowwise(_post_rwkv, [y, r, k2, v, g], post_consts, [(W, BF16)], tile=T1, name="rwkv_post")

    fox_consts = [qg, kg, fb, segf, segft]
    qn, kn, vf, logf = _rowwise(functools.partial(_prep_fox, dims=fdims), [u_f], fox_consts,
                                [(FW, BF16), (FW, BF16), (FW, BF16), (d.FHp, F32)], tile=T1, name="fox_prep")
    c = _cumsum(logf, reverse=False, name="fox_cumsum")
    cT = c[:, :d.FH].T
    ccol, crow = cT[:, :, None], cT[:, None, :]
    tb = _pick(T, (1024, 512, 256, 128))
    qh, kh, vh = _heads(qn), _heads(kn), _heads(vf)
    (o, lse), _ = _attn_fwd(qh, kh, vh, ccol, crow, tb=tb)
    y_f = _unheads(o)

    ycat = jnp.concatenate([y_r, y_f.astype(BF16)], axis=1)
    h1 = _mm(ycat, Wt["w_out"], add=x, name="mm_out")
    (hn,) = _rowwise(lambda h_, g_: _rms(h_, g_), [h1], [vec["ffn_norm_g"]], [(D, BF16)], tile=T1, name="norm_ffn")
    gt = _mm(hn, Wt["w_gate"], name="mm_gate")
    up = _mm(hn, Wt["w_up"], name="mm_up")
    (act,) = _rowwise(_swiglu, [gt, up], [], [(gt.shape[1], BF16)], tile=T1, name="swiglu")
    h2 = _mm(act, Wt["w_down"], add=h1, name="mm_down")
    (hg_,) = _rowwise(lambda h_, g_: _rms(h_, g_), [h2], [vec["ple_gate_norm_g"]], [(D, BF16)], tile=T1, name="norm_gate")
    pe = _mm(p, Wt["ple_proj"], name="mm_ple")
    z = _mm(hg_, Wt["ple_gate_w"], name="mm_pgate")

    def tail(h2_, pe_, z_, tg_, png_, pgb_):
        h3, f = jax.vjp(_tail, h2_, pe_, z_, png_, pgb_)
        err = h3 - tg_
        dh3 = err * (1.0 / D)
        lt = 0.5 * jnp.sum(jnp.sum(err * err, axis=1, keepdims=True) * (1.0 / D), axis=0, keepdims=True)
        dh2_, dpe_, dz_, dpng_, dpgb_ = f(dh3)
        return dh2_, dpe_, dz_, jnp.broadcast_to(lt, (1, LANE)), dpng_, dpgb_

    dh3, dpe, dz, loss, g_png, g_pgb = _rowwise(
        tail, [h2, pe, z, tgt], [vec["ple_norm_g"], vec["ple_gate_b"]], [(D, F32), (D, BF16), (D, BF16)],
        [(1, LANE), (1, D), (1, D)], tile=T1, name="tail")
    G = {}
    gv = {"ple_norm_g": g_png, "ple_gate_b": g_pgb}
    G["ple_gate_w"] = _mm(hg_, dz, ta=True, out_dtype=BF16, name="gw_pgate")
    G["ple_proj"] = _mm(p, dpe, ta=True, out_dtype=BF16, name="gw_ple")
    d_hg = _mm(dz, Wt["ple_gate_w"], tb=True, name="mmb_pgate")

    def norm_bwd(h_, dres_, dn_, g_):
        _, f = jax.vjp(_rms, h_, g_)
        dh_, dg_ = f(dn_)
        return dres_ + dh_, dg_

    dh2, gv["ple_gate_norm_g"] = _rowwise(norm_bwd, [h2, dh3, d_hg], [vec["ple_gate_norm_g"]], [(D, F32)], [(1, D)],
                                          tile=T1, name="norm_gate_bwd")
    G["w_down"] = _mm(act, dh2, ta=True, out_dtype=BF16, name="gw_down")
    d_act = _mm(dh2, Wt["w_down"], tb=True, name="mmb_down")

    def swiglu_bwd(gt_, up_, da_):
        _, f = jax.vjp(_swiglu, gt_, up_)
        return f(da_)

    d_gt, d_up = _rowwise(swiglu_bwd, [gt, up, d_act], [], [(gt.shape[1], BF16)] * 2, tile=T1, name="swiglu_bwd")
    G["w_gate"] = _mm(hn, d_gt, ta=True, out_dtype=BF16, name="gw_gate")
    G["w_up"] = _mm(hn, d_up, ta=True, out_dtype=BF16, name="gw_up")
    d_hn = _mm(d_gt, Wt["w_gate"], tb=True, name="mmb_gate")
    d_hn = _mm(d_up, Wt["w_up"], tb=True, add=d_hn, name="mmb_up")
    dh1, gv["ffn_norm_g"] = _rowwise(norm_bwd, [h1, dh2, d_hn], [vec["ffn_norm_g"]], [(D, F32)], [(1, D)],
                                     tile=T1, name="norm_ffn_bwd")
    G["w_out"] = _mm(ycat, dh1, ta=True, out_dtype=BF16, name="gw_out")
    d_ycat = _mm(dh1, Wt["w_out"], tb=True, name="mmb_out")

    def post_bwd(y_, r_, k2_, v_, g_, dy_, *cs):
        lg, lb, rk, sg, sgt = cs
        _, f = jax.vjp(lambda *a: _post_rwkv(*a, sg, sgt), y_, r_, k2_, v_, g_, lg, lb, rk)
        return f(dy_)

    dy, dr1, dk1, dv1, dg, gv["lnx_g"], gv["lnx_b"], g_rk = _rowwise(
        post_bwd, [y, r, k2, v, g, (d_ycat, W, 0)], post_consts, [(W, F32)] * 5, [(1, W)] * 3, tile=128, name="rwkv_post_bwd")
    gv["r_k"] = g_rk.reshape(H, HEAD)
    pieces = late_shards and ([_pieces(n, G[n]).astype(BF16) for n in _LATE], False)
    (dr, ddec, dk2, dkk, dkka, dvT), recv = _scan_bwd(r, dec, k2, kk, kka, _headsT(v), _headsT(dy), chk, hg=hg, tc=tc,
                                                      comm=pieces)
    recv = dict(zip(_LATE, recv))
    dv = _unheadsT(dvT)

    def prep_bwd(u_, dr_, dr1_, ddec_, dk2_, dk1_, dv_, dv1_, dkk_, dkka_, dg_, up_, mu_, *cs):
        um, sh = _shift_mix(u_, up_, mu_)
        cs_d, sg, sgt = cs[:7], cs[7], cs[8]
        _, f = jax.vjp(lambda um_, *c_: _prep_rwkv(um_, *c_, sg, sgt, dims=pdims), um, *cs_d)
        res = f((dr_ + dr1_, ddec_, dk2_ + dk1_, dv_ + dv1_, dkk_, dkka_, dg_))
        dum = res[0]
        dmu = jnp.sum(dum * (sh - u_), axis=0, keepdims=True)
        return (dum, dmu) + tuple(res[1:])

    LP = [Wt["w2"].shape, Wt["a2"].shape, Wt["g2"].shape]
    dum, g_mu, gv["w0"], g_w2, gv["a0"], g_a2, g_g2, gv["k_k"], gv["k_a"] = _rowwise(
        prep_bwd, [u_r, dr, dr1, ddec, dk2, dk1, dv, dv1, dkk, dkka, dg], prep_consts, [(d.RP, F32)],
        [(1, d.RP), (1, W), LP[0], (1, W), LP[1], LP[2], (1, W), (1, W)], tile=128, name="rwkv_prep_bwd", prev=[u_r])
    gv["shift_mu"] = d.unpad_r(g_mu)
    G["w2"], G["a2"], G["g2"] = g_w2, g_a2, g_g2
    (du_r,) = _rowwise(lambda a_, an_, mu_: a_ * (1.0 - mu_) + jnp.where(
        lax.broadcasted_iota(jnp.int32, a_.shape, 0) == a_.shape[0] - 1, an_, pltpu.roll(a_, a_.shape[0] - 1, 0)) * mu_,
        [dum], [mu], [(d.RP, BF16)], tile=T1, name="shift_bwd", nxt=[dum])

    do = _heads(d_ycat[:, W:])
    (dq, dcq, dk_, dv_, dck), _ = _attn_bwd(qh, kh, vh, ccol, crow, o, lse, do, tb=tb)
    dc = _padc((dcq[:, :, 0] - dck[:, 0, :]).T, d.FHp)
    dlogf = _cumsum(dc, reverse=True, name="fox_cumsum_bwd")

    def fox_bwd(uf_, dq_, dk__, dv__, dlf_, *cs):
        qg_, kg_, fb_, sg, sgt = cs
        _, f = jax.vjp(lambda uf__, a, b, c_: _prep_fox(uf__, a, b, c_, sg, sgt, dims=fdims), uf_, qg_, kg_, fb_)
        return f((dq_, dk__, dv__, dlf_))

    du_f, g_qg, g_kg, g_fb = _rowwise(fox_bwd, [u_f, _unheads(dq), _unheads(dk_), _unheads(dv_), dlogf], fox_consts,
                                      [(d.FP, BF16)], [(1, FW), (1, FW), (1, d.FHp)], tile=T1, name="fox_prep_bwd")
    gv["q_norm_g"] = g_qg.reshape(d.FH, HEAD).sum(0, keepdims=True)
    gv["k_norm_g"] = g_kg.reshape(d.FH, HEAD).sum(0, keepdims=True)
    gv["fgate_b"] = g_fb[:, :d.FH]

    G["w_in_r"] = _mm(xn, du_r, ta=True, out_dtype=BF16, name="gw_in_r")
    G["w_in_f"] = _mm(xn, du_f, ta=True, out_dtype=BF16, name="gw_in_f")
    d_xn = _mm(du_r, Wt["w_in_r"], tb=True, name="mmb_in_r")
    d_xn = _mm(du_f, Wt["w_in_f"], tb=True, add=d_xn, name="mmb_in_f")
    dx, gv["attn_norm_g"] = _rowwise(norm_bwd, [x, dh1, d_xn], [vec["attn_norm_g"]], [(D, F32)], [(1, D)],
                                     tile=T1, name="norm_attn_bwd")
    return loss, dx, G, gv, recv


_HBM = pl.BlockSpec(memory_space=pl.ANY)
_OTHER_CHIPS = ((0, 1), (1, 0), (1, 1))


def _flip(v, bit):
    return 1 - v if bit else v


def _exchange(arrs, *, gather, name):
    n = len(arrs)

    def body(*refs):
        copies = _exchange_copies(refs[:n], refs[n:2 * n], refs[2 * n:], gather)
        for cp in copies:
            cp.start()
        for cp in copies:
            cp.wait()

    return pl.pallas_call(
        body, name=name, in_specs=[_HBM] * n, out_specs=[_HBM] * n, out_shape=_exchange_shapes(arrs, gather),
        scratch_shapes=_exchange_sems(n),
    )(*arrs)


def _exchange_shapes(arrs, gather):
    return [jax.ShapeDtypeStruct(((N_CHIPS,) + a.shape) if gather else a.shape, a.dtype) for a in arrs]


def _exchange_sems(n):
    return [pltpu.SemaphoreType.DMA((3 * n,)), pltpu.SemaphoreType.DMA((3 * n,)), pltpu.SemaphoreType.DMA((n,))]


def _exchange_copies(ins, outs, sems, gather):
    send_sems, recv_sems, own_sems = sems
    x, y, c = lax.axis_index("x"), lax.axis_index("y"), lax.axis_index("c")
    me = 2 * x + y
    copies = []
    for a in range(len(ins)):
        copies.append(pltpu.make_async_copy(ins[a] if gather else ins[a].at[me], outs[a].at[me], own_sems.at[a]))
        for k, (dx, dy) in enumerate(_OTHER_CHIPS):
            px, py = _flip(x, dx), _flip(y, dy)
            copies.append(pltpu.make_async_remote_copy(
                src_ref=ins[a] if gather else ins[a].at[2 * px + py], dst_ref=outs[a].at[me],
                send_sem=send_sems.at[3 * a + k], recv_sem=recv_sems.at[3 * a + k],
                device_id=(px, py, c), device_id_type=MESH))
    return copies


def _carrying(body, n_in, n_out, grid, comm):
    arrs, gather = comm
    n = len(arrs)

    def wrapped(*refs):
        c_in = refs[n_in:n_in + n]
        c_out = refs[n_in + n + n_out:n_in + 2 * n + n_out]
        ids = [pl.program_id(a) for a in range(len(grid))]
        first = functools.reduce(jnp.logical_and, [i == 0 for i in ids])
        last = functools.reduce(jnp.logical_and, [i == g - 1 for i, g in zip(ids, grid)])

        @pl.when(first)
        def _():
            for cp in _exchange_copies(c_in, c_out, refs[-3:], gather):
                cp.start()

        body(*refs[:n_in], *refs[n_in + n:n_in + n + n_out], *refs[n_in + 2 * n + n_out:-3])

        @pl.when(last)
        def _():
            for cp in _exchange_copies(c_in, c_out, refs[-3:], gather):
                cp.wait()

    return wrapped


def _swap_cores(arrs, *, name):
    n = len(arrs)

    def body(*refs):
        ins, outs = refs[:n], refs[n:2 * n]
        send_sems, recv_sems = refs[2 * n:]
        peer = (lax.axis_index("x"), lax.axis_index("y"), 1 - lax.axis_index("c"))
        cps = [pltpu.make_async_remote_copy(src_ref=ins[a], dst_ref=outs[a], send_sem=send_sems.at[a],
                                            recv_sem=recv_sems.at[a], device_id=peer, device_id_type=MESH) for a in range(n)]
        for cp in cps:
            cp.start()
        for cp in cps:
            cp.wait()

    return pl.pallas_call(
        body, name=name, in_specs=[_HBM] * n, out_specs=[_HBM] * n,
        out_shape=[jax.ShapeDtypeStruct(a.shape, a.dtype) for a in arrs],
        scratch_shapes=[pltpu.SemaphoreType.DMA((n,)), pltpu.SemaphoreType.DMA((n,))],
    )(*arrs)


def _allreduce_small(pack, *, name):
    R, C = pack.shape

    def body(p_ref, o_ref, recv, send_sems, recv_sems):
        x, y, c = lax.axis_index("x"), lax.axis_index("y"), lax.axis_index("c")
        me = 4 * x + 2 * y + c
        recv[me] = p_ref[...]
        cps = []
        for k in range(1, N_DEV):
            peer = (_flip(x, k & 4), _flip(y, k & 2), _flip(c, k & 1))
            cp = pltpu.make_async_remote_copy(src_ref=p_ref, dst_ref=recv.at[me], send_sem=send_sems.at[k - 1],
                                              recv_sem=recv_sems.at[k - 1], device_id=peer, device_id_type=MESH)
            cp.start()
            cps.append(cp)
        for cp in cps:
            cp.wait()
        acc = recv[0]
        for s in range(1, N_DEV):
            acc = acc + recv[s]
        o_ref[...] = acc

    vm = pl.BlockSpec(memory_space=pltpu.VMEM)
    return pl.pallas_call(
        body, name=name, in_specs=[vm], out_specs=vm, out_shape=jax.ShapeDtypeStruct((R, C), F32),
        scratch_shapes=[pltpu.VMEM((N_DEV, R, C), F32), pltpu.SemaphoreType.DMA((N_DEV - 1,)), pltpu.SemaphoreType.DMA((N_DEV - 1,))],
    )(pack)


def _sum_slots(a, *, name):
    S, R, C = a.shape
    tr = _pick(R, (256, 128, 64, 32, 16, 8))

    def body(a_ref, o_ref):
        acc = a_ref[0].astype(F32)
        for s in range(1, S):
            acc = acc + a_ref[s].astype(F32)
        o_ref[...] = acc

    return pl.pallas_call(
        body, name=name, grid=(R // tr,), in_specs=[pl.BlockSpec((S, tr, C), lambda i: (0, i, 0))],
        out_specs=pl.BlockSpec((tr, C), lambda i: (i, 0)), out_shape=jax.ShapeDtypeStruct((R, C), F32),
        compiler_params=_cparams("parallel"),
    )(a)


def _adamw(w, m, v, gs, *, name):
    R, C = w.shape
    tile = _pick(R, (128, 96, 64, 32, 16, 8))

    def fn(w_, m_, v_, *g_):
        g = g_[0]
        for e in g_[1:]:
            g = g + e
        m2 = ADAM_B1 * m_ + (1.0 - ADAM_B1) * g
        v2 = ADAM_B2 * v_ + (1.0 - ADAM_B2) * jnp.square(g)
        m_hat = m2 / (1.0 - ADAM_B1 ** ADAM_STEP)
        v_hat = v2 / (1.0 - ADAM_B2 ** ADAM_STEP)
        delta = -ADAM_LR * (m_hat / (jnp.sqrt(v_hat) + ADAM_EPS) + ADAM_WD * w_)
        return g, delta, m2, v2

    return _rowwise(fn, [w, m, v, *gs], [], [(C, F32)] * 4, tile=tile, name=name)


_ARGS = "x, p, attn_norm_g, w_in, shift_mu, w0, w2, a0, a2, g2, k_k, k_a, r_k, lnx_g, lnx_b, q_norm_g, k_norm_g, fgate_b, w_out, ffn_norm_g, w_gate, w_up, w_down, ple_proj, ple_norm_g, ple_gate_norm_g, ple_gate_w, ple_gate_b, loss_target, m_attn_norm_g, m_w_in, m_shift_mu, m_w0, m_w2, m_a0, m_a2, m_g2, m_k_k, m_k_a, m_r_k, m_lnx_g, m_lnx_b, m_q_norm_g, m_k_norm_g, m_fgate_b, m_w_out, m_ffn_norm_g, m_w_gate, m_w_up, m_w_down, m_ple_proj, m_ple_norm_g, m_ple_gate_norm_g, m_ple_gate_w, m_ple_gate_b, v_attn_norm_g, v_w_in, v_shift_mu, v_w0, v_w2, v_a0, v_a2, v_g2, v_k_k, v_k_a, v_r_k, v_lnx_g, v_lnx_b, v_q_norm_g, v_k_norm_g, v_fgate_b, v_w_out, v_ffn_norm_g, v_w_gate, v_w_up, v_w_down, v_ple_proj, v_ple_norm_g, v_ple_gate_norm_g, v_ple_gate_w, v_ple_gate_b".split(", ")
_WEIGHTS = _ARGS[2:28]
_COL_SHARDED = ("w_in", "w2", "a2", "g2", "w_gate", "w_up", "ple_proj")
_ROW_SHARDED = ("w_out", "w_down", "ple_gate_w")
_MATRICES = _COL_SHARDED + _ROW_SHARDED
_EARLY = ("w_in", "w2", "a2", "g2")
_LATE = tuple(n for n in _MATRICES if n not in _EARLY)
_VECTORS = tuple(n for n in _WEIGHTS if n not in _MATRICES)


def _whole(name, g):
    if name in _COL_SHARDED:
        return g.transpose(1, 0, 2).reshape(g.shape[1], -1)
    return g.reshape(-1, g.shape[2])


def _pieces(name, a):
    if name in _COL_SHARDED:
        return a.reshape(a.shape[0], N_CHIPS, -1).transpose(1, 0, 2)
    return a.reshape(N_CHIPS, -1, a.shape[1])


def kernel(x, p, attn_norm_g, w_in, shift_mu, w0, w2, a0, a2, g2, k_k, k_a, r_k, lnx_g, lnx_b, q_norm_g, k_norm_g, fgate_b, w_out, ffn_norm_g, w_gate, w_up, w_down, ple_proj, ple_norm_g, ple_gate_norm_g, ple_gate_w, ple_gate_b, loss_target, m_attn_norm_g, m_w_in, m_shift_mu, m_w0, m_w2, m_a0, m_a2, m_g2, m_k_k, m_k_a, m_r_k, m_lnx_g, m_lnx_b, m_q_norm_g, m_k_norm_g, m_fgate_b, m_w_out, m_ffn_norm_g, m_w_gate, m_w_up, m_w_down, m_ple_proj, m_ple_norm_g, m_ple_gate_norm_g, m_ple_gate_w, m_ple_gate_b, v_attn_norm_g, v_w_in, v_shift_mu, v_w0, v_w2, v_a0, v_a2, v_g2, v_k_k, v_k_a, v_r_k, v_lnx_g, v_lnx_b, v_q_norm_g, v_k_norm_g, v_fgate_b, v_w_out, v_ffn_norm_g, v_w_gate, v_w_up, v_w_down, v_ple_proj, v_ple_norm_g, v_ple_gate_norm_g, v_ple_gate_w, v_ple_gate_b):
    A = dict(zip(_ARGS, (x, p, attn_norm_g, w_in, shift_mu, w0, w2, a0, a2, g2, k_k, k_a, r_k, lnx_g, lnx_b, q_norm_g, k_norm_g, fgate_b, w_out, ffn_norm_g, w_gate, w_up, w_down, ple_proj, ple_norm_g, ple_gate_norm_g, ple_gate_w, ple_gate_b, loss_target, m_attn_norm_g, m_w_in, m_shift_mu, m_w0, m_w2, m_a0, m_a2, m_g2, m_k_k, m_k_a, m_r_k, m_lnx_g, m_lnx_b, m_q_norm_g, m_k_norm_g, m_fgate_b, m_w_out, m_ffn_norm_g, m_w_gate, m_w_up, m_w_down, m_ple_proj, m_ple_norm_g, m_ple_gate_norm_g, m_ple_gate_w, m_ple_gate_b, v_attn_norm_g, v_w_in, v_shift_mu, v_w0, v_w2, v_a0, v_a2, v_g2, v_k_k, v_k_a, v_r_k, v_lnx_g, v_lnx_b, v_q_norm_g, v_k_norm_g, v_fgate_b, v_w_out, v_ffn_norm_g, v_w_gate, v_w_up, v_w_down, v_ple_proj, v_ple_norm_g, v_ple_gate_norm_g, v_ple_gate_w, v_ple_gate_b)))
    x, p, tgt = A["x"][0], A["p"][0, 0], A["loss_target"][0]
    d = _Dims(W=A["w0"].shape[-1], DL=A["w2"].shape[1], AL=A["a2"].shape[1], GL=A["g2"].shape[1], FH=A["fgate_b"].shape[-1])

    shard = lambda n: A[n][0].astype(BF16)
    gathered = _exchange([shard(n) for n in _EARLY], gather=True, name="gather_early")
    full = {n: _whole(n, g) for n, g in zip(_EARLY, gathered)}
    Wt = {"w_in_r": d.pad_r(full["w_in"][:, :d.RC]), "w_in_f": d.pad_f(full["w_in"][:, d.RC:]),
          "w2": _padr(full["w2"], d.DLp), "a2": _padr(full["a2"], d.ALp), "g2": _padr(full["g2"], d.GLp)}
    vec = {n: A[n].reshape(-1, A[n].shape[-1]) for n in _VECTORS}

    loss, dx, G, gv, recv = _local_step(x, p, tgt, Wt, vec, d, late_shards=[shard(n) for n in _LATE])

    gw = {"w_in": jnp.concatenate([d.unpad_r(G["w_in_r"]), d.unpad_f(G["w_in_f"])], axis=1),
          "w2": G["w2"][:d.DL], "a2": G["a2"][:d.AL], "g2": G["g2"][:d.GL]}
    recv.update(zip(_EARLY, _exchange([_pieces(n, gw[n]).astype(BF16) for n in _EARLY], gather=False, name="scatter_early")))
    part = [_sum_slots(recv[n], name="sum_" + n) for n in _MATRICES]
    sib = _swap_cores(part, name="swap_cores")

    sizes = [1] + [A[n].size for n in _VECTORS]
    rows = _rup(_rup(sum(sizes), LANE) // LANE, 8)

    def pack(items):
        flat = jnp.concatenate([i.reshape(-1) for i in items])
        return jnp.pad(flat, (0, rows * LANE - flat.shape[0])).reshape(rows, LANE)

    red = _allreduce_small(pack([loss[0, :1]] + [gv[n] for n in _VECTORS]), name="allreduce_vectors")
    zero = jnp.zeros((1,), F32)
    upd = _adamw(pack([zero] + [A[n] for n in _VECTORS]), pack([zero] + [A["m_" + n] for n in _VECTORS]),
                 pack([zero + 1.0] + [A["v_" + n] for n in _VECTORS]), [red], name="adamw_vectors")
    offs = [0]
    for s in sizes:
        offs.append(offs[-1] + s)
    unpack = lambda a, i, n: a.reshape(-1)[offs[i + 1]:offs[i + 2]].reshape(A[n].shape)

    out = {"grad": {}, "delta": {}, "new_m": {}, "new_v": {}}
    for i, n in enumerate(_VECTORS):
        for kind, a in zip(out, upd):
            out[kind][n] = unpack(a, i, n)
    for n, mine, other in zip(_MATRICES, part, sib):
        res = _adamw(A[n][0], A["m_" + n][0], A["v_" + n][0], [mine, other], name="adamw_" + n)
        for kind, a in zip(out, res):
            out[kind][n] = a[None]
    return (red[0, 0], dx[None], *[out[k][n] for k in out for n in _WEIGHTS])
```

```python
import functools

import jax
import jax.numpy as jnp
from jax import lax
from jax.experimental import pallas as pl
from jax.experimental.pallas import tpu as pltpu

F32 = jnp.float32
BF16 = jnp.bfloat16
LANE = 128
HEAD = 64
RMS_EPS = 1e-6
GN_EPS = 64e-5
ADAM_LR, ADAM_B1, ADAM_B2, ADAM_EPS, ADAM_WD, ADAM_STEP = 0.001, 0.9, 0.999, 1e-08, 0.01, 10
VMEM_LIMIT = 56 * 1024 * 1024
MM_TILE_BYTES = 40 * 1024 * 1024
NEG = -1e30
MESH = pl.DeviceIdType.MESH
N_CHIPS = 4
N_DEV = 8


def _rup(n, m):
    return -(-n // m) * m


def _pick(n, cands):
    for c in cands:
        if n % c == 0:
            return c
    return n


def _cparams(*sem):
    return pltpu.CompilerParams(dimension_semantics=sem, vmem_limit_bytes=VMEM_LIMIT)


def _mm(a, b, *, ta=False, tb=False, add=None, out_dtype=F32, name):
    M, K = (a.shape[1], a.shape[0]) if ta else a.shape
    N = b.shape[0] if tb else b.shape[1]
    tn = _pick(N, (512, 640, 256, 128))
    fits = lambda m, t: 2 * (m * t * a.dtype.itemsize + t * tn * b.dtype.itemsize + m * tn * 8) <= MM_TILE_BYTES
    tm, tk = next((m, t) for t in (K, 2048, 1024, 512, 640, 256, 128) for m in (1024, 512, 256, 128, M)
                  if K % t == 0 and M % m == 0 and fits(m, t))
    nk = K // tk
    dn = (((0 if ta else 1,), (1 if tb else 0,)), ((), ()))

    def body(*refs):
        if add is None:
            a_ref, b_ref, o_ref, acc = refs
        else:
            a_ref, b_ref, add_ref, o_ref, acc = refs
        ks = pl.program_id(2)
        part = lax.dot_general(a_ref[...].astype(BF16), b_ref[...].astype(BF16), dn, preferred_element_type=F32)
        if nk > 1:
            @pl.when(ks == 0)
            def _():
                acc[...] = jnp.zeros_like(acc)

            acc[...] += part

        @pl.when(ks == nk - 1)
        def _():
            res = acc[...] if nk > 1 else part
            if add is not None:
                res = res + add_ref[...].astype(F32)
            o_ref[...] = res.astype(out_dtype)

    a_spec = pl.BlockSpec((tk, tm), lambda i, j, k: (k, i)) if ta else pl.BlockSpec((tm, tk), lambda i, j, k: (i, k))
    b_spec = pl.BlockSpec((tn, tk), lambda i, j, k: (j, k)) if tb else pl.BlockSpec((tk, tn), lambda i, j, k: (k, j))
    o_spec = pl.BlockSpec((tm, tn), lambda i, j, k: (i, j))
    ins, specs = [a, b], [a_spec, b_spec]
    if add is not None:
        ins.append(add)
        specs.append(o_spec)
    return pl.pallas_call(
        body, name=name, grid=(M // tm, N // tn, nk), in_specs=specs, out_specs=o_spec,
        out_shape=jax.ShapeDtypeStruct((M, N), out_dtype),
        scratch_shapes=[pltpu.VMEM((tm, tn) if nk > 1 else (8, LANE), F32)],
        compiler_params=_cparams("parallel", "parallel", "arbitrary"),
    )(*ins)


def _rowwise(fn, rows, consts, outs, accs=(), *, tile, name, prev=(), nxt=()):
    paired = [not isinstance(r, tuple) and r.ndim == 3 for r in rows]
    rows = [r if isinstance(r, tuple) else (r, r.shape[-1], 0) for r in rows]
    T = rows[0][0].shape[-2]
    tile = min(tile, T)
    n = T // tile
    sub = 8
    nr, npv, nnx, ncst, no, na = len(rows), len(prev), len(nxt), len(consts), len(outs), len(accs)
    out_paired = [len(o) == 3 for o in outs]

    def body(*refs):
        i = pl.program_id(0)
        it = iter(refs)
        rv = [next(it) for _ in range(nr)]
        rv = [jnp.concatenate([r[g] for g in range(r.shape[0])], axis=1) if pr else r[...] for r, pr in zip(rv, paired)]
        pv = [jnp.where(i > 0, next(it)[sub - 1:sub, :], 0.0) for _ in range(npv)]
        nv = [jnp.where(i < n - 1, next(it)[0:1, :], 0.0) for _ in range(nnx)]
        cv = [next(it)[...] for _ in range(ncst)]
        o_refs = [next(it) for _ in range(no)]
        a_refs = [next(it) for _ in range(na)]
        res = fn(*rv, *pv, *nv, *cv)
        if not isinstance(res, (tuple, list)):
            res = (res,)
        for r, o, pr in zip(o_refs, res[:no], out_paired):
            if pr:
                for g in range(r.shape[0]):
                    r[g] = o[:, g * LANE:(g + 1) * LANE].astype(r.dtype)
            else:
                r[...] = o.astype(r.dtype)
        if na:
            @pl.when(i == 0)
            def _():
                for r in a_refs:
                    r[...] = jnp.zeros_like(r)
            for r, o in zip(a_refs, res[no:]):
                r[...] += o.astype(F32)

    in_specs = [pl.BlockSpec((a.shape[0], tile, w), lambda i: (0, i, 0)) if pr else
                pl.BlockSpec((tile, w), functools.partial(lambda cb, i: (i, cb), cb)) for (a, w, cb), pr in zip(rows, paired)]
    in_specs += [pl.BlockSpec((sub, a.shape[1]), lambda i: (jnp.maximum(i * (tile // sub) - 1, 0), 0)) for a in prev]
    in_specs += [pl.BlockSpec((sub, a.shape[1]), lambda i: (jnp.minimum((i + 1) * (tile // sub), T // sub - 1), 0)) for a in nxt]
    in_specs += [pl.BlockSpec(c.shape, lambda i: (0, 0)) for c in consts]
    out_specs = [pl.BlockSpec((o[0] // LANE, tile, LANE), lambda i: (0, i, 0)) if pr else
                 pl.BlockSpec((tile, o[0]), lambda i: (i, 0)) for o, pr in zip(outs, out_paired)]
    out_specs += [pl.BlockSpec(s, lambda i: (0, 0)) for s in accs]
    out_shape = [jax.ShapeDtypeStruct((o[0] // LANE, T, LANE) if pr else (T, o[0]), o[1]) for o, pr in zip(outs, out_paired)]
    out_shape += [jax.ShapeDtypeStruct(s, F32) for s in accs]
    res = pl.pallas_call(
        body, name=name, grid=(n,), in_specs=in_specs, out_specs=out_specs, out_shape=out_shape,
        compiler_params=_cparams("arbitrary"),
    )(*[r[0] for r in rows], *prev, *nxt, *consts)
    return res


@jax.custom_vjp
def _bdot(a, b):
    return jnp.dot(a.astype(BF16), b.astype(BF16), preferred_element_type=F32)


def _bdot_fwd(a, b):
    return _bdot(a, b), (a.astype(BF16), b.astype(BF16))


def _bdot_bwd(res, ct):
    a, b = res
    c = ct.astype(BF16)
    return (lax.dot_general(c, b, (((1,), (1,)), ((), ())), preferred_element_type=F32),
            lax.dot_general(a, c, (((0,), (0,)), ((), ())), preferred_element_type=F32))


_bdot.defvjp(_bdot_fwd, _bdot_bwd)


def _split3(x):
    hi = x.astype(BF16)
    r1 = x - hi.astype(F32)
    mid = r1.astype(BF16)
    return hi, mid, (r1 - mid.astype(F32)).astype(BF16)


def _dot_exact(a, b):
    return sum(jnp.dot(p, b, preferred_element_type=F32) for p in _split3(a))


@jax.custom_vjp
def _head_sums(x, seg, segt):
    return _dot_exact(_dot_exact(x, seg), segt)


def _head_sums_fwd(x, seg, segt):
    return _head_sums(x, seg, segt), (seg, segt)


def _head_sums_bwd(res, ct):
    seg, segt = res
    return _head_sums(ct, seg, segt), jnp.zeros_like(seg), jnp.zeros_like(segt)


_head_sums.defvjp(_head_sums_fwd, _head_sums_bwd)


def _rms(x, g, eps=RMS_EPS):
    return x * lax.rsqrt(jnp.mean(x * x, axis=-1, keepdims=True) + eps) * g


def _softplus(x):
    return jnp.maximum(x, 0.0) + jnp.log(1.0 + jnp.exp(-jnp.abs(x)))


def _sigmoid(x):
    return 1.0 / (1.0 + jnp.exp(-x))


def _seg_mats(width):
    h = lax.broadcasted_iota(jnp.int32, (width, LANE), 0) // HEAD
    j = lax.broadcasted_iota(jnp.int32, (width, LANE), 1)
    seg = (h == j).astype(BF16)
    return seg, seg.T


def _prep_rwkv(um, w0, w2, a0, a2, g2, k_k, k_a, seg, segt, *, dims):
    W, DLp, ALp, GLp = dims
    r, k, v = um[:, :W], um[:, W:2 * W], um[:, 2 * W:3 * W]
    o = 3 * W
    xw, xa, xg = um[:, o:o + DLp], um[:, o + DLp:o + DLp + ALp], um[:, o + DLp + ALp:o + DLp + ALp + GLp]
    w_log = -_softplus(-(w0 + _bdot(jnp.tanh(xw), w2))) - 0.5
    decay = jnp.exp(-jnp.exp(w_log))
    a = _sigmoid(a0 + _bdot(xa, a2))
    g = _bdot(_sigmoid(xg), g2)
    kk = k * k_k
    nrm = jnp.sqrt(_head_sums(kk * kk, seg, segt))
    kk = kk / jnp.maximum(nrm, 1e-12)
    k2 = k * (1.0 + (a - 1.0) * k_a)
    return r, decay, k2, v, kk, kk * a, g


def _shift_mix(u, uprev, mu):
    first = lax.broadcasted_iota(jnp.int32, u.shape, 0) == 0
    sh = jnp.where(first, uprev, pltpu.roll(u, 1, 0))
    return u + (sh - u) * mu, sh


def _post_rwkv(y, r, k2, v, g, lnx_g, lnx_b, r_k, seg, segt):
    inv = 1.0 / HEAD
    mean = _head_sums(y, seg, segt) * inv
    yc = y - mean
    var = _head_sums(yc * yc, seg, segt) * inv
    yn = yc * lax.rsqrt(var + GN_EPS) * lnx_g + lnx_b
    bonus = _head_sums(r * k2 * r_k, seg, segt) * v
    return (yn + bonus) * g


def _prep_fox(uf, qg, kg, fb, seg, segt, *, dims):
    FW, FHp = dims
    q, k, v, f = uf[:, :FW], uf[:, FW:2 * FW], uf[:, 2 * FW:3 * FW], uf[:, 3 * FW:3 * FW + FHp]
    inv = 1.0 / HEAD
    qn = q * lax.rsqrt(_head_sums(q * q, seg, segt) * inv + RMS_EPS) * qg * (HEAD ** -0.5)
    kn = k * lax.rsqrt(_head_sums(k * k, seg, segt) * inv + RMS_EPS) * kg
    return qn, kn, v, -_softplus(-(f + fb))


def _tail(h2, pe, z, png, pgb):
    return h2 + _sigmoid(z + pgb) * _rms(pe, png)


def _swiglu(gt, up):
    return gt * _sigmoid(gt) * up


def _cumsum(x, *, reverse, name):
    T, C = x.shape
    tc = _pick(T, (256, 128))
    n = T // tc
    i0 = lax.broadcasted_iota(jnp.int32, (tc, tc), 0)
    i1 = lax.broadcasted_iota(jnp.int32, (tc, tc), 1)
    tri = ((i0 <= i1) if reverse else (i0 >= i1)).astype(BF16)

    def body(x_ref, tri_ref, o_ref, carry):
        i = pl.program_id(0)

        @pl.when(i == 0)
        def _():
            carry[...] = jnp.zeros_like(carry)

        v = x_ref[...]
        hi = v.astype(BF16)
        r1 = v - hi.astype(F32)
        mid = r1.astype(BF16)
        lo = (r1 - mid.astype(F32)).astype(BF16)
        t = tri_ref[...]
        d = lambda p: jnp.dot(t, p, preferred_element_type=F32)
        c = d(hi) + d(mid) + d(lo) + carry[0:1, :]
        o_ref[...] = c
        edge = c[0:1, :] if reverse else c[tc - 1:tc, :]
        carry[...] = jnp.broadcast_to(edge, carry.shape)

    blk = pl.BlockSpec((tc, C), (lambda i: (n - 1 - i, 0)) if reverse else (lambda i: (i, 0)))
    return pl.pallas_call(
        body, name=name, grid=(n,), in_specs=[blk, pl.BlockSpec((tc, tc), lambda i: (0, 0))], out_specs=blk,
        out_shape=jax.ShapeDtypeStruct((T, C), F32), scratch_shapes=[pltpu.VMEM((8, C), F32)],
        compiler_params=_cparams("arbitrary"),
    )(x, tri)


BWD_PAIRS_PER_TRIP = 8
BWD_STEPS_PER_TRIP = 4
FWD_STEPS_PER_TRIP = 8


def _steps(n, per_trip, step, start=0):
    def trip(i, carry):
        for j in range(per_trip):
            carry = step(start + i * per_trip + j, carry)
        return carry

    lax.fori_loop(0, n // per_trip, trip, 0)


def _col(tile, lane, t):
    return jnp.sum(jnp.where(lane == t, tile, 0.0), axis=1, keepdims=True)


def _halves(x, left):
    a = jnp.sum(jnp.where(left, x, 0.0), axis=1, keepdims=True)
    b = jnp.sum(jnp.where(left, 0.0, x), axis=1, keepdims=True)
    return a, b, jnp.where(left, a, b)


def _pair_col(ref, p, lane, left, t):
    return jnp.where(left, _col(ref[2 * p], lane, t), _col(ref[2 * p + 1], lane, t))


def _scan_fwd(r, w, k, kk, kka, vT, *, hg, tc, comm=None):
    H, N, T = vT.shape
    nc = T // tc
    hp = hg // 2
    L = 2 * N
    half = tc // 2

    def body(r_ref, w_ref, k_ref, kk_ref, kka_ref, vT_ref, yT_ref, chk_ref, st_hbm, s_ref, stage, st_sems):
        g, c = pl.program_id(0), pl.program_id(1)

        @pl.when(c == 0)
        def _():
            s_ref[...] = jnp.zeros_like(s_ref)

        def flush(hf, chunk):
            return pltpu.make_async_copy(stage.at[:, pl.ds(hf * half, half)],
                                         st_hbm.at[pl.ds(g * hp, hp), pl.ds(chunk * tc + hf * half, half)], st_sems.at[hf])

        chk_ref[:, 0] = s_ref[...]
        yT_ref[...] = jnp.zeros_like(yT_ref)
        lane = lax.broadcasted_iota(jnp.int32, (N, tc), 1)
        left = lax.broadcasted_iota(jnp.int32, (N, L), 1) < N

        def emit_y(S, p, t):
            ya, yb, _ = _halves(S * r_ref[p, pl.ds(jnp.maximum(t, 0), 1), :], left)
            yT_ref[2 * p] = jnp.where(lane == t, ya, yT_ref[2 * p])
            yT_ref[2 * p + 1] = jnp.where(lane == t, yb, yT_ref[2 * p + 1])

        def step(t, carry):
            for p in range(hp):
                row = lambda ref: ref[p, pl.ds(t, 1), :]
                S = s_ref[p]
                emit_y(S, p, t - 1)
                sa = _halves(S * row(kk_ref), left)[2]
                new = S * row(w_ref) - sa * row(kka_ref) + _pair_col(vT_ref, p, lane, left, t) * row(k_ref)
                s_ref[p] = new
                stage[p, t] = new
            return carry

        for hf in range(2):
            @pl.when(c > 0)
            def _():
                flush(hf, c - 1).wait()

            _steps(half, FWD_STEPS_PER_TRIP, step, start=hf * half)
            flush(hf, c).start()
        for p in range(hp):
            emit_y(s_ref[p], p, tc - 1)

        @pl.when(c == nc - 1)
        def _():
            flush(0, c).wait()
            flush(1, c).wait()

    rows = pl.BlockSpec((hp, tc, L), lambda g, c: (g, c, 0))
    cols = pl.BlockSpec((hg, N, tc), lambda g, c: (g, 0, c))
    return _call_carrying(
        body, "scan_fwd", (H // hg, nc), [rows] * 5 + [cols],
        [cols, pl.BlockSpec((hp, 1, N, L), lambda g, c: (g, c, 0, 0)), _HBM],
        [jax.ShapeDtypeStruct((H, N, T), F32), jax.ShapeDtypeStruct((H // 2, nc, N, L), F32),
         jax.ShapeDtypeStruct((H // 2, T, N, L), F32)],
        [pltpu.VMEM((hp, N, L), F32), pltpu.VMEM((hp, tc, N, L), F32), pltpu.SemaphoreType.DMA((2,))],
        [r, w, k, kk, kka, vT], comm)


def _scan_bwd(r, w, k, kk, kka, vT, dyT, chk, states, *, hg, tc, comm=None):
    H, N, T = vT.shape
    nc = T // tc
    hp = hg // 2
    L = 2 * N
    half = tc // 2

    def body(r_ref, w_ref, k_ref, kk_ref, kka_ref, vT_ref, dyT_ref, chk_ref, st_hbm,
             dr_ref, dw_ref, dk_ref, dkk_ref, dkka_ref, dvT_ref, sp_ref, ds_ref, ld_sems):
        g, chunk = pl.program_id(0), nc - 1 - pl.program_id(1)

        def load(hf):
            first, n = (half - 1, half + 1) if hf else (0, half - 1)
            return pltpu.make_async_copy(st_hbm.at[pl.ds(g * hp, hp), pl.ds(chunk * tc + first, n)],
                                         sp_ref.at[:, pl.ds(1 + first, n)], ld_sems.at[hf])

        load(1).start()
        load(0).start()

        @pl.when(pl.program_id(1) == 0)
        def _():
            ds_ref[...] = jnp.zeros_like(ds_ref)

        dvT_ref[...] = jnp.zeros_like(dvT_ref)
        lane = lax.broadcasted_iota(jnp.int32, (N, tc), 1)
        left = lax.broadcasted_iota(jnp.int32, (N, L), 1) < N
        left_row = lax.broadcasted_iota(jnp.int32, (1, L), 1) < N

        halves = _halves
        pair_col = lambda ref, p, t: _pair_col(ref, p, lane, left, t)
        for p in range(hp):
            sp_ref[p, 0] = chk_ref[p, 0]

        def bstep(p0, last, i, carry):
            for j in range(BWD_STEPS_PER_TRIP):
                back_one(p0, last - (i * BWD_STEPS_PER_TRIP + j))
            return carry

        def back_one(p0, t):
            for p in range(p0, min(p0 + BWD_PAIRS_PER_TRIP, hp)):
                row = lambda ref: ref[p, pl.ds(t, 1), :]
                rr, wr, kr, kkr, kkar = row(r_ref), row(w_ref), row(k_ref), row(kk_ref), row(kka_ref)
                Sp = sp_ref[p, t]
                Sn = sp_ref[p, t + 1]
                dycol, vcol = pair_col(dyT_ref, p, t), pair_col(vT_ref, p, t)
                dS = ds_ref[p]
                dSn = dS + dycol * rr
                dsa = halves(dS * kkar, left)[2] + dycol * halves(rr * kkar, left_row)[2]
                dr_ref[p, pl.ds(t, 1), :] = jnp.sum(Sn * dycol, axis=0, keepdims=True)
                sa = halves(Sp * kkr, left)[2]
                dw_ref[p, pl.ds(t, 1), :] = jnp.sum(dSn * Sp, axis=0, keepdims=True)
                dkka_ref[p, pl.ds(t, 1), :] = -jnp.sum(dSn * sa, axis=0, keepdims=True)
                dva, dvb, _ = halves(dSn * kr, left)
                dk_ref[p, pl.ds(t, 1), :] = jnp.sum(dSn * vcol, axis=0, keepdims=True)
                dkk_ref[p, pl.ds(t, 1), :] = -jnp.sum(Sp * dsa, axis=0, keepdims=True)
                ds_ref[p] = dSn * wr - dsa * kkr
                dvT_ref[2 * p] = jnp.where(lane == t, dva, dvT_ref[2 * p])
                dvT_ref[2 * p + 1] = jnp.where(lane == t, dvb, dvT_ref[2 * p + 1])

        for hf in (1, 0):
            load(hf).wait()
            for p0 in range(0, hp, BWD_PAIRS_PER_TRIP):
                lax.fori_loop(0, half // BWD_STEPS_PER_TRIP, functools.partial(bstep, p0, (hf + 1) * half - 1), 0)

    rows = pl.BlockSpec((hp, tc, L), lambda g, c: (g, nc - 1 - c, 0))
    cols = pl.BlockSpec((hg, N, tc), lambda g, c: (g, 0, nc - 1 - c))
    return _call_carrying(
        body, "scan_bwd", (H // hg, nc),
        [rows] * 5 + [cols, cols, pl.BlockSpec((hp, 1, N, L), lambda g, c: (g, nc - 1 - c, 0, 0)), _HBM],
        [rows] * 5 + [cols],
        [jax.ShapeDtypeStruct((H // 2, T, L), F32)] * 5 + [jax.ShapeDtypeStruct((H, N, T), F32)],
        [pltpu.VMEM((hp, tc + 1, N, L), F32), pltpu.VMEM((hp, N, L), F32), pltpu.SemaphoreType.DMA((2,))],
        [r, w, k, kk, kka, vT, dyT, chk, states], comm)


_NT = (((1,), (1,)), ((), ()))
_TN = (((0,), (0,)), ((), ()))


def _scores(q, k, cc, cr, masked):
    s = lax.dot_general(q, k, _NT, preferred_element_type=F32) + cc - cr
    if masked:
        tb = s.shape[0]
        keep = lax.broadcasted_iota(jnp.int32, (tb, tb), 0) >= lax.broadcasted_iota(jnp.int32, (tb, tb), 1)
        s = jnp.where(keep, s, NEG)
    return s


def _attn_specs(T, N, tb):
    blk = pl.BlockSpec((1, tb, N), lambda h, i: (h, i, 0))
    whole = pl.BlockSpec((1, T, N), lambda h, i: (h, 0, 0))
    col = pl.BlockSpec((1, tb, 1), lambda h, i: (h, i, 0))
    wcol = pl.BlockSpec((1, T, 1), lambda h, i: (h, 0, 0))
    row = pl.BlockSpec((1, 1, tb), lambda h, i: (h, 0, i))
    wrow = pl.BlockSpec((1, 1, T), lambda h, i: (h, 0, 0))
    return blk, whole, col, wcol, row, wrow


def _call_carrying(body, name, grid, in_specs, out_specs, out_shape, scratch, args, comm):
    n_out = len(out_specs)
    if comm is not None:
        n = len(comm[0])
        body = _carrying(body, len(in_specs), n_out, grid, comm)
        in_specs, out_specs = in_specs + [_HBM] * n, out_specs + [_HBM] * n
        out_shape, scratch, args = out_shape + _exchange_shapes(*comm), scratch + _exchange_sems(n), args + list(comm[0])
    res = pl.pallas_call(
        body, name=name, grid=grid, in_specs=in_specs, out_specs=out_specs, out_shape=out_shape,
        scratch_shapes=scratch, compiler_params=_cparams(*["arbitrary"] * len(grid)),
    )(*args)
    return res[:n_out], res[n_out:]


def _attn_fwd(q, k, v, ccol, crow, *, tb, comm=None):
    H, T, N = q.shape

    def body(q_ref, k_ref, v_ref, cc_ref, cr_ref, o_ref, lse_ref, m_s, l_s, acc_s):
        qi = pl.program_id(1)
        m_s[...] = jnp.full_like(m_s, NEG)
        l_s[...] = jnp.zeros_like(l_s)
        acc_s[...] = jnp.zeros_like(acc_s)
        q_, cc = q_ref[0], cc_ref[0]

        def block(j, masked):
            at = pl.ds(pl.multiple_of(j * tb, tb), tb)
            s = _scores(q_, k_ref[0, at, :], cc, cr_ref[0, :, at], masked)
            m_new = jnp.maximum(m_s[...], jnp.max(s, axis=1, keepdims=True))
            p = jnp.exp(s - m_new)
            alpha = jnp.exp(m_s[...] - m_new)
            l_s[...] = alpha * l_s[...] + jnp.sum(p, axis=1, keepdims=True)
            acc_s[...] = alpha * acc_s[...] + jnp.dot(p.astype(BF16), v_ref[0, at, :], preferred_element_type=F32)
            m_s[...] = m_new

        def below(j, carry):
            block(j, False)
            return carry

        lax.fori_loop(0, qi, below, 0)
        block(qi, True)
        o_ref[0] = acc_s[...] / l_s[...]
        lse_ref[0] = m_s[...] + jnp.log(l_s[...])

    blk, whole, col, wcol, row, wrow = _attn_specs(T, N, tb)
    return _call_carrying(
        body, "fox_fwd", (H, T // tb), [blk, whole, whole, col, wrow], [blk, col],
        [jax.ShapeDtypeStruct((H, T, N), F32), jax.ShapeDtypeStruct((H, T, 1), F32)],
        [pltpu.VMEM((tb, 1), F32), pltpu.VMEM((tb, 1), F32), pltpu.VMEM((tb, N), F32)],
        [q, k, v, ccol, crow], comm)


def _attn_bwd(q, k, v, ccol, crow, o, lse, do, *, tb, comm=None):
    H, T, N = q.shape
    nb = T // tb

    def body(q_ref, k_ref, v_ref, cc_ref, cr_ref, o_ref, lse_ref, do_ref,
             dq_ref, dcq_ref, dk_ref, dv_ref, dck_ref, dq_s, dcq_s, dk_s, dv_s, dck_s):
        qi = pl.program_id(1)

        @pl.when(qi == 0)
        def _():
            dk_s[...] = jnp.zeros_like(dk_s)
            dv_s[...] = jnp.zeros_like(dv_s)
            dck_s[...] = jnp.zeros_like(dck_s)

        dq_s[...] = jnp.zeros_like(dq_s)
        dcq_s[...] = jnp.zeros_like(dcq_s)
        q_, cc, lse_, do_ = q_ref[0], cc_ref[0], lse_ref[0], do_ref[0]
        delta = jnp.sum(do_ * o_ref[0], axis=1, keepdims=True)
        dob = do_.astype(BF16)

        def block(j, masked):
            at = pl.ds(pl.multiple_of(j * tb, tb), tb)
            kb = k_ref[0, at, :]
            p = jnp.exp(_scores(q_, kb, cc, cr_ref[0, :, at], masked) - lse_)
            dp = lax.dot_general(dob, v_ref[0, at, :], _NT, preferred_element_type=F32)
            ds = p * (dp - delta)
            dsb = ds.astype(BF16)
            dq_s[...] += jnp.dot(dsb, kb, preferred_element_type=F32)
            dcq_s[...] += jnp.sum(ds, axis=1, keepdims=True)
            dv_s[at, :] += lax.dot_general(p.astype(BF16), dob, _TN, preferred_element_type=F32)
            dk_s[at, :] += lax.dot_general(dsb, q_, _TN, preferred_element_type=F32)
            dck_s[:, at] += jnp.sum(ds, axis=0, keepdims=True)

        def below(j, carry):
            block(j, False)
            return carry

        lax.fori_loop(0, qi, below, 0)
        block(qi, True)
        dq_ref[0] = dq_s[...]
        dcq_ref[0] = dcq_s[...]

        @pl.when(qi == nb - 1)
        def _():
            dk_ref[0] = dk_s[...]
            dv_ref[0] = dv_s[...]
            dck_ref[0] = dck_s[...]

    blk, whole, col, wcol, row, wrow = _attn_specs(T, N, tb)
    return _call_carrying(
        body, "fox_bwd", (H, nb), [blk, whole, whole, col, wrow, blk, col, blk], [blk, col, whole, whole, wrow],
        [jax.ShapeDtypeStruct((H, T, N), F32), jax.ShapeDtypeStruct((H, T, 1), F32), jax.ShapeDtypeStruct((H, T, N), F32),
         jax.ShapeDtypeStruct((H, T, N), F32), jax.ShapeDtypeStruct((H, 1, T), F32)],
        [pltpu.VMEM((tb, N), F32), pltpu.VMEM((tb, 1), F32), pltpu.VMEM((T, N), F32), pltpu.VMEM((T, N), F32),
         pltpu.VMEM((1, T), F32)],
        [q, k, v, ccol, crow, o, lse, do], comm)


def _heads(x):
    T = x.shape[0]
    return x.reshape(T, -1, HEAD).transpose(1, 0, 2)


def _headsT(x):
    T = x.shape[0]
    return x.reshape(T, -1, HEAD).transpose(1, 2, 0)


def _unheads(x):
    return x.transpose(1, 0, 2).reshape(x.shape[1], -1)


def _unheadsT(x):
    return x.transpose(2, 0, 1).reshape(x.shape[2], -1)


def _padc(x, n):
    return jnp.pad(x, ((0, 0), (0, n - x.shape[1])))


def _padr(x, n):
    return jnp.pad(x, ((0, n - x.shape[0]), (0, 0)))


class _Dims:
    def __init__(self, W, DL, AL, GL, FH):
        self.W, self.DL, self.AL, self.GL, self.FH = W, DL, AL, GL, FH
        self.DLp, self.ALp, self.GLp, self.FHp = _rup(DL, LANE), _rup(AL, LANE), _rup(GL, LANE), _rup(FH, LANE)
        self.FW = FH * HEAD
        self.RC = 3 * W + DL + AL + GL
        self.RP = 3 * W + self.DLp + self.ALp + self.GLp
        self.FC = 3 * self.FW + FH
        self.FP = 3 * self.FW + self.FHp

    def pad_r(self, a):
        W, o = self.W, 3 * self.W
        return jnp.concatenate([a[:, :o], _padc(a[:, o:o + self.DL], self.DLp),
                                _padc(a[:, o + self.DL:o + self.DL + self.AL], self.ALp),
                                _padc(a[:, o + self.DL + self.AL:self.RC], self.GLp)], axis=1)

    def unpad_r(self, a):
        o = 3 * self.W
        return jnp.concatenate([a[:, :o], a[:, o:o + self.DL], a[:, o + self.DLp:o + self.DLp + self.AL],
                                a[:, o + self.DLp + self.ALp:o + self.DLp + self.ALp + self.GL]], axis=1)

    def pad_f(self, a):
        return _padc(a, self.FP)

    def unpad_f(self, a):
        return a[:, :self.FC]


def _local_step(x, p, tgt, Wt, vec, d, late_shards=None):
    Wt = dict(Wt)
    T, D = x.shape
    W, FW = d.W, d.FW
    H = W // HEAD
    seg, segt = _seg_mats(W)
    segf, segft = _seg_mats(FW)
    T1 = 256
    rk_flat = vec["r_k"].reshape(1, W)
    mu = d.pad_r(vec["shift_mu"])
    qg = jnp.tile(vec["q_norm_g"], (1, d.FH))
    kg = jnp.tile(vec["k_norm_g"], (1, d.FH))
    fb = _padc(vec["fgate_b"], d.FHp)
    pdims = (W, d.DLp, d.ALp, d.GLp)
    fdims = (FW, d.FHp)

    (xn,) = _rowwise(lambda x_, g_: _rms(x_, g_), [x], [vec["attn_norm_g"]], [(D, BF16)], tile=T1, name="norm_attn")
    u_r = _mm(xn, Wt["w_in_r"], name="mm_in_r")
    u_f = _mm(xn, Wt["w_in_f"], name="mm_in_f")

    prep_consts = [mu, vec["w0"], Wt["w2"].astype(F32), vec["a0"], Wt["a2"].astype(F32), Wt["g2"].astype(F32), vec["k_k"], vec["k_a"], seg, segt]

    def prep_fwd(u_, up_, mu_, *cs):
        um, _ = _shift_mix(u_, up_, mu_)
        return _prep_rwkv(um, *cs, dims=pdims)

    pw, tw = (W, F32, "pairs"), (W, F32)
    r, dec, k2, v, kk, kka, g = _rowwise(prep_fwd, [u_r], prep_consts, [pw, pw, pw, tw, pw, pw, tw], tile=128,
                                         name="rwkv_prep", prev=[u_r])
    hg, tc = min(16, H), 128
    (yT, chk, states), gathered = _scan_fwd(r, dec, k2, kk, kka, _headsT(v), hg=hg, tc=tc,
                                            comm=late_shards and (late_shards, True))
    Wt.update({n: _whole(n, g) for n, g in zip(_LATE, gathered)})
    y = _unheadsT(yT)
    post_consts = [vec["lnx_g"], vec["lnx_b"], rk_flat, seg, segt]
    (y_r,) = _rowwise(_post_rwkv, [y, r, k2, v, g], post_consts, [(W, BF16)], tile=T1, name="rwkv_post")

    fox_consts = [qg, kg, fb, segf, segft]
    qn, kn, vf, logf = _rowwise(functools.partial(_prep_fox, dims=fdims), [u_f], fox_consts,
                                [(FW, BF16), (FW, BF16), (FW, BF16), (d.FHp, F32)], tile=T1, name="fox_prep")
    c = _cumsum(logf, reverse=False, name="fox_cumsum")
    cT = c[:, :d.FH].T
    ccol, crow = cT[:, :, None], cT[:, None, :]
    tb = _pick(T, (1024, 512, 256, 128))
    qh, kh, vh = _heads(qn), _heads(kn), _heads(vf)
    (o, lse), _ = _attn_fwd(qh, kh, vh, ccol, crow, tb=tb)
    y_f = _unheads(o)

    ycat = jnp.concatenate([y_r, y_f.astype(BF16)], axis=1)
    h1 = _mm(ycat, Wt["w_out"], add=x, name="mm_out")
    (hn,) = _rowwise(lambda h_, g_: _rms(h_, g_), [h1], [vec["ffn_norm_g"]], [(D, BF16)], tile=T1, name="norm_ffn")
    gt = _mm(hn, Wt["w_gate"], name="mm_gate")
    up = _mm(hn, Wt["w_up"], name="mm_up")
    (act,) = _rowwise(_swiglu, [gt, up], [], [(gt.shape[1], BF16)], tile=T1, name="swiglu")
    h2 = _mm(act, Wt["w_down"], add=h1, name="mm_down")
    (hg_,) = _rowwise(lambda h_, g_: _rms(h_, g_), [h2], [vec["ple_gate_norm_g"]], [(D, BF16)], tile=T1, name="norm_gate")
    pe = _mm(p, Wt["ple_proj"], name="mm_ple")
    z = _mm(hg_, Wt["ple_gate_w"], name="mm_pgate")

    def tail(h2_, pe_, z_, tg_, png_, pgb_):
        h3, f = jax.vjp(_tail, h2_, pe_, z_, png_, pgb_)
        err = h3 - tg_
        dh3 = err * (1.0 / D)
        lt = 0.5 * jnp.sum(jnp.sum(err * err, axis=1, keepdims=True) * (1.0 / D), axis=0, keepdims=True)
        dh2_, dpe_, dz_, dpng_, dpgb_ = f(dh3)
        return dh2_, dpe_, dz_, jnp.broadcast_to(lt, (1, LANE)), dpng_, dpgb_

    dh3, dpe, dz, loss, g_png, g_pgb = _rowwise(
        tail, [h2, pe, z, tgt], [vec["ple_norm_g"], vec["ple_gate_b"]], [(D, F32), (D, BF16), (D, BF16)],
        [(1, LANE), (1, D), (1, D)], tile=T1, name="tail")
    G = {}
    gv = {"ple_norm_g": g_png, "ple_gate_b": g_pgb}
    G["ple_gate_w"] = _mm(hg_, dz, ta=True, out_dtype=BF16, name="gw_pgate")
    G["ple_proj"] = _mm(p, dpe, ta=True, out_dtype=BF16, name="gw_ple")
    d_hg = _mm(dz, Wt["ple_gate_w"], tb=True, name="mmb_pgate")

    def norm_bwd(h_, dres_, dn_, g_):
        _, f = jax.vjp(_rms, h_, g_)
        dh_, dg_ = f(dn_)
        return dres_ + dh_, dg_

    dh2, gv["ple_gate_norm_g"] = _rowwise(norm_bwd, [h2, dh3, d_hg], [vec["ple_gate_norm_g"]], [(D, F32)], [(1, D)],
                                          tile=T1, name="norm_gate_bwd")
    G["w_down"] = _mm(act, dh2, ta=True, out_dtype=BF16, name="gw_down")
    d_act = _mm(dh2, Wt["w_down"], tb=True, name="mmb_down")

    def swiglu_bwd(gt_, up_, da_):
        _, f = jax.vjp(_swiglu, gt_, up_)
        return f(da_)

    d_gt, d_up = _rowwise(swiglu_bwd, [gt, up, d_act], [], [(gt.shape[1], BF16)] * 2, tile=T1, name="swiglu_bwd")
    G["w_gate"] = _mm(hn, d_gt, ta=True, out_dtype=BF16, name="gw_gate")
    G["w_up"] = _mm(hn, d_up, ta=True, out_dtype=BF16, name="gw_up")
    d_hn = _mm(d_gt, Wt["w_gate"], tb=True, name="mmb_gate")
    d_hn = _mm(d_up, Wt["w_up"], tb=True, add=d_hn, name="mmb_up")
    dh1, gv["ffn_norm_g"] = _rowwise(norm_bwd, [h1, dh2, d_hn], [vec["ffn_norm_g"]], [(D, F32)], [(1, D)],
                                     tile=T1, name="norm_ffn_bwd")
    G["w_out"] = _mm(ycat, dh1, ta=True, out_dtype=BF16, name="gw_out")
    d_ycat = _mm(dh1, Wt["w_out"], tb=True, name="mmb_out")

    def post_bwd(y_, r_, k2_, v_, g_, dy_, *cs):
        lg, lb, rk, sg, sgt = cs
        _, f = jax.vjp(lambda *a: _post_rwkv(*a, sg, sgt), y_, r_, k2_, v_, g_, lg, lb, rk)
        return f(dy_)

    dy, dr1, dk1, dv1, dg, gv["lnx_g"], gv["lnx_b"], g_rk = _rowwise(
        post_bwd, [y, r, k2, v, g, (d_ycat, W, 0)], post_consts, [(W, F32)] * 5, [(1, W)] * 3, tile=128, name="rwkv_post_bwd")
    gv["r_k"] = g_rk.reshape(H, HEAD)
    pieces = late_shards and ([_pieces(n, G[n]).astype(BF16) for n in _LATE], False)
    (dr, ddec, dk2, dkk, dkka, dvT), recv = _scan_bwd(r, dec, k2, kk, kka, _headsT(v), _headsT(dy), chk, states, hg=hg,
                                                      tc=tc, comm=pieces)
    recv = dict(zip(_LATE, recv))
    dv = _unheadsT(dvT)

    def prep_bwd(u_, dr_, dr1_, ddec_, dk2_, dk1_, dv_, dv1_, dkk_, dkka_, dg_, up_, mu_, *cs):
        um, sh = _shift_mix(u_, up_, mu_)
        cs_d, sg, sgt = cs[:7], cs[7], cs[8]
        _, f = jax.vjp(lambda um_, *c_: _prep_rwkv(um_, *c_, sg, sgt, dims=pdims), um, *cs_d)
        res = f((dr_ + dr1_, ddec_, dk2_ + dk1_, dv_ + dv1_, dkk_, dkka_, dg_))
        dum = res[0]
        dmu = jnp.sum(dum * (sh - u_), axis=0, keepdims=True)
        return (dum, dmu) + tuple(res[1:])

    LP = [Wt["w2"].shape, Wt["a2"].shape, Wt["g2"].shape]
    dum, g_mu, gv["w0"], g_w2, gv["a0"], g_a2, g_g2, gv["k_k"], gv["k_a"] = _rowwise(
        prep_bwd, [u_r, dr, dr1, ddec, dk2, dk1, dv, dv1, dkk, dkka, dg], prep_consts, [(d.RP, F32)],
        [(1, d.RP), (1, W), LP[0], (1, W), LP[1], LP[2], (1, W), (1, W)], tile=128, name="rwkv_prep_bwd", prev=[u_r])
    gv["shift_mu"] = d.unpad_r(g_mu)
    G["w2"], G["a2"], G["g2"] = g_w2, g_a2, g_g2
    (du_r,) = _rowwise(lambda a_, an_, mu_: a_ * (1.0 - mu_) + jnp.where(
        lax.broadcasted_iota(jnp.int32, a_.shape, 0) == a_.shape[0] - 1, an_, pltpu.roll(a_, a_.shape[0] - 1, 0)) * mu_,
        [dum], [mu], [(d.RP, BF16)], tile=T1, name="shift_bwd", nxt=[dum])

    do = _heads(d_ycat[:, W:])
    (dq, dcq, dk_, dv_, dck), _ = _attn_bwd(qh, kh, vh, ccol, crow, o, lse, do, tb=tb)
    dc = _padc((dcq[:, :, 0] - dck[:, 0, :]).T, d.FHp)
    dlogf = _cumsum(dc, reverse=True, name="fox_cumsum_bwd")

    def fox_bwd(uf_, dq_, dk__, dv__, dlf_, *cs):
        qg_, kg_, fb_, sg, sgt = cs
        _, f = jax.vjp(lambda uf__, a, b, c_: _prep_fox(uf__, a, b, c_, sg, sgt, dims=fdims), uf_, qg_, kg_, fb_)
        return f((dq_, dk__, dv__, dlf_))

    du_f, g_qg, g_kg, g_fb = _rowwise(fox_bwd, [u_f, _unheads(dq), _unheads(dk_), _unheads(dv_), dlogf], fox_consts,
                                      [(d.FP, BF16)], [(1, FW), (1, FW), (1, d.FHp)], tile=T1, name="fox_prep_bwd")
    gv["q_norm_g"] = g_qg.reshape(d.FH, HEAD).sum(0, keepdims=True)
    gv["k_norm_g"] = g_kg.reshape(d.FH, HEAD).sum(0, keepdims=True)
    gv["fgate_b"] = g_fb[:, :d.FH]

    G["w_in_r"] = _mm(xn, du_r, ta=True, out_dtype=BF16, name="gw_in_r")
    G["w_in_f"] = _mm(xn, du_f, ta=True, out_dtype=BF16, name="gw_in_f")
    d_xn = _mm(du_r, Wt["w_in_r"], tb=True, name="mmb_in_r")
    d_xn = _mm(du_f, Wt["w_in_f"], tb=True, add=d_xn, name="mmb_in_f")
    dx, gv["attn_norm_g"] = _rowwise(norm_bwd, [x, dh1, d_xn], [vec["attn_norm_g"]], [(D, F32)], [(1, D)],
                                     tile=T1, name="norm_attn_bwd")
    return loss, dx, G, gv, recv


_HBM = pl.BlockSpec(memory_space=pl.ANY)
_OTHER_CHIPS = ((0, 1), (1, 0), (1, 1))


def _flip(v, bit):
    return 1 - v if bit else v


def _exchange(arrs, *, gather, name):
    n = len(arrs)

    def body(*refs):
        copies = _exchange_copies(refs[:n], refs[n:2 * n], refs[2 * n:], gather)
        for cp in copies:
            cp.start()
        for cp in copies:
            cp.wait()

    return pl.pallas_call(
        body, name=name, in_specs=[_HBM] * n, out_specs=[_HBM] * n, out_shape=_exchange_shapes(arrs, gather),
        scratch_shapes=_exchange_sems(n),
    )(*arrs)


def _exchange_shapes(arrs, gather):
    return [jax.ShapeDtypeStruct(((N_CHIPS,) + a.shape) if gather else a.shape, a.dtype) for a in arrs]


def _exchange_sems(n):
    return [pltpu.SemaphoreType.DMA((3 * n,)), pltpu.SemaphoreType.DMA((3 * n,)), pltpu.SemaphoreType.DMA((n,))]


def _exchange_copies(ins, outs, sems, gather):
    send_sems, recv_sems, own_sems = sems
    x, y, c = lax.axis_index("x"), lax.axis_index("y"), lax.axis_index("c")
    me = 2 * x + y
    copies = []
    for a in range(len(ins)):
        copies.append(pltpu.make_async_copy(ins[a] if gather else ins[a].at[me], outs[a].at[me], own_sems.at[a]))
        for k, (dx, dy) in enumerate(_OTHER_CHIPS):
            px, py = _flip(x, dx), _flip(y, dy)
            copies.append(pltpu.make_async_remote_copy(
                src_ref=ins[a] if gather else ins[a].at[2 * px + py], dst_ref=outs[a].at[me],
                send_sem=send_sems.at[3 * a + k], recv_sem=recv_sems.at[3 * a + k],
                device_id=(px, py, c), device_id_type=MESH))
    return copies


def _carrying(body, n_in, n_out, grid, comm):
    arrs, gather = comm
    n = len(arrs)

    def wrapped(*refs):
        c_in = refs[n_in:n_in + n]
        c_out = refs[n_in + n + n_out:n_in + 2 * n + n_out]
        ids = [pl.program_id(a) for a in range(len(grid))]
        first = functools.reduce(jnp.logical_and, [i == 0 for i in ids])
        last = functools.reduce(jnp.logical_and, [i == g - 1 for i, g in zip(ids, grid)])

        @pl.when(first)
        def _():
            for cp in _exchange_copies(c_in, c_out, refs[-3:], gather):
                cp.start()

        body(*refs[:n_in], *refs[n_in + n:n_in + n + n_out], *refs[n_in + 2 * n + n_out:-3])

        @pl.when(last)
        def _():
            for cp in _exchange_copies(c_in, c_out, refs[-3:], gather):
                cp.wait()

    return wrapped


def _swap_cores(arrs, *, name):
    n = len(arrs)

    def body(*refs):
        ins, outs = refs[:n], refs[n:2 * n]
        send_sems, recv_sems = refs[2 * n:]
        peer = (lax.axis_index("x"), lax.axis_index("y"), 1 - lax.axis_index("c"))
        cps = [pltpu.make_async_remote_copy(src_ref=ins[a], dst_ref=outs[a], send_sem=send_sems.at[a],
                                            recv_sem=recv_sems.at[a], device_id=peer, device_id_type=MESH) for a in range(n)]
        for cp in cps:
            cp.start()
        for cp in cps:
            cp.wait()

    return pl.pallas_call(
        body, name=name, in_specs=[_HBM] * n, out_specs=[_HBM] * n,
        out_shape=[jax.ShapeDtypeStruct(a.shape, a.dtype) for a in arrs],
        scratch_shapes=[pltpu.SemaphoreType.DMA((n,)), pltpu.SemaphoreType.DMA((n,))],
    )(*arrs)


def _allreduce_small(pack, *, name):
    R, C = pack.shape

    def body(p_ref, o_ref, recv, send_sems, recv_sems):
        x, y, c = lax.axis_index("x"), lax.axis_index("y"), lax.axis_index("c")
        me = 4 * x + 2 * y + c
        recv[me] = p_ref[...]
        cps = []
        for k in range(1, N_DEV):
            peer = (_flip(x, k & 4), _flip(y, k & 2), _flip(c, k & 1))
            cp = pltpu.make_async_remote_copy(src_ref=p_ref, dst_ref=recv.at[me], send_sem=send_sems.at[k - 1],
                                              recv_sem=recv_sems.at[k - 1], device_id=peer, device_id_type=MESH)
            cp.start()
            cps.append(cp)
        for cp in cps:
            cp.wait()
        acc = recv[0]
        for s in range(1, N_DEV):
            acc = acc + recv[s]
        o_ref[...] = acc

    vm = pl.BlockSpec(memory_space=pltpu.VMEM)
    return pl.pallas_call(
        body, name=name, in_specs=[vm], out_specs=vm, out_shape=jax.ShapeDtypeStruct((R, C), F32),
        scratch_shapes=[pltpu.VMEM((N_DEV, R, C), F32), pltpu.SemaphoreType.DMA((N_DEV - 1,)), pltpu.SemaphoreType.DMA((N_DEV - 1,))],
    )(pack)


def _sum_slots(a, *, name):
    S, R, C = a.shape
    tr = _pick(R, (256, 128, 64, 32, 16, 8))

    def body(a_ref, o_ref):
        acc = a_ref[0].astype(F32)
        for s in range(1, S):
            acc = acc + a_ref[s].astype(F32)
        o_ref[...] = acc

    return pl.pallas_call(
        body, name=name, grid=(R // tr,), in_specs=[pl.BlockSpec((S, tr, C), lambda i: (0, i, 0))],
        out_specs=pl.BlockSpec((tr, C), lambda i: (i, 0)), out_shape=jax.ShapeDtypeStruct((R, C), F32),
        compiler_params=_cparams("parallel"),
    )(a)


def _adamw(w, m, v, gs, *, name):
    R, C = w.shape
    tile = _pick(R, (128, 96, 64, 32, 16, 8))

    def fn(w_, m_, v_, *g_):
        g = g_[0]
        for e in g_[1:]:
            g = g + e
        m2 = ADAM_B1 * m_ + (1.0 - ADAM_B1) * g
        v2 = ADAM_B2 * v_ + (1.0 - ADAM_B2) * jnp.square(g)
        m_hat = m2 / (1.0 - ADAM_B1 ** ADAM_STEP)
        v_hat = v2 / (1.0 - ADAM_B2 ** ADAM_STEP)
        delta = -ADAM_LR * (m_hat / (jnp.sqrt(v_hat) + ADAM_EPS) + ADAM_WD * w_)
        return g, delta, m2, v2

    return _rowwise(fn, [w, m, v, *gs], [], [(C, F32)] * 4, tile=tile, name=name)


_ARGS = "x, p, attn_norm_g, w_in, shift_mu, w0, w2, a0, a2, g2, k_k, k_a, r_k, lnx_g, lnx_b, q_norm_g, k_norm_g, fgate_b, w_out, ffn_norm_g, w_gate, w_up, w_down, ple_proj, ple_norm_g, ple_gate_norm_g, ple_gate_w, ple_gate_b, loss_target, m_attn_norm_g, m_w_in, m_shift_mu, m_w0, m_w2, m_a0, m_a2, m_g2, m_k_k, m_k_a, m_r_k, m_lnx_g, m_lnx_b, m_q_norm_g, m_k_norm_g, m_fgate_b, m_w_out, m_ffn_norm_g, m_w_gate, m_w_up, m_w_down, m_ple_proj, m_ple_norm_g, m_ple_gate_norm_g, m_ple_gate_w, m_ple_gate_b, v_attn_norm_g, v_w_in, v_shift_mu, v_w0, v_w2, v_a0, v_a2, v_g2, v_k_k, v_k_a, v_r_k, v_lnx_g, v_lnx_b, v_q_norm_g, v_k_norm_g, v_fgate_b, v_w_out, v_ffn_norm_g, v_w_gate, v_w_up, v_w_down, v_ple_proj, v_ple_norm_g, v_ple_gate_norm_g, v_ple_gate_w, v_ple_gate_b".split(", ")
_WEIGHTS = _ARGS[2:28]
_COL_SHARDED = ("w_in", "w2", "a2", "g2", "w_gate", "w_up", "ple_proj")
_ROW_SHARDED = ("w_out", "w_down", "ple_gate_w")
_MATRICES = _COL_SHARDED + _ROW_SHARDED
_EARLY = ("w_in", "w2", "a2", "g2")
_LATE = tuple(n for n in _MATRICES if n not in _EARLY)
_VECTORS = tuple(n for n in _WEIGHTS if n not in _MATRICES)


def _whole(name, g):
    if name in _COL_SHARDED:
        return g.transpose(1, 0, 2).reshape(g.shape[1], -1)
    return g.reshape(-1, g.shape[2])


def _pieces(name, a):
    if name in _COL_SHARDED:
        return a.reshape(a.shape[0], N_CHIPS, -1).transpose(1, 0, 2)
    return a.reshape(N_CHIPS, -1, a.shape[1])


def kernel(x, p, attn_norm_g, w_in, shift_mu, w0, w2, a0, a2, g2, k_k, k_a, r_k, lnx_g, lnx_b, q_norm_g, k_norm_g, fgate_b, w_out, ffn_norm_g, w_gate, w_up, w_down, ple_proj, ple_norm_g, ple_gate_norm_g, ple_gate_w, ple_gate_b, loss_target, m_attn_norm_g, m_w_in, m_shift_mu, m_w0, m_w2, m_a0, m_a2, m_g2, m_k_k, m_k_a, m_r_k, m_lnx_g, m_lnx_b, m_q_norm_g, m_k_norm_g, m_fgate_b, m_w_out, m_ffn_norm_g, m_w_gate, m_w_up, m_w_down, m_ple_proj, m_ple_norm_g, m_ple_gate_norm_g, m_ple_gate_w, m_ple_gate_b, v_attn_norm_g, v_w_in, v_shift_mu, v_w0, v_w2, v_a0, v_a2, v_g2, v_k_k, v_k_a, v_r_k, v_lnx_g, v_lnx_b, v_q_norm_g, v_k_norm_g, v_fgate_b, v_w_out, v_ffn_norm_g, v_w_gate, v_w_up, v_w_down, v_ple_proj, v_ple_norm_g, v_ple_gate_norm_g, v_ple_gate_w, v_ple_gate_b):
    A = dict(zip(_ARGS, (x, p, attn_norm_g, w_in, shift_mu, w0, w2, a0, a2, g2, k_k, k_a, r_k, lnx_g, lnx_b, q_norm_g, k_norm_g, fgate_b, w_out, ffn_norm_g, w_gate, w_up, w_down, ple_proj, ple_norm_g, ple_gate_norm_g, ple_gate_w, ple_gate_b, loss_target, m_attn_norm_g, m_w_in, m_shift_mu, m_w0, m_w2, m_a0, m_a2, m_g2, m_k_k, m_k_a, m_r_k, m_lnx_g, m_lnx_b, m_q_norm_g, m_k_norm_g, m_fgate_b, m_w_out, m_ffn_norm_g, m_w_gate, m_w_up, m_w_down, m_ple_proj, m_ple_norm_g, m_ple_gate_norm_g, m_ple_gate_w, m_ple_gate_b, v_attn_norm_g, v_w_in, v_shift_mu, v_w0, v_w2, v_a0, v_a2, v_g2, v_k_k, v_k_a, v_r_k, v_lnx_g, v_lnx_b, v_q_norm_g, v_k_norm_g, v_fgate_b, v_w_out, v_ffn_norm_g, v_w_gate, v_w_up, v_w_down, v_ple_proj, v_ple_norm_g, v_ple_gate_norm_g, v_ple_gate_w, v_ple_gate_b)))
    x, p, tgt = A["x"][0], A["p"][0, 0], A["loss_target"][0]
    d = _Dims(W=A["w0"].shape[-1], DL=A["w2"].shape[1], AL=A["a2"].shape[1], GL=A["g2"].shape[1], FH=A["fgate_b"].shape[-1])

    shard = lambda n: A[n][0].astype(BF16)
    gathered = _exchange([shard(n) for n in _EARLY], gather=True, name="gather_early")
    full = {n: _whole(n, g) for n, g in zip(_EARLY, gathered)}
    Wt = {"w_in_r": d.pad_r(full["w_in"][:, :d.RC]), "w_in_f": d.pad_f(full["w_in"][:, d.RC:]),
          "w2": _padr(full["w2"], d.DLp), "a2": _padr(full["a2"], d.ALp), "g2": _padr(full["g2"], d.GLp)}
    vec = {n: A[n].reshape(-1, A[n].shape[-1]) for n in _VECTORS}

    loss, dx, G, gv, recv = _local_step(x, p, tgt, Wt, vec, d, late_shards=[shard(n) for n in _LATE])

    gw = {"w_in": jnp.concatenate([d.unpad_r(G["w_in_r"]), d.unpad_f(G["w_in_f"])], axis=1),
          "w2": G["w2"][:d.DL], "a2": G["a2"][:d.AL], "g2": G["g2"][:d.GL]}
    recv.update(zip(_EARLY, _exchange([_pieces(n, gw[n]).astype(BF16) for n in _EARLY], gather=False, name="scatter_early")))
    part = [_sum_slots(recv[n], name="sum_" + n) for n in _MATRICES]
    sib = _swap_cores(part, name="swap_cores")

    sizes = [1] + [A[n].size for n in _VECTORS]
    rows = _rup(_rup(sum(sizes), LANE) // LANE, 8)

    def pack(items):
        flat = jnp.concatenate([i.reshape(-1) for i in items])
        return jnp.pad(flat, (0, rows * LANE - flat.shape[0])).reshape(rows, LANE)

    red = _allreduce_small(pack([loss[0, :1]] + [gv[n] for n in _VECTORS]), name="allreduce_vectors")
    zero = jnp.zeros((1,), F32)
    upd = _adamw(pack([zero] + [A[n] for n in _VECTORS]), pack([zero] + [A["m_" + n] for n in _VECTORS]),
                 pack([zero + 1.0] + [A["v_" + n] for n in _VECTORS]), [red], name="adamw_vectors")
    offs = [0]
    for s in sizes:
        offs.append(offs[-1] + s)
    unpack = lambda a, i, n: a.reshape(-1)[offs[i + 1]:offs[i + 2]].reshape(A[n].shape)

    out = {"grad": {}, "delta": {}, "new_m": {}, "new_v": {}}
    for i, n in enumerate(_VECTORS):
        for kind, a in zip(out, upd):
            out[kind][n] = unpack(a, i, n)
    for n, mine, other in zip(_MATRICES, part, sib):
        res = _adamw(A[n][0], A["m_" + n][0], A["v_" + n][0], [mine, other], name="adamw_" + n)
        for kind, a in zip(out, res):
            out[kind][n] = a[None]
    return (red[0, 0], dx[None], *[out[k][n] for k in out for n in _WEIGHTS])
```

```python
import functools

import jax
import jax.numpy as jnp
from jax import lax
from jax.experimental import pallas as pl
from jax.experimental.pallas import tpu as pltpu

F32 = jnp.float32
BF16 = jnp.bfloat16
LANE = 128
HEAD = 64
RMS_EPS = 1e-6
GN_EPS = 64e-5
ADAM_LR, ADAM_B1, ADAM_B2, ADAM_EPS, ADAM_WD, ADAM_STEP = 0.001, 0.9, 0.999, 1e-08, 0.01, 10
VMEM_LIMIT = 56 * 1024 * 1024
MM_TILE_BYTES = 40 * 1024 * 1024
NEG = -1e30
MESH = pl.DeviceIdType.MESH
N_CHIPS = 4
N_DEV = 8


def _rup(n, m):
    return -(-n // m) * m


def _pick(n, cands):
    for c in cands:
        if n % c == 0:
            return c
    return n


def _cparams(*sem):
    return pltpu.CompilerParams(dimension_semantics=sem, vmem_limit_bytes=VMEM_LIMIT)


def _mm(a, b, *, ta=False, tb=False, add=None, out_dtype=F32, name):
    M, K = (a.shape[1], a.shape[0]) if ta else a.shape
    N = b.shape[0] if tb else b.shape[1]
    tn = _pick(N, (512, 640, 256, 128))
    fits = lambda m, t: 2 * (m * t * a.dtype.itemsize + t * tn * b.dtype.itemsize + m * tn * 8) <= MM_TILE_BYTES
    tm, tk = next((m, t) for t in (K, 2048, 1024, 512, 640, 256, 128) for m in (1024, 512, 256, 128, M)
                  if K % t == 0 and M % m == 0 and fits(m, t))
    nk = K // tk
    dn = (((0 if ta else 1,), (1 if tb else 0,)), ((), ()))

    def body(*refs):
        if add is None:
            a_ref, b_ref, o_ref, acc = refs
        else:
            a_ref, b_ref, add_ref, o_ref, acc = refs
        ks = pl.program_id(2)
        part = lax.dot_general(a_ref[...].astype(BF16), b_ref[...].astype(BF16), dn, preferred_element_type=F32)
        if nk > 1:
            @pl.when(ks == 0)
            def _():
                acc[...] = jnp.zeros_like(acc)

            acc[...] += part

        @pl.when(ks == nk - 1)
        def _():
            res = acc[...] if nk > 1 else part
            if add is not None:
                res = res + add_ref[...].astype(F32)
            o_ref[...] = res.astype(out_dtype)

    a_spec = pl.BlockSpec((tk, tm), lambda i, j, k: (k, i)) if ta else pl.BlockSpec((tm, tk), lambda i, j, k: (i, k))
    b_spec = pl.BlockSpec((tn, tk), lambda i, j, k: (j, k)) if tb else pl.BlockSpec((tk, tn), lambda i, j, k: (k, j))
    o_spec = pl.BlockSpec((tm, tn), lambda i, j, k: (i, j))
    ins, specs = [a, b], [a_spec, b_spec]
    if add is not None:
        ins.append(add)
        specs.append(o_spec)
    return pl.pallas_call(
        body, name=name, grid=(M // tm, N // tn, nk), in_specs=specs, out_specs=o_spec,
        out_shape=jax.ShapeDtypeStruct((M, N), out_dtype),
        scratch_shapes=[pltpu.VMEM((tm, tn) if nk > 1 else (8, LANE), F32)],
        compiler_params=_cparams("parallel", "parallel", "arbitrary"),
    )(*ins)


def _rowwise(fn, rows, consts, outs, accs=(), *, tile, name, prev=(), nxt=()):
    paired = [not isinstance(r, tuple) and r.ndim == 3 for r in rows]
    rows = [r if isinstance(r, tuple) else (r, r.shape[-1], 0) for r in rows]
    T = rows[0][0].shape[-2]
    tile = min(tile, T)
    n = T // tile
    sub = 8
    nr, npv, nnx, ncst, no, na = len(rows), len(prev), len(nxt), len(consts), len(outs), len(accs)
    out_paired = [len(o) == 3 for o in outs]

    def body(*refs):
        i = pl.program_id(0)
        it = iter(refs)
        rv = [next(it) for _ in range(nr)]
        rv = [jnp.concatenate([r[g] for g in range(r.shape[0])], axis=1) if pr else r[...] for r, pr in zip(rv, paired)]
        pv = [jnp.where(i > 0, next(it)[sub - 1:sub, :], 0.0) for _ in range(npv)]
        nv = [jnp.where(i < n - 1, next(it)[0:1, :], 0.0) for _ in range(nnx)]
        cv = [next(it)[...] for _ in range(ncst)]
        o_refs = [next(it) for _ in range(no)]
        a_refs = [next(it) for _ in range(na)]
        res = fn(*rv, *pv, *nv, *cv)
        if not isinstance(res, (tuple, list)):
            res = (res,)
        for r, o, pr in zip(o_refs, res[:no], out_paired):
            if pr:
                for g in range(r.shape[0]):
                    r[g] = o[:, g * LANE:(g + 1) * LANE].astype(r.dtype)
            else:
                r[...] = o.astype(r.dtype)
        if na:
            @pl.when(i == 0)
            def _():
                for r in a_refs:
                    r[...] = jnp.zeros_like(r)
            for r, o in zip(a_refs, res[no:]):
                r[...] += o.astype(F32)

    in_specs = [pl.BlockSpec((a.shape[0], tile, w), lambda i: (0, i, 0)) if pr else
                pl.BlockSpec((tile, w), functools.partial(lambda cb, i: (i, cb), cb)) for (a, w, cb), pr in zip(rows, paired)]
    in_specs += [pl.BlockSpec((sub, a.shape[1]), lambda i: (jnp.maximum(i * (tile // sub) - 1, 0), 0)) for a in prev]
    in_specs += [pl.BlockSpec((sub, a.shape[1]), lambda i: (jnp.minimum((i + 1) * (tile // sub), T // sub - 1), 0)) for a in nxt]
    in_specs += [pl.BlockSpec(c.shape, lambda i: (0, 0)) for c in consts]
    out_specs = [pl.BlockSpec((o[0] // LANE, tile, LANE), lambda i: (0, i, 0)) if pr else
                 pl.BlockSpec((tile, o[0]), lambda i: (i, 0)) for o, pr in zip(outs, out_paired)]
    out_specs += [pl.BlockSpec(s, lambda i: (0, 0)) for s in accs]
    out_shape = [jax.ShapeDtypeStruct((o[0] // LANE, T, LANE) if pr else (T, o[0]), o[1]) for o, pr in zip(outs, out_paired)]
    out_shape += [jax.ShapeDtypeStruct(s, F32) for s in accs]
    res = pl.pallas_call(
        body, name=name, grid=(n,), in_specs=in_specs, out_specs=out_specs, out_shape=out_shape,
        compiler_params=_cparams("arbitrary"),
    )(*[r[0] for r in rows], *prev, *nxt, *consts)
    return res


@jax.custom_vjp
def _bdot(a, b):
    return jnp.dot(a.astype(BF16), b.astype(BF16), preferred_element_type=F32)


def _bdot_fwd(a, b):
    return _bdot(a, b), (a.astype(BF16), b.astype(BF16))


def _bdot_bwd(res, ct):
    a, b = res
    c = ct.astype(BF16)
    return (lax.dot_general(c, b, (((1,), (1,)), ((), ())), preferred_element_type=F32),
            lax.dot_general(a, c, (((0,), (0,)), ((), ())), preferred_element_type=F32))


_bdot.defvjp(_bdot_fwd, _bdot_bwd)


def _split3(x):
    hi = x.astype(BF16)
    r1 = x - hi.astype(F32)
    mid = r1.astype(BF16)
    return hi, mid, (r1 - mid.astype(F32)).astype(BF16)


def _dot_exact(a, b):
    return sum(jnp.dot(p, b, preferred_element_type=F32) for p in _split3(a))


@jax.custom_vjp
def _head_sums(x, seg, segt):
    return _dot_exact(_dot_exact(x, seg), segt)


def _head_sums_fwd(x, seg, segt):
    return _head_sums(x, seg, segt), (seg, segt)


def _head_sums_bwd(res, ct):
    seg, segt = res
    return _head_sums(ct, seg, segt), jnp.zeros_like(seg), jnp.zeros_like(segt)


_head_sums.defvjp(_head_sums_fwd, _head_sums_bwd)


def _rms(x, g, eps=RMS_EPS):
    return x * lax.rsqrt(jnp.mean(x * x, axis=-1, keepdims=True) + eps) * g


def _softplus(x):
    return jnp.maximum(x, 0.0) + jnp.log(1.0 + jnp.exp(-jnp.abs(x)))


def _sigmoid(x):
    return 1.0 / (1.0 + jnp.exp(-x))


def _seg_mats(width):
    h = lax.broadcasted_iota(jnp.int32, (width, LANE), 0) // HEAD
    j = lax.broadcasted_iota(jnp.int32, (width, LANE), 1)
    seg = (h == j).astype(BF16)
    return seg, seg.T


def _prep_rwkv(um, w0, w2, a0, a2, g2, k_k, k_a, seg, segt, *, dims):
    W, DLp, ALp, GLp = dims
    r, k, v = um[:, :W], um[:, W:2 * W], um[:, 2 * W:3 * W]
    o = 3 * W
    xw, xa, xg = um[:, o:o + DLp], um[:, o + DLp:o + DLp + ALp], um[:, o + DLp + ALp:o + DLp + ALp + GLp]
    w_log = -_softplus(-(w0 + _bdot(jnp.tanh(xw), w2))) - 0.5
    decay = jnp.exp(-jnp.exp(w_log))
    a = _sigmoid(a0 + _bdot(xa, a2))
    g = _bdot(_sigmoid(xg), g2)
    kk = k * k_k
    nrm = jnp.sqrt(_head_sums(kk * kk, seg, segt))
    kk = kk / jnp.maximum(nrm, 1e-12)
    k2 = k * (1.0 + (a - 1.0) * k_a)
    return r, decay, k2, v, kk, kk * a, g


def _shift_mix(u, uprev, mu):
    first = lax.broadcasted_iota(jnp.int32, u.shape, 0) == 0
    sh = jnp.where(first, uprev, pltpu.roll(u, 1, 0))
    return u + (sh - u) * mu, sh


def _post_rwkv(y, r, k2, v, g, lnx_g, lnx_b, r_k, seg, segt):
    inv = 1.0 / HEAD
    mean = _head_sums(y, seg, segt) * inv
    yc = y - mean
    var = _head_sums(yc * yc, seg, segt) * inv
    yn = yc * lax.rsqrt(var + GN_EPS) * lnx_g + lnx_b
    bonus = _head_sums(r * k2 * r_k, seg, segt) * v
    return (yn + bonus) * g


def _prep_fox(uf, qg, kg, fb, seg, segt, *, dims):
    FW, FHp = dims
    q, k, v, f = uf[:, :FW], uf[:, FW:2 * FW], uf[:, 2 * FW:3 * FW], uf[:, 3 * FW:3 * FW + FHp]
    inv = 1.0 / HEAD
    qn = q * lax.rsqrt(_head_sums(q * q, seg, segt) * inv + RMS_EPS) * qg * (HEAD ** -0.5)
    kn = k * lax.rsqrt(_head_sums(k * k, seg, segt) * inv + RMS_EPS) * kg
    return qn, kn, v, -_softplus(-(f + fb))


def _tail(h2, pe, z, png, pgb):
    return h2 + _sigmoid(z + pgb) * _rms(pe, png)


def _swiglu(gt, up):
    return gt * _sigmoid(gt) * up


def _cumsum(x, *, reverse, name):
    T, C = x.shape
    tc = _pick(T, (256, 128))
    n = T // tc
    i0 = lax.broadcasted_iota(jnp.int32, (tc, tc), 0)
    i1 = lax.broadcasted_iota(jnp.int32, (tc, tc), 1)
    tri = ((i0 <= i1) if reverse else (i0 >= i1)).astype(BF16)

    def body(x_ref, tri_ref, o_ref, carry):
        i = pl.program_id(0)

        @pl.when(i == 0)
        def _():
            carry[...] = jnp.zeros_like(carry)

        v = x_ref[...]
        hi = v.astype(BF16)
        r1 = v - hi.astype(F32)
        mid = r1.astype(BF16)
        lo = (r1 - mid.astype(F32)).astype(BF16)
        t = tri_ref[...]
        d = lambda p: jnp.dot(t, p, preferred_element_type=F32)
        c = d(hi) + d(mid) + d(lo) + carry[0:1, :]
        o_ref[...] = c
        edge = c[0:1, :] if reverse else c[tc - 1:tc, :]
        carry[...] = jnp.broadcast_to(edge, carry.shape)

    blk = pl.BlockSpec((tc, C), (lambda i: (n - 1 - i, 0)) if reverse else (lambda i: (i, 0)))
    return pl.pallas_call(
        body, name=name, grid=(n,), in_specs=[blk, pl.BlockSpec((tc, tc), lambda i: (0, 0))], out_specs=blk,
        out_shape=jax.ShapeDtypeStruct((T, C), F32), scratch_shapes=[pltpu.VMEM((8, C), F32)],
        compiler_params=_cparams("arbitrary"),
    )(x, tri)


BWD_PAIRS_PER_TRIP = 8
BWD_STEPS_PER_TRIP = 4
BWD_LOAD_PARTS = 4
FWD_STEPS_PER_TRIP = 8


def _steps(n, per_trip, step, start=0):
    def trip(i, carry):
        for j in range(per_trip):
            carry = step(start + i * per_trip + j, carry)
        return carry

    lax.fori_loop(0, n // per_trip, trip, 0)


def _col(tile, lane, t):
    return jnp.sum(jnp.where(lane == t, tile, 0.0), axis=1, keepdims=True)


def _halves(x, left):
    a = jnp.sum(jnp.where(left, x, 0.0), axis=1, keepdims=True)
    b = jnp.sum(jnp.where(left, 0.0, x), axis=1, keepdims=True)
    return a, b, jnp.where(left, a, b)


def _pair_col(ref, p, lane, left, t):
    return jnp.where(left, _col(ref[2 * p], lane, t), _col(ref[2 * p + 1], lane, t))


def _scan_fwd(r, w, k, kk, kka, vT, *, hg, tc, comm=None):
    H, N, T = vT.shape
    nc = T // tc
    hp = hg // 2
    L = 2 * N
    half = tc // 2

    def body(r_ref, w_ref, k_ref, kk_ref, kka_ref, vT_ref, yT_ref, chk_ref, st_hbm, s_ref, stage, st_sems):
        g, c = pl.program_id(0), pl.program_id(1)

        @pl.when(c == 0)
        def _():
            s_ref[...] = jnp.zeros_like(s_ref)

        def flush(hf, chunk):
            return pltpu.make_async_copy(stage.at[:, pl.ds(hf * half, half)],
                                         st_hbm.at[pl.ds(g * hp, hp), pl.ds(chunk * tc + hf * half, half)], st_sems.at[hf])

        chk_ref[:, 0] = s_ref[...]
        yT_ref[...] = jnp.zeros_like(yT_ref)
        lane = lax.broadcasted_iota(jnp.int32, (N, tc), 1)
        left = lax.broadcasted_iota(jnp.int32, (N, L), 1) < N

        def emit_y(S, p, t):
            ya, yb, _ = _halves(S * r_ref[p, pl.ds(jnp.maximum(t, 0), 1), :], left)
            yT_ref[2 * p] = jnp.where(lane == t, ya, yT_ref[2 * p])
            yT_ref[2 * p + 1] = jnp.where(lane == t, yb, yT_ref[2 * p + 1])

        def step(t, carry):
            for p in range(hp):
                row = lambda ref: ref[p, pl.ds(t, 1), :]
                S = s_ref[p]
                emit_y(S, p, t - 1)
                sa = _halves(S * row(kk_ref), left)[2]
                new = S * row(w_ref) - sa * row(kka_ref) + _pair_col(vT_ref, p, lane, left, t) * row(k_ref)
                s_ref[p] = new
                stage[p, t] = new
            return carry

        for hf in range(2):
            @pl.when(c > 0)
            def _():
                flush(hf, c - 1).wait()

            _steps(half, FWD_STEPS_PER_TRIP, step, start=hf * half)
            flush(hf, c).start()
        for p in range(hp):
            emit_y(s_ref[p], p, tc - 1)

        @pl.when(c == nc - 1)
        def _():
            flush(0, c).wait()
            flush(1, c).wait()

    rows = pl.BlockSpec((hp, tc, L), lambda g, c: (g, c, 0))
    cols = pl.BlockSpec((hg, N, tc), lambda g, c: (g, 0, c))
    return _call_carrying(
        body, "scan_fwd", (H // hg, nc), [rows] * 5 + [cols],
        [cols, pl.BlockSpec((hp, 1, N, L), lambda g, c: (g, c, 0, 0)), _HBM],
        [jax.ShapeDtypeStruct((H, N, T), F32), jax.ShapeDtypeStruct((H // 2, nc, N, L), F32),
         jax.ShapeDtypeStruct((H // 2, T, N, L), F32)],
        [pltpu.VMEM((hp, N, L), F32), pltpu.VMEM((hp, tc, N, L), F32), pltpu.SemaphoreType.DMA((2,))],
        [r, w, k, kk, kka, vT], comm)


def _scan_bwd(r, w, k, kk, kka, vT, dyT, chk, states, *, hg, tc, comm=None):
    H, N, T = vT.shape
    nc = T // tc
    hp = hg // 2
    L = 2 * N
    part = tc // BWD_LOAD_PARTS

    def body(r_ref, w_ref, k_ref, kk_ref, kka_ref, vT_ref, dyT_ref, chk_ref, st_hbm,
             dr_ref, dw_ref, dk_ref, dkk_ref, dkka_ref, dvT_ref, sp_ref, ds_ref, ld_sems):
        g, chunk = pl.program_id(0), nc - 1 - pl.program_id(1)

        def load(i):
            lo = max(1, i * part)
            n = (tc + 1 if i == BWD_LOAD_PARTS - 1 else (i + 1) * part) - lo
            return pltpu.make_async_copy(st_hbm.at[pl.ds(g * hp, hp), pl.ds(chunk * tc + lo - 1, n)],
                                         sp_ref.at[:, pl.ds(lo, n)], ld_sems.at[i])

        for i in reversed(range(BWD_LOAD_PARTS)):
            load(i).start()

        @pl.when(pl.program_id(1) == 0)
        def _():
            ds_ref[...] = jnp.zeros_like(ds_ref)

        dvT_ref[...] = jnp.zeros_like(dvT_ref)
        lane = lax.broadcasted_iota(jnp.int32, (N, tc), 1)
        left = lax.broadcasted_iota(jnp.int32, (N, L), 1) < N
        left_row = lax.broadcasted_iota(jnp.int32, (1, L), 1) < N

        halves = _halves
        pair_col = lambda ref, p, t: _pair_col(ref, p, lane, left, t)
        for p in range(hp):
            sp_ref[p, 0] = chk_ref[p, 0]

        def bstep(p0, last, i, carry):
            for j in range(BWD_STEPS_PER_TRIP):
                back_one(p0, last - (i * BWD_STEPS_PER_TRIP + j))
            return carry

        def back_one(p0, t):
            for p in range(p0, min(p0 + BWD_PAIRS_PER_TRIP, hp)):
                row = lambda ref: ref[p, pl.ds(t, 1), :]
                rr, wr, kr, kkr, kkar = row(r_ref), row(w_ref), row(k_ref), row(kk_ref), row(kka_ref)
                Sp = sp_ref[p, t]
                Sn = sp_ref[p, t + 1]
                dycol, vcol = pair_col(dyT_ref, p, t), pair_col(vT_ref, p, t)
                dS = ds_ref[p]
                dSn = dS + dycol * rr
                dsa = halves(dS * kkar, left)[2] + dycol * halves(rr * kkar, left_row)[2]
                dr_ref[p, pl.ds(t, 1), :] = jnp.sum(Sn * dycol, axis=0, keepdims=True)
                sa = halves(Sp * kkr, left)[2]
                dw_ref[p, pl.ds(t, 1), :] = jnp.sum(dSn * Sp, axis=0, keepdims=True)
                dkka_ref[p, pl.ds(t, 1), :] = -jnp.sum(dSn * sa, axis=0, keepdims=True)
                dva, dvb, _ = halves(dSn * kr, left)
                dk_ref[p, pl.ds(t, 1), :] = jnp.sum(dSn * vcol, axis=0, keepdims=True)
                dkk_ref[p, pl.ds(t, 1), :] = -jnp.sum(Sp * dsa, axis=0, keepdims=True)
                ds_ref[p] = dSn * wr - dsa * kkr
                dvT_ref[2 * p] = jnp.where(lane == t, dva, dvT_ref[2 * p])
                dvT_ref[2 * p + 1] = jnp.where(lane == t, dvb, dvT_ref[2 * p + 1])

        for i in reversed(range(BWD_LOAD_PARTS)):
            load(i).wait()
            for p0 in range(0, hp, BWD_PAIRS_PER_TRIP):
                lax.fori_loop(0, part // BWD_STEPS_PER_TRIP, functools.partial(bstep, p0, (i + 1) * part - 1), 0)

    rows = pl.BlockSpec((hp, tc, L), lambda g, c: (g, nc - 1 - c, 0))
    cols = pl.BlockSpec((hg, N, tc), lambda g, c: (g, 0, nc - 1 - c))
    return _call_carrying(
        body, "scan_bwd", (H // hg, nc),
        [rows] * 5 + [cols, cols, pl.BlockSpec((hp, 1, N, L), lambda g, c: (g, nc - 1 - c, 0, 0)), _HBM],
        [rows] * 5 + [cols],
        [jax.ShapeDtypeStruct((H // 2, T, L), F32)] * 5 + [jax.ShapeDtypeStruct((H, N, T), F32)],
        [pltpu.VMEM((hp, tc + 1, N, L), F32), pltpu.VMEM((hp, N, L), F32), pltpu.SemaphoreType.DMA((BWD_LOAD_PARTS,))],
        [r, w, k, kk, kka, vT, dyT, chk, states], comm)


_NT = (((1,), (1,)), ((), ()))
_TN = (((0,), (0,)), ((), ()))


def _scores(q, k, cc, cr, masked):
    s = lax.dot_general(q, k, _NT, preferred_element_type=F32) + cc - cr
    if masked:
        tb = s.shape[0]
        keep = lax.broadcasted_iota(jnp.int32, (tb, tb), 0) >= lax.broadcasted_iota(jnp.int32, (tb, tb), 1)
        s = jnp.where(keep, s, NEG)
    return s


def _attn_specs(T, N, tb):
    blk = pl.BlockSpec((1, tb, N), lambda h, i: (h, i, 0))
    whole = pl.BlockSpec((1, T, N), lambda h, i: (h, 0, 0))
    col = pl.BlockSpec((1, tb, 1), lambda h, i: (h, i, 0))
    wcol = pl.BlockSpec((1, T, 1), lambda h, i: (h, 0, 0))
    row = pl.BlockSpec((1, 1, tb), lambda h, i: (h, 0, i))
    wrow = pl.BlockSpec((1, 1, T), lambda h, i: (h, 0, 0))
    return blk, whole, col, wcol, row, wrow


def _call_carrying(body, name, grid, in_specs, out_specs, out_shape, scratch, args, comm):
    n_out = len(out_specs)
    if comm is not None:
        n = len(comm[0])
        body = _carrying(body, len(in_specs), n_out, grid, comm)
        in_specs, out_specs = in_specs + [_HBM] * n, out_specs + [_HBM] * n
        out_shape, scratch, args = out_shape + _exchange_shapes(*comm), scratch + _exchange_sems(n), args + list(comm[0])
    res = pl.pallas_call(
        body, name=name, grid=grid, in_specs=in_specs, out_specs=out_specs, out_shape=out_shape,
        scratch_shapes=scratch, compiler_params=_cparams(*["arbitrary"] * len(grid)),
    )(*args)
    return res[:n_out], res[n_out:]


def _attn_fwd(q, k, v, ccol, crow, *, tb, comm=None):
    H, T, N = q.shape

    def body(q_ref, k_ref, v_ref, cc_ref, cr_ref, o_ref, lse_ref, m_s, l_s, acc_s):
        qi = pl.program_id(1)
        m_s[...] = jnp.full_like(m_s, NEG)
        l_s[...] = jnp.zeros_like(l_s)
        acc_s[...] = jnp.zeros_like(acc_s)
        q_, cc = q_ref[0], cc_ref[0]

        def block(j, masked):
            at = pl.ds(pl.multiple_of(j * tb, tb), tb)
            s = _scores(q_, k_ref[0, at, :], cc, cr_ref[0, :, at], masked)
            m_new = jnp.maximum(m_s[...], jnp.max(s, axis=1, keepdims=True))
            p = jnp.exp(s - m_new)
            alpha = jnp.exp(m_s[...] - m_new)
            l_s[...] = alpha * l_s[...] + jnp.sum(p, axis=1, keepdims=True)
            acc_s[...] = alpha * acc_s[...] + jnp.dot(p.astype(BF16), v_ref[0, at, :], preferred_element_type=F32)
            m_s[...] = m_new

        def below(j, carry):
            block(j, False)
            return carry

        lax.fori_loop(0, qi, below, 0)
        block(qi, True)
        o_ref[0] = acc_s[...] / l_s[...]
        lse_ref[0] = m_s[...] + jnp.log(l_s[...])

    blk, whole, col, wcol, row, wrow = _attn_specs(T, N, tb)
    return _call_carrying(
        body, "fox_fwd", (H, T // tb), [blk, whole, whole, col, wrow], [blk, col],
        [jax.ShapeDtypeStruct((H, T, N), F32), jax.ShapeDtypeStruct((H, T, 1), F32)],
        [pltpu.VMEM((tb, 1), F32), pltpu.VMEM((tb, 1), F32), pltpu.VMEM((tb, N), F32)],
        [q, k, v, ccol, crow], comm)


def _attn_bwd(q, k, v, ccol, crow, o, lse, do, *, tb, comm=None):
    H, T, N = q.shape
    nb = T // tb

    def body(q_ref, k_ref, v_ref, cc_ref, cr_ref, o_ref, lse_ref, do_ref,
             dq_ref, dcq_ref, dk_ref, dv_ref, dck_ref, dq_s, dcq_s, dk_s, dv_s, dck_s):
        qi = pl.program_id(1)

        @pl.when(qi == 0)
        def _():
            dk_s[...] = jnp.zeros_like(dk_s)
            dv_s[...] = jnp.zeros_like(dv_s)
            dck_s[...] = jnp.zeros_like(dck_s)

        dq_s[...] = jnp.zeros_like(dq_s)
        dcq_s[...] = jnp.zeros_like(dcq_s)
        q_, cc, lse_, do_ = q_ref[0], cc_ref[0], lse_ref[0], do_ref[0]
        delta = jnp.sum(do_ * o_ref[0], axis=1, keepdims=True)
        dob = do_.astype(BF16)

        def block(j, masked):
            at = pl.ds(pl.multiple_of(j * tb, tb), tb)
            kb = k_ref[0, at, :]
            p = jnp.exp(_scores(q_, kb, cc, cr_ref[0, :, at], masked) - lse_)
            dp = lax.dot_general(dob, v_ref[0, at, :], _NT, preferred_element_type=F32)
            ds = p * (dp - delta)
            dsb = ds.astype(BF16)
            dq_s[...] += jnp.dot(dsb, kb, preferred_element_type=F32)
            dcq_s[...] += jnp.sum(ds, axis=1, keepdims=True)
            dv_s[at, :] += lax.dot_general(p.astype(BF16), dob, _TN, preferred_element_type=F32)
            dk_s[at, :] += lax.dot_general(dsb, q_, _TN, preferred_element_type=F32)
            dck_s[:, at] += jnp.sum(ds, axis=0, keepdims=True)

        def below(j, carry):
            block(j, False)
            return carry

        lax.fori_loop(0, qi, below, 0)
        block(qi, True)
        dq_ref[0] = dq_s[...]
        dcq_ref[0] = dcq_s[...]

        @pl.when(qi == nb - 1)
        def _():
            dk_ref[0] = dk_s[...]
            dv_ref[0] = dv_s[...]
            dck_ref[0] = dck_s[...]

    blk, whole, col, wcol, row, wrow = _attn_specs(T, N, tb)
    return _call_carrying(
        body, "fox_bwd", (H, nb), [blk, whole, whole, col, wrow, blk, col, blk], [blk, col, whole, whole, wrow],
        [jax.ShapeDtypeStruct((H, T, N), F32), jax.ShapeDtypeStruct((H, T, 1), F32), jax.ShapeDtypeStruct((H, T, N), F32),
         jax.ShapeDtypeStruct((H, T, N), F32), jax.ShapeDtypeStruct((H, 1, T), F32)],
        [pltpu.VMEM((tb, N), F32), pltpu.VMEM((tb, 1), F32), pltpu.VMEM((T, N), F32), pltpu.VMEM((T, N), F32),
         pltpu.VMEM((1, T), F32)],
        [q, k, v, ccol, crow, o, lse, do], comm)


def _heads(x):
    T = x.shape[0]
    return x.reshape(T, -1, HEAD).transpose(1, 0, 2)


def _headsT(x):
    T = x.shape[0]
    return x.reshape(T, -1, HEAD).transpose(1, 2, 0)


def _unheads(x):
    return x.transpose(1, 0, 2).reshape(x.shape[1], -1)


def _unheadsT(x):
    return x.transpose(2, 0, 1).reshape(x.shape[2], -1)


def _padc(x, n):
    return jnp.pad(x, ((0, 0), (0, n - x.shape[1])))


def _padr(x, n):
    return jnp.pad(x, ((0, n - x.shape[0]), (0, 0)))


class _Dims:
    def __init__(self, W, DL, AL, GL, FH):
        self.W, self.DL, self.AL, self.GL, self.FH = W, DL, AL, GL, FH
        self.DLp, self.ALp, self.GLp, self.FHp = _rup(DL, LANE), _rup(AL, LANE), _rup(GL, LANE), _rup(FH, LANE)
        self.FW = FH * HEAD
        self.RC = 3 * W + DL + AL + GL
        self.RP = 3 * W + self.DLp + self.ALp + self.GLp
        self.FC = 3 * self.FW + FH
        self.FP = 3 * self.FW + self.FHp

    def pad_r(self, a):
        W, o = self.W, 3 * self.W
        return jnp.concatenate([a[:, :o], _padc(a[:, o:o + self.DL], self.DLp),
                                _padc(a[:, o + self.DL:o + self.DL + self.AL], self.ALp),
                                _padc(a[:, o + self.DL + self.AL:self.RC], self.GLp)], axis=1)

    def unpad_r(self, a):
        o = 3 * self.W
        return jnp.concatenate([a[:, :o], a[:, o:o + self.DL], a[:, o + self.DLp:o + self.DLp + self.AL],
                                a[:, o + self.DLp + self.ALp:o + self.DLp + self.ALp + self.GL]], axis=1)

    def pad_f(self, a):
        return _padc(a, self.FP)

    def unpad_f(self, a):
        return a[:, :self.FC]


def _local_step(x, p, tgt, Wt, vec, d, late_shards=None):
    Wt = dict(Wt)
    T, D = x.shape
    W, FW = d.W, d.FW
    H = W // HEAD
    seg, segt = _seg_mats(W)
    segf, segft = _seg_mats(FW)
    T1 = 256
    rk_flat = vec["r_k"].reshape(1, W)
    mu = d.pad_r(vec["shift_mu"])
    qg = jnp.tile(vec["q_norm_g"], (1, d.FH))
    kg = jnp.tile(vec["k_norm_g"], (1, d.FH))
    fb = _padc(vec["fgate_b"], d.FHp)
    pdims = (W, d.DLp, d.ALp, d.GLp)
    fdims = (FW, d.FHp)

    (xn,) = _rowwise(lambda x_, g_: _rms(x_, g_), [x], [vec["attn_norm_g"]], [(D, BF16)], tile=T1, name="norm_attn")
    u_r = _mm(xn, Wt["w_in_r"], name="mm_in_r")
    u_f = _mm(xn, Wt["w_in_f"], name="mm_in_f")

    prep_consts = [mu, vec["w0"], Wt["w2"].astype(F32), vec["a0"], Wt["a2"].astype(F32), Wt["g2"].astype(F32), vec["k_k"], vec["k_a"], seg, segt]

    def prep_fwd(u_, up_, mu_, *cs):
        um, _ = _shift_mix(u_, up_, mu_)
        return _prep_rwkv(um, *cs, dims=pdims)

    pw, tw = (W, F32, "pairs"), (W, F32)
    r, dec, k2, v, kk, kka, g = _rowwise(prep_fwd, [u_r], prep_consts, [pw, pw, pw, tw, pw, pw, tw], tile=128,
                                         name="rwkv_prep", prev=[u_r])
    hg, tc = min(16, H), 128
    (yT, chk, states), gathered = _scan_fwd(r, dec, k2, kk, kka, _headsT(v), hg=hg, tc=tc,
                                            comm=late_shards and (late_shards, True))
    Wt.update({n: _whole(n, g) for n, g in zip(_LATE, gathered)})
    y = _unheadsT(yT)
    post_consts = [vec["lnx_g"], vec["lnx_b"], rk_flat, seg, segt]
    (y_r,) = _rowwise(_post_rwkv, [y, r, k2, v, g], post_consts, [(W, BF16)], tile=T1, name="rwkv_post")

    fox_consts = [qg, kg, fb, segf, segft]
    qn, kn, vf, logf = _rowwise(functools.partial(_prep_fox, dims=fdims), [u_f], fox_consts,
                                [(FW, BF16), (FW, BF16), (FW, BF16), (d.FHp, F32)], tile=T1, name="fox_prep")
    c = _cumsum(logf, reverse=False, name="fox_cumsum")
    cT = c[:, :d.FH].T
    ccol, crow = cT[:, :, None], cT[:, None, :]
    tb = _pick(T, (1024, 512, 256, 128))
    qh, kh, vh = _heads(qn), _heads(kn), _heads(vf)
    (o, lse), _ = _attn_fwd(qh, kh, vh, ccol, crow, tb=tb)
    y_f = _unheads(o)

    ycat = jnp.concatenate([y_r, y_f.astype(BF16)], axis=1)
    h1 = _mm(ycat, Wt["w_out"], add=x, name="mm_out")
    (hn,) = _rowwise(lambda h_, g_: _rms(h_, g_), [h1], [vec["ffn_norm_g"]], [(D, BF16)], tile=T1, name="norm_ffn")
    gt = _mm(hn, Wt["w_gate"], name="mm_gate")
    up = _mm(hn, Wt["w_up"], name="mm_up")
    (act,) = _rowwise(_swiglu, [gt, up], [], [(gt.shape[1], BF16)], tile=T1, name="swiglu")
    h2 = _mm(act, Wt["w_down"], add=h1, name="mm_down")
    (hg_,) = _rowwise(lambda h_, g_: _rms(h_, g_), [h2], [vec["ple_gate_norm_g"]], [(D, BF16)], tile=T1, name="norm_gate")
    pe = _mm(p, Wt["ple_proj"], name="mm_ple")
    z = _mm(hg_, Wt["ple_gate_w"], name="mm_pgate")

    def tail(h2_, pe_, z_, tg_, png_, pgb_):
        h3, f = jax.vjp(_tail, h2_, pe_, z_, png_, pgb_)
        err = h3 - tg_
        dh3 = err * (1.0 / D)
        lt = 0.5 * jnp.sum(jnp.sum(err * err, axis=1, keepdims=True) * (1.0 / D), axis=0, keepdims=True)
        dh2_, dpe_, dz_, dpng_, dpgb_ = f(dh3)
        return dh2_, dpe_, dz_, jnp.broadcast_to(lt, (1, LANE)), dpng_, dpgb_

    dh3, dpe, dz, loss, g_png, g_pgb = _rowwise(
        tail, [h2, pe, z, tgt], [vec["ple_norm_g"], vec["ple_gate_b"]], [(D, F32), (D, BF16), (D, BF16)],
        [(1, LANE), (1, D), (1, D)], tile=T1, name="tail")
    G = {}
    gv = {"ple_norm_g": g_png, "ple_gate_b": g_pgb}
    G["ple_gate_w"] = _mm(hg_, dz, ta=True, out_dtype=BF16, name="gw_pgate")
    G["ple_proj"] = _mm(p, dpe, ta=True, out_dtype=BF16, name="gw_ple")
    d_hg = _mm(dz, Wt["ple_gate_w"], tb=True, name="mmb_pgate")

    def norm_bwd(h_, dres_, dn_, g_):
        _, f = jax.vjp(_rms, h_, g_)
        dh_, dg_ = f(dn_)
        return dres_ + dh_, dg_

    dh2, gv["ple_gate_norm_g"] = _rowwise(norm_bwd, [h2, dh3, d_hg], [vec["ple_gate_norm_g"]], [(D, F32)], [(1, D)],
                                          tile=T1, name="norm_gate_bwd")
    G["w_down"] = _mm(act, dh2, ta=True, out_dtype=BF16, name="gw_down")
    d_act = _mm(dh2, Wt["w_down"], tb=True, name="mmb_down")

    def swiglu_bwd(gt_, up_, da_):
        _, f = jax.vjp(_swiglu, gt_, up_)
        return f(da_)

    d_gt, d_up = _rowwise(swiglu_bwd, [gt, up, d_act], [], [(gt.shape[1], BF16)] * 2, tile=T1, name="swiglu_bwd")
    G["w_gate"] = _mm(hn, d_gt, ta=True, out_dtype=BF16, name="gw_gate")
    G["w_up"] = _mm(hn, d_up, ta=True, out_dtype=BF16, name="gw_up")
    d_hn = _mm(d_gt, Wt["w_gate"], tb=True, name="mmb_gate")
    d_hn = _mm(d_up, Wt["w_up"], tb=True, add=d_hn, name="mmb_up")
    dh1, gv["ffn_norm_g"] = _rowwise(norm_bwd, [h1, dh2, d_hn], [vec["ffn_norm_g"]], [(D, F32)], [(1, D)],
                                     tile=T1, name="norm_ffn_bwd")
    G["w_out"] = _mm(ycat, dh1, ta=True, out_dtype=BF16, name="gw_out")
    d_ycat = _mm(dh1, Wt["w_out"], tb=True, name="mmb_out")

    def post_bwd(y_, r_, k2_, v_, g_, dy_, *cs):
        lg, lb, rk, sg, sgt = cs
        _, f = jax.vjp(lambda *a: _post_rwkv(*a, sg, sgt), y_, r_, k2_, v_, g_, lg, lb, rk)
        return f(dy_)

    dy, dr1, dk1, dv1, dg, gv["lnx_g"], gv["lnx_b"], g_rk = _rowwise(
        post_bwd, [y, r, k2, v, g, (d_ycat, W, 0)], post_consts, [(W, F32)] * 5, [(1, W)] * 3, tile=128, name="rwkv_post_bwd")
    gv["r_k"] = g_rk.reshape(H, HEAD)
    pieces = late_shards and ([_pieces(n, G[n]).astype(BF16) for n in _LATE], False)
    (dr, ddec, dk2, dkk, dkka, dvT), recv = _scan_bwd(r, dec, k2, kk, kka, _headsT(v), _headsT(dy), chk, states, hg=hg,
                                                      tc=tc, comm=pieces)
    recv = dict(zip(_LATE, recv))
    dv = _unheadsT(dvT)

    def prep_bwd(u_, dr_, dr1_, ddec_, dk2_, dk1_, dv_, dv1_, dkk_, dkka_, dg_, up_, mu_, *cs):
        um, sh = _shift_mix(u_, up_, mu_)
        cs_d, sg, sgt = cs[:7], cs[7], cs[8]
        _, f = jax.vjp(lambda um_, *c_: _prep_rwkv(um_, *c_, sg, sgt, dims=pdims), um, *cs_d)
        res = f((dr_ + dr1_, ddec_, dk2_ + dk1_, dv_ + dv1_, dkk_, dkka_, dg_))
        dum = res[0]
        dmu = jnp.sum(dum * (sh - u_), axis=0, keepdims=True)
        return (dum, dmu) + tuple(res[1:])

    LP = [Wt["w2"].shape, Wt["a2"].shape, Wt["g2"].shape]
    dum, g_mu, gv["w0"], g_w2, gv["a0"], g_a2, g_g2, gv["k_k"], gv["k_a"] = _rowwise(
        prep_bwd, [u_r, dr, dr1, ddec, dk2, dk1, dv, dv1, dkk, dkka, dg], prep_consts, [(d.RP, F32)],
        [(1, d.RP), (1, W), LP[0], (1, W), LP[1], LP[2], (1, W), (1, W)], tile=128, name="rwkv_prep_bwd", prev=[u_r])
    gv["shift_mu"] = d.unpad_r(g_mu)
    G["w2"], G["a2"], G["g2"] = g_w2, g_a2, g_g2
    (du_r,) = _rowwise(lambda a_, an_, mu_: a_ * (1.0 - mu_) + jnp.where(
        lax.broadcasted_iota(jnp.int32, a_.shape, 0) == a_.shape[0] - 1, an_, pltpu.roll(a_, a_.shape[0] - 1, 0)) * mu_,
        [dum], [mu], [(d.RP, BF16)], tile=T1, name="shift_bwd", nxt=[dum])

    do = _heads(d_ycat[:, W:])
    (dq, dcq, dk_, dv_, dck), _ = _attn_bwd(qh, kh, vh, ccol, crow, o, lse, do, tb=tb)
    dc = _padc((dcq[:, :, 0] - dck[:, 0, :]).T, d.FHp)
    dlogf = _cumsum(dc, reverse=True, name="fox_cumsum_bwd")

    def fox_bwd(uf_, dq_, dk__, dv__, dlf_, *cs):
        qg_, kg_, fb_, sg, sgt = cs
        _, f = jax.vjp(lambda uf__, a, b, c_: _prep_fox(uf__, a, b, c_, sg, sgt, dims=fdims), uf_, qg_, kg_, fb_)
        return f((dq_, dk__, dv__, dlf_))

    du_f, g_qg, g_kg, g_fb = _rowwise(fox_bwd, [u_f, _unheads(dq), _unheads(dk_), _unheads(dv_), dlogf], fox_consts,
                                      [(d.FP, BF16)], [(1, FW), (1, FW), (1, d.FHp)], tile=T1, name="fox_prep_bwd")
    gv["q_norm_g"] = g_qg.reshape(d.FH, HEAD).sum(0, keepdims=True)
    gv["k_norm_g"] = g_kg.reshape(d.FH, HEAD).sum(0, keepdims=True)
    gv["fgate_b"] = g_fb[:, :d.FH]

    G["w_in_r"] = _mm(xn, du_r, ta=True, out_dtype=BF16, name="gw_in_r")
    G["w_in_f"] = _mm(xn, du_f, ta=True, out_dtype=BF16, name="gw_in_f")
    d_xn = _mm(du_r, Wt["w_in_r"], tb=True, name="mmb_in_r")
    d_xn = _mm(du_f, Wt["w_in_f"], tb=True, add=d_xn, name="mmb_in_f")
    dx, gv["attn_norm_g"] = _rowwise(norm_bwd, [x, dh1, d_xn], [vec["attn_norm_g"]], [(D, F32)], [(1, D)],
                                     tile=T1, name="norm_attn_bwd")
    return loss, dx, G, gv, recv


_HBM = pl.BlockSpec(memory_space=pl.ANY)
_OTHER_CHIPS = ((0, 1), (1, 0), (1, 1))


def _flip(v, bit):
    return 1 - v if bit else v


def _exchange(arrs, *, gather, name):
    n = len(arrs)

    def body(*refs):
        copies = _exchange_copies(refs[:n], refs[n:2 * n], refs[2 * n:], gather)
        for cp in copies:
            cp.start()
        for cp in copies:
            cp.wait()

    return pl.pallas_call(
        body, name=name, in_specs=[_HBM] * n, out_specs=[_HBM] * n, out_shape=_exchange_shapes(arrs, gather),
        scratch_shapes=_exchange_sems(n),
    )(*arrs)


def _exchange_shapes(arrs, gather):
    return [jax.ShapeDtypeStruct(((N_CHIPS,) + a.shape) if gather else a.shape, a.dtype) for a in arrs]


def _exchange_sems(n):
    return [pltpu.SemaphoreType.DMA((3 * n,)), pltpu.SemaphoreType.DMA((3 * n,)), pltpu.SemaphoreType.DMA((n,))]


def _exchange_copies(ins, outs, sems, gather):
    send_sems, recv_sems, own_sems = sems
    x, y, c = lax.axis_index("x"), lax.axis_index("y"), lax.axis_index("c")
    me = 2 * x + y
    copies = []
    for a in range(len(ins)):
        copies.append(pltpu.make_async_copy(ins[a] if gather else ins[a].at[me], outs[a].at[me], own_sems.at[a]))
        for k, (dx, dy) in enumerate(_OTHER_CHIPS):
            px, py = _flip(x, dx), _flip(y, dy)
            copies.append(pltpu.make_async_remote_copy(
                src_ref=ins[a] if gather else ins[a].at[2 * px + py], dst_ref=outs[a].at[me],
                send_sem=send_sems.at[3 * a + k], recv_sem=recv_sems.at[3 * a + k],
                device_id=(px, py, c), device_id_type=MESH))
    return copies


def _carrying(body, n_in, n_out, grid, comm):
    arrs, gather = comm
    n = len(arrs)

    def wrapped(*refs):
        c_in = refs[n_in:n_in + n]
        c_out = refs[n_in + n + n_out:n_in + 2 * n + n_out]
        ids = [pl.program_id(a) for a in range(len(grid))]
        first = functools.reduce(jnp.logical_and, [i == 0 for i in ids])
        last = functools.reduce(jnp.logical_and, [i == g - 1 for i, g in zip(ids, grid)])

        @pl.when(first)
        def _():
            for cp in _exchange_copies(c_in, c_out, refs[-3:], gather):
                cp.start()

        body(*refs[:n_in], *refs[n_in + n:n_in + n + n_out], *refs[n_in + 2 * n + n_out:-3])

        @pl.when(last)
        def _():
            for cp in _exchange_copies(c_in, c_out, refs[-3:], gather):
                cp.wait()

    return wrapped


def _swap_cores(arrs, *, name):
    n = len(arrs)

    def body(*refs):
        ins, outs = refs[:n], refs[n:2 * n]
        send_sems, recv_sems = refs[2 * n:]
        peer = (lax.axis_index("x"), lax.axis_index("y"), 1 - lax.axis_index("c"))
        cps = [pltpu.make_async_remote_copy(src_ref=ins[a], dst_ref=outs[a], send_sem=send_sems.at[a],
                                            recv_sem=recv_sems.at[a], device_id=peer, device_id_type=MESH) for a in range(n)]
        for cp in cps:
            cp.start()
        for cp in cps:
            cp.wait()

    return pl.pallas_call(
        body, name=name, in_specs=[_HBM] * n, out_specs=[_HBM] * n,
        out_shape=[jax.ShapeDtypeStruct(a.shape, a.dtype) for a in arrs],
        scratch_shapes=[pltpu.SemaphoreType.DMA((n,)), pltpu.SemaphoreType.DMA((n,))],
    )(*arrs)


def _allreduce_small(pack, *, name):
    R, C = pack.shape

    def body(p_ref, o_ref, recv, send_sems, recv_sems):
        x, y, c = lax.axis_index("x"), lax.axis_index("y"), lax.axis_index("c")
        me = 4 * x + 2 * y + c
        recv[me] = p_ref[...]
        cps = []
        for k in range(1, N_DEV):
            peer = (_flip(x, k & 4), _flip(y, k & 2), _flip(c, k & 1))
            cp = pltpu.make_async_remote_copy(src_ref=p_ref, dst_ref=recv.at[me], send_sem=send_sems.at[k - 1],
                                              recv_sem=recv_sems.at[k - 1], device_id=peer, device_id_type=MESH)
            cp.start()
            cps.append(cp)
        for cp in cps:
            cp.wait()
        acc = recv[0]
        for s in range(1, N_DEV):
            acc = acc + recv[s]
        o_ref[...] = acc

    vm = pl.BlockSpec(memory_space=pltpu.VMEM)
    return pl.pallas_call(
        body, name=name, in_specs=[vm], out_specs=vm, out_shape=jax.ShapeDtypeStruct((R, C), F32),
        scratch_shapes=[pltpu.VMEM((N_DEV, R, C), F32), pltpu.SemaphoreType.DMA((N_DEV - 1,)), pltpu.SemaphoreType.DMA((N_DEV - 1,))],
    )(pack)


def _sum_slots(a, *, name):
    S, R, C = a.shape
    tr = _pick(R, (256, 128, 64, 32, 16, 8))

    def body(a_ref, o_ref):
        acc = a_ref[0].astype(F32)
        for s in range(1, S):
            acc = acc + a_ref[s].astype(F32)
        o_ref[...] = acc

    return pl.pallas_call(
        body, name=name, grid=(R // tr,), in_specs=[pl.BlockSpec((S, tr, C), lambda i: (0, i, 0))],
        out_specs=pl.BlockSpec((tr, C), lambda i: (i, 0)), out_shape=jax.ShapeDtypeStruct((R, C), F32),
        compiler_params=_cparams("parallel"),
    )(a)


def _adamw(w, m, v, gs, *, name):
    R, C = w.shape
    tile = _pick(R, (128, 96, 64, 32, 16, 8))

    def fn(w_, m_, v_, *g_):
        g = g_[0]
        for e in g_[1:]:
            g = g + e
        m2 = ADAM_B1 * m_ + (1.0 - ADAM_B1) * g
        v2 = ADAM_B2 * v_ + (1.0 - ADAM_B2) * jnp.square(g)
        m_hat = m2 / (1.0 - ADAM_B1 ** ADAM_STEP)
        v_hat = v2 / (1.0 - ADAM_B2 ** ADAM_STEP)
        delta = -ADAM_LR * (m_hat / (jnp.sqrt(v_hat) + ADAM_EPS) + ADAM_WD * w_)
        return g, delta, m2, v2

    return _rowwise(fn, [w, m, v, *gs], [], [(C, F32)] * 4, tile=tile, name=name)


_ARGS = "x, p, attn_norm_g, w_in, shift_mu, w0, w2, a0, a2, g2, k_k, k_a, r_k, lnx_g, lnx_b, q_norm_g, k_norm_g, fgate_b, w_out, ffn_norm_g, w_gate, w_up, w_down, ple_proj, ple_norm_g, ple_gate_norm_g, ple_gate_w, ple_gate_b, loss_target, m_attn_norm_g, m_w_in, m_shift_mu, m_w0, m_w2, m_a0, m_a2, m_g2, m_k_k, m_k_a, m_r_k, m_lnx_g, m_lnx_b, m_q_norm_g, m_k_norm_g, m_fgate_b, m_w_out, m_ffn_norm_g, m_w_gate, m_w_up, m_w_down, m_ple_proj, m_ple_norm_g, m_ple_gate_norm_g, m_ple_gate_w, m_ple_gate_b, v_attn_norm_g, v_w_in, v_shift_mu, v_w0, v_w2, v_a0, v_a2, v_g2, v_k_k, v_k_a, v_r_k, v_lnx_g, v_lnx_b, v_q_norm_g, v_k_norm_g, v_fgate_b, v_w_out, v_ffn_norm_g, v_w_gate, v_w_up, v_w_down, v_ple_proj, v_ple_norm_g, v_ple_gate_norm_g, v_ple_gate_w, v_ple_gate_b".split(", ")
_WEIGHTS = _ARGS[2:28]
_COL_SHARDED = ("w_in", "w2", "a2", "g2", "w_gate", "w_up", "ple_proj")
_ROW_SHARDED = ("w_out", "w_down", "ple_gate_w")
_MATRICES = _COL_SHARDED + _ROW_SHARDED
_EARLY = ("w_in", "w2", "a2", "g2")
_LATE = tuple(n for n in _MATRICES if n not in _EARLY)
_VECTORS = tuple(n for n in _WEIGHTS if n not in _MATRICES)


def _whole(name, g):
    if name in _COL_SHARDED:
        return g.transpose(1, 0, 2).reshape(g.shape[1], -1)
    return g.reshape(-1, g.shape[2])


def _pieces(name, a):
    if name in _COL_SHARDED:
        return a.reshape(a.shape[0], N_CHIPS, -1).transpose(1, 0, 2)
    return a.reshape(N_CHIPS, -1, a.shape[1])


def kernel(x, p, attn_norm_g, w_in, shift_mu, w0, w2, a0, a2, g2, k_k, k_a, r_k, lnx_g, lnx_b, q_norm_g, k_norm_g, fgate_b, w_out, ffn_norm_g, w_gate, w_up, w_down, ple_proj, ple_norm_g, ple_gate_norm_g, ple_gate_w, ple_gate_b, loss_target, m_attn_norm_g, m_w_in, m_shift_mu, m_w0, m_w2, m_a0, m_a2, m_g2, m_k_k, m_k_a, m_r_k, m_lnx_g, m_lnx_b, m_q_norm_g, m_k_norm_g, m_fgate_b, m_w_out, m_ffn_norm_g, m_w_gate, m_w_up, m_w_down, m_ple_proj, m_ple_norm_g, m_ple_gate_norm_g, m_ple_gate_w, m_ple_gate_b, v_attn_norm_g, v_w_in, v_shift_mu, v_w0, v_w2, v_a0, v_a2, v_g2, v_k_k, v_k_a, v_r_k, v_lnx_g, v_lnx_b, v_q_norm_g, v_k_norm_g, v_fgate_b, v_w_out, v_ffn_norm_g, v_w_gate, v_w_up, v_w_down, v_ple_proj, v_ple_norm_g, v_ple_gate_norm_g, v_ple_gate_w, v_ple_gate_b):
    A = dict(zip(_ARGS, (x, p, attn_norm_g, w_in, shift_mu, w0, w2, a0, a2, g2, k_k, k_a, r_k, lnx_g, lnx_b, q_norm_g, k_norm_g, fgate_b, w_out, ffn_norm_g, w_gate, w_up, w_down, ple_proj, ple_norm_g, ple_gate_norm_g, ple_gate_w, ple_gate_b, loss_target, m_attn_norm_g, m_w_in, m_shift_mu, m_w0, m_w2, m_a0, m_a2, m_g2, m_k_k, m_k_a, m_r_k, m_lnx_g, m_lnx_b, m_q_norm_g, m_k_norm_g, m_fgate_b, m_w_out, m_ffn_norm_g, m_w_gate, m_w_up, m_w_down, m_ple_proj, m_ple_norm_g, m_ple_gate_norm_g, m_ple_gate_w, m_ple_gate_b, v_attn_norm_g, v_w_in, v_shift_mu, v_w0, v_w2, v_a0, v_a2, v_g2, v_k_k, v_k_a, v_r_k, v_lnx_g, v_lnx_b, v_q_norm_g, v_k_norm_g, v_fgate_b, v_w_out, v_ffn_norm_g, v_w_gate, v_w_up, v_w_down, v_ple_proj, v_ple_norm_g, v_ple_gate_norm_g, v_ple_gate_w, v_ple_gate_b)))
    x, p, tgt = A["x"][0], A["p"][0, 0], A["loss_target"][0]
    d = _Dims(W=A["w0"].shape[-1], DL=A["w2"].shape[1], AL=A["a2"].shape[1], GL=A["g2"].shape[1], FH=A["fgate_b"].shape[-1])

    shard = lambda n: A[n][0].astype(BF16)
    gathered = _exchange([shard(n) for n in _EARLY], gather=True, name="gather_early")
    full = {n: _whole(n, g) for n, g in zip(_EARLY, gathered)}
    Wt = {"w_in_r": d.pad_r(full["w_in"][:, :d.RC]), "w_in_f": d.pad_f(full["w_in"][:, d.RC:]),
          "w2": _padr(full["w2"], d.DLp), "a2": _padr(full["a2"], d.ALp), "g2": _padr(full["g2"], d.GLp)}
    vec = {n: A[n].reshape(-1, A[n].shape[-1]) for n in _VECTORS}

    loss, dx, G, gv, recv = _local_step(x, p, tgt, Wt, vec, d, late_shards=[shard(n) for n in _LATE])

    gw = {"w_in": jnp.concatenate([d.unpad_r(G["w_in_r"]), d.unpad_f(G["w_in_f"])], axis=1),
          "w2": G["w2"][:d.DL], "a2": G["a2"][:d.AL], "g2": G["g2"][:d.GL]}
    recv.update(zip(_EARLY, _exchange([_pieces(n, gw[n]).astype(BF16) for n in _EARLY], gather=False, name="scatter_early")))
    part = [_sum_slots(recv[n], name="sum_" + n) for n in _MATRICES]
    sib = _swap_cores(part, name="swap_cores")

    sizes = [1] + [A[n].size for n in _VECTORS]
    rows = _rup(_rup(sum(sizes), LANE) // LANE, 8)

    def pack(items):
        flat = jnp.concatenate([i.reshape(-1) for i in items])
        return jnp.pad(flat, (0, rows * LANE - flat.shape[0])).reshape(rows, LANE)

    red = _allreduce_small(pack([loss[0, :1]] + [gv[n] for n in _VECTORS]), name="allreduce_vectors")
    zero = jnp.zeros((1,), F32)
    upd = _adamw(pack([zero] + [A[n] for n in _VECTORS]), pack([zero] + [A["m_" + n] for n in _VECTORS]),
                 pack([zero + 1.0] + [A["v_" + n] for n in _VECTORS]), [red], name="adamw_vectors")
    offs = [0]
    for s in sizes:
        offs.append(offs[-1] + s)
    unpack = lambda a, i, n: a.reshape(-1)[offs[i + 1]:offs[i + 2]].reshape(A[n].shape)

    out = {"grad": {}, "delta": {}, "new_m": {}, "new_v": {}}
    for i, n in enumerate(_VECTORS):
        for kind, a in zip(out, upd):
            out[kind][n] = unpack(a, i, n)
    for n, mine, other in zip(_MATRICES, part, sib):
        res = _adamw(A[n][0], A["m_" + n][0], A["v_" + n][0], [mine, other], name="adamw_" + n)
        for kind, a in zip(out, res):
            out[kind][n] = a[None]
    return (red[0, 0], dx[None], *[out[k][n] for k in out for n in _WEIGHTS])
```

```python
import functools

import jax
import jax.numpy as jnp
from jax import lax
from jax.experimental import pallas as pl
from jax.experimental.pallas import tpu as pltpu

F32 = jnp.float32
BF16 = jnp.bfloat16
LANE = 128
HEAD = 64
RMS_EPS = 1e-6
GN_EPS = 64e-5
ADAM_LR, ADAM_B1, ADAM_B2, ADAM_EPS, ADAM_WD, ADAM_STEP = 0.001, 0.9, 0.999, 1e-08, 0.01, 10
VMEM_LIMIT = 56 * 1024 * 1024
MM_TILE_BYTES = 40 * 1024 * 1024
NEG = -1e30
MESH = pl.DeviceIdType.MESH
N_CHIPS = 4
N_DEV = 8


def _rup(n, m):
    return -(-n // m) * m


def _pick(n, cands):
    for c in cands:
        if n % c == 0:
            return c
    return n


def _cparams(*sem):
    return pltpu.CompilerParams(dimension_semantics=sem, vmem_limit_bytes=VMEM_LIMIT)


def _mm(a, b, *, ta=False, tb=False, add=None, out_dtype=F32, name):
    M, K = (a.shape[1], a.shape[0]) if ta else a.shape
    N = b.shape[0] if tb else b.shape[1]
    tn = _pick(N, (512, 640, 256, 128))
    fits = lambda m, t: 2 * (m * t * a.dtype.itemsize + t * tn * b.dtype.itemsize + m * tn * 8) <= MM_TILE_BYTES
    tm, tk = next((m, t) for t in (K, 2048, 1024, 512, 640, 256, 128) for m in (1024, 512, 256, 128, M)
                  if K % t == 0 and M % m == 0 and fits(m, t))
    nk = K // tk
    dn = (((0 if ta else 1,), (1 if tb else 0,)), ((), ()))

    def body(*refs):
        if add is None:
            a_ref, b_ref, o_ref, acc = refs
        else:
            a_ref, b_ref, add_ref, o_ref, acc = refs
        ks = pl.program_id(2)
        part = lax.dot_general(a_ref[...].astype(BF16), b_ref[...].astype(BF16), dn, preferred_element_type=F32)
        if nk > 1:
            @pl.when(ks == 0)
            def _():
                acc[...] = jnp.zeros_like(acc)

            acc[...] += part

        @pl.when(ks == nk - 1)
        def _():
            res = acc[...] if nk > 1 else part
            if add is not None:
                res = res + add_ref[...].astype(F32)
            o_ref[...] = res.astype(out_dtype)

    a_spec = pl.BlockSpec((tk, tm), lambda i, j, k: (k, i)) if ta else pl.BlockSpec((tm, tk), lambda i, j, k: (i, k))
    b_spec = pl.BlockSpec((tn, tk), lambda i, j, k: (j, k)) if tb else pl.BlockSpec((tk, tn), lambda i, j, k: (k, j))
    o_spec = pl.BlockSpec((tm, tn), lambda i, j, k: (i, j))
    ins, specs = [a, b], [a_spec, b_spec]
    if add is not None:
        ins.append(add)
        specs.append(o_spec)
    return pl.pallas_call(
        body, name=name, grid=(M // tm, N // tn, nk), in_specs=specs, out_specs=o_spec,
        out_shape=jax.ShapeDtypeStruct((M, N), out_dtype),
        scratch_shapes=[pltpu.VMEM((tm, tn) if nk > 1 else (8, LANE), F32)],
        compiler_params=_cparams("parallel", "parallel", "arbitrary"),
    )(*ins)


def _rowwise(fn, rows, consts, outs, accs=(), *, tile, name, prev=(), nxt=()):
    paired = [not isinstance(r, tuple) and r.ndim == 3 for r in rows]
    rows = [r if isinstance(r, tuple) else (r, r.shape[-1], 0) for r in rows]
    T = rows[0][0].shape[-2]
    tile = min(tile, T)
    n = T // tile
    sub = 8
    nr, npv, nnx, ncst, no, na = len(rows), len(prev), len(nxt), len(consts), len(outs), len(accs)
    out_paired = [len(o) == 3 for o in outs]

    def body(*refs):
        i = pl.program_id(0)
        it = iter(refs)
        rv = [next(it) for _ in range(nr)]
        rv = [jnp.concatenate([r[g] for g in range(r.shape[0])], axis=1) if pr else r[...] for r, pr in zip(rv, paired)]
        pv = [jnp.where(i > 0, next(it)[sub - 1:sub, :], 0.0) for _ in range(npv)]
        nv = [jnp.where(i < n - 1, next(it)[0:1, :], 0.0) for _ in range(nnx)]
        cv = [next(it)[...] for _ in range(ncst)]
        o_refs = [next(it) for _ in range(no)]
        a_refs = [next(it) for _ in range(na)]
        res = fn(*rv, *pv, *nv, *cv)
        if not isinstance(res, (tuple, list)):
            res = (res,)
        for r, o, pr in zip(o_refs, res[:no], out_paired):
            if pr:
                for g in range(r.shape[0]):
                    r[g] = o[:, g * LANE:(g + 1) * LANE].astype(r.dtype)
            else:
                r[...] = o.astype(r.dtype)
        if na:
            @pl.when(i == 0)
            def _():
                for r in a_refs:
                    r[...] = jnp.zeros_like(r)
            for r, o in zip(a_refs, res[no:]):
                r[...] += o.astype(F32)

    in_specs = [pl.BlockSpec((a.shape[0], tile, w), lambda i: (0, i, 0)) if pr else
                pl.BlockSpec((tile, w), functools.partial(lambda cb, i: (i, cb), cb)) for (a, w, cb), pr in zip(rows, paired)]
    in_specs += [pl.BlockSpec((sub, a.shape[1]), lambda i: (jnp.maximum(i * (tile // sub) - 1, 0), 0)) for a in prev]
    in_specs += [pl.BlockSpec((sub, a.shape[1]), lambda i: (jnp.minimum((i + 1) * (tile // sub), T // sub - 1), 0)) for a in nxt]
    in_specs += [pl.BlockSpec(c.shape, lambda i: (0, 0)) for c in consts]
    out_specs = [pl.BlockSpec((o[0] // LANE, tile, LANE), lambda i: (0, i, 0)) if pr else
                 pl.BlockSpec((tile, o[0]), lambda i: (i, 0)) for o, pr in zip(outs, out_paired)]
    out_specs += [pl.BlockSpec(s, lambda i: (0, 0)) for s in accs]
    out_shape = [jax.ShapeDtypeStruct((o[0] // LANE, T, LANE) if pr else (T, o[0]), o[1]) for o, pr in zip(outs, out_paired)]
    out_shape += [jax.ShapeDtypeStruct(s, F32) for s in accs]
    res = pl.pallas_call(
        body, name=name, grid=(n,), in_specs=in_specs, out_specs=out_specs, out_shape=out_shape,
        compiler_params=_cparams("arbitrary"),
    )(*[r[0] for r in rows], *prev, *nxt, *consts)
    return res


@jax.custom_vjp
def _bdot(a, b):
    return jnp.dot(a.astype(BF16), b.astype(BF16), preferred_element_type=F32)


def _bdot_fwd(a, b):
    return _bdot(a, b), (a.astype(BF16), b.astype(BF16))


def _bdot_bwd(res, ct):
    a, b = res
    c = ct.astype(BF16)
    return (lax.dot_general(c, b, (((1,), (1,)), ((), ())), preferred_element_type=F32),
            lax.dot_general(a, c, (((0,), (0,)), ((), ())), preferred_element_type=F32))


_bdot.defvjp(_bdot_fwd, _bdot_bwd)


def _split3(x):
    hi = x.astype(BF16)
    r1 = x - hi.astype(F32)
    mid = r1.astype(BF16)
    return hi, mid, (r1 - mid.astype(F32)).astype(BF16)


def _dot_exact(a, b):
    return sum(jnp.dot(p, b, preferred_element_type=F32) for p in _split3(a))


@jax.custom_vjp
def _head_sums(x, seg, segt):
    return _dot_exact(_dot_exact(x, seg), segt)


def _head_sums_fwd(x, seg, segt):
    return _head_sums(x, seg, segt), (seg, segt)


def _head_sums_bwd(res, ct):
    seg, segt = res
    return _head_sums(ct, seg, segt), jnp.zeros_like(seg), jnp.zeros_like(segt)


_head_sums.defvjp(_head_sums_fwd, _head_sums_bwd)


def _rms(x, g, eps=RMS_EPS):
    return x * lax.rsqrt(jnp.mean(x * x, axis=-1, keepdims=True) + eps) * g


def _softplus(x):
    return jnp.maximum(x, 0.0) + jnp.log(1.0 + jnp.exp(-jnp.abs(x)))


def _sigmoid(x):
    return 1.0 / (1.0 + jnp.exp(-x))


def _seg_mats(width):
    h = lax.broadcasted_iota(jnp.int32, (width, LANE), 0) // HEAD
    j = lax.broadcasted_iota(jnp.int32, (width, LANE), 1)
    seg = (h == j).astype(BF16)
    return seg, seg.T


def _prep_rwkv(um, w0, w2, a0, a2, g2, k_k, k_a, seg, segt, *, dims):
    W, DLp, ALp, GLp = dims
    r, k, v = um[:, :W], um[:, W:2 * W], um[:, 2 * W:3 * W]
    o = 3 * W
    xw, xa, xg = um[:, o:o + DLp], um[:, o + DLp:o + DLp + ALp], um[:, o + DLp + ALp:o + DLp + ALp + GLp]
    w_log = -_softplus(-(w0 + _bdot(jnp.tanh(xw), w2))) - 0.5
    decay = jnp.exp(-jnp.exp(w_log))
    a = _sigmoid(a0 + _bdot(xa, a2))
    g = _bdot(_sigmoid(xg), g2)
    kk = k * k_k
    nrm = jnp.sqrt(_head_sums(kk * kk, seg, segt))
    kk = kk / jnp.maximum(nrm, 1e-12)
    k2 = k * (1.0 + (a - 1.0) * k_a)
    return r, decay, k2, v, kk, kk * a, g


def _shift_mix(u, uprev, mu):
    first = lax.broadcasted_iota(jnp.int32, u.shape, 0) == 0
    sh = jnp.where(first, uprev, pltpu.roll(u, 1, 0))
    return u + (sh - u) * mu, sh


def _post_rwkv(y, r, k2, v, g, lnx_g, lnx_b, r_k, seg, segt):
    inv = 1.0 / HEAD
    mean = _head_sums(y, seg, segt) * inv
    yc = y - mean
    var = _head_sums(yc * yc, seg, segt) * inv
    yn = yc * lax.rsqrt(var + GN_EPS) * lnx_g + lnx_b
    bonus = _head_sums(r * k2 * r_k, seg, segt) * v
    return (yn + bonus) * g


def _prep_fox(uf, qg, kg, fb, seg, segt, *, dims):
    FW, FHp = dims
    q, k, v, f = uf[:, :FW], uf[:, FW:2 * FW], uf[:, 2 * FW:3 * FW], uf[:, 3 * FW:3 * FW + FHp]
    inv = 1.0 / HEAD
    qn = q * lax.rsqrt(_head_sums(q * q, seg, segt) * inv + RMS_EPS) * qg * (HEAD ** -0.5)
    kn = k * lax.rsqrt(_head_sums(k * k, seg, segt) * inv + RMS_EPS) * kg
    return qn, kn, v, -_softplus(-(f + fb))


def _tail(h2, pe, z, png, pgb):
    return h2 + _sigmoid(z + pgb) * _rms(pe, png)


def _swiglu(gt, up):
    return gt * _sigmoid(gt) * up


def _cumsum(x, *, reverse, name):
    T, C = x.shape
    tc = _pick(T, (256, 128))
    n = T // tc
    i0 = lax.broadcasted_iota(jnp.int32, (tc, tc), 0)
    i1 = lax.broadcasted_iota(jnp.int32, (tc, tc), 1)
    tri = ((i0 <= i1) if reverse else (i0 >= i1)).astype(BF16)

    def body(x_ref, tri_ref, o_ref, carry):
        i = pl.program_id(0)

        @pl.when(i == 0)
        def _():
            carry[...] = jnp.zeros_like(carry)

        v = x_ref[...]
        hi = v.astype(BF16)
        r1 = v - hi.astype(F32)
        mid = r1.astype(BF16)
        lo = (r1 - mid.astype(F32)).astype(BF16)
        t = tri_ref[...]
        d = lambda p: jnp.dot(t, p, preferred_element_type=F32)
        c = d(hi) + d(mid) + d(lo) + carry[0:1, :]
        o_ref[...] = c
        edge = c[0:1, :] if reverse else c[tc - 1:tc, :]
        carry[...] = jnp.broadcast_to(edge, carry.shape)

    blk = pl.BlockSpec((tc, C), (lambda i: (n - 1 - i, 0)) if reverse else (lambda i: (i, 0)))
    return pl.pallas_call(
        body, name=name, grid=(n,), in_specs=[blk, pl.BlockSpec((tc, tc), lambda i: (0, 0))], out_specs=blk,
        out_shape=jax.ShapeDtypeStruct((T, C), F32), scratch_shapes=[pltpu.VMEM((8, C), F32)],
        compiler_params=_cparams("arbitrary"),
    )(x, tri)


BWD_PAIRS_PER_TRIP = 8
BWD_STEPS_PER_TRIP = 8
BWD_LOAD_PARTS = 8
FWD_STEPS_PER_TRIP = 8


def _steps(n, per_trip, step, start=0):
    def trip(i, carry):
        for j in range(per_trip):
            carry = step(start + i * per_trip + j, carry)
        return carry

    lax.fori_loop(0, n // per_trip, trip, 0)


def _col(tile, lane, t):
    return jnp.sum(jnp.where(lane == t, tile, 0.0), axis=1, keepdims=True)


def _halves(x, left):
    a = jnp.sum(jnp.where(left, x, 0.0), axis=1, keepdims=True)
    b = jnp.sum(jnp.where(left, 0.0, x), axis=1, keepdims=True)
    return a, b, jnp.where(left, a, b)


def _pair_col(ref, p, lane, left, t):
    return jnp.where(left, _col(ref[2 * p], lane, t), _col(ref[2 * p + 1], lane, t))


def _scan_fwd(r, w, k, kk, kka, vT, *, hg, tc, comm=None):
    H, N, T = vT.shape
    nc = T // tc
    hp = hg // 2
    L = 2 * N
    half = tc // 2

    def body(r_ref, w_ref, k_ref, kk_ref, kka_ref, vT_ref, yT_ref, chk_ref, st_hbm, s_ref, stage, st_sems):
        g, c = pl.program_id(0), pl.program_id(1)

        @pl.when(c == 0)
        def _():
            s_ref[...] = jnp.zeros_like(s_ref)

        def flush(hf, chunk):
            return pltpu.make_async_copy(stage.at[:, pl.ds(hf * half, half)],
                                         st_hbm.at[pl.ds(g * hp, hp), pl.ds(chunk * tc + hf * half, half)], st_sems.at[hf])

        chk_ref[:, 0] = s_ref[...]
        yT_ref[...] = jnp.zeros_like(yT_ref)
        lane = lax.broadcasted_iota(jnp.int32, (N, tc), 1)
        left = lax.broadcasted_iota(jnp.int32, (N, L), 1) < N

        def emit_y(S, p, t):
            ya, yb, _ = _halves(S * r_ref[p, pl.ds(jnp.maximum(t, 0), 1), :], left)
            yT_ref[2 * p] = jnp.where(lane == t, ya, yT_ref[2 * p])
            yT_ref[2 * p + 1] = jnp.where(lane == t, yb, yT_ref[2 * p + 1])

        def step(t, carry):
            for p in range(hp):
                row = lambda ref: ref[p, pl.ds(t, 1), :]
                S = s_ref[p]
                emit_y(S, p, t - 1)
                sa = _halves(S * row(kk_ref), left)[2]
                new = S * row(w_ref) - sa * row(kka_ref) + _pair_col(vT_ref, p, lane, left, t) * row(k_ref)
                s_ref[p] = new
                stage[p, t] = new
            return carry

        for hf in range(2):
            @pl.when(c > 0)
            def _():
                flush(hf, c - 1).wait()

            _steps(half, FWD_STEPS_PER_TRIP, step, start=hf * half)
            flush(hf, c).start()
        for p in range(hp):
            emit_y(s_ref[p], p, tc - 1)

        @pl.when(c == nc - 1)
        def _():
            flush(0, c).wait()
            flush(1, c).wait()

    rows = pl.BlockSpec((hp, tc, L), lambda g, c: (g, c, 0))
    cols = pl.BlockSpec((hg, N, tc), lambda g, c: (g, 0, c))
    return _call_carrying(
        body, "scan_fwd", (H // hg, nc), [rows] * 5 + [cols],
        [cols, pl.BlockSpec((hp, 1, N, L), lambda g, c: (g, c, 0, 0)), _HBM],
        [jax.ShapeDtypeStruct((H, N, T), F32), jax.ShapeDtypeStruct((H // 2, nc, N, L), F32),
         jax.ShapeDtypeStruct((H // 2, T, N, L), F32)],
        [pltpu.VMEM((hp, N, L), F32), pltpu.VMEM((hp, tc, N, L), F32), pltpu.SemaphoreType.DMA((2,))],
        [r, w, k, kk, kka, vT], comm)


def _scan_bwd(r, w, k, kk, kka, vT, dyT, chk, states, *, hg, tc, comm=None):
    H, N, T = vT.shape
    nc = T // tc
    hp = hg // 2
    L = 2 * N
    part = tc // BWD_LOAD_PARTS

    def body(r_ref, w_ref, k_ref, kk_ref, kka_ref, vT_ref, dyT_ref, chk_ref, st_hbm,
             dr_ref, dw_ref, dk_ref, dkk_ref, dkka_ref, dvT_ref, sp_ref, ds_ref, ld_sems):
        g, chunk = pl.program_id(0), nc - 1 - pl.program_id(1)

        def load(i):
            lo = max(1, i * part)
            n = (tc + 1 if i == BWD_LOAD_PARTS - 1 else (i + 1) * part) - lo
            return pltpu.make_async_copy(st_hbm.at[pl.ds(g * hp, hp), pl.ds(chunk * tc + lo - 1, n)],
                                         sp_ref.at[:, pl.ds(lo, n)], ld_sems.at[i])

        for i in reversed(range(BWD_LOAD_PARTS)):
            load(i).start()

        @pl.when(pl.program_id(1) == 0)
        def _():
            ds_ref[...] = jnp.zeros_like(ds_ref)

        dvT_ref[...] = jnp.zeros_like(dvT_ref)
        lane = lax.broadcasted_iota(jnp.int32, (N, tc), 1)
        left = lax.broadcasted_iota(jnp.int32, (N, L), 1) < N
        left_row = lax.broadcasted_iota(jnp.int32, (1, L), 1) < N

        halves = _halves
        pair_col = lambda ref, p, t: _pair_col(ref, p, lane, left, t)
        for p in range(hp):
            sp_ref[p, 0] = chk_ref[p, 0]

        def bstep(p0, last, i, carry):
            for j in range(BWD_STEPS_PER_TRIP):
                back_one(p0, last - (i * BWD_STEPS_PER_TRIP + j))
            return carry

        def back_one(p0, t):
            for p in range(p0, min(p0 + BWD_PAIRS_PER_TRIP, hp)):
                row = lambda ref: ref[p, pl.ds(t, 1), :]
                rr, wr, kr, kkr, kkar = row(r_ref), row(w_ref), row(k_ref), row(kk_ref), row(kka_ref)
                Sp = sp_ref[p, t]
                Sn = sp_ref[p, t + 1]
                dycol, vcol = pair_col(dyT_ref, p, t), pair_col(vT_ref, p, t)
                dS = ds_ref[p]
                dSn = dS + dycol * rr
                dsa = halves(dS * kkar, left)[2] + dycol * halves(rr * kkar, left_row)[2]
                dr_ref[p, pl.ds(t, 1), :] = jnp.sum(Sn * dycol, axis=0, keepdims=True)
                sa = halves(Sp * kkr, left)[2]
                dw_ref[p, pl.ds(t, 1), :] = jnp.sum(dSn * Sp, axis=0, keepdims=True)
                dkka_ref[p, pl.ds(t, 1), :] = -jnp.sum(dSn * sa, axis=0, keepdims=True)
                dva, dvb, _ = halves(dSn * kr, left)
                dk_ref[p, pl.ds(t, 1), :] = jnp.sum(dSn * vcol, axis=0, keepdims=True)
                dkk_ref[p, pl.ds(t, 1), :] = -jnp.sum(Sp * dsa, axis=0, keepdims=True)
                ds_ref[p] = dSn * wr - dsa * kkr
                dvT_ref[2 * p] = jnp.where(lane == t, dva, dvT_ref[2 * p])
                dvT_ref[2 * p + 1] = jnp.where(lane == t, dvb, dvT_ref[2 * p + 1])

        for i in reversed(range(BWD_LOAD_PARTS)):
            load(i).wait()
            for p0 in range(0, hp, BWD_PAIRS_PER_TRIP):
                lax.fori_loop(0, part // BWD_STEPS_PER_TRIP, functools.partial(bstep, p0, (i + 1) * part - 1), 0)

    rows = pl.BlockSpec((hp, tc, L), lambda g, c: (g, nc - 1 - c, 0))
    cols = pl.BlockSpec((hg, N, tc), lambda g, c: (g, 0, nc - 1 - c))
    return _call_carrying(
        body, "scan_bwd", (H // hg, nc),
        [rows] * 5 + [cols, cols, pl.BlockSpec((hp, 1, N, L), lambda g, c: (g, nc - 1 - c, 0, 0)), _HBM],
        [rows] * 5 + [cols],
        [jax.ShapeDtypeStruct((H // 2, T, L), F32)] * 5 + [jax.ShapeDtypeStruct((H, N, T), F32)],
        [pltpu.VMEM((hp, tc + 1, N, L), F32), pltpu.VMEM((hp, N, L), F32), pltpu.SemaphoreType.DMA((BWD_LOAD_PARTS,))],
        [r, w, k, kk, kka, vT, dyT, chk, states], comm)


_NT = (((1,), (1,)), ((), ()))
_TN = (((0,), (0,)), ((), ()))


def _scores(q, k, cc, cr, masked):
    s = lax.dot_general(q, k, _NT, preferred_element_type=F32) + cc - cr
    if masked:
        tb = s.shape[0]
        keep = lax.broadcasted_iota(jnp.int32, (tb, tb), 0) >= lax.broadcasted_iota(jnp.int32, (tb, tb), 1)
        s = jnp.where(keep, s, NEG)
    return s


def _attn_specs(T, N, tb):
    blk = pl.BlockSpec((1, tb, N), lambda h, i: (h, i, 0))
    whole = pl.BlockSpec((1, T, N), lambda h, i: (h, 0, 0))
    col = pl.BlockSpec((1, tb, 1), lambda h, i: (h, i, 0))
    wcol = pl.BlockSpec((1, T, 1), lambda h, i: (h, 0, 0))
    row = pl.BlockSpec((1, 1, tb), lambda h, i: (h, 0, i))
    wrow = pl.BlockSpec((1, 1, T), lambda h, i: (h, 0, 0))
    return blk, whole, col, wcol, row, wrow


def _call_carrying(body, name, grid, in_specs, out_specs, out_shape, scratch, args, comm):
    n_out = len(out_specs)
    if comm is not None:
        n = len(comm[0])
        body = _carrying(body, len(in_specs), n_out, grid, comm)
        in_specs, out_specs = in_specs + [_HBM] * n, out_specs + [_HBM] * n
        out_shape, scratch, args = out_shape + _exchange_shapes(*comm), scratch + _exchange_sems(n), args + list(comm[0])
    res = pl.pallas_call(
        body, name=name, grid=grid, in_specs=in_specs, out_specs=out_specs, out_shape=out_shape,
        scratch_shapes=scratch, compiler_params=_cparams(*["arbitrary"] * len(grid)),
    )(*args)
    return res[:n_out], res[n_out:]


def _attn_fwd(q, k, v, ccol, crow, *, tb, comm=None):
    H, T, N = q.shape

    def body(q_ref, k_ref, v_ref, cc_ref, cr_ref, o_ref, lse_ref, m_s, l_s, acc_s):
        qi = pl.program_id(1)
        m_s[...] = jnp.full_like(m_s, NEG)
        l_s[...] = jnp.zeros_like(l_s)
        acc_s[...] = jnp.zeros_like(acc_s)
        q_, cc = q_ref[0], cc_ref[0]

        def block(j, masked):
            at = pl.ds(pl.multiple_of(j * tb, tb), tb)
            s = _scores(q_, k_ref[0, at, :], cc, cr_ref[0, :, at], masked)
            m_new = jnp.maximum(m_s[...], jnp.max(s, axis=1, keepdims=True))
            p = jnp.exp(s - m_new)
            alpha = jnp.exp(m_s[...] - m_new)
            l_s[...] = alpha * l_s[...] + jnp.sum(p, axis=1, keepdims=True)
            acc_s[...] = alpha * acc_s[...] + jnp.dot(p.astype(BF16), v_ref[0, at, :], preferred_element_type=F32)
            m_s[...] = m_new

        def below(j, carry):
            block(j, False)
            return carry

        lax.fori_loop(0, qi, below, 0)
        block(qi, True)
        o_ref[0] = acc_s[...] / l_s[...]
        lse_ref[0] = m_s[...] + jnp.log(l_s[...])

    blk, whole, col, wcol, row, wrow = _attn_specs(T, N, tb)
    return _call_carrying(
        body, "fox_fwd", (H, T // tb), [blk, whole, whole, col, wrow], [blk, col],
        [jax.ShapeDtypeStruct((H, T, N), F32), jax.ShapeDtypeStruct((H, T, 1), F32)],
        [pltpu.VMEM((tb, 1), F32), pltpu.VMEM((tb, 1), F32), pltpu.VMEM((tb, N), F32)],
        [q, k, v, ccol, crow], comm)


def _attn_bwd(q, k, v, ccol, crow, o, lse, do, *, tb, comm=None):
    H, T, N = q.shape
    nb = T // tb

    def body(q_ref, k_ref, v_ref, cc_ref, cr_ref, o_ref, lse_ref, do_ref,
             dq_ref, dcq_ref, dk_ref, dv_ref, dck_ref, dq_s, dcq_s, dk_s, dv_s, dck_s):
        qi = pl.program_id(1)

        @pl.when(qi == 0)
        def _():
            dk_s[...] = jnp.zeros_like(dk_s)
            dv_s[...] = jnp.zeros_like(dv_s)
            dck_s[...] = jnp.zeros_like(dck_s)

        dq_s[...] = jnp.zeros_like(dq_s)
        dcq_s[...] = jnp.zeros_like(dcq_s)
        q_, cc, lse_, do_ = q_ref[0], cc_ref[0], lse_ref[0], do_ref[0]
        delta = jnp.sum(do_ * o_ref[0], axis=1, keepdims=True)
        dob = do_.astype(BF16)

        def block(j, masked):
            at = pl.ds(pl.multiple_of(j * tb, tb), tb)
            kb = k_ref[0, at, :]
            p = jnp.exp(_scores(q_, kb, cc, cr_ref[0, :, at], masked) - lse_)
            dp = lax.dot_general(dob, v_ref[0, at, :], _NT, preferred_element_type=F32)
            ds = p * (dp - delta)
            dsb = ds.astype(BF16)
            dq_s[...] += jnp.dot(dsb, kb, preferred_element_type=F32)
            dcq_s[...] += jnp.sum(ds, axis=1, keepdims=True)
            dv_s[at, :] += lax.dot_general(p.astype(BF16), dob, _TN, preferred_element_type=F32)
            dk_s[at, :] += lax.dot_general(dsb, q_, _TN, preferred_element_type=F32)
            dck_s[:, at] += jnp.sum(ds, axis=0, keepdims=True)

        def below(j, carry):
            block(j, False)
            return carry

        lax.fori_loop(0, qi, below, 0)
        block(qi, True)
        dq_ref[0] = dq_s[...]
        dcq_ref[0] = dcq_s[...]

        @pl.when(qi == nb - 1)
        def _():
            dk_ref[0] = dk_s[...]
            dv_ref[0] = dv_s[...]
            dck_ref[0] = dck_s[...]

    blk, whole, col, wcol, row, wrow = _attn_specs(T, N, tb)
    return _call_carrying(
        body, "fox_bwd", (H, nb), [blk, whole, whole, col, wrow, blk, col, blk], [blk, col, whole, whole, wrow],
        [jax.ShapeDtypeStruct((H, T, N), F32), jax.ShapeDtypeStruct((H, T, 1), F32), jax.ShapeDtypeStruct((H, T, N), F32),
         jax.ShapeDtypeStruct((H, T, N), F32), jax.ShapeDtypeStruct((H, 1, T), F32)],
        [pltpu.VMEM((tb, N), F32), pltpu.VMEM((tb, 1), F32), pltpu.VMEM((T, N), F32), pltpu.VMEM((T, N), F32),
         pltpu.VMEM((1, T), F32)],
        [q, k, v, ccol, crow, o, lse, do], comm)


def _heads(x):
    T = x.shape[0]
    return x.reshape(T, -1, HEAD).transpose(1, 0, 2)


def _headsT(x):
    T = x.shape[0]
    return x.reshape(T, -1, HEAD).transpose(1, 2, 0)


def _unheads(x):
    return x.transpose(1, 0, 2).reshape(x.shape[1], -1)


def _unheadsT(x):
    return x.transpose(2, 0, 1).reshape(x.shape[2], -1)


def _padc(x, n):
    return jnp.pad(x, ((0, 0), (0, n - x.shape[1])))


def _padr(x, n):
    return jnp.pad(x, ((0, n - x.shape[0]), (0, 0)))


class _Dims:
    def __init__(self, W, DL, AL, GL, FH):
        self.W, self.DL, self.AL, self.GL, self.FH = W, DL, AL, GL, FH
        self.DLp, self.ALp, self.GLp, self.FHp = _rup(DL, LANE), _rup(AL, LANE), _rup(GL, LANE), _rup(FH, LANE)
        self.FW = FH * HEAD
        self.RC = 3 * W + DL + AL + GL
        self.RP = 3 * W + self.DLp + self.ALp + self.GLp
        self.FC = 3 * self.FW + FH
        self.FP = 3 * self.FW + self.FHp

    def pad_r(self, a):
        W, o = self.W, 3 * self.W
        return jnp.concatenate([a[:, :o], _padc(a[:, o:o + self.DL], self.DLp),
                                _padc(a[:, o + self.DL:o + self.DL + self.AL], self.ALp),
                                _padc(a[:, o + self.DL + self.AL:self.RC], self.GLp)], axis=1)

    def unpad_r(self, a):
        o = 3 * self.W
        return jnp.concatenate([a[:, :o], a[:, o:o + self.DL], a[:, o + self.DLp:o + self.DLp + self.AL],
                                a[:, o + self.DLp + self.ALp:o + self.DLp + self.ALp + self.GL]], axis=1)

    def pad_f(self, a):
        return _padc(a, self.FP)

    def unpad_f(self, a):
        return a[:, :self.FC]


def _local_step(x, p, tgt, Wt, vec, d, late_shards=None):
    Wt = dict(Wt)
    T, D = x.shape
    W, FW = d.W, d.FW
    H = W // HEAD
    seg, segt = _seg_mats(W)
    segf, segft = _seg_mats(FW)
    T1 = 256
    rk_flat = vec["r_k"].reshape(1, W)
    mu = d.pad_r(vec["shift_mu"])
    qg = jnp.tile(vec["q_norm_g"], (1, d.FH))
    kg = jnp.tile(vec["k_norm_g"], (1, d.FH))
    fb = _padc(vec["fgate_b"], d.FHp)
    pdims = (W, d.DLp, d.ALp, d.GLp)
    fdims = (FW, d.FHp)

    (xn,) = _rowwise(lambda x_, g_: _rms(x_, g_), [x], [vec["attn_norm_g"]], [(D, BF16)], tile=T1, name="norm_attn")
    u_r = _mm(xn, Wt["w_in_r"], name="mm_in_r")
    u_f = _mm(xn, Wt["w_in_f"], name="mm_in_f")

    prep_consts = [mu, vec["w0"], Wt["w2"].astype(F32), vec["a0"], Wt["a2"].astype(F32), Wt["g2"].astype(F32), vec["k_k"], vec["k_a"], seg, segt]

    def prep_fwd(u_, up_, mu_, *cs):
        um, _ = _shift_mix(u_, up_, mu_)
        return _prep_rwkv(um, *cs, dims=pdims)

    pw, tw = (W, F32, "pairs"), (W, F32)
    r, dec, k2, v, kk, kka, g = _rowwise(prep_fwd, [u_r], prep_consts, [pw, pw, pw, tw, pw, pw, tw], tile=128,
                                         name="rwkv_prep", prev=[u_r])
    hg, tc = min(16, H), 128
    (yT, chk, states), gathered = _scan_fwd(r, dec, k2, kk, kka, _headsT(v), hg=hg, tc=tc,
                                            comm=late_shards and (late_shards, True))
    Wt.update({n: _whole(n, g) for n, g in zip(_LATE, gathered)})
    y = _unheadsT(yT)
    post_consts = [vec["lnx_g"], vec["lnx_b"], rk_flat, seg, segt]
    (y_r,) = _rowwise(_post_rwkv, [y, r, k2, v, g], post_consts, [(W, BF16)], tile=T1, name="rwkv_post")

    fox_consts = [qg, kg, fb, segf, segft]
    qn, kn, vf, logf = _rowwise(functools.partial(_prep_fox, dims=fdims), [u_f], fox_consts,
                                [(FW, BF16), (FW, BF16), (FW, BF16), (d.FHp, F32)], tile=T1, name="fox_prep")
    c = _cumsum(logf, reverse=False, name="fox_cumsum")
    cT = c[:, :d.FH].T
    ccol, crow = cT[:, :, None], cT[:, None, :]
    tb = _pick(T, (1024, 512, 256, 128))
    qh, kh, vh = _heads(qn), _heads(kn), _heads(vf)
    (o, lse), _ = _attn_fwd(qh, kh, vh, ccol, crow, tb=tb)
    y_f = _unheads(o)

    ycat = jnp.concatenate([y_r, y_f.astype(BF16)], axis=1)
    h1 = _mm(ycat, Wt["w_out"], add=x, name="mm_out")
    (hn,) = _rowwise(lambda h_, g_: _rms(h_, g_), [h1], [vec["ffn_norm_g"]], [(D, BF16)], tile=T1, name="norm_ffn")
    gt = _mm(hn, Wt["w_gate"], name="mm_gate")
    up = _mm(hn, Wt["w_up"], name="mm_up")
    (act,) = _rowwise(_swiglu, [gt, up], [], [(gt.shape[1], BF16)], tile=T1, name="swiglu")
    h2 = _mm(act, Wt["w_down"], add=h1, name="mm_down")
    (hg_,) = _rowwise(lambda h_, g_: _rms(h_, g_), [h2], [vec["ple_gate_norm_g"]], [(D, BF16)], tile=T1, name="norm_gate")
    pe = _mm(p, Wt["ple_proj"], name="mm_ple")
    z = _mm(hg_, Wt["ple_gate_w"], name="mm_pgate")

    def tail(h2_, pe_, z_, tg_, png_, pgb_):
        h3, f = jax.vjp(_tail, h2_, pe_, z_, png_, pgb_)
        err = h3 - tg_
        dh3 = err * (1.0 / D)
        lt = 0.5 * jnp.sum(jnp.sum(err * err, axis=1, keepdims=True) * (1.0 / D), axis=0, keepdims=True)
        dh2_, dpe_, dz_, dpng_, dpgb_ = f(dh3)
        return dh2_, dpe_, dz_, jnp.broadcast_to(lt, (1, LANE)), dpng_, dpgb_

    dh3, dpe, dz, loss, g_png, g_pgb = _rowwise(
        tail, [h2, pe, z, tgt], [vec["ple_norm_g"], vec["ple_gate_b"]], [(D, F32), (D, BF16), (D, BF16)],
        [(1, LANE), (1, D), (1, D)], tile=T1, name="tail")
    G = {}
    gv = {"ple_norm_g": g_png, "ple_gate_b": g_pgb}
    G["ple_gate_w"] = _mm(hg_, dz, ta=True, out_dtype=BF16, name="gw_pgate")
    G["ple_proj"] = _mm(p, dpe, ta=True, out_dtype=BF16, name="gw_ple")
    d_hg = _mm(dz, Wt["ple_gate_w"], tb=True, name="mmb_pgate")

    def norm_bwd(h_, dres_, dn_, g_):
        _, f = jax.vjp(_rms, h_, g_)
        dh_, dg_ = f(dn_)
        return dres_ + dh_, dg_

    dh2, gv["ple_gate_norm_g"] = _rowwise(norm_bwd, [h2, dh3, d_hg], [vec["ple_gate_norm_g"]], [(D, F32)], [(1, D)],
                                          tile=T1, name="norm_gate_bwd")
    G["w_down"] = _mm(act, dh2, ta=True, out_dtype=BF16, name="gw_down")
    d_act = _mm(dh2, Wt["w_down"], tb=True, name="mmb_down")

    def swiglu_bwd(gt_, up_, da_):
        _, f = jax.vjp(_swiglu, gt_, up_)
        return f(da_)

    d_gt, d_up = _rowwise(swiglu_bwd, [gt, up, d_act], [], [(gt.shape[1], BF16)] * 2, tile=T1, name="swiglu_bwd")
    G["w_gate"] = _mm(hn, d_gt, ta=True, out_dtype=BF16, name="gw_gate")
    G["w_up"] = _mm(hn, d_up, ta=True, out_dtype=BF16, name="gw_up")
    d_hn = _mm(d_gt, Wt["w_gate"], tb=True, name="mmb_gate")
    d_hn = _mm(d_up, Wt["w_up"], tb=True, add=d_hn, name="mmb_up")
    dh1, gv["ffn_norm_g"] = _rowwise(norm_bwd, [h1, dh2, d_hn], [vec["ffn_norm_g"]], [(D, F32)], [(1, D)],
                                     tile=T1, name="norm_ffn_bwd")
    G["w_out"] = _mm(ycat, dh1, ta=True, out_dtype=BF16, name="gw_out")
    d_ycat = _mm(dh1, Wt["w_out"], tb=True, name="mmb_out")

    def post_bwd(y_, r_, k2_, v_, g_, dy_, *cs):
        lg, lb, rk, sg, sgt = cs
        _, f = jax.vjp(lambda *a: _post_rwkv(*a, sg, sgt), y_, r_, k2_, v_, g_, lg, lb, rk)
        return f(dy_)

    dy, dr1, dk1, dv1, dg, gv["lnx_g"], gv["lnx_b"], g_rk = _rowwise(
        post_bwd, [y, r, k2, v, g, (d_ycat, W, 0)], post_consts, [(W, F32)] * 5, [(1, W)] * 3, tile=128, name="rwkv_post_bwd")
    gv["r_k"] = g_rk.reshape(H, HEAD)
    pieces = late_shards and ([_pieces(n, G[n]).astype(BF16) for n in _LATE], False)
    (dr, ddec, dk2, dkk, dkka, dvT), recv = _scan_bwd(r, dec, k2, kk, kka, _headsT(v), _headsT(dy), chk, states, hg=hg,
                                                      tc=tc, comm=pieces)
    recv = dict(zip(_LATE, recv))
    dv = _unheadsT(dvT)

    def prep_bwd(u_, dr_, dr1_, ddec_, dk2_, dk1_, dv_, dv1_, dkk_, dkka_, dg_, up_, mu_, *cs):
        um, sh = _shift_mix(u_, up_, mu_)
        cs_d, sg, sgt = cs[:7], cs[7], cs[8]
        _, f = jax.vjp(lambda um_, *c_: _prep_rwkv(um_, *c_, sg, sgt, dims=pdims), um, *cs_d)
        res = f((dr_ + dr1_, ddec_, dk2_ + dk1_, dv_ + dv1_, dkk_, dkka_, dg_))
        dum = res[0]
        dmu = jnp.sum(dum * (sh - u_), axis=0, keepdims=True)
        return (dum, dmu) + tuple(res[1:])

    LP = [Wt["w2"].shape, Wt["a2"].shape, Wt["g2"].shape]
    dum, g_mu, gv["w0"], g_w2, gv["a0"], g_a2, g_g2, gv["k_k"], gv["k_a"] = _rowwise(
        prep_bwd, [u_r, dr, dr1, ddec, dk2, dk1, dv, dv1, dkk, dkka, dg], prep_consts, [(d.RP, F32)],
        [(1, d.RP), (1, W), LP[0], (1, W), LP[1], LP[2], (1, W), (1, W)], tile=128, name="rwkv_prep_bwd", prev=[u_r])
    gv["shift_mu"] = d.unpad_r(g_mu)
    G["w2"], G["a2"], G["g2"] = g_w2, g_a2, g_g2
    (du_r,) = _rowwise(lambda a_, an_, mu_: a_ * (1.0 - mu_) + jnp.where(
        lax.broadcasted_iota(jnp.int32, a_.shape, 0) == a_.shape[0] - 1, an_, pltpu.roll(a_, a_.shape[0] - 1, 0)) * mu_,
        [dum], [mu], [(d.RP, BF16)], tile=T1, name="shift_bwd", nxt=[dum])

    do = _heads(d_ycat[:, W:])
    (dq, dcq, dk_, dv_, dck), _ = _attn_bwd(qh, kh, vh, ccol, crow, o, lse, do, tb=tb)
    dc = _padc((dcq[:, :, 0] - dck[:, 0, :]).T, d.FHp)
    dlogf = _cumsum(dc, reverse=True, name="fox_cumsum_bwd")

    def fox_bwd(uf_, dq_, dk__, dv__, dlf_, *cs):
        qg_, kg_, fb_, sg, sgt = cs
        _, f = jax.vjp(lambda uf__, a, b, c_: _prep_fox(uf__, a, b, c_, sg, sgt, dims=fdims), uf_, qg_, kg_, fb_)
        return f((dq_, dk__, dv__, dlf_))

    du_f, g_qg, g_kg, g_fb = _rowwise(fox_bwd, [u_f, _unheads(dq), _unheads(dk_), _unheads(dv_), dlogf], fox_consts,
                                      [(d.FP, BF16)], [(1, FW), (1, FW), (1, d.FHp)], tile=T1, name="fox_prep_bwd")
    gv["q_norm_g"] = g_qg.reshape(d.FH, HEAD).sum(0, keepdims=True)
    gv["k_norm_g"] = g_kg.reshape(d.FH, HEAD).sum(0, keepdims=True)
    gv["fgate_b"] = g_fb[:, :d.FH]

    G["w_in_r"] = _mm(xn, du_r, ta=True, out_dtype=BF16, name="gw_in_r")
    G["w_in_f"] = _mm(xn, du_f, ta=True, out_dtype=BF16, name="gw_in_f")
    d_xn = _mm(du_r, Wt["w_in_r"], tb=True, name="mmb_in_r")
    d_xn = _mm(du_f, Wt["w_in_f"], tb=True, add=d_xn, name="mmb_in_f")
    dx, gv["attn_norm_g"] = _rowwise(norm_bwd, [x, dh1, d_xn], [vec["attn_norm_g"]], [(D, F32)], [(1, D)],
                                     tile=T1, name="norm_attn_bwd")
    return loss, dx, G, gv, recv


_HBM = pl.BlockSpec(memory_space=pl.ANY)
_OTHER_CHIPS = ((0, 1), (1, 0), (1, 1))


def _flip(v, bit):
    return 1 - v if bit else v


def _exchange(arrs, *, gather, name):
    n = len(arrs)

    def body(*refs):
        copies = _exchange_copies(refs[:n], refs[n:2 * n], refs[2 * n:], gather)
        for cp in copies:
            cp.start()
        for cp in copies:
            cp.wait()

    return pl.pallas_call(
        body, name=name, in_specs=[_HBM] * n, out_specs=[_HBM] * n, out_shape=_exchange_shapes(arrs, gather),
        scratch_shapes=_exchange_sems(n),
    )(*arrs)


def _exchange_shapes(arrs, gather):
    return [jax.ShapeDtypeStruct(((N_CHIPS,) + a.shape) if gather else a.shape, a.dtype) for a in arrs]


def _exchange_sems(n):
    return [pltpu.SemaphoreType.DMA((3 * n,)), pltpu.SemaphoreType.DMA((3 * n,)), pltpu.SemaphoreType.DMA((n,))]


def _exchange_copies(ins, outs, sems, gather):
    send_sems, recv_sems, own_sems = sems
    x, y, c = lax.axis_index("x"), lax.axis_index("y"), lax.axis_index("c")
    me = 2 * x + y
    copies = []
    for a in range(len(ins)):
        copies.append(pltpu.make_async_copy(ins[a] if gather else ins[a].at[me], outs[a].at[me], own_sems.at[a]))
        for k, (dx, dy) in enumerate(_OTHER_CHIPS):
            px, py = _flip(x, dx), _flip(y, dy)
            copies.append(pltpu.make_async_remote_copy(
                src_ref=ins[a] if gather else ins[a].at[2 * px + py], dst_ref=outs[a].at[me],
                send_sem=send_sems.at[3 * a + k], recv_sem=recv_sems.at[3 * a + k],
                device_id=(px, py, c), device_id_type=MESH))
    return copies


def _carrying(body, n_in, n_out, grid, comm):
    arrs, gather = comm
    n = len(arrs)

    def wrapped(*refs):
        c_in = refs[n_in:n_in + n]
        c_out = refs[n_in + n + n_out:n_in + 2 * n + n_out]
        ids = [pl.program_id(a) for a in range(len(grid))]
        first = functools.reduce(jnp.logical_and, [i == 0 for i in ids])
        last = functools.reduce(jnp.logical_and, [i == g - 1 for i, g in zip(ids, grid)])

        @pl.when(first)
        def _():
            for cp in _exchange_copies(c_in, c_out, refs[-3:], gather):
                cp.start()

        body(*refs[:n_in], *refs[n_in + n:n_in + n + n_out], *refs[n_in + 2 * n + n_out:-3])

        @pl.when(last)
        def _():
            for cp in _exchange_copies(c_in, c_out, refs[-3:], gather):
                cp.wait()

    return wrapped


def _swap_cores(arrs, *, name):
    n = len(arrs)

    def body(*refs):
        ins, outs = refs[:n], refs[n:2 * n]
        send_sems, recv_sems = refs[2 * n:]
        peer = (lax.axis_index("x"), lax.axis_index("y"), 1 - lax.axis_index("c"))
        cps = [pltpu.make_async_remote_copy(src_ref=ins[a], dst_ref=outs[a], send_sem=send_sems.at[a],
                                            recv_sem=recv_sems.at[a], device_id=peer, device_id_type=MESH) for a in range(n)]
        for cp in cps:
            cp.start()
        for cp in cps:
            cp.wait()

    return pl.pallas_call(
        body, name=name, in_specs=[_HBM] * n, out_specs=[_HBM] * n,
        out_shape=[jax.ShapeDtypeStruct(a.shape, a.dtype) for a in arrs],
        scratch_shapes=[pltpu.SemaphoreType.DMA((n,)), pltpu.SemaphoreType.DMA((n,))],
    )(*arrs)


def _allreduce_small(pack, *, name):
    R, C = pack.shape

    def body(p_ref, o_ref, recv, send_sems, recv_sems):
        x, y, c = lax.axis_index("x"), lax.axis_index("y"), lax.axis_index("c")
        me = 4 * x + 2 * y + c
        recv[me] = p_ref[...]
        cps = []
        for k in range(1, N_DEV):
            peer = (_flip(x, k & 4), _flip(y, k & 2), _flip(c, k & 1))
            cp = pltpu.make_async_remote_copy(src_ref=p_ref, dst_ref=recv.at[me], send_sem=send_sems.at[k - 1],
                                              recv_sem=recv_sems.at[k - 1], device_id=peer, device_id_type=MESH)
            cp.start()
            cps.append(cp)
        for cp in cps:
            cp.wait()
        acc = recv[0]
        for s in range(1, N_DEV):
            acc = acc + recv[s]
        o_ref[...] = acc

    vm = pl.BlockSpec(memory_space=pltpu.VMEM)
    return pl.pallas_call(
        body, name=name, in_specs=[vm], out_specs=vm, out_shape=jax.ShapeDtypeStruct((R, C), F32),
        scratch_shapes=[pltpu.VMEM((N_DEV, R, C), F32), pltpu.SemaphoreType.DMA((N_DEV - 1,)), pltpu.SemaphoreType.DMA((N_DEV - 1,))],
    )(pack)


def _sum_slots(a, *, name):
    S, R, C = a.shape
    tr = _pick(R, (256, 128, 64, 32, 16, 8))

    def body(a_ref, o_ref):
        acc = a_ref[0].astype(F32)
        for s in range(1, S):
            acc = acc + a_ref[s].astype(F32)
        o_ref[...] = acc

    return pl.pallas_call(
        body, name=name, grid=(R // tr,), in_specs=[pl.BlockSpec((S, tr, C), lambda i: (0, i, 0))],
        out_specs=pl.BlockSpec((tr, C), lambda i: (i, 0)), out_shape=jax.ShapeDtypeStruct((R, C), F32),
        compiler_params=_cparams("parallel"),
    )(a)


def _adamw(w, m, v, gs, *, name):
    R, C = w.shape
    tile = _pick(R, (128, 96, 64, 32, 16, 8))

    def fn(w_, m_, v_, *g_):
        g = g_[0]
        for e in g_[1:]:
            g = g + e
        m2 = ADAM_B1 * m_ + (1.0 - ADAM_B1) * g
        v2 = ADAM_B2 * v_ + (1.0 - ADAM_B2) * jnp.square(g)
        m_hat = m2 / (1.0 - ADAM_B1 ** ADAM_STEP)
        v_hat = v2 / (1.0 - ADAM_B2 ** ADAM_STEP)
        delta = -ADAM_LR * (m_hat / (jnp.sqrt(v_hat) + ADAM_EPS) + ADAM_WD * w_)
        return g, delta, m2, v2

    return _rowwise(fn, [w, m, v, *gs], [], [(C, F32)] * 4, tile=tile, name=name)


_ARGS = "x, p, attn_norm_g, w_in, shift_mu, w0, w2, a0, a2, g2, k_k, k_a, r_k, lnx_g, lnx_b, q_norm_g, k_norm_g, fgate_b, w_out, ffn_norm_g, w_gate, w_up, w_down, ple_proj, ple_norm_g, ple_gate_norm_g, ple_gate_w, ple_gate_b, loss_target, m_attn_norm_g, m_w_in, m_shift_mu, m_w0, m_w2, m_a0, m_a2, m_g2, m_k_k, m_k_a, m_r_k, m_lnx_g, m_lnx_b, m_q_norm_g, m_k_norm_g, m_fgate_b, m_w_out, m_ffn_norm_g, m_w_gate, m_w_up, m_w_down, m_ple_proj, m_ple_norm_g, m_ple_gate_norm_g, m_ple_gate_w, m_ple_gate_b, v_attn_norm_g, v_w_in, v_shift_mu, v_w0, v_w2, v_a0, v_a2, v_g2, v_k_k, v_k_a, v_r_k, v_lnx_g, v_lnx_b, v_q_norm_g, v_k_norm_g, v_fgate_b, v_w_out, v_ffn_norm_g, v_w_gate, v_w_up, v_w_down, v_ple_proj, v_ple_norm_g, v_ple_gate_norm_g, v_ple_gate_w, v_ple_gate_b".split(", ")
_WEIGHTS = _ARGS[2:28]
_COL_SHARDED = ("w_in", "w2", "a2", "g2", "w_gate", "w_up", "ple_proj")
_ROW_SHARDED = ("w_out", "w_down", "ple_gate_w")
_MATRICES = _COL_SHARDED + _ROW_SHARDED
_EARLY = ("w_in", "w2", "a2", "g2")
_LATE = tuple(n for n in _MATRICES if n not in _EARLY)
_VECTORS = tuple(n for n in _WEIGHTS if n not in _MATRICES)


def _whole(name, g):
    if name in _COL_SHARDED:
        return g.transpose(1, 0, 2).reshape(g.shape[1], -1)
    return g.reshape(-1, g.shape[2])


def _pieces(name, a):
    if name in _COL_SHARDED:
        return a.reshape(a.shape[0], N_CHIPS, -1).transpose(1, 0, 2)
    return a.reshape(N_CHIPS, -1, a.shape[1])


def kernel(x, p, attn_norm_g, w_in, shift_mu, w0, w2, a0, a2, g2, k_k, k_a, r_k, lnx_g, lnx_b, q_norm_g, k_norm_g, fgate_b, w_out, ffn_norm_g, w_gate, w_up, w_down, ple_proj, ple_norm_g, ple_gate_norm_g, ple_gate_w, ple_gate_b, loss_target, m_attn_norm_g, m_w_in, m_shift_mu, m_w0, m_w2, m_a0, m_a2, m_g2, m_k_k, m_k_a, m_r_k, m_lnx_g, m_lnx_b, m_q_norm_g, m_k_norm_g, m_fgate_b, m_w_out, m_ffn_norm_g, m_w_gate, m_w_up, m_w_down, m_ple_proj, m_ple_norm_g, m_ple_gate_norm_g, m_ple_gate_w, m_ple_gate_b, v_attn_norm_g, v_w_in, v_shift_mu, v_w0, v_w2, v_a0, v_a2, v_g2, v_k_k, v_k_a, v_r_k, v_lnx_g, v_lnx_b, v_q_norm_g, v_k_norm_g, v_fgate_b, v_w_out, v_ffn_norm_g, v_w_gate, v_w_up, v_w_down, v_ple_proj, v_ple_norm_g, v_ple_gate_norm_g, v_ple_gate_w, v_ple_gate_b):
    A = dict(zip(_ARGS, (x, p, attn_norm_g, w_in, shift_mu, w0, w2, a0, a2, g2, k_k, k_a, r_k, lnx_g, lnx_b, q_norm_g, k_norm_g, fgate_b, w_out, ffn_norm_g, w_gate, w_up, w_down, ple_proj, ple_norm_g, ple_gate_norm_g, ple_gate_w, ple_gate_b, loss_target, m_attn_norm_g, m_w_in, m_shift_mu, m_w0, m_w2, m_a0, m_a2, m_g2, m_k_k, m_k_a, m_r_k, m_lnx_g, m_lnx_b, m_q_norm_g, m_k_norm_g, m_fgate_b, m_w_out, m_ffn_norm_g, m_w_gate, m_w_up, m_w_down, m_ple_proj, m_ple_norm_g, m_ple_gate_norm_g, m_ple_gate_w, m_ple_gate_b, v_attn_norm_g, v_w_in, v_shift_mu, v_w0, v_w2, v_a0, v_a2, v_g2, v_k_k, v_k_a, v_r_k, v_lnx_g, v_lnx_b, v_q_norm_g, v_k_norm_g, v_fgate_b, v_w_out, v_ffn_norm_g, v_w_gate, v_w_up, v_w_down, v_ple_proj, v_ple_norm_g, v_ple_gate_norm_g, v_ple_gate_w, v_ple_gate_b)))
    x, p, tgt = A["x"][0], A["p"][0, 0], A["loss_target"][0]
    d = _Dims(W=A["w0"].shape[-1], DL=A["w2"].shape[1], AL=A["a2"].shape[1], GL=A["g2"].shape[1], FH=A["fgate_b"].shape[-1])

    shard = lambda n: A[n][0].astype(BF16)
    gathered = _exchange([shard(n) for n in _EARLY], gather=True, name="gather_early")
    full = {n: _whole(n, g) for n, g in zip(_EARLY, gathered)}
    Wt = {"w_in_r": d.pad_r(full["w_in"][:, :d.RC]), "w_in_f": d.pad_f(full["w_in"][:, d.RC:]),
          "w2": _padr(full["w2"], d.DLp), "a2": _padr(full["a2"], d.ALp), "g2": _padr(full["g2"], d.GLp)}
    vec = {n: A[n].reshape(-1, A[n].shape[-1]) for n in _VECTORS}

    loss, dx, G, gv, recv = _local_step(x, p, tgt, Wt, vec, d, late_shards=[shard(n) for n in _LATE])

    gw = {"w_in": jnp.concatenate([d.unpad_r(G["w_in_r"]), d.unpad_f(G["w_in_f"])], axis=1),
          "w2": G["w2"][:d.DL], "a2": G["a2"][:d.AL], "g2": G["g2"][:d.GL]}
    recv.update(zip(_EARLY, _exchange([_pieces(n, gw[n]).astype(BF16) for n in _EARLY], gather=False, name="scatter_early")))
    part = [_sum_slots(recv[n], name="sum_" + n) for n in _MATRICES]
    sib = _swap_cores(part, name="swap_cores")

    sizes = [1] + [A[n].size for n in _VECTORS]
    rows = _rup(_rup(sum(sizes), LANE) // LANE, 8)

    def pack(items):
        flat = jnp.concatenate([i.reshape(-1) for i in items])
        return jnp.pad(flat, (0, rows * LANE - flat.shape[0])).reshape(rows, LANE)

    red = _allreduce_small(pack([loss[0, :1]] + [gv[n] for n in _VECTORS]), name="allreduce_vectors")
    zero = jnp.zeros((1,), F32)
    upd = _adamw(pack([zero] + [A[n] for n in _VECTORS]), pack([zero] + [A["m_" + n] for n in _VECTORS]),
                 pack([zero + 1.0] + [A["v_" + n] for n in _VECTORS]), [red], name="adamw_vectors")
    offs = [0]
    for s in sizes:
        offs.append(offs[-1] + s)
    unpack = lambda a, i, n: a.reshape(-1)[offs[i + 1]:offs[i + 2]].reshape(A[n].shape)

    out = {"grad": {}, "delta": {}, "new_m": {}, "new_v": {}}
    for i, n in enumerate(_VECTORS):
        for kind, a in zip(out, upd):
            out[kind][n] = unpack(a, i, n)
    for n, mine, other in zip(_MATRICES, part, sib):
        res = _adamw(A[n][0], A["m_" + n][0], A["v_" + n][0], [mine, other], name="adamw_" + n)
        for kind, a in zip(out, res):
            out[kind][n] = a[None]
    return (red[0, 0], dx[None], *[out[k][n] for k in out for n in _WEIGHTS])
```

```python
import functools

import jax
import jax.numpy as jnp
from jax import lax
from jax.experimental import pallas as pl
from jax.experimental.pallas import tpu as pltpu

F32 = jnp.float32
BF16 = jnp.bfloat16
LANE = 128
HEAD = 64
RMS_EPS = 1e-6
GN_EPS = 64e-5
ADAM_LR, ADAM_B1, ADAM_B2, ADAM_EPS, ADAM_WD, ADAM_STEP = 0.001, 0.9, 0.999, 1e-08, 0.01, 10
VMEM_LIMIT = 56 * 1024 * 1024
MM_TILE_BYTES = 40 * 1024 * 1024
NEG = -1e30
MESH = pl.DeviceIdType.MESH
N_CHIPS = 4
N_DEV = 8


def _rup(n, m):
    return -(-n // m) * m


def _pick(n, cands):
    for c in cands:
        if n % c == 0:
            return c
    return n


def _cparams(*sem):
    return pltpu.CompilerParams(dimension_semantics=sem, vmem_limit_bytes=VMEM_LIMIT)


def _mm(a, b, *, ta=False, tb=False, add=None, out_dtype=F32, name):
    M, K = (a.shape[1], a.shape[0]) if ta else a.shape
    N = b.shape[0] if tb else b.shape[1]
    tn = _pick(N, (512, 640, 256, 128))
    fits = lambda m, t: 2 * (m * t * a.dtype.itemsize + t * tn * b.dtype.itemsize + m * tn * 8) <= MM_TILE_BYTES
    tm, tk = next((m, t) for t in (K, 2048, 1024, 512, 640, 256, 128) for m in (1024, 512, 256, 128, M)
                  if K % t == 0 and M % m == 0 and fits(m, t))
    nk = K // tk
    dn = (((0 if ta else 1,), (1 if tb else 0,)), ((), ()))

    def body(*refs):
        if add is None:
            a_ref, b_ref, o_ref, acc = refs
        else:
            a_ref, b_ref, add_ref, o_ref, acc = refs
        ks = pl.program_id(2)
        part = lax.dot_general(a_ref[...].astype(BF16), b_ref[...].astype(BF16), dn, preferred_element_type=F32)
        if nk > 1:
            @pl.when(ks == 0)
            def _():
                acc[...] = jnp.zeros_like(acc)

            acc[...] += part

        @pl.when(ks == nk - 1)
        def _():
            res = acc[...] if nk > 1 else part
            if add is not None:
                res = res + add_ref[...].astype(F32)
            o_ref[...] = res.astype(out_dtype)

    a_spec = pl.BlockSpec((tk, tm), lambda i, j, k: (k, i)) if ta else pl.BlockSpec((tm, tk), lambda i, j, k: (i, k))
    b_spec = pl.BlockSpec((tn, tk), lambda i, j, k: (j, k)) if tb else pl.BlockSpec((tk, tn), lambda i, j, k: (k, j))
    o_spec = pl.BlockSpec((tm, tn), lambda i, j, k: (i, j))
    ins, specs = [a, b], [a_spec, b_spec]
    if add is not None:
        ins.append(add)
        specs.append(o_spec)
    return pl.pallas_call(
        body, name=name, grid=(M // tm, N // tn, nk), in_specs=specs, out_specs=o_spec,
        out_shape=jax.ShapeDtypeStruct((M, N), out_dtype),
        scratch_shapes=[pltpu.VMEM((tm, tn) if nk > 1 else (8, LANE), F32)],
        compiler_params=_cparams("parallel", "parallel", "arbitrary"),
    )(*ins)


def _rowwise(fn, rows, consts, outs, accs=(), *, tile, name, prev=(), nxt=()):
    paired = [not isinstance(r, tuple) and r.ndim == 3 for r in rows]
    rows = [r if isinstance(r, tuple) else (r, r.shape[-1], 0) for r in rows]
    T = rows[0][0].shape[-2]
    tile = min(tile, T)
    n = T // tile
    sub = 8
    nr, npv, nnx, ncst, no, na = len(rows), len(prev), len(nxt), len(consts), len(outs), len(accs)
    out_paired = [len(o) == 3 for o in outs]

    def body(*refs):
        i = pl.program_id(0)
        it = iter(refs)
        rv = [next(it) for _ in range(nr)]
        rv = [jnp.concatenate([r[g] for g in range(r.shape[0])], axis=1) if pr else r[...] for r, pr in zip(rv, paired)]
        pv = [jnp.where(i > 0, next(it)[sub - 1:sub, :], 0.0) for _ in range(npv)]
        nv = [jnp.where(i < n - 1, next(it)[0:1, :], 0.0) for _ in range(nnx)]
        cv = [next(it)[...] for _ in range(ncst)]
        o_refs = [next(it) for _ in range(no)]
        a_refs = [next(it) for _ in range(na)]
        res = fn(*rv, *pv, *nv, *cv)
        if not isinstance(res, (tuple, list)):
            res = (res,)
        for r, o, pr in zip(o_refs, res[:no], out_paired):
            if pr:
                for g in range(r.shape[0]):
                    r[g] = o[:, g * LANE:(g + 1) * LANE].astype(r.dtype)
            else:
                r[...] = o.astype(r.dtype)
        if na:
            @pl.when(i == 0)
            def _():
                for r in a_refs:
                    r[...] = jnp.zeros_like(r)
            for r, o in zip(a_refs, res[no:]):
                r[...] += o.astype(F32)

    in_specs = [pl.BlockSpec((a.shape[0], tile, w), lambda i: (0, i, 0)) if pr else
                pl.BlockSpec((tile, w), functools.partial(lambda cb, i: (i, cb), cb)) for (a, w, cb), pr in zip(rows, paired)]
    in_specs += [pl.BlockSpec((sub, a.shape[1]), lambda i: (jnp.maximum(i * (tile // sub) - 1, 0), 0)) for a in prev]
    in_specs += [pl.BlockSpec((sub, a.shape[1]), lambda i: (jnp.minimum((i + 1) * (tile // sub), T // sub - 1), 0)) for a in nxt]
    in_specs += [pl.BlockSpec(c.shape, lambda i: (0, 0)) for c in consts]
    out_specs = [pl.BlockSpec((o[0] // LANE, tile, LANE), lambda i: (0, i, 0)) if pr else
                 pl.BlockSpec((tile, o[0]), lambda i: (i, 0)) for o, pr in zip(outs, out_paired)]
    out_specs += [pl.BlockSpec(s, lambda i: (0, 0)) for s in accs]
    out_shape = [jax.ShapeDtypeStruct((o[0] // LANE, T, LANE) if pr else (T, o[0]), o[1]) for o, pr in zip(outs, out_paired)]
    out_shape += [jax.ShapeDtypeStruct(s, F32) for s in accs]
    res = pl.pallas_call(
        body, name=name, grid=(n,), in_specs=in_specs, out_specs=out_specs, out_shape=out_shape,
        compiler_params=_cparams("arbitrary"),
    )(*[r[0] for r in rows], *prev, *nxt, *consts)
    return res


@jax.custom_vjp
def _bdot(a, b):
    return jnp.dot(a.astype(BF16), b.astype(BF16), preferred_element_type=F32)


def _bdot_fwd(a, b):
    return _bdot(a, b), (a.astype(BF16), b.astype(BF16))


def _bdot_bwd(res, ct):
    a, b = res
    c = ct.astype(BF16)
    return (lax.dot_general(c, b, (((1,), (1,)), ((), ())), preferred_element_type=F32),
            lax.dot_general(a, c, (((0,), (0,)), ((), ())), preferred_element_type=F32))


_bdot.defvjp(_bdot_fwd, _bdot_bwd)


def _split3(x):
    hi = x.astype(BF16)
    r1 = x - hi.astype(F32)
    mid = r1.astype(BF16)
    return hi, mid, (r1 - mid.astype(F32)).astype(BF16)


def _dot_exact(a, b):
    return sum(jnp.dot(p, b, preferred_element_type=F32) for p in _split3(a))


@jax.custom_vjp
def _head_sums(x, seg, segt):
    return _dot_exact(_dot_exact(x, seg), segt)


def _head_sums_fwd(x, seg, segt):
    return _head_sums(x, seg, segt), (seg, segt)


def _head_sums_bwd(res, ct):
    seg, segt = res
    return _head_sums(ct, seg, segt), jnp.zeros_like(seg), jnp.zeros_like(segt)


_head_sums.defvjp(_head_sums_fwd, _head_sums_bwd)


def _rms(x, g, eps=RMS_EPS):
    return x * lax.rsqrt(jnp.mean(x * x, axis=-1, keepdims=True) + eps) * g


def _softplus(x):
    return jnp.maximum(x, 0.0) + jnp.log(1.0 + jnp.exp(-jnp.abs(x)))


def _sigmoid(x):
    return 1.0 / (1.0 + jnp.exp(-x))


def _seg_mats(width):
    h = lax.broadcasted_iota(jnp.int32, (width, LANE), 0) // HEAD
    j = lax.broadcasted_iota(jnp.int32, (width, LANE), 1)
    seg = (h == j).astype(BF16)
    return seg, seg.T


def _prep_rwkv(um, w0, w2, a0, a2, g2, k_k, k_a, seg, segt, *, dims):
    W, DLp, ALp, GLp = dims
    r, k, v = um[:, :W], um[:, W:2 * W], um[:, 2 * W:3 * W]
    o = 3 * W
    xw, xa, xg = um[:, o:o + DLp], um[:, o + DLp:o + DLp + ALp], um[:, o + DLp + ALp:o + DLp + ALp + GLp]
    w_log = -_softplus(-(w0 + _bdot(jnp.tanh(xw), w2))) - 0.5
    decay = jnp.exp(-jnp.exp(w_log))
    a = _sigmoid(a0 + _bdot(xa, a2))
    g = _bdot(_sigmoid(xg), g2)
    kk = k * k_k
    nrm = jnp.sqrt(_head_sums(kk * kk, seg, segt))
    kk = kk / jnp.maximum(nrm, 1e-12)
    k2 = k * (1.0 + (a - 1.0) * k_a)
    return r, decay, k2, v, kk, kk * a, g


def _shift_mix(u, uprev, mu):
    first = lax.broadcasted_iota(jnp.int32, u.shape, 0) == 0
    sh = jnp.where(first, uprev, pltpu.roll(u, 1, 0))
    return u + (sh - u) * mu, sh


def _post_rwkv(y, r, k2, v, g, lnx_g, lnx_b, r_k, seg, segt):
    inv = 1.0 / HEAD
    mean = _head_sums(y, seg, segt) * inv
    yc = y - mean
    var = _head_sums(yc * yc, seg, segt) * inv
    yn = yc * lax.rsqrt(var + GN_EPS) * lnx_g + lnx_b
    bonus = _head_sums(r * k2 * r_k, seg, segt) * v
    return (yn + bonus) * g


def _prep_fox(uf, qg, kg, fb, seg, segt, *, dims):
    FW, FHp = dims
    q, k, v, f = uf[:, :FW], uf[:, FW:2 * FW], uf[:, 2 * FW:3 * FW], uf[:, 3 * FW:3 * FW + FHp]
    inv = 1.0 / HEAD
    qn = q * lax.rsqrt(_head_sums(q * q, seg, segt) * inv + RMS_EPS) * qg * (HEAD ** -0.5)
    kn = k * lax.rsqrt(_head_sums(k * k, seg, segt) * inv + RMS_EPS) * kg
    return qn, kn, v, -_softplus(-(f + fb))


def _tail(h2, pe, z, png, pgb):
    return h2 + _sigmoid(z + pgb) * _rms(pe, png)


def _swiglu(gt, up):
    return gt * _sigmoid(gt) * up


def _cumsum(x, *, reverse, name):
    T, C = x.shape
    tc = _pick(T, (256, 128))
    n = T // tc
    i0 = lax.broadcasted_iota(jnp.int32, (tc, tc), 0)
    i1 = lax.broadcasted_iota(jnp.int32, (tc, tc), 1)
    tri = ((i0 <= i1) if reverse else (i0 >= i1)).astype(BF16)

    def body(x_ref, tri_ref, o_ref, carry):
        i = pl.program_id(0)

        @pl.when(i == 0)
        def _():
            carry[...] = jnp.zeros_like(carry)

        v = x_ref[...]
        hi = v.astype(BF16)
        r1 = v - hi.astype(F32)
        mid = r1.astype(BF16)
        lo = (r1 - mid.astype(F32)).astype(BF16)
        t = tri_ref[...]
        d = lambda p: jnp.dot(t, p, preferred_element_type=F32)
        c = d(hi) + d(mid) + d(lo) + carry[0:1, :]
        o_ref[...] = c
        edge = c[0:1, :] if reverse else c[tc - 1:tc, :]
        carry[...] = jnp.broadcast_to(edge, carry.shape)

    blk = pl.BlockSpec((tc, C), (lambda i: (n - 1 - i, 0)) if reverse else (lambda i: (i, 0)))
    return pl.pallas_call(
        body, name=name, grid=(n,), in_specs=[blk, pl.BlockSpec((tc, tc), lambda i: (0, 0))], out_specs=blk,
        out_shape=jax.ShapeDtypeStruct((T, C), F32), scratch_shapes=[pltpu.VMEM((8, C), F32)],
        compiler_params=_cparams("arbitrary"),
    )(x, tri)


BWD_PAIRS_PER_TRIP = 8
BWD_STEPS_PER_TRIP = 4
BWD_LOAD_PARTS = 4
FWD_STEPS_PER_TRIP = 8


def _steps(n, per_trip, step, start=0):
    def trip(i, carry):
        for j in range(per_trip):
            carry = step(start + i * per_trip + j, carry)
        return carry

    lax.fori_loop(0, n // per_trip, trip, 0)


def _col(tile, lane, t):
    return jnp.sum(jnp.where(lane == t, tile, 0.0), axis=1, keepdims=True)


def _halves(x, left):
    a = jnp.sum(jnp.where(left, x, 0.0), axis=1, keepdims=True)
    b = jnp.sum(jnp.where(left, 0.0, x), axis=1, keepdims=True)
    return a, b, jnp.where(left, a, b)


def _pair_col(ref, p, lane, left, t):
    return jnp.where(left, _col(ref[2 * p], lane, t), _col(ref[2 * p + 1], lane, t))


def _scan_fwd(r, w, k, kk, kka, vT, *, hg, tc, comm=None):
    H, N, T = vT.shape
    nc = T // tc
    hp = hg // 2
    L = 2 * N
    half = tc // 2

    def body(r_ref, w_ref, k_ref, kk_ref, kka_ref, vT_ref, yT_ref, chk_ref, st_hbm, s_ref, stage, st_sems):
        g, c = pl.program_id(0), pl.program_id(1)

        @pl.when(c == 0)
        def _():
            s_ref[...] = jnp.zeros_like(s_ref)

        def flush(hf, chunk):
            return pltpu.make_async_copy(stage.at[:, pl.ds(hf * half, half)],
                                         st_hbm.at[pl.ds(g * hp, hp), pl.ds(chunk * tc + hf * half, half)], st_sems.at[hf])

        chk_ref[:, 0] = s_ref[...]
        yT_ref[...] = jnp.zeros_like(yT_ref)
        lane = lax.broadcasted_iota(jnp.int32, (N, tc), 1)
        left = lax.broadcasted_iota(jnp.int32, (N, L), 1) < N

        def emit_y(S, p, t):
            ya, yb, _ = _halves(S * r_ref[p, pl.ds(jnp.maximum(t, 0), 1), :], left)
            yT_ref[2 * p] = jnp.where(lane == t, ya, yT_ref[2 * p])
            yT_ref[2 * p + 1] = jnp.where(lane == t, yb, yT_ref[2 * p + 1])

        def step(t, carry):
            for p in range(hp):
                row = lambda ref: ref[p, pl.ds(t, 1), :]
                S = s_ref[p]
                emit_y(S, p, t - 1)
                sa = _halves(S * row(kk_ref), left)[2]
                new = S * row(w_ref) - sa * row(kka_ref) + _pair_col(vT_ref, p, lane, left, t) * row(k_ref)
                s_ref[p] = new
                stage[p, t] = new
            return carry

        for hf in range(2):
            @pl.when(c > 0)
            def _():
                flush(hf, c - 1).wait()

            _steps(half, FWD_STEPS_PER_TRIP, step, start=hf * half)
            flush(hf, c).start()
        for p in range(hp):
            emit_y(s_ref[p], p, tc - 1)

        @pl.when(c == nc - 1)
        def _():
            flush(0, c).wait()
            flush(1, c).wait()

    rows = pl.BlockSpec((hp, tc, L), lambda g, c: (g, c, 0))
    cols = pl.BlockSpec((hg, N, tc), lambda g, c: (g, 0, c))
    return _call_carrying(
        body, "scan_fwd", (H // hg, nc), [rows] * 5 + [cols],
        [cols, pl.BlockSpec((hp, 1, N, L), lambda g, c: (g, c, 0, 0)), _HBM],
        [jax.ShapeDtypeStruct((H, N, T), F32), jax.ShapeDtypeStruct((H // 2, nc, N, L), F32),
         jax.ShapeDtypeStruct((H // 2, T, N, L), F32)],
        [pltpu.VMEM((hp, N, L), F32), pltpu.VMEM((hp, tc, N, L), F32), pltpu.SemaphoreType.DMA((2,))],
        [r, w, k, kk, kka, vT], comm)


def _scan_bwd(r, w, k, kk, kka, vT, dyT, chk, states, *, hg, tc, comm=None):
    H, N, T = vT.shape
    nc = T // tc
    hp = hg // 2
    L = 2 * N
    part = tc // BWD_LOAD_PARTS

    def body(r_ref, w_ref, k_ref, kk_ref, kka_ref, vT_ref, dyT_ref, chk_ref, st_hbm,
             dr_ref, dw_ref, dk_ref, dkk_ref, dkka_ref, dvT_ref, sp_ref, ds_ref, ld_sems):
        g, chunk = pl.program_id(0), nc - 1 - pl.program_id(1)

        def load(i):
            lo = max(1, i * part)
            n = (tc + 1 if i == BWD_LOAD_PARTS - 1 else (i + 1) * part) - lo
            return pltpu.make_async_copy(st_hbm.at[pl.ds(g * hp, hp), pl.ds(chunk * tc + lo - 1, n)],
                                         sp_ref.at[:, pl.ds(lo, n)], ld_sems.at[i])

        for i in reversed(range(BWD_LOAD_PARTS)):
            load(i).start(priority=1)

        @pl.when(pl.program_id(1) == 0)
        def _():
            ds_ref[...] = jnp.zeros_like(ds_ref)

        dvT_ref[...] = jnp.zeros_like(dvT_ref)
        lane = lax.broadcasted_iota(jnp.int32, (N, tc), 1)
        left = lax.broadcasted_iota(jnp.int32, (N, L), 1) < N
        left_row = lax.broadcasted_iota(jnp.int32, (1, L), 1) < N

        halves = _halves
        pair_col = lambda ref, p, t: _pair_col(ref, p, lane, left, t)
        for p in range(hp):
            sp_ref[p, 0] = chk_ref[p, 0]

        def bstep(p0, last, i, carry):
            for j in range(BWD_STEPS_PER_TRIP):
                back_one(p0, last - (i * BWD_STEPS_PER_TRIP + j))
            return carry

        def back_one(p0, t):
            for p in range(p0, min(p0 + BWD_PAIRS_PER_TRIP, hp)):
                row = lambda ref: ref[p, pl.ds(t, 1), :]
                rr, wr, kr, kkr, kkar = row(r_ref), row(w_ref), row(k_ref), row(kk_ref), row(kka_ref)
                Sp = sp_ref[p, t]
                Sn = sp_ref[p, t + 1]
                dycol, vcol = pair_col(dyT_ref, p, t), pair_col(vT_ref, p, t)
                dS = ds_ref[p]
                dSn = dS + dycol * rr
                dsa = halves(dS * kkar, left)[2] + dycol * halves(rr * kkar, left_row)[2]
                dr_ref[p, pl.ds(t, 1), :] = jnp.sum(Sn * dycol, axis=0, keepdims=True)
                sa = halves(Sp * kkr, left)[2]
                dw_ref[p, pl.ds(t, 1), :] = jnp.sum(dSn * Sp, axis=0, keepdims=True)
                dkka_ref[p, pl.ds(t, 1), :] = -jnp.sum(dSn * sa, axis=0, keepdims=True)
                dva, dvb, _ = halves(dSn * kr, left)
                dk_ref[p, pl.ds(t, 1), :] = jnp.sum(dSn * vcol, axis=0, keepdims=True)
                dkk_ref[p, pl.ds(t, 1), :] = -jnp.sum(Sp * dsa, axis=0, keepdims=True)
                ds_ref[p] = dSn * wr - dsa * kkr
                dvT_ref[2 * p] = jnp.where(lane == t, dva, dvT_ref[2 * p])
                dvT_ref[2 * p + 1] = jnp.where(lane == t, dvb, dvT_ref[2 * p + 1])

        for i in reversed(range(BWD_LOAD_PARTS)):
            load(i).wait()
            for p0 in range(0, hp, BWD_PAIRS_PER_TRIP):
                lax.fori_loop(0, part // BWD_STEPS_PER_TRIP, functools.partial(bstep, p0, (i + 1) * part - 1), 0)

    rows = pl.BlockSpec((hp, tc, L), lambda g, c: (g, nc - 1 - c, 0))
    cols = pl.BlockSpec((hg, N, tc), lambda g, c: (g, 0, nc - 1 - c))
    return _call_carrying(
        body, "scan_bwd", (H // hg, nc),
        [rows] * 5 + [cols, cols, pl.BlockSpec((hp, 1, N, L), lambda g, c: (g, nc - 1 - c, 0, 0)), _HBM],
        [rows] * 5 + [cols],
        [jax.ShapeDtypeStruct((H // 2, T, L), F32)] * 5 + [jax.ShapeDtypeStruct((H, N, T), F32)],
        [pltpu.VMEM((hp, tc + 1, N, L), F32), pltpu.VMEM((hp, N, L), F32), pltpu.SemaphoreType.DMA((BWD_LOAD_PARTS,))],
        [r, w, k, kk, kka, vT, dyT, chk, states], comm)


_NT = (((1,), (1,)), ((), ()))
_TN = (((0,), (0,)), ((), ()))


def _scores(q, k, cc, cr, masked):
    s = lax.dot_general(q, k, _NT, preferred_element_type=F32) + cc - cr
    if masked:
        tb = s.shape[0]
        keep = lax.broadcasted_iota(jnp.int32, (tb, tb), 0) >= lax.broadcasted_iota(jnp.int32, (tb, tb), 1)
        s = jnp.where(keep, s, NEG)
    return s


def _attn_specs(T, N, tb):
    blk = pl.BlockSpec((1, tb, N), lambda h, i: (h, i, 0))
    whole = pl.BlockSpec((1, T, N), lambda h, i: (h, 0, 0))
    col = pl.BlockSpec((1, tb, 1), lambda h, i: (h, i, 0))
    wcol = pl.BlockSpec((1, T, 1), lambda h, i: (h, 0, 0))
    row = pl.BlockSpec((1, 1, tb), lambda h, i: (h, 0, i))
    wrow = pl.BlockSpec((1, 1, T), lambda h, i: (h, 0, 0))
    return blk, whole, col, wcol, row, wrow


def _call_carrying(body, name, grid, in_specs, out_specs, out_shape, scratch, args, comm):
    n_out = len(out_specs)
    if comm is not None:
        n = len(comm[0])
        body = _carrying(body, len(in_specs), n_out, grid, comm)
        in_specs, out_specs = in_specs + [_HBM] * n, out_specs + [_HBM] * n
        out_shape, scratch, args = out_shape + _exchange_shapes(*comm), scratch + _exchange_sems(n), args + list(comm[0])
    res = pl.pallas_call(
        body, name=name, grid=grid, in_specs=in_specs, out_specs=out_specs, out_shape=out_shape,
        scratch_shapes=scratch, compiler_params=_cparams(*["arbitrary"] * len(grid)),
    )(*args)
    return res[:n_out], res[n_out:]


def _attn_fwd(q, k, v, ccol, crow, *, tb, comm=None):
    H, T, N = q.shape

    def body(q_ref, k_ref, v_ref, cc_ref, cr_ref, o_ref, lse_ref, m_s, l_s, acc_s):
        qi = pl.program_id(1)
        m_s[...] = jnp.full_like(m_s, NEG)
        l_s[...] = jnp.zeros_like(l_s)
        acc_s[...] = jnp.zeros_like(acc_s)
        q_, cc = q_ref[0], cc_ref[0]

        def block(j, masked):
            at = pl.ds(pl.multiple_of(j * tb, tb), tb)
            s = _scores(q_, k_ref[0, at, :], cc, cr_ref[0, :, at], masked)
            m_new = jnp.maximum(m_s[...], jnp.max(s, axis=1, keepdims=True))
            p = jnp.exp(s - m_new)
            alpha = jnp.exp(m_s[...] - m_new)
            l_s[...] = alpha * l_s[...] + jnp.sum(p, axis=1, keepdims=True)
            acc_s[...] = alpha * acc_s[...] + jnp.dot(p.astype(BF16), v_ref[0, at, :], preferred_element_type=F32)
            m_s[...] = m_new

        def below(j, carry):
            block(j, False)
            return carry

        lax.fori_loop(0, qi, below, 0)
        block(qi, True)
        o_ref[0] = acc_s[...] / l_s[...]
        lse_ref[0] = m_s[...] + jnp.log(l_s[...])

    blk, whole, col, wcol, row, wrow = _attn_specs(T, N, tb)
    return _call_carrying(
        body, "fox_fwd", (H, T // tb), [blk, whole, whole, col, wrow], [blk, col],
        [jax.ShapeDtypeStruct((H, T, N), F32), jax.ShapeDtypeStruct((H, T, 1), F32)],
        [pltpu.VMEM((tb, 1), F32), pltpu.VMEM((tb, 1), F32), pltpu.VMEM((tb, N), F32)],
        [q, k, v, ccol, crow], comm)


def _attn_bwd(q, k, v, ccol, crow, o, lse, do, *, tb, comm=None):
    H, T, N = q.shape
    nb = T // tb

    def body(q_ref, k_ref, v_ref, cc_ref, cr_ref, o_ref, lse_ref, do_ref,
             dq_ref, dcq_ref, dk_ref, dv_ref, dck_ref, dq_s, dcq_s, dk_s, dv_s, dck_s):
        qi = pl.program_id(1)

        @pl.when(qi == 0)
        def _():
            dk_s[...] = jnp.zeros_like(dk_s)
            dv_s[...] = jnp.zeros_like(dv_s)
            dck_s[...] = jnp.zeros_like(dck_s)

        dq_s[...] = jnp.zeros_like(dq_s)
        dcq_s[...] = jnp.zeros_like(dcq_s)
        q_, cc, lse_, do_ = q_ref[0], cc_ref[0], lse_ref[0], do_ref[0]
        delta = jnp.sum(do_ * o_ref[0], axis=1, keepdims=True)
        dob = do_.astype(BF16)

        def block(j, masked):
            at = pl.ds(pl.multiple_of(j * tb, tb), tb)
            kb = k_ref[0, at, :]
            p = jnp.exp(_scores(q_, kb, cc, cr_ref[0, :, at], masked) - lse_)
            dp = lax.dot_general(dob, v_ref[0, at, :], _NT, preferred_element_type=F32)
            ds = p * (dp - delta)
            dsb = ds.astype(BF16)
            dq_s[...] += jnp.dot(dsb, kb, preferred_element_type=F32)
            dcq_s[...] += jnp.sum(ds, axis=1, keepdims=True)
            dv_s[at, :] += lax.dot_general(p.astype(BF16), dob, _TN, preferred_element_type=F32)
            dk_s[at, :] += lax.dot_general(dsb, q_, _TN, preferred_element_type=F32)
            dck_s[:, at] += jnp.sum(ds, axis=0, keepdims=True)

        def below(j, carry):
            block(j, False)
            return carry

        lax.fori_loop(0, qi, below, 0)
        block(qi, True)
        dq_ref[0] = dq_s[...]
        dcq_ref[0] = dcq_s[...]

        @pl.when(qi == nb - 1)
        def _():
            dk_ref[0] = dk_s[...]
            dv_ref[0] = dv_s[...]
            dck_ref[0] = dck_s[...]

    blk, whole, col, wcol, row, wrow = _attn_specs(T, N, tb)
    return _call_carrying(
        body, "fox_bwd", (H, nb), [blk, whole, whole, col, wrow, blk, col, blk], [blk, col, whole, whole, wrow],
        [jax.ShapeDtypeStruct((H, T, N), F32), jax.ShapeDtypeStruct((H, T, 1), F32), jax.ShapeDtypeStruct((H, T, N), F32),
         jax.ShapeDtypeStruct((H, T, N), F32), jax.ShapeDtypeStruct((H, 1, T), F32)],
        [pltpu.VMEM((tb, N), F32), pltpu.VMEM((tb, 1), F32), pltpu.VMEM((T, N), F32), pltpu.VMEM((T, N), F32),
         pltpu.VMEM((1, T), F32)],
        [q, k, v, ccol, crow, o, lse, do], comm)


def _heads(x):
    T = x.shape[0]
    return x.reshape(T, -1, HEAD).transpose(1, 0, 2)


def _headsT(x):
    T = x.shape[0]
    return x.reshape(T, -1, HEAD).transpose(1, 2, 0)


def _unheads(x):
    return x.transpose(1, 0, 2).reshape(x.shape[1], -1)


def _unheadsT(x):
    return x.transpose(2, 0, 1).reshape(x.shape[2], -1)


def _padc(x, n):
    return jnp.pad(x, ((0, 0), (0, n - x.shape[1])))


def _padr(x, n):
    return jnp.pad(x, ((0, n - x.shape[0]), (0, 0)))


class _Dims:
    def __init__(self, W, DL, AL, GL, FH):
        self.W, self.DL, self.AL, self.GL, self.FH = W, DL, AL, GL, FH
        self.DLp, self.ALp, self.GLp, self.FHp = _rup(DL, LANE), _rup(AL, LANE), _rup(GL, LANE), _rup(FH, LANE)
        self.FW = FH * HEAD
        self.RC = 3 * W + DL + AL + GL
        self.RP = 3 * W + self.DLp + self.ALp + self.GLp
        self.FC = 3 * self.FW + FH
        self.FP = 3 * self.FW + self.FHp

    def pad_r(self, a):
        W, o = self.W, 3 * self.W
        return jnp.concatenate([a[:, :o], _padc(a[:, o:o + self.DL], self.DLp),
                                _padc(a[:, o + self.DL:o + self.DL + self.AL], self.ALp),
                                _padc(a[:, o + self.DL + self.AL:self.RC], self.GLp)], axis=1)

    def unpad_r(self, a):
        o = 3 * self.W
        return jnp.concatenate([a[:, :o], a[:, o:o + self.DL], a[:, o + self.DLp:o + self.DLp + self.AL],
                                a[:, o + self.DLp + self.ALp:o + self.DLp + self.ALp + self.GL]], axis=1)

    def pad_f(self, a):
        return _padc(a, self.FP)

    def unpad_f(self, a):
        return a[:, :self.FC]


def _local_step(x, p, tgt, Wt, vec, d, late_shards=None):
    Wt = dict(Wt)
    T, D = x.shape
    W, FW = d.W, d.FW
    H = W // HEAD
    seg, segt = _seg_mats(W)
    segf, segft = _seg_mats(FW)
    T1 = 256
    rk_flat = vec["r_k"].reshape(1, W)
    mu = d.pad_r(vec["shift_mu"])
    qg = jnp.tile(vec["q_norm_g"], (1, d.FH))
    kg = jnp.tile(vec["k_norm_g"], (1, d.FH))
    fb = _padc(vec["fgate_b"], d.FHp)
    pdims = (W, d.DLp, d.ALp, d.GLp)
    fdims = (FW, d.FHp)

    (xn,) = _rowwise(lambda x_, g_: _rms(x_, g_), [x], [vec["attn_norm_g"]], [(D, BF16)], tile=T1, name="norm_attn")
    u_r = _mm(xn, Wt["w_in_r"], name="mm_in_r")
    u_f = _mm(xn, Wt["w_in_f"], name="mm_in_f")

    prep_consts = [mu, vec["w0"], Wt["w2"].astype(F32), vec["a0"], Wt["a2"].astype(F32), Wt["g2"].astype(F32), vec["k_k"], vec["k_a"], seg, segt]

    def prep_fwd(u_, up_, mu_, *cs):
        um, _ = _shift_mix(u_, up_, mu_)
        return _prep_rwkv(um, *cs, dims=pdims)

    pw, tw = (W, F32, "pairs"), (W, F32)
    r, dec, k2, v, kk, kka, g = _rowwise(prep_fwd, [u_r], prep_consts, [pw, pw, pw, tw, pw, pw, tw], tile=128,
                                         name="rwkv_prep", prev=[u_r])
    hg, tc = min(16, H), 128
    (yT, chk, states), gathered = _scan_fwd(r, dec, k2, kk, kka, _headsT(v), hg=hg, tc=tc,
                                            comm=late_shards and (late_shards, True))
    Wt.update({n: _whole(n, g) for n, g in zip(_LATE, gathered)})
    y = _unheadsT(yT)
    post_consts = [vec["lnx_g"], vec["lnx_b"], rk_flat, seg, segt]
    (y_r,) = _rowwise(_post_rwkv, [y, r, k2, v, g], post_consts, [(W, BF16)], tile=T1, name="rwkv_post")

    fox_consts = [qg, kg, fb, segf, segft]
    qn, kn, vf, logf = _rowwise(functools.partial(_prep_fox, dims=fdims), [u_f], fox_consts,
                                [(FW, BF16), (FW, BF16), (FW, BF16), (d.FHp, F32)], tile=T1, name="fox_prep")
    c = _cumsum(logf, reverse=False, name="fox_cumsum")
    cT = c[:, :d.FH].T
    ccol, crow = cT[:, :, None], cT[:, None, :]
    tb = _pick(T, (1024, 512, 256, 128))
    qh, kh, vh = _heads(qn), _heads(kn), _heads(vf)
    (o, lse), _ = _attn_fwd(qh, kh, vh, ccol, crow, tb=tb)
    y_f = _unheads(o)

    ycat = jnp.concatenate([y_r, y_f.astype(BF16)], axis=1)
    h1 = _mm(ycat, Wt["w_out"], add=x, name="mm_out")
    (hn,) = _rowwise(lambda h_, g_: _rms(h_, g_), [h1], [vec["ffn_norm_g"]], [(D, BF16)], tile=T1, name="norm_ffn")
    gt = _mm(hn, Wt["w_gate"], name="mm_gate")
    up = _mm(hn, Wt["w_up"], name="mm_up")
    (act,) = _rowwise(_swiglu, [gt, up], [], [(gt.shape[1], BF16)], tile=T1, name="swiglu")
    h2 = _mm(act, Wt["w_down"], add=h1, name="mm_down")
    (hg_,) = _rowwise(lambda h_, g_: _rms(h_, g_), [h2], [vec["ple_gate_norm_g"]], [(D, BF16)], tile=T1, name="norm_gate")
    pe = _mm(p, Wt["ple_proj"], name="mm_ple")
    z = _mm(hg_, Wt["ple_gate_w"], name="mm_pgate")

    def tail(h2_, pe_, z_, tg_, png_, pgb_):
        h3, f = jax.vjp(_tail, h2_, pe_, z_, png_, pgb_)
        err = h3 - tg_
        dh3 = err * (1.0 / D)
        lt = 0.5 * jnp.sum(jnp.sum(err * err, axis=1, keepdims=True) * (1.0 / D), axis=0, keepdims=True)
        dh2_, dpe_, dz_, dpng_, dpgb_ = f(dh3)
        return dh2_, dpe_, dz_, jnp.broadcast_to(lt, (1, LANE)), dpng_, dpgb_

    dh3, dpe, dz, loss, g_png, g_pgb = _rowwise(
        tail, [h2, pe, z, tgt], [vec["ple_norm_g"], vec["ple_gate_b"]], [(D, F32), (D, BF16), (D, BF16)],
        [(1, LANE), (1, D), (1, D)], tile=T1, name="tail")
    G = {}
    gv = {"ple_norm_g": g_png, "ple_gate_b": g_pgb}
    G["ple_gate_w"] = _mm(hg_, dz, ta=True, out_dtype=BF16, name="gw_pgate")
    G["ple_proj"] = _mm(p, dpe, ta=True, out_dtype=BF16, name="gw_ple")
    d_hg = _mm(dz, Wt["ple_gate_w"], tb=True, name="mmb_pgate")

    def norm_bwd(h_, dres_, dn_, g_):
        _, f = jax.vjp(_rms, h_, g_)
        dh_, dg_ = f(dn_)
        return dres_ + dh_, dg_

    dh2, gv["ple_gate_norm_g"] = _rowwise(norm_bwd, [h2, dh3, d_hg], [vec["ple_gate_norm_g"]], [(D, F32)], [(1, D)],
                                          tile=T1, name="norm_gate_bwd")
    G["w_down"] = _mm(act, dh2, ta=True, out_dtype=BF16, name="gw_down")
    d_act = _mm(dh2, Wt["w_down"], tb=True, name="mmb_down")

    def swiglu_bwd(gt_, up_, da_):
        _, f = jax.vjp(_swiglu, gt_, up_)
        return f(da_)

    d_gt, d_up = _rowwise(swiglu_bwd, [gt, up, d_act], [], [(gt.shape[1], BF16)] * 2, tile=T1, name="swiglu_bwd")
    G["w_gate"] = _mm(hn, d_gt, ta=True, out_dtype=BF16, name="gw_gate")
    G["w_up"] = _mm(hn, d_up, ta=True, out_dtype=BF16, name="gw_up")
    d_hn = _mm(d_gt, Wt["w_gate"], tb=True, name="mmb_gate")
    d_hn = _mm(d_up, Wt["w_up"], tb=True, add=d_hn, name="mmb_up")
    dh1, gv["ffn_norm_g"] = _rowwise(norm_bwd, [h1, dh2, d_hn], [vec["ffn_norm_g"]], [(D, F32)], [(1, D)],
                                     tile=T1, name="norm_ffn_bwd")
    G["w_out"] = _mm(ycat, dh1, ta=True, out_dtype=BF16, name="gw_out")
    d_ycat = _mm(dh1, Wt["w_out"], tb=True, name="mmb_out")

    def post_bwd(y_, r_, k2_, v_, g_, dy_, *cs):
        lg, lb, rk, sg, sgt = cs
        _, f = jax.vjp(lambda *a: _post_rwkv(*a, sg, sgt), y_, r_, k2_, v_, g_, lg, lb, rk)
        return f(dy_)

    dy, dr1, dk1, dv1, dg, gv["lnx_g"], gv["lnx_b"], g_rk = _rowwise(
        post_bwd, [y, r, k2, v, g, (d_ycat, W, 0)], post_consts, [(W, F32)] * 5, [(1, W)] * 3, tile=128, name="rwkv_post_bwd")
    gv["r_k"] = g_rk.reshape(H, HEAD)
    pieces = late_shards and ([_pieces(n, G[n]).astype(BF16) for n in _LATE], False)
    (dr, ddec, dk2, dkk, dkka, dvT), recv = _scan_bwd(r, dec, k2, kk, kka, _headsT(v), _headsT(dy), chk, states, hg=hg,
                                                      tc=tc, comm=pieces)
    recv = dict(zip(_LATE, recv))
    dv = _unheadsT(dvT)

    def prep_bwd(u_, dr_, dr1_, ddec_, dk2_, dk1_, dv_, dv1_, dkk_, dkka_, dg_, up_, mu_, *cs):
        um, sh = _shift_mix(u_, up_, mu_)
        cs_d, sg, sgt = cs[:7], cs[7], cs[8]
        _, f = jax.vjp(lambda um_, *c_: _prep_rwkv(um_, *c_, sg, sgt, dims=pdims), um, *cs_d)
        res = f((dr_ + dr1_, ddec_, dk2_ + dk1_, dv_ + dv1_, dkk_, dkka_, dg_))
        dum = res[0]
        dmu = jnp.sum(dum * (sh - u_), axis=0, keepdims=True)
        return (dum, dmu) + tuple(res[1:])

    LP = [Wt["w2"].shape, Wt["a2"].shape, Wt["g2"].shape]
    dum, g_mu, gv["w0"], g_w2, gv["a0"], g_a2, g_g2, gv["k_k"], gv["k_a"] = _rowwise(
        prep_bwd, [u_r, dr, dr1, ddec, dk2, dk1, dv, dv1, dkk, dkka, dg], prep_consts, [(d.RP, F32)],
        [(1, d.RP), (1, W), LP[0], (1, W), LP[1], LP[2], (1, W), (1, W)], tile=128, name="rwkv_prep_bwd", prev=[u_r])
    gv["shift_mu"] = d.unpad_r(g_mu)
    G["w2"], G["a2"], G["g2"] = g_w2, g_a2, g_g2
    (du_r,) = _rowwise(lambda a_, an_, mu_: a_ * (1.0 - mu_) + jnp.where(
        lax.broadcasted_iota(jnp.int32, a_.shape, 0) == a_.shape[0] - 1, an_, pltpu.roll(a_, a_.shape[0] - 1, 0)) * mu_,
        [dum], [mu], [(d.RP, BF16)], tile=T1, name="shift_bwd", nxt=[dum])

    do = _heads(d_ycat[:, W:])
    (dq, dcq, dk_, dv_, dck), _ = _attn_bwd(qh, kh, vh, ccol, crow, o, lse, do, tb=tb)
    dc = _padc((dcq[:, :, 0] - dck[:, 0, :]).T, d.FHp)
    dlogf = _cumsum(dc, reverse=True, name="fox_cumsum_bwd")

    def fox_bwd(uf_, dq_, dk__, dv__, dlf_, *cs):
        qg_, kg_, fb_, sg, sgt = cs
        _, f = jax.vjp(lambda uf__, a, b, c_: _prep_fox(uf__, a, b, c_, sg, sgt, dims=fdims), uf_, qg_, kg_, fb_)
        return f((dq_, dk__, dv__, dlf_))

    du_f, g_qg, g_kg, g_fb = _rowwise(fox_bwd, [u_f, _unheads(dq), _unheads(dk_), _unheads(dv_), dlogf], fox_consts,
                                      [(d.FP, BF16)], [(1, FW), (1, FW), (1, d.FHp)], tile=T1, name="fox_prep_bwd")
    gv["q_norm_g"] = g_qg.reshape(d.FH, HEAD).sum(0, keepdims=True)
    gv["k_norm_g"] = g_kg.reshape(d.FH, HEAD).sum(0, keepdims=True)
    gv["fgate_b"] = g_fb[:, :d.FH]

    G["w_in_r"] = _mm(xn, du_r, ta=True, out_dtype=BF16, name="gw_in_r")
    G["w_in_f"] = _mm(xn, du_f, ta=True, out_dtype=BF16, name="gw_in_f")
    d_xn = _mm(du_r, Wt["w_in_r"], tb=True, name="mmb_in_r")
    d_xn = _mm(du_f, Wt["w_in_f"], tb=True, add=d_xn, name="mmb_in_f")
    dx, gv["attn_norm_g"] = _rowwise(norm_bwd, [x, dh1, d_xn], [vec["attn_norm_g"]], [(D, F32)], [(1, D)],
                                     tile=T1, name="norm_attn_bwd")
    return loss, dx, G, gv, recv


_HBM = pl.BlockSpec(memory_space=pl.ANY)
_OTHER_CHIPS = ((0, 1), (1, 0), (1, 1))


def _flip(v, bit):
    return 1 - v if bit else v


def _exchange(arrs, *, gather, name):
    n = len(arrs)

    def body(*refs):
        copies = _exchange_copies(refs[:n], refs[n:2 * n], refs[2 * n:], gather)
        for cp in copies:
            cp.start()
        for cp in copies:
            cp.wait()

    return pl.pallas_call(
        body, name=name, in_specs=[_HBM] * n, out_specs=[_HBM] * n, out_shape=_exchange_shapes(arrs, gather),
        scratch_shapes=_exchange_sems(n),
    )(*arrs)


def _exchange_shapes(arrs, gather):
    return [jax.ShapeDtypeStruct(((N_CHIPS,) + a.shape) if gather else a.shape, a.dtype) for a in arrs]


def _exchange_sems(n):
    return [pltpu.SemaphoreType.DMA((3 * n,)), pltpu.SemaphoreType.DMA((3 * n,)), pltpu.SemaphoreType.DMA((n,))]


def _exchange_copies(ins, outs, sems, gather):
    send_sems, recv_sems, own_sems = sems
    x, y, c = lax.axis_index("x"), lax.axis_index("y"), lax.axis_index("c")
    me = 2 * x + y
    copies = []
    for a in range(len(ins)):
        copies.append(pltpu.make_async_copy(ins[a] if gather else ins[a].at[me], outs[a].at[me], own_sems.at[a]))
        for k, (dx, dy) in enumerate(_OTHER_CHIPS):
            px, py = _flip(x, dx), _flip(y, dy)
            copies.append(pltpu.make_async_remote_copy(
                src_ref=ins[a] if gather else ins[a].at[2 * px + py], dst_ref=outs[a].at[me],
                send_sem=send_sems.at[3 * a + k], recv_sem=recv_sems.at[3 * a + k],
                device_id=(px, py, c), device_id_type=MESH))
    return copies


def _carrying(body, n_in, n_out, grid, comm):
    arrs, gather = comm
    n = len(arrs)

    def wrapped(*refs):
        c_in = refs[n_in:n_in + n]
        c_out = refs[n_in + n + n_out:n_in + 2 * n + n_out]
        ids = [pl.program_id(a) for a in range(len(grid))]
        first = functools.reduce(jnp.logical_and, [i == 0 for i in ids])
        last = functools.reduce(jnp.logical_and, [i == g - 1 for i, g in zip(ids, grid)])

        @pl.when(first)
        def _():
            for cp in _exchange_copies(c_in, c_out, refs[-3:], gather):
                cp.start()

        body(*refs[:n_in], *refs[n_in + n:n_in + n + n_out], *refs[n_in + 2 * n + n_out:-3])

        @pl.when(last)
        def _():
            for cp in _exchange_copies(c_in, c_out, refs[-3:], gather):
                cp.wait()

    return wrapped


def _swap_cores(arrs, *, name):
    n = len(arrs)

    def body(*refs):
        ins, outs = refs[:n], refs[n:2 * n]
        send_sems, recv_sems = refs[2 * n:]
        peer = (lax.axis_index("x"), lax.axis_index("y"), 1 - lax.axis_index("c"))
        cps = [pltpu.make_async_remote_copy(src_ref=ins[a], dst_ref=outs[a], send_sem=send_sems.at[a],
                                            recv_sem=recv_sems.at[a], device_id=peer, device_id_type=MESH) for a in range(n)]
        for cp in cps:
            cp.start()
        for cp in cps:
            cp.wait()

    return pl.pallas_call(
        body, name=name, in_specs=[_HBM] * n, out_specs=[_HBM] * n,
        out_shape=[jax.ShapeDtypeStruct(a.shape, a.dtype) for a in arrs],
        scratch_shapes=[pltpu.SemaphoreType.DMA((n,)), pltpu.SemaphoreType.DMA((n,))],
    )(*arrs)


def _allreduce_small(pack, *, name):
    R, C = pack.shape

    def body(p_ref, o_ref, recv, send_sems, recv_sems):
        x, y, c = lax.axis_index("x"), lax.axis_index("y"), lax.axis_index("c")
        me = 4 * x + 2 * y + c
        recv[me] = p_ref[...]
        cps = []
        for k in range(1, N_DEV):
            peer = (_flip(x, k & 4), _flip(y, k & 2), _flip(c, k & 1))
            cp = pltpu.make_async_remote_copy(src_ref=p_ref, dst_ref=recv.at[me], send_sem=send_sems.at[k - 1],
                                              recv_sem=recv_sems.at[k - 1], device_id=peer, device_id_type=MESH)
            cp.start()
            cps.append(cp)
        for cp in cps:
            cp.wait()
        acc = recv[0]
        for s in range(1, N_DEV):
            acc = acc + recv[s]
        o_ref[...] = acc

    vm = pl.BlockSpec(memory_space=pltpu.VMEM)
    return pl.pallas_call(
        body, name=name, in_specs=[vm], out_specs=vm, out_shape=jax.ShapeDtypeStruct((R, C), F32),
        scratch_shapes=[pltpu.VMEM((N_DEV, R, C), F32), pltpu.SemaphoreType.DMA((N_DEV - 1,)), pltpu.SemaphoreType.DMA((N_DEV - 1,))],
    )(pack)


def _sum_slots(a, *, name):
    S, R, C = a.shape
    tr = _pick(R, (256, 128, 64, 32, 16, 8))

    def body(a_ref, o_ref):
        acc = a_ref[0].astype(F32)
        for s in range(1, S):
            acc = acc + a_ref[s].astype(F32)
        o_ref[...] = acc

    return pl.pallas_call(
        body, name=name, grid=(R // tr,), in_specs=[pl.BlockSpec((S, tr, C), lambda i: (0, i, 0))],
        out_specs=pl.BlockSpec((tr, C), lambda i: (i, 0)), out_shape=jax.ShapeDtypeStruct((R, C), F32),
        compiler_params=_cparams("parallel"),
    )(a)


def _adamw(w, m, v, gs, *, name):
    R, C = w.shape
    tile = _pick(R, (128, 96, 64, 32, 16, 8))

    def fn(w_, m_, v_, *g_):
        g = g_[0]
        for e in g_[1:]:
            g = g + e
        m2 = ADAM_B1 * m_ + (1.0 - ADAM_B1) * g
        v2 = ADAM_B2 * v_ + (1.0 - ADAM_B2) * jnp.square(g)
        m_hat = m2 / (1.0 - ADAM_B1 ** ADAM_STEP)
        v_hat = v2 / (1.0 - ADAM_B2 ** ADAM_STEP)
        delta = -ADAM_LR * (m_hat / (jnp.sqrt(v_hat) + ADAM_EPS) + ADAM_WD * w_)
        return g, delta, m2, v2

    return _rowwise(fn, [w, m, v, *gs], [], [(C, F32)] * 4, tile=tile, name=name)


_ARGS = "x, p, attn_norm_g, w_in, shift_mu, w0, w2, a0, a2, g2, k_k, k_a, r_k, lnx_g, lnx_b, q_norm_g, k_norm_g, fgate_b, w_out, ffn_norm_g, w_gate, w_up, w_down, ple_proj, ple_norm_g, ple_gate_norm_g, ple_gate_w, ple_gate_b, loss_target, m_attn_norm_g, m_w_in, m_shift_mu, m_w0, m_w2, m_a0, m_a2, m_g2, m_k_k, m_k_a, m_r_k, m_lnx_g, m_lnx_b, m_q_norm_g, m_k_norm_g, m_fgate_b, m_w_out, m_ffn_norm_g, m_w_gate, m_w_up, m_w_down, m_ple_proj, m_ple_norm_g, m_ple_gate_norm_g, m_ple_gate_w, m_ple_gate_b, v_attn_norm_g, v_w_in, v_shift_mu, v_w0, v_w2, v_a0, v_a2, v_g2, v_k_k, v_k_a, v_r_k, v_lnx_g, v_lnx_b, v_q_norm_g, v_k_norm_g, v_fgate_b, v_w_out, v_ffn_norm_g, v_w_gate, v_w_up, v_w_down, v_ple_proj, v_ple_norm_g, v_ple_gate_norm_g, v_ple_gate_w, v_ple_gate_b".split(", ")
_WEIGHTS = _ARGS[2:28]
_COL_SHARDED = ("w_in", "w2", "a2", "g2", "w_gate", "w_up", "ple_proj")
_ROW_SHARDED = ("w_out", "w_down", "ple_gate_w")
_MATRICES = _COL_SHARDED + _ROW_SHARDED
_EARLY = ("w_in", "w2", "a2", "g2")
_LATE = tuple(n for n in _MATRICES if n not in _EARLY)
_VECTORS = tuple(n for n in _WEIGHTS if n not in _MATRICES)


def _whole(name, g):
    if name in _COL_SHARDED:
        return g.transpose(1, 0, 2).reshape(g.shape[1], -1)
    return g.reshape(-1, g.shape[2])


def _pieces(name, a):
    if name in _COL_SHARDED:
        return a.reshape(a.shape[0], N_CHIPS, -1).transpose(1, 0, 2)
    return a.reshape(N_CHIPS, -1, a.shape[1])


def kernel(x, p, attn_norm_g, w_in, shift_mu, w0, w2, a0, a2, g2, k_k, k_a, r_k, lnx_g, lnx_b, q_norm_g, k_norm_g, fgate_b, w_out, ffn_norm_g, w_gate, w_up, w_down, ple_proj, ple_norm_g, ple_gate_norm_g, ple_gate_w, ple_gate_b, loss_target, m_attn_norm_g, m_w_in, m_shift_mu, m_w0, m_w2, m_a0, m_a2, m_g2, m_k_k, m_k_a, m_r_k, m_lnx_g, m_lnx_b, m_q_norm_g, m_k_norm_g, m_fgate_b, m_w_out, m_ffn_norm_g, m_w_gate, m_w_up, m_w_down, m_ple_proj, m_ple_norm_g, m_ple_gate_norm_g, m_ple_gate_w, m_ple_gate_b, v_attn_norm_g, v_w_in, v_shift_mu, v_w0, v_w2, v_a0, v_a2, v_g2, v_k_k, v_k_a, v_r_k, v_lnx_g, v_lnx_b, v_q_norm_g, v_k_norm_g, v_fgate_b, v_w_out, v_ffn_norm_g, v_w_gate, v_w_up, v_w_down, v_ple_proj, v_ple_norm_g, v_ple_gate_norm_g, v_ple_gate_w, v_ple_gate_b):
    A = dict(zip(_ARGS, (x, p, attn_norm_g, w_in, shift_mu, w0, w2, a0, a2, g2, k_k, k_a, r_k, lnx_g, lnx_b, q_norm_g, k_norm_g, fgate_b, w_out, ffn_norm_g, w_gate, w_up, w_down, ple_proj, ple_norm_g, ple_gate_norm_g, ple_gate_w, ple_gate_b, loss_target, m_attn_norm_g, m_w_in, m_shift_mu, m_w0, m_w2, m_a0, m_a2, m_g2, m_k_k, m_k_a, m_r_k, m_lnx_g, m_lnx_b, m_q_norm_g, m_k_norm_g, m_fgate_b, m_w_out, m_ffn_norm_g, m_w_gate, m_w_up, m_w_down, m_ple_proj, m_ple_norm_g, m_ple_gate_norm_g, m_ple_gate_w, m_ple_gate_b, v_attn_norm_g, v_w_in, v_shift_mu, v_w0, v_w2, v_a0, v_a2, v_g2, v_k_k, v_k_a, v_r_k, v_lnx_g, v_lnx_b, v_q_norm_g, v_k_norm_g, v_fgate_b, v_w_out, v_ffn_norm_g, v_w_gate, v_w_up, v_w_down, v_ple_proj, v_ple_norm_g, v_ple_gate_norm_g, v_ple_gate_w, v_ple_gate_b)))
    x, p, tgt = A["x"][0], A["p"][0, 0], A["loss_target"][0]
    d = _Dims(W=A["w0"].shape[-1], DL=A["w2"].shape[1], AL=A["a2"].shape[1], GL=A["g2"].shape[1], FH=A["fgate_b"].shape[-1])

    shard = lambda n: A[n][0].astype(BF16)
    gathered = _exchange([shard(n) for n in _EARLY], gather=True, name="gather_early")
    full = {n: _whole(n, g) for n, g in zip(_EARLY, gathered)}
    Wt = {"w_in_r": d.pad_r(full["w_in"][:, :d.RC]), "w_in_f": d.pad_f(full["w_in"][:, d.RC:]),
          "w2": _padr(full["w2"], d.DLp), "a2": _padr(full["a2"], d.ALp), "g2": _padr(full["g2"], d.GLp)}
    vec = {n: A[n].reshape(-1, A[n].shape[-1]) for n in _VECTORS}

    loss, dx, G, gv, recv = _local_step(x, p, tgt, Wt, vec, d, late_shards=[shard(n) for n in _LATE])

    gw = {"w_in": jnp.concatenate([d.unpad_r(G["w_in_r"]), d.unpad_f(G["w_in_f"])], axis=1),
          "w2": G["w2"][:d.DL], "a2": G["a2"][:d.AL], "g2": G["g2"][:d.GL]}
    recv.update(zip(_EARLY, _exchange([_pieces(n, gw[n]).astype(BF16) for n in _EARLY], gather=False, name="scatter_early")))
    part = [_sum_slots(recv[n], name="sum_" + n) for n in _MATRICES]
    sib = _swap_cores(part, name="swap_cores")

    sizes = [1] + [A[n].size for n in _VECTORS]
    rows = _rup(_rup(sum(sizes), LANE) // LANE, 8)

    def pack(items):
        flat = jnp.concatenate([i.reshape(-1) for i in items])
        return jnp.pad(flat, (0, rows * LANE - flat.shape[0])).reshape(rows, LANE)

    red = _allreduce_small(pack([loss[0, :1]] + [gv[n] for n in _VECTORS]), name="allreduce_vectors")
    zero = jnp.zeros((1,), F32)
    upd = _adamw(pack([zero] + [A[n] for n in _VECTORS]), pack([zero] + [A["m_" + n] for n in _VECTORS]),
                 pack([zero + 1.0] + [A["v_" + n] for n in _VECTORS]), [red], name="adamw_vectors")
    offs = [0]
    for s in sizes:
        offs.append(offs[-1] + s)
    unpack = lambda a, i, n: a.reshape(-1)[offs[i + 1]:offs[i + 2]].reshape(A[n].shape)

    out = {"grad": {}, "delta": {}, "new_m": {}, "new_v": {}}
    for i, n in enumerate(_VECTORS):
        for kind, a in zip(out, upd):
            out[kind][n] = unpack(a, i, n)
    for n, mine, other in zip(_MATRICES, part, sib):
        res = _adamw(A[n][0], A["m_" + n][0], A["v_" + n][0], [mine, other], name="adamw_" + n)
        for kind, a in zip(out, res):
            out[kind][n] = a[None]
    return (red[0, 0], dx[None], *[out[k][n] for k in out for n in _WEIGHTS])
```
